```python
import functools
import jax, jax.numpy as jnp
from jax import lax
import numpy as np

D_MODEL = 1024
BATCH = 1
SEQ = 16384
DEPTH = 2
DEC_BATCH = 8
DEC_SEQ = 16
PAST_LEN = 1024

CHUNK = 64
Q_BLOCK = 128
HEAD_DIM = 64
SB_HEADS = 4
GDN_HEADS = 6
ML_HEADS = 6
SB_W = SB_HEADS * HEAD_DIM
GDN_W = GDN_HEADS * HEAD_DIM
ML_W = ML_HEADS * HEAD_DIM
MIX_W = SB_W + GDN_W + ML_W
CONV_W = 4
N_MEM = 256
MEM_HEADS = 4
MEM_HEAD_DIM = D_MODEL // MEM_HEADS
FF_RAW = -(-8 * D_MODEL // 3)
D_FF = -(-FF_RAW // 256) * 256
IN_SIZES = (SB_W, SB_W, SB_W, 3 * GDN_W, GDN_W, GDN_HEADS, GDN_HEADS, 3 * ML_W, ML_W, ML_HEADS, ML_HEADS)
IN_W = sum(IN_SIZES)
DEEPNORM_ALPHA = (2 * DEPTH) ** 0.25
DEEPNORM_BETA = (8 * DEPTH) ** -0.25
LN_EPS = 1e-5
NORM_EPS = 1e-6

kernel_name = 'hybrid_sb_gdn_mlstm_stream_step'

F32 = jnp.float32


def layer_norm(x, g, b):
    xf = x.astype(F32)
    mu = xf.mean(-1, keepdims=True)
    var = jnp.mean(jnp.square(xf - mu), -1, keepdims=True)
    return ((xf - mu) * lax.rsqrt(var + LN_EPS) * g.astype(F32) + b.astype(F32)).astype(x.dtype)


def rms_norm(x, g):
    xf = x.astype(F32)
    return xf * lax.rsqrt(jnp.mean(jnp.square(xf), -1, keepdims=True) + NORM_EPS) * g.astype(F32)


def l2norm(x):
    xf = x.astype(F32)
    return xf * lax.rsqrt(jnp.sum(jnp.square(xf), -1, keepdims=True) + NORM_EPS)


def sb_attend(q, k, v, q_pos, k_pos):
    z = jnp.einsum('bqhd,bkhd->bhqk', q.astype(F32), k.astype(F32)) * HEAD_DIM ** -0.5
    valid = k_pos[None, :] < q_pos[:, None]
    log_beta = jax.nn.log_sigmoid(z)
    log_1m = jnp.where(valid, log_beta - z, 0.0)
    after = lax.cumsum(log_1m, axis=3, reverse=True) - log_1m
    a = jnp.where(valid, jnp.exp(log_beta + after), 0.0)
    return jnp.einsum('bhqk,bkhd->bqhd', a, v.astype(F32))


def sb_prompt(q, k, v):
    B, S, H, d = q.shape
    nb = S // Q_BLOCK
    qb = jnp.moveaxis(q.reshape(B, nb, Q_BLOCK, H, d), 1, 0)
    k_pos = jnp.arange(S)

    def block(args):
        i, qi = args
        return sb_attend(qi, k, v, i * Q_BLOCK + jnp.arange(Q_BLOCK), k_pos)

    out = lax.map(block, (jnp.arange(nb), qb))
    return jnp.moveaxis(out, 0, 1).reshape(B, S, H, d)


def sb_sample(past_k, past_v, q, k, v):
    P, T = past_k.shape[1], q.shape[1]
    kk = jnp.concatenate([past_k.astype(k.dtype), k], axis=1)
    vv = jnp.concatenate([past_v.astype(v.dtype), v], axis=1)
    return sb_attend(q, kk, vv, P + jnp.arange(T), jnp.arange(P + T))


def chunked_scan(step, state, xs):
    T = xs[0].shape[2]
    L = min(T, CHUNK)
    n = T // L

    def to_blocks(a):
        return jnp.moveaxis(a.reshape(a.shape[:2] + (n, L) + a.shape[3:]), 2, 0)

    state, ys = lax.scan(step, state, [to_blocks(a) for a in xs])
    ys = jnp.moveaxis(ys, 0, 2)
    return state, ys.reshape(ys.shape[:2] + (T,) + ys.shape[4:])


def gdn_block(S, inp):
    q, k, v, g, beta = inp
    L = q.shape[2]
    dv = v.shape[-1]
    tri = jnp.tril(jnp.ones((L, L), bool))
    stri = jnp.tril(jnp.ones((L, L), bool), -1)
    G = jnp.cumsum(g, -1)
    decay = jnp.where(tri, jnp.exp(jnp.where(tri, G[..., :, None] - G[..., None, :], 0.0)), 0.0)
    kb = k * beta[..., None]
    A = jnp.where(stri, jnp.einsum('bhtd,bhjd->bhtj', kb, k) * decay, 0.0)
    M = A + jnp.eye(L, dtype=A.dtype)
    rhs = jnp.concatenate([v * beta[..., None], kb * jnp.exp(G)[..., None]], -1)
    sol = lax.linalg.triangular_solve(M, rhs, left_side=True, lower=True, unit_diagonal=True)
    u = sol[..., :dv] - jnp.einsum('bhtk,bhkv->bhtv', sol[..., dv:], S)
    attn = jnp.einsum('bhtd,bhjd->bhtj', q, k) * decay
    o = jnp.einsum('bhtk,bhkv->bhtv', q * jnp.exp(G)[..., None], S) + jnp.einsum('bhtj,bhjv->bhtv', attn, u)
    GL = G[..., -1:]
    S_new = jnp.exp(GL)[..., None] * S + jnp.einsum('bhtk,bhtv->bhkv', k * jnp.exp(GL - G)[..., None], u)
    return S_new, o


def mlstm_block(state, inp):
    C, n, m = state
    q, k, v, ig, lf = inp
    L = q.shape[2]
    tri = jnp.tril(jnp.ones((L, L), bool))
    F = jnp.cumsum(lf, -1)
    D = jnp.where(tri, F[..., :, None] - F[..., None, :] + ig[..., None, :], -jnp.inf)
    inter = F + m[..., None]
    m_t = jnp.maximum(inter, D.max(-1))
    W = jnp.exp(D - m_t[..., None]) * jnp.einsum('bhtd,bhjd->bhtj', q, k)
    c_inter = jnp.exp(inter - m_t)
    num = c_inter[..., None] * jnp.einsum('bhtk,bhkv->bhtv', q, C) + jnp.einsum('bhtj,bhjv->bhtv', W, v)
    den = c_inter * jnp.einsum('bhtk,bhk->bht', q, n) + W.sum(-1)
    h = num / jnp.maximum(jnp.abs(den), jnp.exp(-m_t))[..., None]
    m_new = m_t[..., -1]
    w_end = jnp.exp(F[..., -1:] - F + ig - m_new[..., None])
    c_prev = jnp.exp(F[..., -1] + m - m_new)
    C_new = c_prev[..., None, None] * C + jnp.einsum('bht,bhtk,bhtv->bhkv', w_end, k, v)
    n_new = c_prev[..., None] * n + jnp.einsum('bht,bhtk->bhk', w_end, k)
    return (C_new, n_new, m_new), h


def causal_conv(x_ext, w):
    T = x_ext.shape[1] - (CONV_W - 1)
    return sum(x_ext[:, j:j + T] * w[j] for j in range(CONV_W))


def token_mix(x, sb_fn, conv_ctx, gdn_S, ml_C, ml_n, ml_m, p):
    B, T, _ = x.shape
    splits = np.cumsum(IN_SIZES)[:-1].tolist()
    proj = x @ p['w_in']
    sq, sk, sv, g_qkv, g_z, g_b, g_a, m_qkv, m_o, m_i, m_f = jnp.split(proj, splits, axis=-1)

    def heads(a, h):
        return a.reshape(B, T, h, HEAD_DIM)

    def bht(a):
        return jnp.swapaxes(a.astype(F32), 1, 2)

    sk, sv = heads(sk, SB_HEADS), heads(sv, SB_HEADS)
    o_sb = sb_fn(heads(sq, SB_HEADS), sk, sv).reshape(B, T, SB_W)

    qkv_ext = jnp.concatenate([conv_ctx.astype(g_qkv.dtype), g_qkv], axis=1)
    new_conv = qkv_ext[:, -(CONV_W - 1):]
    qkv = jax.nn.silu(causal_conv(qkv_ext, p['gdn_conv_w']))
    gq, gk, gv = jnp.split(qkv, 3, axis=-1)
    gq = l2norm(heads(gq, GDN_HEADS)) * HEAD_DIM ** -0.5
    gk = l2norm(heads(gk, GDN_HEADS))
    beta = jax.nn.sigmoid(g_b.astype(F32))
    g = -jnp.exp(p['gdn_A_log'].astype(F32)) * jax.nn.softplus(g_a.astype(F32) + p['gdn_dt_bias'].astype(F32))
    gdn_S, o_g = chunked_scan(gdn_block, gdn_S.astype(F32), [bht(gq), bht(gk), bht(heads(gv, GDN_HEADS)), bht(g), bht(beta)])
    o_g = rms_norm(jnp.swapaxes(o_g, 1, 2), p['gdn_norm_w']) * jax.nn.silu(heads(g_z, GDN_HEADS).astype(F32))
    o_g = o_g.reshape(B, T, GDN_W)

    mq, mk, mv = jnp.split(m_qkv, 3, axis=-1)
    mk = heads(mk, ML_HEADS).astype(F32) * HEAD_DIM ** -0.5
    ig = m_i.astype(F32) + p['mlstm_i_bias'].astype(F32)
    lf = jax.nn.log_sigmoid(m_f.astype(F32) + p['mlstm_f_bias'].astype(F32))
    ml_state, h = chunked_scan(mlstm_block, (ml_C.astype(F32), ml_n.astype(F32), ml_m.astype(F32)),
                               [bht(heads(mq, ML_HEADS)), bht(mk), bht(heads(mv, ML_HEADS)), bht(ig), bht(lf)])
    h = rms_norm(jnp.swapaxes(h, 1, 2), p['mlstm_norm_w'].reshape(ML_HEADS, HEAD_DIM)).reshape(B, T, ML_W)
    o_m = jax.nn.sigmoid(m_o.astype(F32)) * h

    mixed = jnp.concatenate([o_sb, o_g, o_m], axis=-1).astype(x.dtype) @ p['w_out']
    return mixed, sk, sv, new_conv, gdn_S, ml_state


def mem_kv(mem, w_ckv):
    B, N, _ = mem.shape
    k, v = jnp.split(mem @ w_ckv, 2, axis=-1)
    return k.reshape(B, N, MEM_HEADS, MEM_HEAD_DIM), v.reshape(B, N, MEM_HEADS, MEM_HEAD_DIM)


def mem_attend(x, k, v, w_cq, w_co):
    B, T, _ = x.shape
    q = (x @ w_cq).reshape(B, T, MEM_HEADS, MEM_HEAD_DIM)
    s = jnp.einsum('bqhd,bkhd->bhqk', q.astype(F32), k.astype(F32)) * MEM_HEAD_DIM ** -0.5
    a = jax.nn.softmax(s, axis=-1)
    o = jnp.einsum('bhqk,bkhd->bqhd', a, v.astype(F32)).reshape(B, T, D_MODEL)
    return o.astype(x.dtype) @ w_co


def swiglu(x, w_up, w_down):
    gate, up = jnp.split(x @ w_up, 2, axis=-1)
    return (jax.nn.silu(gate) * up) @ w_down


def trunk_layer(x, mk, mv, sb_fn, conv_ctx, gdn_S, ml_C, ml_n, ml_m, p):
    mixed, sk, sv, new_conv, gdn_S, (C, n, m) = token_mix(x, sb_fn, conv_ctx, gdn_S, ml_C, ml_n, ml_m, p)
    x = layer_norm(DEEPNORM_ALPHA * x + mixed, p['ln1_g'], p['ln1_b'])
    x = layer_norm(DEEPNORM_ALPHA * x + mem_attend(x, mk, mv, p['w_cq'], p['w_co']), p['ln2_g'], p['ln2_b'])
    x = layer_norm(DEEPNORM_ALPHA * x + swiglu(x, p['w_up'], p['w_down']), p['ln3_g'], p['ln3_b'])
    return x, sk, sv, new_conv, gdn_S, C, n, m


def setup_inputs(seed: int = 0) -> dict:
    key = jax.random.key(seed)
    keys = jax.random.split(key, 48)
    ctr = [0]

    def nxt():
        k = keys[ctr[0]]
        ctr[0] += 1
        return k

    def nrm(shape, scale):
        return jax.random.normal(nxt(), shape, F32) * scale

    d = D_MODEL
    x_prompt = nrm((BATCH, SEQ, d), 1.0)
    x_sample = nrm((DEC_BATCH, DEC_SEQ, d), 1.0)
    cache_sb_k = nrm((DEPTH, DEC_BATCH, PAST_LEN, SB_HEADS, HEAD_DIM), 1.0)
    cache_sb_v = nrm((DEPTH, DEC_BATCH, PAST_LEN, SB_HEADS, HEAD_DIM), 1.0)
    cache_gdn_conv = nrm((DEPTH, DEC_BATCH, CONV_W - 1, 3 * GDN_W), 1.0)
    state_gdn = nrm((DEPTH, DEC_BATCH, GDN_HEADS, HEAD_DIM, HEAD_DIM), 0.1)
    state_mlstm_C = nrm((DEPTH, DEC_BATCH, ML_HEADS, HEAD_DIM, HEAD_DIM), 0.1)
    state_mlstm_n = nrm((DEPTH, DEC_BATCH, ML_HEADS, HEAD_DIM), 0.1)
    state_mlstm_m = nrm((DEPTH, DEC_BATCH, ML_HEADS), 0.5)
    cache_mem_k = nrm((DEPTH, DEC_BATCH, N_MEM, MEM_HEADS, MEM_HEAD_DIM), 1.0)
    cache_mem_v = nrm((DEPTH, DEC_BATCH, N_MEM, MEM_HEADS, MEM_HEAD_DIM), 1.0)
    mem_prompt = nrm((BATCH, N_MEM, d), 1.0)
    w_in = nrm((DEPTH, d, IN_W), d ** -0.5)
    gdn_conv_w = nrm((DEPTH, CONV_W, 3 * GDN_W), CONV_W ** -0.5)
    gdn_A_log = jnp.log(jax.random.uniform(nxt(), (DEPTH, GDN_HEADS), F32, 1.0, 16.0))
    dt = jnp.exp(jax.random.uniform(nxt(), (DEPTH, GDN_HEADS), F32, float(np.log(1e-3)), float(np.log(1e-1))))
    gdn_dt_bias = dt + jnp.log(-jnp.expm1(-dt))
    gdn_norm_w = 1.0 + nrm((DEPTH, HEAD_DIM), 0.02)
    mlstm_i_bias = nrm((DEPTH, ML_HEADS), 0.5)
    mlstm_f_bias = 3.0 + 3.0 * jax.random.uniform(nxt(), (DEPTH, ML_HEADS), F32)
    mlstm_norm_w = 1.0 + nrm((DEPTH, ML_W), 0.02)
    w_out = nrm((DEPTH, MIX_W, d), MIX_W ** -0.5 * DEEPNORM_BETA)
    ln1_g = 1.0 + nrm((DEPTH, d), 0.02)
    ln1_b = nrm((DEPTH, d), 0.02)
    w_cq = nrm((DEPTH, d, d), d ** -0.5)
    w_ckv = nrm((DEPTH, d, 2 * d), d ** -0.5)
    w_co = nrm((DEPTH, d, d), d ** -0.5 * DEEPNORM_BETA)
    ln2_g = 1.0 + nrm((DEPTH, d), 0.02)
    ln2_b = nrm((DEPTH, d), 0.02)
    w_up = nrm((DEPTH, d, 2 * D_FF), d ** -0.5)
    w_down = nrm((DEPTH, D_FF, d), D_FF ** -0.5 * DEEPNORM_BETA)
    ln3_g = 1.0 + nrm((DEPTH, d), 0.02)
    ln3_b = nrm((DEPTH, d), 0.02)
    return {'x_prompt': x_prompt, 'x_sample': x_sample,
            'cache_sb_k': cache_sb_k, 'cache_sb_v': cache_sb_v, 'cache_gdn_conv': cache_gdn_conv,
            'state_gdn': state_gdn, 'state_mlstm_C': state_mlstm_C, 'state_mlstm_n': state_mlstm_n,
            'state_mlstm_m': state_mlstm_m, 'cache_mem_k': cache_mem_k, 'cache_mem_v': cache_mem_v,
            'mem_prompt': mem_prompt, 'w_in': w_in, 'gdn_conv_w': gdn_conv_w, 'gdn_A_log': gdn_A_log,
            'gdn_dt_bias': gdn_dt_bias, 'gdn_norm_w': gdn_norm_w, 'mlstm_i_bias': mlstm_i_bias,
            'mlstm_f_bias': mlstm_f_bias, 'mlstm_norm_w': mlstm_norm_w, 'w_out': w_out,
            'ln1_g': ln1_g, 'ln1_b': ln1_b, 'w_cq': w_cq, 'w_ckv': w_ckv, 'w_co': w_co,
            'ln2_g': ln2_g, 'ln2_b': ln2_b, 'w_up': w_up, 'w_down': w_down, 'ln3_g': ln3_g, 'ln3_b': ln3_b}


def reference(x_prompt, x_sample, cache_sb_k, cache_sb_v, cache_gdn_conv, state_gdn, state_mlstm_C,
              state_mlstm_n, state_mlstm_m, cache_mem_k, cache_mem_v, mem_prompt, w_in, gdn_conv_w,
              gdn_A_log, gdn_dt_bias, gdn_norm_w, mlstm_i_bias, mlstm_f_bias, mlstm_norm_w, w_out,
              ln1_g, ln1_b, w_cq, w_ckv, w_co, ln2_g, ln2_b, w_up, w_down, ln3_g, ln3_b):
    xp, xs = x_prompt, x_sample
    Bp = xp.shape[0]
    P = [[] for _ in range(9)]
    Q = [[] for _ in range(7)]
    for l in range(DEPTH):
        p = {'w_in': w_in[l], 'gdn_conv_w': gdn_conv_w[l], 'gdn_A_log': gdn_A_log[l], 'gdn_dt_bias': gdn_dt_bias[l],
             'gdn_norm_w': gdn_norm_w[l], 'mlstm_i_bias': mlstm_i_bias[l], 'mlstm_f_bias': mlstm_f_bias[l],
             'mlstm_norm_w': mlstm_norm_w[l], 'w_out': w_out[l], 'ln1_g': ln1_g[l], 'ln1_b': ln1_b[l],
             'w_cq': w_cq[l], 'w_co': w_co[l], 'ln2_g': ln2_g[l], 'ln2_b': ln2_b[l],
             'w_up': w_up[l], 'w_down': w_down[l], 'ln3_g': ln3_g[l], 'ln3_b': ln3_b[l]}
        mk, mv = mem_kv(mem_prompt, w_ckv[l])
        res = trunk_layer(xp, mk, mv, sb_prompt,
                          jnp.zeros((Bp, CONV_W - 1, 3 * GDN_W), xp.dtype),
                          jnp.zeros((Bp, GDN_HEADS, HEAD_DIM, HEAD_DIM), F32),
                          jnp.zeros((Bp, ML_HEADS, HEAD_DIM, HEAD_DIM), F32),
                          jnp.zeros((Bp, ML_HEADS, HEAD_DIM), F32),
                          jnp.zeros((Bp, ML_HEADS), F32), p)
        xp = res[0]
        for j, a in enumerate(res[1:] + (mk, mv)):
            P[j].append(a)
        res = trunk_layer(xs, cache_mem_k[l], cache_mem_v[l],
                          functools.partial(sb_sample, cache_sb_k[l], cache_sb_v[l]),
                          cache_gdn_conv[l], state_gdn[l], state_mlstm_C[l], state_mlstm_n[l], state_mlstm_m[l], p)
        xs = res[0]
        for j, a in enumerate(res[1:]):
            Q[j].append(a)
    P = [jnp.stack(a) for a in P]
    Q = [jnp.stack(a) for a in Q]
    return (xp, xs, P[0], P[1], P[2], P[3], P[4], P[5], P[6], P[7], P[8],
            Q[0], Q[1], Q[2], Q[3], Q[4], Q[5], Q[6])
```

```python
import functools

import numpy as np
import jax
import jax.numpy as jnp
from jax import lax
from jax.experimental import pallas as pl
from jax.experimental.pallas import tpu as pltpu

F32 = jnp.float32
BF16 = jnp.bfloat16

HEAD_DIM = 64
SB_HEADS = 4
GDN_HEADS = 6
ML_HEADS = 6
REC_W = GDN_HEADS * HEAD_DIM
SB_W = SB_HEADS * HEAD_DIM
CONV_W = 4
CHUNK = 64
MEM_HEADS = 4
LN_EPS = 1e-5
NORM_EPS = 1e-6
DEPTH = 2
DEEPNORM_ALPHA = (2 * DEPTH) ** 0.25

LANES = 128
SUBLANES = 8
VMEM_LIMIT = 56 * 1024 * 1024

PROJ_W = 4096
COL_GQKV = 0
COL_MQKV = 1152
COL_GZ = 2304
COL_MO = 2688
COL_SB = 3072
COL_GATE = 3840
GATE_BETA, GATE_G, GATE_I, GATE_F = 0, 6, 12, 18


def _cparams(*sem):
    return pltpu.CompilerParams(dimension_semantics=sem, vmem_limit_bytes=VMEM_LIMIT)


def _dot(a, b):
    return jnp.dot(a.astype(BF16), b.astype(BF16), preferred_element_type=F32)


def _dot_nt(a, b):
    return lax.dot_general(a.astype(BF16), b.astype(BF16), (((1,), (1,)), ((), ())),
                           preferred_element_type=F32)


def _dot_tn(a, b):
    return lax.dot_general(a.astype(BF16), b.astype(BF16), (((0,), (0,)), ((), ())),
                           preferred_element_type=F32)


def _split(x, parts):
    out = []
    r = x
    for _ in range(parts - 1):
        p = r.astype(BF16)
        out.append(p)
        r = r - p.astype(F32)
    out.append(r.astype(BF16))
    return out


def _dot_exact_lhs(m, x, parts):
    acc = None
    for p in _split(x, parts):
        t = jnp.dot(m, p, preferred_element_type=F32)
        acc = t if acc is None else acc + t
    return acc


def _dot_exact_rhs(x, m, parts):
    acc = None
    for p in _split(x, parts):
        t = jnp.dot(p, m, preferred_element_type=F32)
        acc = t if acc is None else acc + t
    return acc


def _ones_where(mask, dtype):
    return jnp.where(mask, 1.0, 0.0).astype(dtype)


def _sigmoid(x):
    return 1.0 / (1.0 + jnp.exp(-x))


def _layer_norm(y, g, b):
    mu = jnp.mean(y, axis=-1, keepdims=True)
    d = y - mu
    var = jnp.mean(d * d, axis=-1, keepdims=True)
    return d * lax.rsqrt(var + LN_EPS) * g + b


def _mm_kernel(x_ref, w_ref, o_ref):
    o_ref[...] = jnp.dot(x_ref[...], w_ref[...], preferred_element_type=F32).astype(o_ref.dtype)


def _matmul(x, w, tm, tn, out_dtype=F32):
    n, k = x.shape
    m = w.shape[1]
    return pl.pallas_call(
        _mm_kernel,
        grid=(n // tm, m // tn),
        in_specs=[pl.BlockSpec((tm, k), lambda i, j: (i, 0)),
                  pl.BlockSpec((k, tn), lambda i, j: (0, j))],
        out_specs=pl.BlockSpec((tm, tn), lambda i, j: (i, j)),
        out_shape=jax.ShapeDtypeStruct((n, m), out_dtype),
        compiler_params=_cparams("parallel", "parallel"),
        name="proj_matmul",
    )(x, w)


def _gdn_pre_kernel(x_ref, ctx_ref, w_ref, e_ref, o_ref, xbuf, *, tt):
    i = pl.program_id(1)

    @pl.when(i == 0)
    def _():
        xbuf[0:SUBLANES, :] = ctx_ref[0]

    xbuf[SUBLANES:SUBLANES + tt, :] = x_ref[...]
    acc = w_ref[CONV_W - 1:CONV_W, :] * xbuf[SUBLANES:SUBLANES + tt, :]
    for j in range(CONV_W - 1):
        off = SUBLANES - (CONV_W - 1) + j
        acc = acc + w_ref[j:j + 1, :] * xbuf[off:off + tt, :]
    y = acc * _sigmoid(acc)
    q = y[:, 0:REC_W]
    k = y[:, REC_W:2 * REC_W]
    e = e_ref[...]
    qs = _dot_exact_rhs(q * q, e, 2)
    ks = _dot_exact_rhs(k * k, e, 2)
    o_ref[:, 0:REC_W] = q * lax.rsqrt(qs + NORM_EPS) * (HEAD_DIM ** -0.5)
    o_ref[:, REC_W:2 * REC_W] = k * lax.rsqrt(ks + NORM_EPS)
    o_ref[:, 2 * REC_W:3 * REC_W] = y[:, 2 * REC_W:3 * REC_W]
    xbuf[0:SUBLANES, :] = xbuf[tt:tt + SUBLANES, :]


def _gdn_pre(proj, ctx8, conv_w8, e_heads, b, t, tt):
    w3 = 3 * REC_W
    nt = t // tt
    return pl.pallas_call(
        functools.partial(_gdn_pre_kernel, tt=tt),
        grid=(b, nt),
        in_specs=[pl.BlockSpec((tt, w3), lambda bi, i: (bi * nt + i, COL_GQKV // w3)),
                  pl.BlockSpec((1, SUBLANES, w3), lambda bi, i: (bi, 0, 0)),
                  pl.BlockSpec((SUBLANES, w3), lambda bi, i: (0, 0)),
                  pl.BlockSpec((REC_W, REC_W), lambda bi, i: (0, 0))],
        out_specs=pl.BlockSpec((tt, w3), lambda bi, i: (bi * nt + i, 0)),
        out_shape=jax.ShapeDtypeStruct((b * t, w3), F32),
        scratch_shapes=[pltpu.VMEM((tt + SUBLANES, w3), F32)],
        compiler_params=_cparams("parallel", "arbitrary"),
        name="gdn_pre",
    )(proj, ctx8, conv_w8, e_heads)


def _sb_kernel(q_ref, k_ref, vt_ref, o_ref, *, tq, tk, nkb, q_off):
    qi = pl.program_id(1)
    q = q_ref[0] * (HEAD_DIM ** -0.5)
    q_lo = q_off + qi * tq
    nblk = jnp.minimum(nkb, (q_lo + tq - 1 + tk - 1) // tk)
    qpos = q_lo + lax.broadcasted_iota(jnp.int32, (tk, tq), 1)
    krow = lax.broadcasted_iota(jnp.int32, (tk, tq), 0)
    upper = _ones_where(lax.broadcasted_iota(jnp.int32, (tk, tk), 0)
                        < lax.broadcasted_iota(jnp.int32, (tk, tk), 1), BF16)

    def body(i, carry):
        acc, c = carry
        kb = nblk - 1 - i
        z = _dot_nt(k_ref[0, kb], q)
        valid = (kb * tk + krow) < qpos
        sp = jnp.log1p(jnp.exp(-jnp.abs(z)))
        log_beta = jnp.minimum(z, 0.0) - sp
        log_1m = jnp.where(valid, log_beta - z, 0.0)
        after = _dot_exact_lhs(upper, log_1m, 2)
        a = jnp.where(valid, jnp.exp(log_beta + after + c), 0.0)
        acc = acc + jnp.dot(vt_ref[0, kb], a.astype(BF16), preferred_element_type=F32)
        c = c + after[0:1, :] + log_1m[0:1, :]
        return acc, c

    acc, _ = lax.fori_loop(0, nblk, body,
                           (jnp.zeros((HEAD_DIM, tq), F32), jnp.zeros((1, tq), F32)))
    o_ref[0] = acc


def _sb_attn(qh, kh, vth, tq, q_off):
    g, tq_total, d = qh.shape
    _, nkb, tk, _ = kh.shape
    return pl.pallas_call(
        functools.partial(_sb_kernel, tq=tq, tk=tk, nkb=nkb, q_off=q_off),
        grid=(g, tq_total // tq),
        in_specs=[pl.BlockSpec((1, tq, d), lambda gi, i: (gi, i, 0)),
                  pl.BlockSpec((1, nkb, tk, d), lambda gi, i: (gi, 0, 0, 0)),
                  pl.BlockSpec((1, nkb, d, tk), lambda gi, i: (gi, 0, 0, 0))],
        out_specs=pl.BlockSpec((1, d, tq), lambda gi, i: (gi, 0, i)),
        out_shape=jax.ShapeDtypeStruct((g, d, tq_total), F32),
        compiler_params=_cparams("parallel", "parallel"),
        name="sb_attn",
    )(qh, kh, vth)


def _gate_values(pre, neg_a, lane_id):
    sp = jnp.log1p(jnp.exp(-jnp.abs(pre)))
    softplus = jnp.maximum(pre, 0.0) + sp
    log_sig = jnp.minimum(pre, 0.0) - sp
    return jnp.where(lane_id < GATE_G, _sigmoid(pre),
                     jnp.where(lane_id < GATE_I, neg_a * softplus,
                               jnp.where(lane_id < GATE_F, pre, log_sig)))


def _gates(gt_ref, gtt_ref, prow_ref, pcol_ref, l):
    lane = lax.broadcasted_iota(jnp.int32, (l, LANES), 1)
    val = _gate_values(gt_ref[...] + prow_ref[0:1, :], -jnp.exp(prow_ref[1:2, :]), lane)
    sub = lax.broadcasted_iota(jnp.int32, (LANES, l), 0)
    val_t = _gate_values(gtt_ref[0] + pcol_ref[:, 0:1], -jnp.exp(pcol_ref[:, 1:2]), sub)
    r = lax.broadcasted_iota(jnp.int32, (l, l), 0)
    c = lax.broadcasted_iota(jnp.int32, (l, l), 1)
    csum = _dot_exact_lhs(_ones_where(r >= c, BF16), val, 3)
    csum_t = _dot_exact_rhs(val_t, _ones_where(r <= c, BF16), 3)
    return val, csum, val_t, csum_t


def _gdn_scan_kernel(qkv_ref, gt_ref, gtt_ref, prow_ref, pcol_ref, s0_ref, o_ref, s_out_ref, s_scr,
                     *, l, nc):
    ci = pl.program_id(1)

    @pl.when(ci == 0)
    def _():
        s_scr[...] = s0_ref[0]

    val, csum, _, csum_t = _gates(gt_ref, gtt_ref, prow_ref, pcol_ref, l)
    r = lax.broadcasted_iota(jnp.int32, (l, l), 0)
    c = lax.broadcasted_iota(jnp.int32, (l, l), 1)
    tri = r >= c
    stri = r > c
    eye = _ones_where(r == c, F32)
    n_double = max(int(np.ceil(np.log2(l))) - 1, 0)
    for h in range(GDN_HEADS):
        q = qkv_ref[:, h * HEAD_DIM:(h + 1) * HEAD_DIM]
        k = qkv_ref[:, REC_W + h * HEAD_DIM:REC_W + (h + 1) * HEAD_DIM]
        v = qkv_ref[:, 2 * REC_W + h * HEAD_DIM:2 * REC_W + (h + 1) * HEAD_DIM]
        beta = val[:, GATE_BETA + h:GATE_BETA + h + 1]
        g_col = csum[:, GATE_G + h:GATE_G + h + 1]
        g_row = csum_t[GATE_G + h:GATE_G + h + 1, :]
        g_last = g_col[l - 1:l, :]
        decay = jnp.where(tri, jnp.exp(jnp.where(tri, g_col - g_row, 0.0)), 0.0)
        kb = k * beta
        a = jnp.where(stri, _dot_nt(kb, k) * decay, 0.0)
        p = -a
        tinv = eye + p
        for _ in range(n_double):
            p = _dot(p, p)
            tinv = tinv + _dot(tinv, p)
        e_g = jnp.exp(g_col)
        s = s_scr[h]
        sol_v = _dot(tinv, v * beta)
        sol_k = _dot(tinv, kb * e_g)
        u = sol_v - _dot(sol_k, s)
        attn = _dot_nt(q, k) * decay
        o = _dot(q * e_g, s) + _dot(attn, u)
        s_scr[h] = jnp.exp(g_last) * s + _dot_tn(k * jnp.exp(g_last - g_col), u)
        o_ref[:, h * HEAD_DIM:(h + 1) * HEAD_DIM] = o

    @pl.when(ci == nc - 1)
    def _():
        s_out_ref[0] = s_scr[...]


def _gdn_scan(qkv, proj, gate_t, prow, pcol, s0, b, t, l):
    nc = t // l
    w3 = 3 * REC_W
    return pl.pallas_call(
        functools.partial(_gdn_scan_kernel, l=l, nc=nc),
        grid=(b, nc),
        in_specs=[pl.BlockSpec((l, w3), lambda bi, i: (bi * nc + i, 0)),
                  pl.BlockSpec((l, LANES), lambda bi, i: (bi * nc + i, COL_GATE // LANES)),
                  pl.BlockSpec((1, LANES, l), lambda bi, i: (bi * nc + i, 0, 0)),
                  pl.BlockSpec((SUBLANES, LANES), lambda bi, i: (0, 0)),
                  pl.BlockSpec((LANES, SUBLANES), lambda bi, i: (0, 0)),
                  pl.BlockSpec((1, GDN_HEADS, HEAD_DIM, HEAD_DIM), lambda bi, i: (bi, 0, 0, 0))],
        out_specs=[pl.BlockSpec((l, REC_W), lambda bi, i: (bi * nc + i, 0)),
                   pl.BlockSpec((1, GDN_HEADS, HEAD_DIM, HEAD_DIM), lambda bi, i: (bi, 0, 0, 0))],
        out_shape=[jax.ShapeDtypeStruct((b * t, REC_W), F32),
                   jax.ShapeDtypeStruct((b, GDN_HEADS, HEAD_DIM, HEAD_DIM), F32)],
        scratch_shapes=[pltpu.VMEM((GDN_HEADS, HEAD_DIM, HEAD_DIM), F32)],
        compiler_params=_cparams("parallel", "arbitrary"),
        name="gdn_scan",
    )(qkv, proj, gate_t, prow, pcol, s0)


def _ml_scan_kernel(qkv_ref, gt_ref, gtt_ref, prow_ref, pcol_ref, c0_ref, n0_ref, m0_ref,
                    o_ref, c_out_ref, n_out_ref, m_out_ref, c_scr, n_scr, m_scr, *, l, nc):
    ci = pl.program_id(1)

    @pl.when(ci == 0)
    def _():
        c_scr[...] = c0_ref[0]
        n_scr[...] = n0_ref[0]
        m_scr[...] = m0_ref[0]

    val, csum, val_t, csum_t = _gates(gt_ref, gtt_ref, prow_ref, pcol_ref, l)
    r = lax.broadcasted_iota(jnp.int32, (l, l), 0)
    c = lax.broadcasted_iota(jnp.int32, (l, l), 1)
    tri = r >= c
    lane = lax.broadcasted_iota(jnp.int32, (1, LANES), 1)
    m_all = m_scr[...]
    m_next = m_all
    for h in range(ML_HEADS):
        q = qkv_ref[:, h * HEAD_DIM:(h + 1) * HEAD_DIM]
        k = qkv_ref[:, REC_W + h * HEAD_DIM:REC_W + (h + 1) * HEAD_DIM] * (HEAD_DIM ** -0.5)
        v = qkv_ref[:, 2 * REC_W + h * HEAD_DIM:2 * REC_W + (h + 1) * HEAD_DIM]
        ig_col = val[:, GATE_I + h:GATE_I + h + 1]
        ig_row = val_t[GATE_I + h:GATE_I + h + 1, :]
        f_col = csum[:, GATE_F + h:GATE_F + h + 1]
        f_row = csum_t[GATE_F + h:GATE_F + h + 1, :]
        f_last = f_col[l - 1:l, :]
        m_prev = m_all[:, h:h + 1]
        d = jnp.where(tri, f_col - f_row + ig_row, -jnp.inf)
        inter = f_col + m_prev
        m_t = jnp.maximum(inter, jnp.max(d, axis=1, keepdims=True))
        w = jnp.exp(d - m_t) * _dot_nt(q, k)
        c_inter = jnp.exp(inter - m_t)
        c_h = c_scr[h]
        n_h = n_scr[h:h + 1, :]
        num = c_inter * _dot(q, c_h) + _dot(w, v)
        den = c_inter * jnp.sum(q * n_h, axis=1, keepdims=True) + jnp.sum(w, axis=1, keepdims=True)
        o_ref[:, h * HEAD_DIM:(h + 1) * HEAD_DIM] = num / jnp.maximum(jnp.abs(den), jnp.exp(-m_t))
        m_new = m_t[l - 1:l, :]
        w_end = jnp.exp(f_last - f_col + ig_col - m_new)
        c_prev = jnp.exp(f_last + m_prev - m_new)
        wk = w_end * k
        c_scr[h] = c_prev * c_h + _dot_tn(wk, v)
        n_scr[h:h + 1, :] = c_prev * n_h + jnp.sum(wk, axis=0, keepdims=True)
        m_next = jnp.where(lane == h, m_new, m_next)
    m_scr[...] = m_next

    @pl.when(ci == nc - 1)
    def _():
        c_out_ref[0] = c_scr[...]
        n_out_ref[0] = n_scr[...]
        m_out_ref[0] = m_scr[...]


def _ml_scan(proj, gate_t, prow, pcol, c0, n0, m0, b, t, l):
    nc = t // l
    w3 = 3 * REC_W
    st = lambda bi, i: (bi, 0, 0, 0)
    st3 = lambda bi, i: (bi, 0, 0)
    return pl.pallas_call(
        functools.partial(_ml_scan_kernel, l=l, nc=nc),
        grid=(b, nc),
        in_specs=[pl.BlockSpec((l, w3), lambda bi, i: (bi * nc + i, COL_MQKV // w3)),
                  pl.BlockSpec((l, LANES), lambda bi, i: (bi * nc + i, COL_GATE // LANES)),
                  pl.BlockSpec((1, LANES, l), lambda bi, i: (bi * nc + i, 0, 0)),
                  pl.BlockSpec((SUBLANES, LANES), lambda bi, i: (0, 0)),
                  pl.BlockSpec((LANES, SUBLANES), lambda bi, i: (0, 0)),
                  pl.BlockSpec((1, ML_HEADS, HEAD_DIM, HEAD_DIM), st),
                  pl.BlockSpec((1, SUBLANES, HEAD_DIM), st3),
                  pl.BlockSpec((1, 1, LANES), st3)],
        out_specs=[pl.BlockSpec((l, REC_W), lambda bi, i: (bi * nc + i, 0)),
                   pl.BlockSpec((1, ML_HEADS, HEAD_DIM, HEAD_DIM), st),
                   pl.BlockSpec((1, SUBLANES, HEAD_DIM), st3),
                   pl.BlockSpec((1, 1, LANES), st3)],
        out_shape=[jax.ShapeDtypeStruct((b * t, REC_W), F32),
                   jax.ShapeDtypeStruct((b, ML_HEADS, HEAD_DIM, HEAD_DIM), F32),
                   jax.ShapeDtypeStruct((b, SUBLANES, HEAD_DIM), F32),
                   jax.ShapeDtypeStruct((b, 1, LANES), F32)],
        scratch_shapes=[pltpu.VMEM((ML_HEADS, HEAD_DIM, HEAD_DIM), F32),
                        pltpu.VMEM((SUBLANES, HEAD_DIM), F32),
                        pltpu.VMEM((1, LANES), F32)],
        compiler_params=_cparams("parallel", "arbitrary"),
        name="ml_scan",
    )(proj, proj, gate_t, prow, pcol, c0, n0, m0)


def _post_kernel(x_ref, osb_ref, og_ref, hm_ref, gz_ref, mo_ref, w_ref, e_ref, gnw_ref, mnw_ref,
                 lg_ref, lb_ref, o_ref, ob_ref):
    e = e_ref[...]
    og = og_ref[...]
    hm = hm_ref[...]
    og_ms = _dot_exact_rhs(og * og, e, 2) * (1.0 / HEAD_DIM)
    hm_ms = _dot_exact_rhs(hm * hm, e, 2) * (1.0 / HEAD_DIM)
    gz = gz_ref[...]
    o_g = og * lax.rsqrt(og_ms + NORM_EPS) * gnw_ref[...] * (gz * _sigmoid(gz))
    o_m = _sigmoid(mo_ref[...]) * (hm * lax.rsqrt(hm_ms + NORM_EPS) * mnw_ref[...])
    mixed = (_dot(osb_ref[...], w_ref[0:SB_W, :])
             + _dot(o_g, w_ref[SB_W:SB_W + REC_W, :])
             + _dot(o_m, w_ref[SB_W + REC_W:SB_W + 2 * REC_W, :]))
    y = _layer_norm(DEEPNORM_ALPHA * x_ref[...] + mixed, lg_ref[...], lb_ref[...])
    o_ref[...] = y
    ob_ref[...] = y.astype(BF16)


def _post(x, osb, og, hm, proj, w_out, e_heads, gnw, mnw, lg, lb, tm):
    n, d = x.shape
    row = lambda i: (i, 0)
    fixed = lambda i: (0, 0)
    return pl.pallas_call(
        _post_kernel,
        grid=(n // tm,),
        in_specs=[pl.BlockSpec((tm, d), row),
                  pl.BlockSpec((tm, SB_W), row),
                  pl.BlockSpec((tm, REC_W), row),
                  pl.BlockSpec((tm, REC_W), row),
                  pl.BlockSpec((tm, REC_W), lambda i: (i, COL_GZ // REC_W)),
                  pl.BlockSpec((tm, REC_W), lambda i: (i, COL_MO // REC_W)),
                  pl.BlockSpec(w_out.shape, fixed),
                  pl.BlockSpec((REC_W, REC_W), fixed),
                  pl.BlockSpec((1, REC_W), fixed),
                  pl.BlockSpec((1, REC_W), fixed),
                  pl.BlockSpec((1, d), fixed),
                  pl.BlockSpec((1, d), fixed)],
        out_specs=[pl.BlockSpec((tm, d), row), pl.BlockSpec((tm, d), row)],
        out_shape=[jax.ShapeDtypeStruct((n, d), F32), jax.ShapeDtypeStruct((n, d), BF16)],
        compiler_params=_cparams("parallel"),
        name="post",
    )(x, osb, og, hm, proj, proj, w_out, e_heads, gnw, mnw, lg, lb)


def _memattn_kernel(x_ref, xb_ref, wq_ref, wo_ref, mk_ref, mv_ref, lg_ref, lb_ref, o_ref, ob_ref):
    d = x_ref.shape[1]
    hd = d // MEM_HEADS
    q = jnp.dot(xb_ref[...], wq_ref[...], preferred_element_type=F32)
    out = None
    for h in range(MEM_HEADS):
        sl = slice(h * hd, (h + 1) * hd)
        s = _dot_nt(q[:, sl], mk_ref[0, :, sl]) * (hd ** -0.5)
        p = jnp.exp(s - jnp.max(s, axis=1, keepdims=True))
        o_h = _dot(p, mv_ref[0, :, sl]) / jnp.sum(p, axis=1, keepdims=True)
        t = _dot(o_h, wo_ref[sl, :])
        out = t if out is None else out + t
    y = _layer_norm(DEEPNORM_ALPHA * x_ref[...] + out, lg_ref[...], lb_ref[...])
    o_ref[...] = y
    ob_ref[...] = y.astype(BF16)


def _memattn(x, xb, w_cq, w_co, mk, mv, lg, lb, b, t, tm):
    n, d = x.shape
    nt = t // tm
    nm = mk.shape[1]
    row = lambda bi, i: (bi * nt + i, 0)
    fixed = lambda bi, i: (0, 0)
    return pl.pallas_call(
        _memattn_kernel,
        grid=(b, nt),
        in_specs=[pl.BlockSpec((tm, d), row),
                  pl.BlockSpec((tm, d), row),
                  pl.BlockSpec((d, d), fixed),
                  pl.BlockSpec((d, d), fixed),
                  pl.BlockSpec((1, nm, d), lambda bi, i: (bi, 0, 0)),
                  pl.BlockSpec((1, nm, d), lambda bi, i: (bi, 0, 0)),
                  pl.BlockSpec((1, d), fixed),
                  pl.BlockSpec((1, d), fixed)],
        out_specs=[pl.BlockSpec((tm, d), row), pl.BlockSpec((tm, d), row)],
        out_shape=[jax.ShapeDtypeStruct((n, d), F32), jax.ShapeDtypeStruct((n, d), BF16)],
        compiler_params=_cparams("parallel", "parallel"),
        name="memattn",
    )(x, xb, w_cq, w_co, mk, mv, lg, lb)


def _ffn_kernel(x_ref, xb_ref, wg_ref, wu_ref, wd_ref, lg_ref, lb_ref, o_ref, ob_ref, acc_ref, *, nf):
    j = pl.program_id(1)

    @pl.when(j == 0)
    def _():
        acc_ref[...] = jnp.zeros_like(acc_ref)

    xb = xb_ref[...]
    gate = jnp.dot(xb, wg_ref[...], preferred_element_type=F32)
    up = jnp.dot(xb, wu_ref[...], preferred_element_type=F32)
    acc_ref[...] += _dot(gate * _sigmoid(gate) * up, wd_ref[...])

    @pl.when(j == nf - 1)
    def _():
        y = _layer_norm(DEEPNORM_ALPHA * x_ref[...] + acc_ref[...], lg_ref[...], lb_ref[...])
        o_ref[...] = y
        ob_ref[...] = y.astype(BF16)


def _ffn(x, xb, w_up, w_down, lg, lb, tm, tf):
    n, d = x.shape
    dff = w_down.shape[0]
    nf = dff // tf
    row = lambda i, j: (i, 0)
    fixed = lambda i, j: (0, 0)
    return pl.pallas_call(
        functools.partial(_ffn_kernel, nf=nf),
        grid=(n // tm, nf),
        in_specs=[pl.BlockSpec((tm, d), row),
                  pl.BlockSpec((tm, d), row),
                  pl.BlockSpec((d, tf), lambda i, j: (0, j)),
                  pl.BlockSpec((d, tf), lambda i, j: (0, nf + j)),
                  pl.BlockSpec((tf, d), lambda i, j: (j, 0)),
                  pl.BlockSpec((1, d), fixed),
                  pl.BlockSpec((1, d), fixed)],
        out_specs=[pl.BlockSpec((tm, d), row), pl.BlockSpec((tm, d), row)],
        out_shape=[jax.ShapeDtypeStruct((n, d), F32), jax.ShapeDtypeStruct((n, d), BF16)],
        scratch_shapes=[pltpu.VMEM((tm, d), F32)],
        compiler_params=_cparams("parallel", "arbitrary"),
        name="ffn",
    )(x, xb, w_up, w_up, w_down, lg, lb)


def _pack_layer(p):
    w_in = p['w_in']
    d = w_in.shape[0]
    o_sb, o_gqkv, o_gz = 0, 3 * SB_W, 3 * SB_W + 3 * REC_W
    o_gb = o_gz + REC_W
    o_mqkv = o_gb + 2 * GDN_HEADS
    o_mo = o_mqkv + 3 * REC_W
    o_mi = o_mo + REC_W
    gate_cols = jnp.concatenate([w_in[:, o_gb:o_gb + 2 * GDN_HEADS], w_in[:, o_mi:o_mi + 2 * ML_HEADS]], axis=1)
    used = COL_GATE + gate_cols.shape[1]
    w_al = jnp.concatenate([w_in[:, o_gqkv:o_gqkv + 3 * REC_W], w_in[:, o_mqkv:o_mqkv + 3 * REC_W],
                            w_in[:, o_gz:o_gz + REC_W], w_in[:, o_mo:o_mo + REC_W],
                            w_in[:, o_sb:o_sb + 3 * SB_W], gate_cols,
                            jnp.zeros((d, PROJ_W - used), w_in.dtype)], axis=1).astype(BF16)
    prow = jnp.zeros((SUBLANES, LANES), F32)
    prow = prow.at[0, GATE_G:GATE_G + GDN_HEADS].set(p['gdn_dt_bias'])
    prow = prow.at[0, GATE_I:GATE_I + ML_HEADS].set(p['mlstm_i_bias'])
    prow = prow.at[0, GATE_F:GATE_F + ML_HEADS].set(p['mlstm_f_bias'])
    prow = prow.at[1, GATE_G:GATE_G + GDN_HEADS].set(p['gdn_A_log'])
    conv_w8 = jnp.zeros((SUBLANES, 3 * REC_W), F32).at[0:CONV_W].set(p['gdn_conv_w'])
    return dict(
        w_al=w_al, prow=prow, pcol=prow.T, conv_w8=conv_w8,
        gnw=jnp.tile(p['gdn_norm_w'], GDN_HEADS)[None, :], mnw=p['mlstm_norm_w'][None, :],
        w_out=p['w_out'].astype(BF16), w_cq=p['w_cq'].astype(BF16), w_co=p['w_co'].astype(BF16),
        w_up=p['w_up'].astype(BF16), w_down=p['w_down'].astype(BF16),
        ln1=(p['ln1_g'][None, :], p['ln1_b'][None, :]), ln2=(p['ln2_g'][None, :], p['ln2_b'][None, :]),
        ln3=(p['ln3_g'][None, :], p['ln3_b'][None, :]))


def _head_consts():
    hid = np.arange(REC_W) // HEAD_DIM
    return jnp.asarray(hid[:, None] == hid[None, :], BF16)


def _sb_layout(q, k, v, tk):
    b, tq, _ = q.shape
    tkv = k.shape[1]
    nkb = tkv // tk
    qh = q.reshape(b, tq, SB_HEADS, HEAD_DIM).transpose(0, 2, 1, 3).reshape(b * SB_HEADS, tq, HEAD_DIM)
    kh = k.reshape(b, nkb, tk, SB_HEADS, HEAD_DIM).transpose(0, 3, 1, 2, 4)
    kh = kh.reshape(b * SB_HEADS, nkb, tk, HEAD_DIM)
    vth = v.reshape(b, nkb, tk, SB_HEADS, HEAD_DIM).transpose(0, 3, 1, 4, 2)
    vth = vth.reshape(b * SB_HEADS, nkb, HEAD_DIM, tk)
    return qh.astype(BF16), kh.astype(BF16), vth.astype(BF16)


def _trunk_layer(x, xb, pk, e_heads, mk, mv, b, t, cfg, sb_past, conv_ctx, gdn_s, ml_c, ml_n, ml_m):
    n, d = x.shape
    l = min(t, CHUNK)
    nc = t // l
    proj = _matmul(xb, pk['w_al'], cfg['tm'], 1024)
    sq = proj[:, COL_SB:COL_SB + SB_W].reshape(b, t, SB_W)
    sk = proj[:, COL_SB + SB_W:COL_SB + 2 * SB_W].reshape(b, t, SB_W)
    sv = proj[:, COL_SB + 2 * SB_W:COL_SB + 3 * SB_W].reshape(b, t, SB_W)
    g_qkv = proj[:, COL_GQKV:COL_GQKV + 3 * REC_W].reshape(b, t, 3 * REC_W)

    tk = cfg['sb_tk']
    if sb_past is None:
        q_off, kk, vv, qq = 0, sk, sv, sq
    else:
        past_k, past_v = sb_past
        q_off = past_k.shape[1]
        kv_len = q_off + t
        kv_pad = -(-kv_len // tk) * tk - kv_len
        kk = jnp.pad(jnp.concatenate([past_k.reshape(b, q_off, SB_W), sk], axis=1), ((0, 0), (0, kv_pad), (0, 0)))
        vv = jnp.pad(jnp.concatenate([past_v.reshape(b, q_off, SB_W), sv], axis=1), ((0, 0), (0, kv_pad), (0, 0)))
        qq = jnp.pad(sq, ((0, 0), (0, cfg['sb_tq'] - t), (0, 0)))
    qh, kh, vth = _sb_layout(qq, kk, vv, tk)
    o_sb_t = _sb_attn(qh, kh, vth, cfg['sb_tq'], q_off)
    o_sb = o_sb_t.reshape(b, SB_HEADS, HEAD_DIM, -1)[:, :, :, :t].transpose(0, 3, 1, 2).reshape(n, SB_W)

    ctx8 = jnp.pad(conv_ctx, ((0, 0), (SUBLANES - (CONV_W - 1), 0), (0, 0)))
    gqkv_act = _gdn_pre(proj, ctx8, pk['conv_w8'], e_heads, b, t, cfg['conv_tt'])
    gate_t = proj[:, COL_GATE:COL_GATE + LANES].reshape(b * nc, l, LANES).transpose(0, 2, 1)
    o_g, gdn_s_new = _gdn_scan(gqkv_act, proj, gate_t, pk['prow'], pk['pcol'], gdn_s, b, t, l)
    n0 = jnp.pad(ml_n, ((0, 0), (0, SUBLANES - ML_HEADS), (0, 0)))
    m0 = jnp.pad(ml_m, ((0, 0), (0, LANES - ML_HEADS)))[:, None, :]
    h_m, c_new, n_new, m_new = _ml_scan(proj, gate_t, pk['prow'], pk['pcol'], ml_c, n0, m0, b, t, l)

    x1, x1b = _post(x, o_sb, o_g, h_m, proj, pk['w_out'], e_heads, pk['gnw'], pk['mnw'], *pk['ln1'], cfg['tm'])
    x2, x2b = _memattn(x1, x1b, pk['w_cq'], pk['w_co'], mk, mv, *pk['ln2'], b, t, cfg['mem_tm'])
    x3, x3b = _ffn(x2, x2b, pk['w_up'], pk['w_down'], *pk['ln3'], cfg['tm'], cfg['ffn_tf'])

    new_conv = g_qkv[:, t - (CONV_W - 1):, :]
    return (x3, x3b, sk.reshape(b, t, SB_HEADS, HEAD_DIM), sv.reshape(b, t, SB_HEADS, HEAD_DIM), new_conv,
            gdn_s_new, c_new, n_new[:, :ML_HEADS, :], m_new[:, 0, :ML_HEADS])


def kernel(x_prompt, x_sample, cache_sb_k, cache_sb_v, cache_gdn_conv, state_gdn, state_mlstm_C, state_mlstm_n, state_mlstm_m, cache_mem_k, cache_mem_v, mem_prompt, w_in, gdn_conv_w, gdn_A_log, gdn_dt_bias, gdn_norm_w, mlstm_i_bias, mlstm_f_bias, mlstm_norm_w, w_out, ln1_g, ln1_b, w_cq, w_ckv, w_co, ln2_g, ln2_b, w_up, w_down, ln3_g, ln3_b):
    bp, tp, d = x_prompt.shape
    bs, ts, _ = x_sample.shape
    depth = w_in.shape[0]
    n_mem = mem_prompt.shape[1]
    dff = w_down.shape[1]
    cfg_p = dict(tm=512, sb_tq=256, sb_tk=256, conv_tt=512, mem_tm=512, ffn_tf=dff // 2)
    cfg_s = dict(tm=bs * ts, sb_tq=LANES, sb_tk=256, conv_tt=ts, mem_tm=ts, ffn_tf=dff // 2)
    e_heads = _head_consts()

    xp = x_prompt.reshape(bp * tp, d)
    xs = x_sample.reshape(bs * ts, d)
    xpb, xsb = xp.astype(BF16), xs.astype(BF16)
    mem_b = mem_prompt.reshape(bp * n_mem, d).astype(BF16)
    p_out = [[] for _ in range(9)]
    s_out = [[] for _ in range(7)]
    for li in range(depth):
        p = {'w_in': w_in[li], 'gdn_conv_w': gdn_conv_w[li], 'gdn_A_log': gdn_A_log[li],
             'gdn_dt_bias': gdn_dt_bias[li], 'gdn_norm_w': gdn_norm_w[li], 'mlstm_i_bias': mlstm_i_bias[li],
             'mlstm_f_bias': mlstm_f_bias[li], 'mlstm_norm_w': mlstm_norm_w[li], 'w_out': w_out[li],
             'ln1_g': ln1_g[li], 'ln1_b': ln1_b[li], 'w_cq': w_cq[li], 'w_co': w_co[li],
             'ln2_g': ln2_g[li], 'ln2_b': ln2_b[li], 'w_up': w_up[li], 'w_down': w_down[li],
             'ln3_g': ln3_g[li], 'ln3_b': ln3_b[li]}
        pk = _pack_layer(p)
        mkv = _matmul(mem_b, w_ckv[li].astype(BF16), n_mem, 1024)
        mk_p = mkv[:, :d].reshape(bp, n_mem, d)
        mv_p = mkv[:, d:].reshape(bp, n_mem, d)
        res = _trunk_layer(xp, xpb, pk, e_heads, mk_p.astype(BF16), mv_p.astype(BF16), bp, tp, cfg_p, None,
                           jnp.zeros((bp, CONV_W - 1, 3 * REC_W), F32),
                           jnp.zeros((bp, GDN_HEADS, HEAD_DIM, HEAD_DIM), F32),
                           jnp.zeros((bp, ML_HEADS, HEAD_DIM, HEAD_DIM), F32),
                           jnp.zeros((bp, ML_HEADS, HEAD_DIM), F32),
                           jnp.zeros((bp, ML_HEADS), F32))
        xp, xpb = res[0], res[1]
        hd = d // MEM_HEADS
        for j, a in enumerate(res[2:] + (mk_p.reshape(bp, n_mem, MEM_HEADS, hd), mv_p.reshape(bp, n_mem, MEM_HEADS, hd))):
            p_out[j].append(a)
        res = _trunk_layer(xs, xsb, pk, e_heads,
                           cache_mem_k[li].reshape(bs, n_mem, d).astype(BF16),
                           cache_mem_v[li].reshape(bs, n_mem, d).astype(BF16),
                           bs, ts, cfg_s, (cache_sb_k[li], cache_sb_v[li]), cache_gdn_conv[li],
                           state_gdn[li], state_mlstm_C[li], state_mlstm_n[li], state_mlstm_m[li])
        xs, xsb = res[0], res[1]
        for j, a in enumerate(res[2:]):
            s_out[j].append(a)
    p_st = [jnp.stack(a) for a in p_out]
    s_st = [jnp.stack(a) for a in s_out]
    return (xp.reshape(bp, tp, d), xs.reshape(bs, ts, d), *p_st, *s_st)
```

```python
import functools

import numpy as np
import jax
import jax.numpy as jnp
from jax import lax
from jax.experimental import pallas as pl
from jax.experimental.pallas import tpu as pltpu

F32 = jnp.float32
BF16 = jnp.bfloat16

HEAD_DIM = 64
SB_HEADS = 4
GDN_HEADS = 6
ML_HEADS = 6
REC_W = GDN_HEADS * HEAD_DIM
SB_W = SB_HEADS * HEAD_DIM
CONV_W = 4
CHUNK = 64
MEM_HEADS = 4
LN_EPS = 1e-5
NORM_EPS = 1e-6
DEPTH = 2
DEEPNORM_ALPHA = (2 * DEPTH) ** 0.25
LOG2E = float(np.log2(np.e))

LANES = 128
SUBLANES = 8
VMEM_LIMIT = 56 * 1024 * 1024

PROJ_W = 4096
COL_GQKV = 0
COL_MQKV = 1152
COL_GZ = 2304
COL_MO = 2688
COL_SB = 3072
COL_GATE = 3840
GATE_BETA, GATE_G, GATE_I, GATE_F = 0, 6, 12, 18


def _cparams(*sem):
    return pltpu.CompilerParams(dimension_semantics=sem, vmem_limit_bytes=VMEM_LIMIT)


def _dot(a, b):
    return jnp.dot(a.astype(BF16), b.astype(BF16), preferred_element_type=F32)


def _dot_nt(a, b):
    return lax.dot_general(a.astype(BF16), b.astype(BF16), (((1,), (1,)), ((), ())),
                           preferred_element_type=F32)


def _dot_tn(a, b):
    return lax.dot_general(a.astype(BF16), b.astype(BF16), (((0,), (0,)), ((), ())),
                           preferred_element_type=F32)


def _split(x, parts):
    out = []
    r = x
    for _ in range(parts - 1):
        p = r.astype(BF16)
        out.append(p)
        r = r - p.astype(F32)
    out.append(r.astype(BF16))
    return out


def _dot_exact_lhs(m, x, parts):
    acc = None
    for p in _split(x, parts):
        t = jnp.dot(m, p, preferred_element_type=F32)
        acc = t if acc is None else acc + t
    return acc


def _dot_exact_rhs(x, m, parts):
    acc = None
    for p in _split(x, parts):
        t = jnp.dot(p, m, preferred_element_type=F32)
        acc = t if acc is None else acc + t
    return acc


def _ones_where(mask, dtype):
    return jnp.where(mask, 1.0, 0.0).astype(dtype)


def _sigmoid(x):
    return 1.0 / (1.0 + jnp.exp(-x))


def _layer_norm(y, g, b):
    mu = jnp.mean(y, axis=-1, keepdims=True)
    d = y - mu
    var = jnp.mean(d * d, axis=-1, keepdims=True)
    return d * lax.rsqrt(var + LN_EPS) * g + b


def _mm_kernel(x_ref, w_ref, o_ref):
    o_ref[...] = jnp.dot(x_ref[...], w_ref[...], preferred_element_type=F32).astype(o_ref.dtype)


def _matmul(x, w, tm, tn, out_dtype=F32):
    n, k = x.shape
    m = w.shape[1]
    return pl.pallas_call(
        _mm_kernel,
        grid=(n // tm, m // tn),
        in_specs=[pl.BlockSpec((tm, k), lambda i, j: (i, 0)),
                  pl.BlockSpec((k, tn), lambda i, j: (0, j))],
        out_specs=pl.BlockSpec((tm, tn), lambda i, j: (i, j)),
        out_shape=jax.ShapeDtypeStruct((n, m), out_dtype),
        compiler_params=_cparams("parallel", "parallel"),
        name="proj_matmul",
    )(x, w)


def _gdn_pre_kernel(x_ref, ctx_ref, w_ref, e_ref, o_ref, xbuf, *, tt):
    i = pl.program_id(1)

    @pl.when(i == 0)
    def _():
        xbuf[0:SUBLANES, :] = ctx_ref[0]

    xbuf[SUBLANES:SUBLANES + tt, :] = x_ref[...]
    acc = w_ref[CONV_W - 1:CONV_W, :] * xbuf[SUBLANES:SUBLANES + tt, :]
    for j in range(CONV_W - 1):
        off = SUBLANES - (CONV_W - 1) + j
        acc = acc + w_ref[j:j + 1, :] * xbuf[off:off + tt, :]
    y = acc * _sigmoid(acc)
    q = y[:, 0:REC_W]
    k = y[:, REC_W:2 * REC_W]
    e = e_ref[...]
    qs = _dot_exact_rhs(q * q, e, 2)
    ks = _dot_exact_rhs(k * k, e, 2)
    o_ref[:, 0:REC_W] = q * lax.rsqrt(qs + NORM_EPS) * (HEAD_DIM ** -0.5)
    o_ref[:, REC_W:2 * REC_W] = k * lax.rsqrt(ks + NORM_EPS)
    o_ref[:, 2 * REC_W:3 * REC_W] = y[:, 2 * REC_W:3 * REC_W]
    xbuf[0:SUBLANES, :] = xbuf[tt:tt + SUBLANES, :]


def _gdn_pre(proj, ctx8, conv_w8, e_heads, b, t, tt):
    w3 = 3 * REC_W
    nt = t // tt
    return pl.pallas_call(
        functools.partial(_gdn_pre_kernel, tt=tt),
        grid=(b, nt),
        in_specs=[pl.BlockSpec((tt, w3), lambda bi, i: (bi * nt + i, COL_GQKV // w3)),
                  pl.BlockSpec((1, SUBLANES, w3), lambda bi, i: (bi, 0, 0)),
                  pl.BlockSpec((SUBLANES, w3), lambda bi, i: (0, 0)),
                  pl.BlockSpec((REC_W, REC_W), lambda bi, i: (0, 0))],
        out_specs=pl.BlockSpec((tt, w3), lambda bi, i: (bi * nt + i, 0)),
        out_shape=jax.ShapeDtypeStruct((b * t, w3), F32),
        scratch_shapes=[pltpu.VMEM((tt + SUBLANES, w3), F32)],
        compiler_params=_cparams("parallel", "arbitrary"),
        name="gdn_pre",
    )(proj, ctx8, conv_w8, e_heads)


def _sb_kernel(q_ref, kt_ref, v_ref, o_ref, acc_ref, c_ref, *, hb, tq, tk, nkb, q_off):
    qi = pl.program_id(1)
    q_lo = q_off + qi * tq
    nblk = jnp.minimum(nkb, (q_lo + tq - 1 + tk - 1) // tk)
    nfull = jnp.minimum(nblk, q_lo // tk)
    scale = HEAD_DIM ** -0.5 * LOG2E
    qs = [(q_ref[:, h * HEAD_DIM:(h + 1) * HEAD_DIM] * scale).astype(BF16) for h in range(hb)]
    lower_incl = _ones_where(lax.broadcasted_iota(jnp.int32, (tk, tk), 0)
                             >= lax.broadcasted_iota(jnp.int32, (tk, tk), 1), BF16)
    acc_ref[...] = jnp.zeros_like(acc_ref)
    c_ref[...] = jnp.zeros_like(c_ref)

    def block(kb, masked):
        if masked:
            valid = (kb * tk + lax.broadcasted_iota(jnp.int32, (tq, tk), 1)
                     < q_lo + lax.broadcasted_iota(jnp.int32, (tq, tk), 0))
        for h in range(hb):
            z = jnp.dot(qs[h], kt_ref[h, kb], preferred_element_type=F32)
            lneg = jnp.maximum(z, 0.0) + jnp.log2(1.0 + jnp.exp2(-jnp.abs(z)))
            if masked:
                lneg = jnp.where(valid, lneg, 0.0)
            incl = jnp.dot(lneg.astype(BF16), lower_incl, preferred_element_type=F32)
            c = c_ref[h]
            a = jnp.exp2(z - incl - jnp.tile(c, (1, tk // LANES)))
            if masked:
                a = jnp.where(valid, a, 0.0)
            acc_ref[h] += jnp.dot(a.astype(BF16), v_ref[h, kb], preferred_element_type=F32)
            c_ref[h] = c + incl[:, 0:1]

    @pl.loop(0, nblk - nfull)
    def _(i):
        block(nblk - 1 - i, True)

    @pl.loop(0, nfull)
    def _(i):
        block(nfull - 1 - i, False)

    for h in range(hb):
        o_ref[:, h * HEAD_DIM:(h + 1) * HEAD_DIM] = acc_ref[h]


def _sb_attn(proj, kth, vh, b, t, hb, tq, q_off):
    g, nkb, d, tk = kth.shape
    nq = t // tq
    w = hb * d
    return pl.pallas_call(
        functools.partial(_sb_kernel, hb=hb, tq=tq, tk=tk, nkb=nkb, q_off=q_off),
        grid=(b, nq),
        in_specs=[pl.BlockSpec((tq, w), lambda bi, i: (bi * nq + i, COL_SB // w)),
                  pl.BlockSpec((hb, nkb, d, tk), lambda bi, i: (bi, 0, 0, 0), pipeline_mode=pl.Buffered(1)),
                  pl.BlockSpec((hb, nkb, tk, d), lambda bi, i: (bi, 0, 0, 0), pipeline_mode=pl.Buffered(1))],
        out_specs=pl.BlockSpec((tq, w), lambda bi, i: (bi * nq + i, 0)),
        out_shape=jax.ShapeDtypeStruct((b * t, w), F32),
        scratch_shapes=[pltpu.VMEM((hb, tq, d), F32), pltpu.VMEM((hb, tq, LANES), F32)],
        compiler_params=_cparams("parallel", "parallel"),
        name="sb_attn",
    )(proj, kth, vh)


def _gate_values(pre, neg_a, lane_id):
    sp = jnp.log1p(jnp.exp(-jnp.abs(pre)))
    softplus = jnp.maximum(pre, 0.0) + sp
    log_sig = jnp.minimum(pre, 0.0) - sp
    return jnp.where(lane_id < GATE_G, _sigmoid(pre),
                     jnp.where(lane_id < GATE_I, neg_a * softplus,
                               jnp.where(lane_id < GATE_F, pre, log_sig)))


def _gates(gt_ref, gtt_ref, prow_ref, pcol_ref, l):
    lane = lax.broadcasted_iota(jnp.int32, (l, LANES), 1)
    val = _gate_values(gt_ref[...] + prow_ref[0:1, :], -jnp.exp(prow_ref[1:2, :]), lane)
    sub = lax.broadcasted_iota(jnp.int32, (LANES, l), 0)
    val_t = _gate_values(gtt_ref[0] + pcol_ref[:, 0:1], -jnp.exp(pcol_ref[:, 1:2]), sub)
    r = lax.broadcasted_iota(jnp.int32, (l, l), 0)
    c = lax.broadcasted_iota(jnp.int32, (l, l), 1)
    csum = _dot_exact_lhs(_ones_where(r >= c, BF16), val, 3)
    csum_t = _dot_exact_rhs(val_t, _ones_where(r <= c, BF16), 3)
    return val, csum, val_t, csum_t


def _gdn_scan_kernel(qkv_ref, kt_ref, gt_ref, gtt_ref, prow_ref, pcol_ref, s0_ref, o_ref, s_out_ref, s_scr,
                     *, l, nc):
    ci = pl.program_id(1)

    @pl.when(ci == 0)
    def _():
        s_scr[...] = s0_ref[0]

    val, csum, _, csum_t = _gates(gt_ref, gtt_ref, prow_ref, pcol_ref, l)
    r = lax.broadcasted_iota(jnp.int32, (l, l), 0)
    c = lax.broadcasted_iota(jnp.int32, (l, l), 1)
    tri = r >= c
    stri = r > c
    eye = _ones_where(r == c, F32)
    n_double = max(int(np.ceil(np.log2(l))) - 1, 0)
    hs = range(GDN_HEADS)
    q = [qkv_ref[:, h * HEAD_DIM:(h + 1) * HEAD_DIM] for h in hs]
    k = [qkv_ref[:, REC_W + h * HEAD_DIM:REC_W + (h + 1) * HEAD_DIM] for h in hs]
    v = [qkv_ref[:, 2 * REC_W + h * HEAD_DIM:2 * REC_W + (h + 1) * HEAD_DIM] for h in hs]
    kt = [kt_ref[0, h * HEAD_DIM:(h + 1) * HEAD_DIM, :] for h in hs]
    beta = [val[:, GATE_BETA + h:GATE_BETA + h + 1] for h in hs]
    g_col = [csum[:, GATE_G + h:GATE_G + h + 1] for h in hs]
    g_row = [csum_t[GATE_G + h:GATE_G + h + 1, :] for h in hs]
    g_last = [g_col[h][l - 1:l, :] for h in hs]
    decay = [jnp.where(tri, jnp.exp(jnp.where(tri, g_col[h] - g_row[h], 0.0)), 0.0) for h in hs]
    kb = [k[h] * beta[h] for h in hs]
    e_g = [jnp.exp(g_col[h]) for h in hs]
    kk = [_dot(kb[h], kt[h]) for h in hs]
    qk = [_dot(q[h], kt[h]) for h in hs]
    p = [-jnp.where(stri, kk[h] * decay[h], 0.0) for h in hs]
    tinv = [eye + p[h] for h in hs]
    for _ in range(n_double):
        p = [_dot(p[h], p[h]) for h in hs]
        tinv = [tinv[h] + _dot(tinv[h], p[h]) for h in hs]
    s = [s_scr[h] for h in hs]
    sol_v = [_dot(tinv[h], v[h] * beta[h]) for h in hs]
    sol_k = [_dot(tinv[h], kb[h] * e_g[h]) for h in hs]
    qs = [_dot(q[h] * e_g[h], s[h]) for h in hs]
    u = [sol_v[h] - _dot(sol_k[h], s[h]) for h in hs]
    ktd = [kt[h] * jnp.exp(g_last[h] - g_row[h]) for h in hs]
    s_new = [jnp.exp(g_last[h]) * s[h] + _dot(ktd[h], u[h]) for h in hs]
    o = [qs[h] + _dot(qk[h] * decay[h], u[h]) for h in hs]
    for h in hs:
        s_scr[h] = s_new[h]
        o_ref[:, h * HEAD_DIM:(h + 1) * HEAD_DIM] = o[h]

    @pl.when(ci == nc - 1)
    def _():
        s_out_ref[0] = s_scr[...]


def _gdn_scan(qkv, k_t, proj, gate_t, prow, pcol, s0, b, t, l):
    nc = t // l
    w3 = 3 * REC_W
    return pl.pallas_call(
        functools.partial(_gdn_scan_kernel, l=l, nc=nc),
        grid=(b, nc),
        in_specs=[pl.BlockSpec((l, w3), lambda bi, i: (bi * nc + i, 0)),
                  pl.BlockSpec((1, REC_W, l), lambda bi, i: (bi * nc + i, 0, 0)),
                  pl.BlockSpec((l, LANES), lambda bi, i: (bi * nc + i, COL_GATE // LANES)),
                  pl.BlockSpec((1, LANES, l), lambda bi, i: (bi * nc + i, 0, 0)),
                  pl.BlockSpec((SUBLANES, LANES), lambda bi, i: (0, 0)),
                  pl.BlockSpec((LANES, SUBLANES), lambda bi, i: (0, 0)),
                  pl.BlockSpec((1, GDN_HEADS, HEAD_DIM, HEAD_DIM), lambda bi, i: (bi, 0, 0, 0))],
        out_specs=[pl.BlockSpec((l, REC_W), lambda bi, i: (bi * nc + i, 0)),
                   pl.BlockSpec((1, GDN_HEADS, HEAD_DIM, HEAD_DIM), lambda bi, i: (bi, 0, 0, 0))],
        out_shape=[jax.ShapeDtypeStruct((b * t, REC_W), F32),
                   jax.ShapeDtypeStruct((b, GDN_HEADS, HEAD_DIM, HEAD_DIM), F32)],
        scratch_shapes=[pltpu.VMEM((GDN_HEADS, HEAD_DIM, HEAD_DIM), F32)],
        compiler_params=_cparams("parallel", "arbitrary"),
        name="gdn_scan",
    )(qkv, k_t, proj, gate_t, prow, pcol, s0)


def _ml_scan_kernel(qkv_ref, kt_ref, gt_ref, gtt_ref, prow_ref, pcol_ref, c0_ref, n0_ref, m0_ref,
                    o_ref, c_out_ref, n_out_ref, m_out_ref, c_scr, n_scr, m_scr, *, l, nc):
    ci = pl.program_id(1)

    @pl.when(ci == 0)
    def _():
        c_scr[...] = c0_ref[0]
        n_scr[...] = n0_ref[0]
        m_scr[...] = m0_ref[0]

    val, csum, val_t, csum_t = _gates(gt_ref, gtt_ref, prow_ref, pcol_ref, l)
    r = lax.broadcasted_iota(jnp.int32, (l, l), 0)
    c = lax.broadcasted_iota(jnp.int32, (l, l), 1)
    tri = r >= c
    lane = lax.broadcasted_iota(jnp.int32, (1, LANES), 1)
    m_all = m_scr[...]
    hs = range(ML_HEADS)
    kscale = HEAD_DIM ** -0.5
    q = [qkv_ref[:, h * HEAD_DIM:(h + 1) * HEAD_DIM] for h in hs]
    k = [qkv_ref[:, REC_W + h * HEAD_DIM:REC_W + (h + 1) * HEAD_DIM] * kscale for h in hs]
    v = [qkv_ref[:, 2 * REC_W + h * HEAD_DIM:2 * REC_W + (h + 1) * HEAD_DIM] for h in hs]
    kt = [kt_ref[0, h * HEAD_DIM:(h + 1) * HEAD_DIM, :] * kscale for h in hs]
    ig_col = [val[:, GATE_I + h:GATE_I + h + 1] for h in hs]
    ig_row = [val_t[GATE_I + h:GATE_I + h + 1, :] for h in hs]
    f_col = [csum[:, GATE_F + h:GATE_F + h + 1] for h in hs]
    f_row = [csum_t[GATE_F + h:GATE_F + h + 1, :] for h in hs]
    f_last = [f_col[h][l - 1:l, :] for h in hs]
    m_prev = [m_all[:, h:h + 1] for h in hs]
    c_h = [c_scr[h] for h in hs]
    n_h = [n_scr[h:h + 1, :] for h in hs]
    qk = [_dot(q[h], kt[h]) for h in hs]
    qc = [_dot(q[h], c_h[h]) for h in hs]
    d = [jnp.where(tri, f_col[h] - f_row[h] + ig_row[h], -jnp.inf) for h in hs]
    inter = [f_col[h] + m_prev[h] for h in hs]
    m_t = [jnp.maximum(inter[h], jnp.max(d[h], axis=1, keepdims=True)) for h in hs]
    m_new = [m_t[h][l - 1:l, :] for h in hs]
    w = [jnp.exp(d[h] - m_t[h]) * qk[h] for h in hs]
    c_inter = [jnp.exp(inter[h] - m_t[h]) for h in hs]
    wv = [_dot(w[h], v[h]) for h in hs]
    w_end_row = [jnp.exp(f_last[h] - f_row[h] + ig_row[h] - m_new[h]) for h in hs]
    w_end_col = [jnp.exp(f_last[h] - f_col[h] + ig_col[h] - m_new[h]) for h in hs]
    c_prev = [jnp.exp(f_last[h] + m_prev[h] - m_new[h]) for h in hs]
    dc = [_dot(kt[h] * w_end_row[h], v[h]) for h in hs]
    m_next = m_all
    for h in hs:
        num = c_inter[h] * qc[h] + wv[h]
        den = (c_inter[h] * jnp.sum(q[h] * n_h[h], axis=1, keepdims=True)
               + jnp.sum(w[h], axis=1, keepdims=True))
        o_ref[:, h * HEAD_DIM:(h + 1) * HEAD_DIM] = num / jnp.maximum(jnp.abs(den), jnp.exp(-m_t[h]))
        c_scr[h] = c_prev[h] * c_h[h] + dc[h]
        n_scr[h:h + 1, :] = c_prev[h] * n_h[h] + jnp.sum(w_end_col[h] * k[h], axis=0, keepdims=True)
        m_next = jnp.where(lane == h, m_new[h], m_next)
    m_scr[...] = m_next

    @pl.when(ci == nc - 1)
    def _():
        c_out_ref[0] = c_scr[...]
        n_out_ref[0] = n_scr[...]
        m_out_ref[0] = m_scr[...]


def _ml_scan(proj, k_t, gate_t, prow, pcol, c0, n0, m0, b, t, l):
    nc = t // l
    w3 = 3 * REC_W
    st = lambda bi, i: (bi, 0, 0, 0)
    st3 = lambda bi, i: (bi, 0, 0)
    return pl.pallas_call(
        functools.partial(_ml_scan_kernel, l=l, nc=nc),
        grid=(b, nc),
        in_specs=[pl.BlockSpec((l, w3), lambda bi, i: (bi * nc + i, COL_MQKV // w3)),
                  pl.BlockSpec((1, REC_W, l), lambda bi, i: (bi * nc + i, 0, 0)),
                  pl.BlockSpec((l, LANES), lambda bi, i: (bi * nc + i, COL_GATE // LANES)),
                  pl.BlockSpec((1, LANES, l), lambda bi, i: (bi * nc + i, 0, 0)),
                  pl.BlockSpec((SUBLANES, LANES), lambda bi, i: (0, 0)),
                  pl.BlockSpec((LANES, SUBLANES), lambda bi, i: (0, 0)),
                  pl.BlockSpec((1, ML_HEADS, HEAD_DIM, HEAD_DIM), st),
                  pl.BlockSpec((1, SUBLANES, HEAD_DIM), st3),
                  pl.BlockSpec((1, 1, LANES), st3)],
        out_specs=[pl.BlockSpec((l, REC_W), lambda bi, i: (bi * nc + i, 0)),
                   pl.BlockSpec((1, ML_HEADS, HEAD_DIM, HEAD_DIM), st),
                   pl.BlockSpec((1, SUBLANES, HEAD_DIM), st3),
                   pl.BlockSpec((1, 1, LANES), st3)],
        out_shape=[jax.ShapeDtypeStruct((b * t, REC_W), F32),
                   jax.ShapeDtypeStruct((b, ML_HEADS, HEAD_DIM, HEAD_DIM), F32),
                   jax.ShapeDtypeStruct((b, SUBLANES, HEAD_DIM), F32),
                   jax.ShapeDtypeStruct((b, 1, LANES), F32)],
        scratch_shapes=[pltpu.VMEM((ML_HEADS, HEAD_DIM, HEAD_DIM), F32),
                        pltpu.VMEM((SUBLANES, HEAD_DIM), F32),
                        pltpu.VMEM((1, LANES), F32)],
        compiler_params=_cparams("parallel", "arbitrary"),
        name="ml_scan",
    )(proj, k_t, proj, gate_t, prow, pcol, c0, n0, m0)


def _post_kernel(x_ref, osb_ref, og_ref, hm_ref, gz_ref, mo_ref, w_ref, e_ref, gnw_ref, mnw_ref,
                 lg_ref, lb_ref, o_ref, ob_ref):
    e = e_ref[...]
    og = og_ref[...]
    hm = hm_ref[...]
    og_ms = _dot_exact_rhs(og * og, e, 2) * (1.0 / HEAD_DIM)
    hm_ms = _dot_exact_rhs(hm * hm, e, 2) * (1.0 / HEAD_DIM)
    gz = gz_ref[...]
    o_g = og * lax.rsqrt(og_ms + NORM_EPS) * gnw_ref[...] * (gz * _sigmoid(gz))
    o_m = _sigmoid(mo_ref[...]) * (hm * lax.rsqrt(hm_ms + NORM_EPS) * mnw_ref[...])
    mixed = (_dot(osb_ref[...], w_ref[0:SB_W, :])
             + _dot(o_g, w_ref[SB_W:SB_W + REC_W, :])
             + _dot(o_m, w_ref[SB_W + REC_W:SB_W + 2 * REC_W, :]))
    y = _layer_norm(DEEPNORM_ALPHA * x_ref[...] + mixed, lg_ref[...], lb_ref[...])
    o_ref[...] = y
    ob_ref[...] = y.astype(BF16)


def _post(x, osb, og, hm, proj, w_out, e_heads, gnw, mnw, lg, lb, tm):
    n, d = x.shape
    row = lambda i: (i, 0)
    fixed = lambda i: (0, 0)
    return pl.pallas_call(
        _post_kernel,
        grid=(n // tm,),
        in_specs=[pl.BlockSpec((tm, d), row),
                  pl.BlockSpec((tm, SB_W), row),
                  pl.BlockSpec((tm, REC_W), row),
                  pl.BlockSpec((tm, REC_W), row),
                  pl.BlockSpec((tm, REC_W), lambda i: (i, COL_GZ // REC_W)),
                  pl.BlockSpec((tm, REC_W), lambda i: (i, COL_MO // REC_W)),
                  pl.BlockSpec(w_out.shape, fixed),
                  pl.BlockSpec((REC_W, REC_W), fixed),
                  pl.BlockSpec((1, REC_W), fixed),
                  pl.BlockSpec((1, REC_W), fixed),
                  pl.BlockSpec((1, d), fixed),
                  pl.BlockSpec((1, d), fixed)],
        out_specs=[pl.BlockSpec((tm, d), row), pl.BlockSpec((tm, d), row)],
        out_shape=[jax.ShapeDtypeStruct((n, d), F32), jax.ShapeDtypeStruct((n, d), BF16)],
        compiler_params=_cparams("parallel"),
        name="post",
    )(x, osb, og, hm, proj, proj, w_out, e_heads, gnw, mnw, lg, lb)


def _memattn_kernel(x_ref, xb_ref, wq_ref, wo_ref, mk_ref, mv_ref, lg_ref, lb_ref, o_ref, ob_ref):
    d = x_ref.shape[1]
    hd = d // MEM_HEADS
    q = jnp.dot(xb_ref[...], wq_ref[...], preferred_element_type=F32)
    out = None
    for h in range(MEM_HEADS):
        sl = slice(h * hd, (h + 1) * hd)
        s = _dot_nt(q[:, sl], mk_ref[0, :, sl]) * (hd ** -0.5)
        p = jnp.exp(s - jnp.max(s, axis=1, keepdims=True))
        o_h = _dot(p, mv_ref[0, :, sl]) / jnp.sum(p, axis=1, keepdims=True)
        t = _dot(o_h, wo_ref[sl, :])
        out = t if out is None else out + t
    y = _layer_norm(DEEPNORM_ALPHA * x_ref[...] + out, lg_ref[...], lb_ref[...])
    o_ref[...] = y
    ob_ref[...] = y.astype(BF16)


def _memattn(x, xb, w_cq, w_co, mk, mv, lg, lb, b, t, tm):
    n, d = x.shape
    nt = t // tm
    nm = mk.shape[1]
    row = lambda bi, i: (bi * nt + i, 0)
    fixed = lambda bi, i: (0, 0)
    return pl.pallas_call(
        _memattn_kernel,
        grid=(b, nt),
        in_specs=[pl.BlockSpec((tm, d), row),
                  pl.BlockSpec((tm, d), row),
                  pl.BlockSpec((d, d), fixed),
                  pl.BlockSpec((d, d), fixed),
                  pl.BlockSpec((1, nm, d), lambda bi, i: (bi, 0, 0)),
                  pl.BlockSpec((1, nm, d), lambda bi, i: (bi, 0, 0)),
                  pl.BlockSpec((1, d), fixed),
                  pl.BlockSpec((1, d), fixed)],
        out_specs=[pl.BlockSpec((tm, d), row), pl.BlockSpec((tm, d), row)],
        out_shape=[jax.ShapeDtypeStruct((n, d), F32), jax.ShapeDtypeStruct((n, d), BF16)],
        compiler_params=_cparams("parallel", "parallel"),
        name="memattn",
    )(x, xb, w_cq, w_co, mk, mv, lg, lb)


def _ffn_kernel(x_ref, xb_ref, wg_ref, wu_ref, wd_ref, lg_ref, lb_ref, o_ref, ob_ref, acc_ref, *, nf):
    j = pl.program_id(1)

    @pl.when(j == 0)
    def _():
        acc_ref[...] = jnp.zeros_like(acc_ref)

    xb = xb_ref[...]
    gate = jnp.dot(xb, wg_ref[...], preferred_element_type=F32)
    up = jnp.dot(xb, wu_ref[...], preferred_element_type=F32)
    acc_ref[...] += _dot(gate * _sigmoid(gate) * up, wd_ref[...])

    @pl.when(j == nf - 1)
    def _():
        y = _layer_norm(DEEPNORM_ALPHA * x_ref[...] + acc_ref[...], lg_ref[...], lb_ref[...])
        o_ref[...] = y
        ob_ref[...] = y.astype(BF16)


def _ffn(x, xb, w_up, w_down, lg, lb, tm, tf):
    n, d = x.shape
    dff = w_down.shape[0]
    nf = dff // tf
    row = lambda i, j: (i, 0)
    fixed = lambda i, j: (0, 0)
    return pl.pallas_call(
        functools.partial(_ffn_kernel, nf=nf),
        grid=(n // tm, nf),
        in_specs=[pl.BlockSpec((tm, d), row),
                  pl.BlockSpec((tm, d), row),
                  pl.BlockSpec((d, tf), lambda i, j: (0, j)),
                  pl.BlockSpec((d, tf), lambda i, j: (0, nf + j)),
                  pl.BlockSpec((tf, d), lambda i, j: (j, 0)),
                  pl.BlockSpec((1, d), fixed),
                  pl.BlockSpec((1, d), fixed)],
        out_specs=[pl.BlockSpec((tm, d), row), pl.BlockSpec((tm, d), row)],
        out_shape=[jax.ShapeDtypeStruct((n, d), F32), jax.ShapeDtypeStruct((n, d), BF16)],
        scratch_shapes=[pltpu.VMEM((tm, d), F32)],
        compiler_params=_cparams("parallel", "arbitrary"),
        name="ffn",
    )(x, xb, w_up, w_up, w_down, lg, lb)


def _pack_layer(p):
    w_in = p['w_in']
    d = w_in.shape[0]
    o_sb, o_gqkv, o_gz = 0, 3 * SB_W, 3 * SB_W + 3 * REC_W
    o_gb = o_gz + REC_W
    o_mqkv = o_gb + 2 * GDN_HEADS
    o_mo = o_mqkv + 3 * REC_W
    o_mi = o_mo + REC_W
    gate_cols = jnp.concatenate([w_in[:, o_gb:o_gb + 2 * GDN_HEADS], w_in[:, o_mi:o_mi + 2 * ML_HEADS]], axis=1)
    used = COL_GATE + gate_cols.shape[1]
    w_al = jnp.concatenate([w_in[:, o_gqkv:o_gqkv + 3 * REC_W], w_in[:, o_mqkv:o_mqkv + 3 * REC_W],
                            w_in[:, o_gz:o_gz + REC_W], w_in[:, o_mo:o_mo + REC_W],
                            w_in[:, o_sb:o_sb + 3 * SB_W], gate_cols,
                            jnp.zeros((d, PROJ_W - used), w_in.dtype)], axis=1).astype(BF16)
    prow = jnp.zeros((SUBLANES, LANES), F32)
    prow = prow.at[0, GATE_G:GATE_G + GDN_HEADS].set(p['gdn_dt_bias'])
    prow = prow.at[0, GATE_I:GATE_I + ML_HEADS].set(p['mlstm_i_bias'])
    prow = prow.at[0, GATE_F:GATE_F + ML_HEADS].set(p['mlstm_f_bias'])
    prow = prow.at[1, GATE_G:GATE_G + GDN_HEADS].set(p['gdn_A_log'])
    conv_w8 = jnp.zeros((SUBLANES, 3 * REC_W), F32).at[0:CONV_W].set(p['gdn_conv_w'])
    return dict(
        w_al=w_al, prow=prow, pcol=prow.T, conv_w8=conv_w8,
        gnw=jnp.tile(p['gdn_norm_w'], GDN_HEADS)[None, :], mnw=p['mlstm_norm_w'][None, :],
        w_out=p['w_out'].astype(BF16), w_cq=p['w_cq'].astype(BF16), w_co=p['w_co'].astype(BF16),
        w_up=p['w_up'].astype(BF16), w_down=p['w_down'].astype(BF16),
        ln1=(p['ln1_g'][None, :], p['ln1_b'][None, :]), ln2=(p['ln2_g'][None, :], p['ln2_b'][None, :]),
        ln3=(p['ln3_g'][None, :], p['ln3_b'][None, :]))


def _head_consts():
    hid = np.arange(REC_W) // HEAD_DIM
    return jnp.asarray(hid[:, None] == hid[None, :], BF16)


def _sb_layout(k, v, tk):
    b, tkv, _ = k.shape
    nkb = tkv // tk
    kth = k.reshape(b, nkb, tk, SB_HEADS, HEAD_DIM).transpose(0, 3, 1, 4, 2)
    kth = kth.reshape(b * SB_HEADS, nkb, HEAD_DIM, tk)
    vh = v.reshape(b, nkb, tk, SB_HEADS, HEAD_DIM).transpose(0, 3, 1, 2, 4)
    vh = vh.reshape(b * SB_HEADS, nkb, tk, HEAD_DIM)
    return kth.astype(BF16), vh.astype(BF16)


def _trunk_layer(x, xb, pk, e_heads, mk, mv, b, t, cfg, sb_past, conv_ctx, gdn_s, ml_c, ml_n, ml_m):
    n, d = x.shape
    l = min(t, CHUNK)
    nc = t // l
    proj = _matmul(xb, pk['w_al'], cfg['tm'], 1024)
    sk =proj[:, COL_SB + SB_W:COL_SB + 2 * SB_W].reshape(b, t, SB_W)
    sv = proj[:, COL_SB + 2 * SB_W:COL_SB + 3 * SB_W].reshape(b, t, SB_W)
    g_qkv = proj[:, COL_GQKV:COL_GQKV + 3 * REC_W].reshape(b, t, 3 * REC_W)

    tk = cfg['sb_tk']
    if sb_past is None:
        q_off, kk, vv = 0, sk, sv
    else:
        past_k, past_v = sb_past
        q_off = past_k.shape[1]
        kv_len = q_off + t
        kv_pad = -(-kv_len // tk) * tk - kv_len
        kk = jnp.pad(jnp.concatenate([past_k.reshape(b, q_off, SB_W), sk], axis=1), ((0, 0), (0, kv_pad), (0, 0)))
        vv = jnp.pad(jnp.concatenate([past_v.reshape(b, q_off, SB_W), sv], axis=1), ((0, 0), (0, kv_pad), (0, 0)))
    kth, vh = _sb_layout(kk, vv, tk)
    o_sb = _sb_attn(proj, kth, vh, b, t, SB_HEADS, cfg['sb_tq'], q_off)

    ctx8 = jnp.pad(conv_ctx, ((0, 0), (SUBLANES - (CONV_W - 1), 0), (0, 0)))
    gqkv_act = _gdn_pre(proj, ctx8, pk['conv_w8'], e_heads, b, t, cfg['conv_tt'])
    gate_t = proj[:, COL_GATE:COL_GATE + LANES].reshape(b * nc, l, LANES).transpose(0, 2, 1)
    gk_t = gqkv_act[:, REC_W:2 * REC_W].reshape(b * nc, l, REC_W).transpose(0, 2, 1)
    o_g, gdn_s_new = _gdn_scan(gqkv_act, gk_t, proj, gate_t, pk['prow'], pk['pcol'], gdn_s, b, t, l)
    n0 = jnp.pad(ml_n, ((0, 0), (0, SUBLANES - ML_HEADS), (0, 0)))
    m0 = jnp.pad(ml_m, ((0, 0), (0, LANES - ML_HEADS)))[:, None, :]
    mk_t = proj[:, COL_MQKV + REC_W:COL_MQKV + 2 * REC_W].reshape(b * nc, l, REC_W).transpose(0, 2, 1)
    h_m, c_new, n_new, m_new = _ml_scan(proj, mk_t, gate_t, pk['prow'], pk['pcol'], ml_c, n0, m0, b, t, l)

    x1, x1b = _post(x, o_sb, o_g, h_m, proj, pk['w_out'], e_heads, pk['gnw'], pk['mnw'], *pk['ln1'], cfg['tm'])
    x2, x2b = _memattn(x1, x1b, pk['w_cq'], pk['w_co'], mk, mv, *pk['ln2'], b, t, cfg['mem_tm'])
    x3, x3b = _ffn(x2, x2b, pk['w_up'], pk['w_down'], *pk['ln3'], cfg['tm'], cfg['ffn_tf'])

    new_conv = g_qkv[:, t - (CONV_W - 1):, :]
    return (x3, x3b, sk.reshape(b, t, SB_HEADS, HEAD_DIM), sv.reshape(b, t, SB_HEADS, HEAD_DIM), new_conv,
            gdn_s_new, c_new, n_new[:, :ML_HEADS, :], m_new[:, 0, :ML_HEADS])


def kernel(x_prompt, x_sample, cache_sb_k, cache_sb_v, cache_gdn_conv, state_gdn, state_mlstm_C, state_mlstm_n, state_mlstm_m, cache_mem_k, cache_mem_v, mem_prompt, w_in, gdn_conv_w, gdn_A_log, gdn_dt_bias, gdn_norm_w, mlstm_i_bias, mlstm_f_bias, mlstm_norm_w, w_out, ln1_g, ln1_b, w_cq, w_ckv, w_co, ln2_g, ln2_b, w_up, w_down, ln3_g, ln3_b):
    bp, tp, d = x_prompt.shape
    bs, ts, _ = x_sample.shape
    depth = w_in.shape[0]
    n_mem = mem_prompt.shape[1]
    dff = w_down.shape[1]
    cfg_p = dict(tm=512, sb_tq=1024, sb_tk=256, conv_tt=512, mem_tm=512, ffn_tf=dff // 2)
    cfg_s = dict(tm=bs * ts, sb_tq=ts, sb_tk=256, conv_tt=ts, mem_tm=ts, ffn_tf=dff // 2)
    e_heads = _head_consts()

    xp = x_prompt.reshape(bp * tp, d)
    xs = x_sample.reshape(bs * ts, d)
    xpb, xsb = xp.astype(BF16), xs.astype(BF16)
    mem_b = mem_prompt.reshape(bp * n_mem, d).astype(BF16)
    p_out = [[] for _ in range(9)]
    s_out = [[] for _ in range(7)]
    for li in range(depth):
        p = {'w_in': w_in[li], 'gdn_conv_w': gdn_conv_w[li], 'gdn_A_log': gdn_A_log[li],
             'gdn_dt_bias': gdn_dt_bias[li], 'gdn_norm_w': gdn_norm_w[li], 'mlstm_i_bias': mlstm_i_bias[li],
             'mlstm_f_bias': mlstm_f_bias[li], 'mlstm_norm_w': mlstm_norm_w[li], 'w_out': w_out[li],
             'ln1_g': ln1_g[li], 'ln1_b': ln1_b[li], 'w_cq': w_cq[li], 'w_co': w_co[li],
             'ln2_g': ln2_g[li], 'ln2_b': ln2_b[li], 'w_up': w_up[li], 'w_down': w_down[li],
             'ln3_g': ln3_g[li], 'ln3_b': ln3_b[li]}
        pk = _pack_layer(p)
        mkv = _matmul(mem_b, w_ckv[li].astype(BF16), n_mem, 1024)
        mk_p = mkv[:, :d].reshape(bp, n_mem, d)
        mv_p = mkv[:, d:].reshape(bp, n_mem, d)
        res = _trunk_layer(xp, xpb, pk, e_heads, mk_p.astype(BF16), mv_p.astype(BF16), bp, tp, cfg_p, None,
                           jnp.zeros((bp, CONV_W - 1, 3 * REC_W), F32),
                           jnp.zeros((bp, GDN_HEADS, HEAD_DIM, HEAD_DIM), F32),
                           jnp.zeros((bp, ML_HEADS, HEAD_DIM, HEAD_DIM), F32),
                           jnp.zeros((bp, ML_HEADS, HEAD_DIM), F32),
                           jnp.zeros((bp, ML_HEADS), F32))
        xp, xpb = res[0], res[1]
        hd = d // MEM_HEADS
        for j, a in enumerate(res[2:] + (mk_p.reshape(bp, n_mem, MEM_HEADS, hd), mv_p.reshape(bp, n_mem, MEM_HEADS, hd))):
            p_out[j].append(a)
        res = _trunk_layer(xs, xsb, pk, e_heads,
                           cache_mem_k[li].reshape(bs, n_mem, d).astype(BF16),
                           cache_mem_v[li].reshape(bs, n_mem, d).astype(BF16),
                           bs, ts, cfg_s, (cache_sb_k[li], cache_sb_v[li]), cache_gdn_conv[li],
                           state_gdn[li], state_mlstm_C[li], state_mlstm_n[li], state_mlstm_m[li])
        xs, xsb = res[0], res[1]
        for j, a in enumerate(res[2:]):
            s_out[j].append(a)
    p_st = [jnp.stack(a) for a in p_out]
    s_st = [jnp.stack(a) for a in s_out]
    return (xp.reshape(bp, tp, d), xs.reshape(bs, ts, d), *p_st, *s_st)
```

```python
import functools

import numpy as np
import jax
import jax.numpy as jnp
from jax import lax
from jax.experimental import pallas as pl
from jax.experimental.pallas import tpu as pltpu

F32 = jnp.float32
BF16 = jnp.bfloat16

HEAD_DIM = 64
SB_HEADS = 4
GDN_HEADS = 6
ML_HEADS = 6
REC_W = GDN_HEADS * HEAD_DIM
SB_W = SB_HEADS * HEAD_DIM
CONV_W = 4
CHUNK = 64
MEM_HEADS = 4
LN_EPS = 1e-5
NORM_EPS = 1e-6
DEPTH = 2
DEEPNORM_ALPHA = (2 * DEPTH) ** 0.25
LOG2E = float(np.log2(np.e))
SOFTPLUS2_CLAMP = 120.0

LANES = 128
SUBLANES = 8
VMEM_LIMIT = 56 * 1024 * 1024

PROJ_W = 4096
COL_GQKV = 0
COL_MQKV = 1152
COL_GZ = 2304
COL_MO = 2688
COL_SB = 3072
COL_GATE = 3840
GATE_BETA, GATE_G, GATE_I, GATE_F = 0, 6, 12, 18


def _cparams(*sem):
    return pltpu.CompilerParams(dimension_semantics=sem, vmem_limit_bytes=VMEM_LIMIT)


def _dot(a, b):
    return jnp.dot(a.astype(BF16), b.astype(BF16), preferred_element_type=F32)


def _dot_nt(a, b):
    return lax.dot_general(a.astype(BF16), b.astype(BF16), (((1,), (1,)), ((), ())),
                           preferred_element_type=F32)


def _dot_tn(a, b):
    return lax.dot_general(a.astype(BF16), b.astype(BF16), (((0,), (0,)), ((), ())),
                           preferred_element_type=F32)


def _split(x, parts):
    out = []
    r = x
    for _ in range(parts - 1):
        p = r.astype(BF16)
        out.append(p)
        r = r - p.astype(F32)
    out.append(r.astype(BF16))
    return out


def _dot_exact_lhs(m, x, parts):
    acc = None
    for p in _split(x, parts):
        t = jnp.dot(m, p, preferred_element_type=F32)
        acc = t if acc is None else acc + t
    return acc


def _dot_exact_rhs(x, m, parts):
    acc = None
    for p in _split(x, parts):
        t = jnp.dot(p, m, preferred_element_type=F32)
        acc = t if acc is None else acc + t
    return acc


def _ones_where(mask, dtype):
    return jnp.where(mask, 1.0, 0.0).astype(dtype)


def _sigmoid(x):
    return 1.0 / (1.0 + jnp.exp(-x))


def _layer_norm(y, g, b):
    mu = jnp.mean(y, axis=-1, keepdims=True)
    d = y - mu
    var = jnp.mean(d * d, axis=-1, keepdims=True)
    return d * lax.rsqrt(var + LN_EPS) * g + b


def _mm_kernel(x_ref, w_ref, o_ref):
    o_ref[...] = jnp.dot(x_ref[...], w_ref[...], preferred_element_type=F32).astype(o_ref.dtype)


def _matmul(x, w, tm, tn, out_dtype=F32):
    n, k = x.shape
    m = w.shape[1]
    return pl.pallas_call(
        _mm_kernel,
        grid=(n // tm, m // tn),
        in_specs=[pl.BlockSpec((tm, k), lambda i, j: (i, 0)),
                  pl.BlockSpec((k, tn), lambda i, j: (0, j))],
        out_specs=pl.BlockSpec((tm, tn), lambda i, j: (i, j)),
        out_shape=jax.ShapeDtypeStruct((n, m), out_dtype),
        compiler_params=_cparams("parallel", "parallel"),
        name="proj_matmul",
    )(x, w)


def _gdn_pre_kernel(x_ref, ctx_ref, w_ref, e_ref, o_ref, xbuf, *, tt):
    i = pl.program_id(1)

    @pl.when(i == 0)
    def _():
        xbuf[0:SUBLANES, :] = ctx_ref[0]

    xbuf[SUBLANES:SUBLANES + tt, :] = x_ref[...]
    acc = w_ref[CONV_W - 1:CONV_W, :] * xbuf[SUBLANES:SUBLANES + tt, :]
    for j in range(CONV_W - 1):
        off = SUBLANES - (CONV_W - 1) + j
        acc = acc + w_ref[j:j + 1, :] * xbuf[off:off + tt, :]
    y = acc * _sigmoid(acc)
    q = y[:, 0:REC_W]
    k = y[:, REC_W:2 * REC_W]
    e = e_ref[...]
    qs = _dot_exact_rhs(q * q, e, 2)
    ks = _dot_exact_rhs(k * k, e, 2)
    o_ref[:, 0:REC_W] = q * lax.rsqrt(qs + NORM_EPS) * (HEAD_DIM ** -0.5)
    o_ref[:, REC_W:2 * REC_W] = k * lax.rsqrt(ks + NORM_EPS)
    o_ref[:, 2 * REC_W:3 * REC_W] = y[:, 2 * REC_W:3 * REC_W]
    xbuf[0:SUBLANES, :] = xbuf[tt:tt + SUBLANES, :]


def _gdn_pre(proj, ctx8, conv_w8, e_heads, b, t, tt):
    w3 = 3 * REC_W
    nt = t // tt
    return pl.pallas_call(
        functools.partial(_gdn_pre_kernel, tt=tt),
        grid=(b, nt),
        in_specs=[pl.BlockSpec((tt, w3), lambda bi, i: (bi * nt + i, COL_GQKV // w3)),
                  pl.BlockSpec((1, SUBLANES, w3), lambda bi, i: (bi, 0, 0)),
                  pl.BlockSpec((SUBLANES, w3), lambda bi, i: (0, 0)),
                  pl.BlockSpec((REC_W, REC_W), lambda bi, i: (0, 0))],
        out_specs=pl.BlockSpec((tt, w3), lambda bi, i: (bi * nt + i, 0)),
        out_shape=jax.ShapeDtypeStruct((b * t, w3), F32),
        scratch_shapes=[pltpu.VMEM((tt + SUBLANES, w3), F32)],
        compiler_params=_cparams("parallel", "arbitrary"),
        name="gdn_pre",
    )(proj, ctx8, conv_w8, e_heads)


def _sb_kernel(q_ref, kt_ref, v_ref, o_ref, acc_ref, c_ref, *, hb, tq, tk, nkb, q_off):
    qi = pl.program_id(1)
    q_lo = q_off + qi * tq
    nblk = jnp.minimum(nkb, (q_lo + tq - 1 + tk - 1) // tk)
    nfull = jnp.minimum(nblk, q_lo // tk)
    scale = HEAD_DIM ** -0.5 * LOG2E
    qs = [(q_ref[:, h * HEAD_DIM:(h + 1) * HEAD_DIM] * scale).astype(BF16) for h in range(hb)]
    lower_incl = _ones_where(lax.broadcasted_iota(jnp.int32, (tk, tk), 0)
                             >= lax.broadcasted_iota(jnp.int32, (tk, tk), 1), BF16)
    acc_ref[...] = jnp.zeros_like(acc_ref)
    c_ref[...] = jnp.zeros_like(c_ref)

    def block(kb, masked):
        if masked:
            valid = (kb * tk + lax.broadcasted_iota(jnp.int32, (tq, tk), 1)
                     < q_lo + lax.broadcasted_iota(jnp.int32, (tq, tk), 0))
        for h in range(hb):
            z = jnp.dot(qs[h], kt_ref[h, kb], preferred_element_type=F32)
            lneg = jnp.maximum(z, jnp.log2(1.0 + jnp.exp2(jnp.minimum(z, SOFTPLUS2_CLAMP))))
            if masked:
                lneg = jnp.where(valid, lneg, 0.0)
            incl = jnp.dot(lneg.astype(BF16), lower_incl, preferred_element_type=F32)
            c = c_ref[h]
            a = jnp.exp2(z - incl - jnp.tile(c, (1, tk // LANES)))
            if masked:
                a = jnp.where(valid, a, 0.0)
            acc_ref[h] += jnp.dot(a.astype(BF16), v_ref[h, kb], preferred_element_type=F32)
            c_ref[h] = c + incl[:, 0:1]

    @pl.loop(0, nblk - nfull)
    def _(i):
        block(nblk - 1 - i, True)

    @pl.loop(0, nfull)
    def _(i):
        block(nfull - 1 - i, False)

    for h in range(hb):
        o_ref[:, h * HEAD_DIM:(h + 1) * HEAD_DIM] = acc_ref[h]


def _sb_attn(proj, kth, vh, b, t, hb, tq, q_off):
    g, nkb, d, tk = kth.shape
    nq = t // tq
    w = hb * d
    return pl.pallas_call(
        functools.partial(_sb_kernel, hb=hb, tq=tq, tk=tk, nkb=nkb, q_off=q_off),
        grid=(b, nq),
        in_specs=[pl.BlockSpec((tq, w), lambda bi, i: (bi * nq + i, COL_SB // w)),
                  pl.BlockSpec((hb, nkb, d, tk), lambda bi, i: (bi, 0, 0, 0), pipeline_mode=pl.Buffered(1)),
                  pl.BlockSpec((hb, nkb, tk, d), lambda bi, i: (bi, 0, 0, 0), pipeline_mode=pl.Buffered(1))],
        out_specs=pl.BlockSpec((tq, w), lambda bi, i: (bi * nq + i, 0)),
        out_shape=jax.ShapeDtypeStruct((b * t, w), F32),
        scratch_shapes=[pltpu.VMEM((hb, tq, d), F32), pltpu.VMEM((hb, tq, LANES), F32)],
        compiler_params=_cparams("parallel", "parallel"),
        name="sb_attn",
    )(proj, kth, vh)


def _gate_values(pre, neg_a, lane_id):
    sp = jnp.log1p(jnp.exp(-jnp.abs(pre)))
    softplus = jnp.maximum(pre, 0.0) + sp
    log_sig = jnp.minimum(pre, 0.0) - sp
    return jnp.where(lane_id < GATE_G, _sigmoid(pre),
                     jnp.where(lane_id < GATE_I, neg_a * softplus,
                               jnp.where(lane_id < GATE_F, pre, log_sig)))


def _stack(a, b):
    return jnp.concatenate([a, b], axis=0)


def _gates(gt, gtt, prow_ref, pcol_ref, l):
    lane = lax.broadcasted_iota(jnp.int32, (l, LANES), 1)
    val = _gate_values(gt + prow_ref[0:1, :], -jnp.exp(prow_ref[1:2, :]), lane)
    sub = lax.broadcasted_iota(jnp.int32, (LANES, l), 0)
    val_t = _gate_values(gtt + pcol_ref[:, 0:1], -jnp.exp(pcol_ref[:, 1:2]), sub)
    r = lax.broadcasted_iota(jnp.int32, (l, l), 0)
    c = lax.broadcasted_iota(jnp.int32, (l, l), 1)
    csum = _dot_exact_lhs(_ones_where(r >= c, BF16), val, 3)
    csum_t = _dot_exact_rhs(val_t, _ones_where(r <= c, BF16), 3)
    return val, csum, val_t, csum_t


def _gdn_scan_kernel(qkv_ref, kt_ref, gt_ref, gtt_ref, prow_ref, pcol_ref, s0_ref, o_ref, s_out_ref, s_scr,
                     *, l, cb, ng):
    gi = pl.program_id(1)

    @pl.when(gi == 0)
    def _():
        s_scr[...] = s0_ref[0]

    r = lax.broadcasted_iota(jnp.int32, (l, l), 0)
    c = lax.broadcasted_iota(jnp.int32, (l, l), 1)
    tri = r >= c
    stri = r > c
    eye = _ones_where(r == c, F32)
    n_double = max(int(np.ceil(np.log2(l))) - 1, 0)
    hs = range(GDN_HEADS)
    ps = [(j, h) for j in range(cb) for h in hs]
    gates = [_gates(gt_ref[j * l:(j + 1) * l, :], gtt_ref[j], prow_ref, pcol_ref, l) for j in range(cb)]

    def head_cols(j, group, h):
        return qkv_ref[j * l:(j + 1) * l, group * REC_W + h * HEAD_DIM:group * REC_W + (h + 1) * HEAD_DIM]

    q = {p: head_cols(p[0], 0, p[1]) for p in ps}
    k = {p: head_cols(p[0], 1, p[1]) for p in ps}
    v = {p: head_cols(p[0], 2, p[1]) for p in ps}
    kt = {p: kt_ref[p[0], p[1] * HEAD_DIM:(p[1] + 1) * HEAD_DIM, :] for p in ps}
    beta = {p: gates[p[0]][0][:, GATE_BETA + p[1]:GATE_BETA + p[1] + 1] for p in ps}
    g_col = {p: gates[p[0]][1][:, GATE_G + p[1]:GATE_G + p[1] + 1] for p in ps}
    g_row = {p: gates[p[0]][3][GATE_G + p[1]:GATE_G + p[1] + 1, :] for p in ps}
    g_last = {p: g_col[p][l - 1:l, :] for p in ps}
    decay = {p: jnp.where(tri, jnp.exp(jnp.where(tri, g_col[p] - g_row[p], 0.0)), 0.0) for p in ps}
    kb = {p: k[p] * beta[p] for p in ps}
    e_g = {p: jnp.exp(g_col[p]) for p in ps}
    kq = {p: _dot(_stack(kb[p], q[p]), kt[p]) for p in ps}
    x = {p: -jnp.where(stri, kq[p][:l] * decay[p], 0.0) for p in ps}
    tinv = {p: eye + x[p] for p in ps}
    if n_double > 0:
        x = {p: _dot(x[p], x[p]) for p in ps}
        for _ in range(n_double - 1):
            tx = {p: _dot(_stack(tinv[p], x[p]), x[p]) for p in ps}
            tinv = {p: tinv[p] + tx[p][:l] for p in ps}
            x = {p: tx[p][l:] for p in ps}
        tinv = {p: tinv[p] + _dot(tinv[p], x[p]) for p in ps}
    sol_v = {p: _dot(tinv[p], v[p] * beta[p]) for p in ps}
    sol_k = {p: _dot(tinv[p], kb[p] * e_g[p]) for p in ps}
    lhs_s = {p: _stack(q[p] * e_g[p], sol_k[p]) for p in ps}
    lhs_u = {p: _stack(kt[p] * jnp.exp(g_last[p] - g_row[p]), kq[p][l:] * decay[p]) for p in ps}
    e_last = {p: jnp.exp(g_last[p]) for p in ps}
    s = [s_scr[h] for h in hs]
    for j in range(cb):
        t = [_dot(lhs_s[(j, h)], s[h]) for h in hs]
        u = [sol_v[(j, h)] - t[h][l:] for h in hs]
        w = [_dot(lhs_u[(j, h)], u[h]) for h in hs]
        s = [e_last[(j, h)] * s[h] + w[h][:HEAD_DIM] for h in hs]
        for h in hs:
            o_ref[j * l:(j + 1) * l, h * HEAD_DIM:(h + 1) * HEAD_DIM] = t[h][:l] + w[h][HEAD_DIM:]
    for h in hs:
        s_scr[h] = s[h]

    @pl.when(gi == ng - 1)
    def _():
        s_out_ref[0] = s_scr[...]


def _gdn_scan(qkv, k_t, proj, gate_t, prow, pcol, s0, b, t, l, cb):
    ng = t // (l * cb)
    w3 = 3 * REC_W
    return pl.pallas_call(
        functools.partial(_gdn_scan_kernel, l=l, cb=cb, ng=ng),
        grid=(b, ng),
        in_specs=[pl.BlockSpec((cb * l, w3), lambda bi, i: (bi * ng + i, 0)),
                  pl.BlockSpec((cb, REC_W, l), lambda bi, i: (bi * ng + i, 0, 0)),
                  pl.BlockSpec((cb * l, LANES), lambda bi, i: (bi * ng + i, COL_GATE // LANES)),
                  pl.BlockSpec((cb, LANES, l), lambda bi, i: (bi * ng + i, 0, 0)),
                  pl.BlockSpec((SUBLANES, LANES), lambda bi, i: (0, 0)),
                  pl.BlockSpec((LANES, SUBLANES), lambda bi, i: (0, 0)),
                  pl.BlockSpec((1, GDN_HEADS, HEAD_DIM, HEAD_DIM), lambda bi, i: (bi, 0, 0, 0))],
        out_specs=[pl.BlockSpec((cb * l, REC_W), lambda bi, i: (bi * ng + i, 0)),
                   pl.BlockSpec((1, GDN_HEADS, HEAD_DIM, HEAD_DIM), lambda bi, i: (bi, 0, 0, 0))],
        out_shape=[jax.ShapeDtypeStruct((b * t, REC_W), F32),
                   jax.ShapeDtypeStruct((b, GDN_HEADS, HEAD_DIM, HEAD_DIM), F32)],
        scratch_shapes=[pltpu.VMEM((GDN_HEADS, HEAD_DIM, HEAD_DIM), F32)],
        compiler_params=_cparams("parallel", "arbitrary"),
        name="gdn_scan",
    )(qkv, k_t, proj, gate_t, prow, pcol, s0)


def _ml_scan_kernel(qkv_ref, kt_ref, gt_ref, gtt_ref, prow_ref, pcol_ref, c0_ref, n0_ref, m0_ref,
                    o_ref, c_out_ref, n_out_ref, m_out_ref, c_scr, n_scr, m_scr, *, l, cb, ng):
    gi = pl.program_id(1)

    @pl.when(gi == 0)
    def _():
        c_scr[...] = c0_ref[0]
        n_scr[...] = n0_ref[0]
        m_scr[...] = m0_ref[0]

    r = lax.broadcasted_iota(jnp.int32, (l, l), 0)
    c = lax.broadcasted_iota(jnp.int32, (l, l), 1)
    tri = r >= c
    lane = lax.broadcasted_iota(jnp.int32, (1, LANES), 1)
    m_all = m_scr[...]
    hs = range(ML_HEADS)
    ps = [(j, h) for j in range(cb) for h in hs]
    kscale = HEAD_DIM ** -0.5
    gates = [_gates(gt_ref[j * l:(j + 1) * l, :], gtt_ref[j], prow_ref, pcol_ref, l) for j in range(cb)]

    def head_cols(j, group, h):
        return qkv_ref[j * l:(j + 1) * l, group * REC_W + h * HEAD_DIM:group * REC_W + (h + 1) * HEAD_DIM]

    q = {p: head_cols(p[0], 0, p[1]) for p in ps}
    k = {p: head_cols(p[0], 1, p[1]) * kscale for p in ps}
    v = {p: head_cols(p[0], 2, p[1]) for p in ps}
    kt = {p: kt_ref[p[0], p[1] * HEAD_DIM:(p[1] + 1) * HEAD_DIM, :] * kscale for p in ps}
    ig_col = {p: gates[p[0]][0][:, GATE_I + p[1]:GATE_I + p[1] + 1] for p in ps}
    ig_row = {p: gates[p[0]][2][GATE_I + p[1]:GATE_I + p[1] + 1, :] for p in ps}
    f_col = {p: gates[p[0]][1][:, GATE_F + p[1]:GATE_F + p[1] + 1] for p in ps}
    f_row = {p: gates[p[0]][3][GATE_F + p[1]:GATE_F + p[1] + 1, :] for p in ps}
    f_last = {p: f_col[p][l - 1:l, :] for p in ps}
    qk = {p: _dot(q[p], kt[p]) for p in ps}
    d = {p: jnp.where(tri, f_col[p] - f_row[p] + ig_row[p], -jnp.inf) for p in ps}
    d_max = {p: jnp.max(d[p], axis=1, keepdims=True) for p in ps}
    m_prev, m_t = {}, {}
    m_run = [m_all[:, h:h + 1] for h in hs]
    for j in range(cb):
        for h in hs:
            m_prev[(j, h)] = m_run[h]
            m_t[(j, h)] = jnp.maximum(f_col[(j, h)] + m_run[h], d_max[(j, h)])
            m_run[h] = m_t[(j, h)][l - 1:l, :]
    m_new = {p: m_t[p][l - 1:l, :] for p in ps}
    w = {p: jnp.exp(d[p] - m_t[p]) * qk[p] for p in ps}
    c_inter = {p: jnp.exp(f_col[p] + m_prev[p] - m_t[p]) for p in ps}
    wv = {p: _dot(w[p], v[p]) for p in ps}
    w_sum = {p: jnp.sum(w[p], axis=1, keepdims=True) for p in ps}
    e_m = {p: jnp.exp(-m_t[p]) for p in ps}
    w_end_row = {p: jnp.exp(f_last[p] - f_row[p] + ig_row[p] - m_new[p]) for p in ps}
    w_end_col = {p: jnp.exp(f_last[p] - f_col[p] + ig_col[p] - m_new[p]) for p in ps}
    c_prev = {p: jnp.exp(f_last[p] + m_prev[p] - m_new[p]) for p in ps}
    dc = {p: _dot(kt[p] * w_end_row[p], v[p]) for p in ps}
    dn = {p: jnp.sum(w_end_col[p] * k[p], axis=0, keepdims=True) for p in ps}
    c_h = [c_scr[h] for h in hs]
    n_h = [n_scr[h:h + 1, :] for h in hs]
    for j in range(cb):
        for h in hs:
            p = (j, h)
            num = c_inter[p] * _dot(q[p], c_h[h]) + wv[p]
            den = c_inter[p] * jnp.sum(q[p] * n_h[h], axis=1, keepdims=True) + w_sum[p]
            o_ref[j * l:(j + 1) * l, h * HEAD_DIM:(h + 1) * HEAD_DIM] = num / jnp.maximum(jnp.abs(den), e_m[p])
            c_h[h] = c_prev[p] * c_h[h] + dc[p]
            n_h[h] = c_prev[p] * n_h[h] + dn[p]
    m_next = m_all
    for h in hs:
        c_scr[h] = c_h[h]
        n_scr[h:h + 1, :] = n_h[h]
        m_next = jnp.where(lane == h, m_run[h], m_next)
    m_scr[...] = m_next

    @pl.when(gi == ng - 1)
    def _():
        c_out_ref[0] = c_scr[...]
        n_out_ref[0] = n_scr[...]
        m_out_ref[0] = m_scr[...]


def _ml_scan(proj, k_t, gate_t, prow, pcol, c0, n0, m0, b, t, l, cb):
    ng = t // (l * cb)
    w3 = 3 * REC_W
    st = lambda bi, i: (bi, 0, 0, 0)
    st3 = lambda bi, i: (bi, 0, 0)
    return pl.pallas_call(
        functools.partial(_ml_scan_kernel, l=l, cb=cb, ng=ng),
        grid=(b, ng),
        in_specs=[pl.BlockSpec((cb * l, w3), lambda bi, i: (bi * ng + i, COL_MQKV // w3)),
                  pl.BlockSpec((cb, REC_W, l), lambda bi, i: (bi * ng + i, 0, 0)),
                  pl.BlockSpec((cb * l, LANES), lambda bi, i: (bi * ng + i, COL_GATE // LANES)),
                  pl.BlockSpec((cb, LANES, l), lambda bi, i: (bi * ng + i, 0, 0)),
                  pl.BlockSpec((SUBLANES, LANES), lambda bi, i: (0, 0)),
                  pl.BlockSpec((LANES, SUBLANES), lambda bi, i: (0, 0)),
                  pl.BlockSpec((1, ML_HEADS, HEAD_DIM, HEAD_DIM), st),
                  pl.BlockSpec((1, SUBLANES, HEAD_DIM), st3),
                  pl.BlockSpec((1, 1, LANES), st3)],
        out_specs=[pl.BlockSpec((cb * l, REC_W), lambda bi, i: (bi * ng + i, 0)),
                   pl.BlockSpec((1, ML_HEADS, HEAD_DIM, HEAD_DIM), st),
                   pl.BlockSpec((1, SUBLANES, HEAD_DIM), st3),
                   pl.BlockSpec((1, 1, LANES), st3)],
        out_shape=[jax.ShapeDtypeStruct((b * t, REC_W), F32),
                   jax.ShapeDtypeStruct((b, ML_HEADS, HEAD_DIM, HEAD_DIM), F32),
                   jax.ShapeDtypeStruct((b, SUBLANES, HEAD_DIM), F32),
                   jax.ShapeDtypeStruct((b, 1, LANES), F32)],
        scratch_shapes=[pltpu.VMEM((ML_HEADS, HEAD_DIM, HEAD_DIM), F32),
                        pltpu.VMEM((SUBLANES, HEAD_DIM), F32),
                        pltpu.VMEM((1, LANES), F32)],
        compiler_params=_cparams("parallel", "arbitrary"),
        name="ml_scan",
    )(proj, k_t, proj, gate_t, prow, pcol, c0, n0, m0)


def _post_kernel(x_ref, osb_ref, og_ref, hm_ref, gz_ref, mo_ref, w_ref, e_ref, gnw_ref, mnw_ref,
                 lg_ref, lb_ref, o_ref, ob_ref):
    e = e_ref[...]
    og = og_ref[...]
    hm = hm_ref[...]
    og_ms = _dot_exact_rhs(og * og, e, 2) * (1.0 / HEAD_DIM)
    hm_ms = _dot_exact_rhs(hm * hm, e, 2) * (1.0 / HEAD_DIM)
    gz = gz_ref[...]
    o_g = og * lax.rsqrt(og_ms + NORM_EPS) * gnw_ref[...] * (gz * _sigmoid(gz))
    o_m = _sigmoid(mo_ref[...]) * (hm * lax.rsqrt(hm_ms + NORM_EPS) * mnw_ref[...])
    mixed = (_dot(osb_ref[...], w_ref[0:SB_W, :])
             + _dot(o_g, w_ref[SB_W:SB_W + REC_W, :])
             + _dot(o_m, w_ref[SB_W + REC_W:SB_W + 2 * REC_W, :]))
    y = _layer_norm(DEEPNORM_ALPHA * x_ref[...] + mixed, lg_ref[...], lb_ref[...])
    o_ref[...] = y
    ob_ref[...] = y.astype(BF16)


def _post(x, osb, og, hm, proj, w_out, e_heads, gnw, mnw, lg, lb, tm):
    n, d = x.shape
    row = lambda i: (i, 0)
    fixed = lambda i: (0, 0)
    return pl.pallas_call(
        _post_kernel,
        grid=(n // tm,),
        in_specs=[pl.BlockSpec((tm, d), row),
                  pl.BlockSpec((tm, SB_W), row),
                  pl.BlockSpec((tm, REC_W), row),
                  pl.BlockSpec((tm, REC_W), row),
                  pl.BlockSpec((tm, REC_W), lambda i: (i, COL_GZ // REC_W)),
                  pl.BlockSpec((tm, REC_W), lambda i: (i, COL_MO // REC_W)),
                  pl.BlockSpec(w_out.shape, fixed),
                  pl.BlockSpec((REC_W, REC_W), fixed),
                  pl.BlockSpec((1, REC_W), fixed),
                  pl.BlockSpec((1, REC_W), fixed),
                  pl.BlockSpec((1, d), fixed),
                  pl.BlockSpec((1, d), fixed)],
        out_specs=[pl.BlockSpec((tm, d), row), pl.BlockSpec((tm, d), row)],
        out_shape=[jax.ShapeDtypeStruct((n, d), F32), jax.ShapeDtypeStruct((n, d), BF16)],
        compiler_params=_cparams("parallel"),
        name="post",
    )(x, osb, og, hm, proj, proj, w_out, e_heads, gnw, mnw, lg, lb)


def _memattn_kernel(x_ref, xb_ref, wq_ref, wo_ref, mk_ref, mv_ref, lg_ref, lb_ref, o_ref, ob_ref):
    d = x_ref.shape[1]
    hd = d // MEM_HEADS
    q = jnp.dot(xb_ref[...], wq_ref[...], preferred_element_type=F32)
    out = None
    for h in range(MEM_HEADS):
        sl = slice(h * hd, (h + 1) * hd)
        s = _dot_nt(q[:, sl], mk_ref[0, :, sl]) * (hd ** -0.5)
        p = jnp.exp(s - jnp.max(s, axis=1, keepdims=True))
        o_h = _dot(p, mv_ref[0, :, sl]) / jnp.sum(p, axis=1, keepdims=True)
        t = _dot(o_h, wo_ref[sl, :])
        out = t if out is None else out + t
    y = _layer_norm(DEEPNORM_ALPHA * x_ref[...] + out, lg_ref[...], lb_ref[...])
    o_ref[...] = y
    ob_ref[...] = y.astype(BF16)


def _memattn(x, xb, w_cq, w_co, mk, mv, lg, lb, b, t, tm):
    n, d = x.shape
    nt = t // tm
    nm = mk.shape[1]
    row = lambda bi, i: (bi * nt + i, 0)
    fixed = lambda bi, i: (0, 0)
    return pl.pallas_call(
        _memattn_kernel,
        grid=(b, nt),
        in_specs=[pl.BlockSpec((tm, d), row),
                  pl.BlockSpec((tm, d), row),
                  pl.BlockSpec((d, d), fixed),
                  pl.BlockSpec((d, d), fixed),
                  pl.BlockSpec((1, nm, d), lambda bi, i: (bi, 0, 0)),
                  pl.BlockSpec((1, nm, d), lambda bi, i: (bi, 0, 0)),
                  pl.BlockSpec((1, d), fixed),
                  pl.BlockSpec((1, d), fixed)],
        out_specs=[pl.BlockSpec((tm, d), row), pl.BlockSpec((tm, d), row)],
        out_shape=[jax.ShapeDtypeStruct((n, d), F32), jax.ShapeDtypeStruct((n, d), BF16)],
        compiler_params=_cparams("parallel", "parallel"),
        name="memattn",
    )(x, xb, w_cq, w_co, mk, mv, lg, lb)


def _ffn_kernel(x_ref, xb_ref, wg_ref, wu_ref, wd_ref, lg_ref, lb_ref, o_ref, ob_ref, acc_ref, *, nf):
    j = pl.program_id(1)

    @pl.when(j == 0)
    def _():
        acc_ref[...] = jnp.zeros_like(acc_ref)

    xb = xb_ref[...]
    gate = jnp.dot(xb, wg_ref[...], preferred_element_type=F32)
    up = jnp.dot(xb, wu_ref[...], preferred_element_type=F32)
    acc_ref[...] += _dot(gate * _sigmoid(gate) * up, wd_ref[...])

    @pl.when(j == nf - 1)
    def _():
        y = _layer_norm(DEEPNORM_ALPHA * x_ref[...] + acc_ref[...], lg_ref[...], lb_ref[...])
        o_ref[...] = y
        ob_ref[...] = y.astype(BF16)


def _ffn(x, xb, w_up, w_down, lg, lb, tm, tf):
    n, d = x.shape
    dff = w_down.shape[0]
    nf = dff // tf
    row = lambda i, j: (i, 0)
    fixed = lambda i, j: (0, 0)
    return pl.pallas_call(
        functools.partial(_ffn_kernel, nf=nf),
        grid=(n // tm, nf),
        in_specs=[pl.BlockSpec((tm, d), row),
                  pl.BlockSpec((tm, d), row),
                  pl.BlockSpec((d, tf), lambda i, j: (0, j)),
                  pl.BlockSpec((d, tf), lambda i, j: (0, nf + j)),
                  pl.BlockSpec((tf, d), lambda i, j: (j, 0)),
                  pl.BlockSpec((1, d), fixed),
                  pl.BlockSpec((1, d), fixed)],
        out_specs=[pl.BlockSpec((tm, d), row), pl.BlockSpec((tm, d), row)],
        out_shape=[jax.ShapeDtypeStruct((n, d), F32), jax.ShapeDtypeStruct((n, d), BF16)],
        scratch_shapes=[pltpu.VMEM((tm, d), F32)],
        compiler_params=_cparams("parallel", "arbitrary"),
        name="ffn",
    )(x, xb, w_up, w_up, w_down, lg, lb)


def _pack_layer(p):
    w_in = p['w_in']
    d = w_in.shape[0]
    o_sb, o_gqkv, o_gz = 0, 3 * SB_W, 3 * SB_W + 3 * REC_W
    o_gb = o_gz + REC_W
    o_mqkv = o_gb + 2 * GDN_HEADS
    o_mo = o_mqkv + 3 * REC_W
    o_mi = o_mo + REC_W
    gate_cols = jnp.concatenate([w_in[:, o_gb:o_gb + 2 * GDN_HEADS], w_in[:, o_mi:o_mi + 2 * ML_HEADS]], axis=1)
    used = COL_GATE + gate_cols.shape[1]
    w_al = jnp.concatenate([w_in[:, o_gqkv:o_gqkv + 3 * REC_W], w_in[:, o_mqkv:o_mqkv + 3 * REC_W],
                            w_in[:, o_gz:o_gz + REC_W], w_in[:, o_mo:o_mo + REC_W],
                            w_in[:, o_sb:o_sb + 3 * SB_W], gate_cols,
                            jnp.zeros((d, PROJ_W - used), w_in.dtype)], axis=1).astype(BF16)
    prow = jnp.zeros((SUBLANES, LANES), F32)
    prow = prow.at[0, GATE_G:GATE_G + GDN_HEADS].set(p['gdn_dt_bias'])
    prow = prow.at[0, GATE_I:GATE_I + ML_HEADS].set(p['mlstm_i_bias'])
    prow = prow.at[0, GATE_F:GATE_F + ML_HEADS].set(p['mlstm_f_bias'])
    prow = prow.at[1, GATE_G:GATE_G + GDN_HEADS].set(p['gdn_A_log'])
    conv_w8 = jnp.zeros((SUBLANES, 3 * REC_W), F32).at[0:CONV_W].set(p['gdn_conv_w'])
    return dict(
        w_al=w_al, prow=prow, pcol=prow.T, conv_w8=conv_w8,
        gnw=jnp.tile(p['gdn_norm_w'], GDN_HEADS)[None, :], mnw=p['mlstm_norm_w'][None, :],
        w_out=p['w_out'].astype(BF16), w_cq=p['w_cq'].astype(BF16), w_co=p['w_co'].astype(BF16),
        w_up=p['w_up'].astype(BF16), w_down=p['w_down'].astype(BF16),
        ln1=(p['ln1_g'][None, :], p['ln1_b'][None, :]), ln2=(p['ln2_g'][None, :], p['ln2_b'][None, :]),
        ln3=(p['ln3_g'][None, :], p['ln3_b'][None, :]))


def _head_consts():
    hid = np.arange(REC_W) // HEAD_DIM
    return jnp.asarray(hid[:, None] == hid[None, :], BF16)


def _sb_layout(k, v, tk):
    b, tkv, _ = k.shape
    nkb = tkv // tk
    kth = k.reshape(b, nkb, tk, SB_HEADS, HEAD_DIM).transpose(0, 3, 1, 4, 2)
    kth = kth.reshape(b * SB_HEADS, nkb, HEAD_DIM, tk)
    vh = v.reshape(b, nkb, tk, SB_HEADS, HEAD_DIM).transpose(0, 3, 1, 2, 4)
    vh = vh.reshape(b * SB_HEADS, nkb, tk, HEAD_DIM)
    return kth.astype(BF16), vh.astype(BF16)


def _trunk_layer(x, xb, pk, e_heads, mk, mv, b, t, cfg, sb_past, conv_ctx, gdn_s, ml_c, ml_n, ml_m):
    n, d = x.shape
    l = min(t, CHUNK)
    nc = t // l
    proj = _matmul(xb, pk['w_al'], cfg['tm'], 1024)
    sk =proj[:, COL_SB + SB_W:COL_SB + 2 * SB_W].reshape(b, t, SB_W)
    sv = proj[:, COL_SB + 2 * SB_W:COL_SB + 3 * SB_W].reshape(b, t, SB_W)
    g_qkv = proj[:, COL_GQKV:COL_GQKV + 3 * REC_W].reshape(b, t, 3 * REC_W)

    tk = cfg['sb_tk']
    if sb_past is None:
        q_off, kk, vv = 0, sk, sv
    else:
        past_k, past_v = sb_past
        q_off = past_k.shape[1]
        kv_len = q_off + t
        kv_pad = -(-kv_len // tk) * tk - kv_len
        kk = jnp.pad(jnp.concatenate([past_k.reshape(b, q_off, SB_W), sk], axis=1), ((0, 0), (0, kv_pad), (0, 0)))
        vv = jnp.pad(jnp.concatenate([past_v.reshape(b, q_off, SB_W), sv], axis=1), ((0, 0), (0, kv_pad), (0, 0)))
    kth, vh = _sb_layout(kk, vv, tk)
    o_sb = _sb_attn(proj, kth, vh, b, t, SB_HEADS, cfg['sb_tq'], q_off)

    ctx8 = jnp.pad(conv_ctx, ((0, 0), (SUBLANES - (CONV_W - 1), 0), (0, 0)))
    gqkv_act = _gdn_pre(proj, ctx8, pk['conv_w8'], e_heads, b, t, cfg['conv_tt'])
    gate_t = proj[:, COL_GATE:COL_GATE + LANES].reshape(b * nc, l, LANES).transpose(0, 2, 1)
    gk_t = gqkv_act[:, REC_W:2 * REC_W].reshape(b * nc, l, REC_W).transpose(0, 2, 1)
    cb = min(cfg['scan_cb'], nc)
    o_g, gdn_s_new = _gdn_scan(gqkv_act, gk_t, proj, gate_t, pk['prow'], pk['pcol'], gdn_s, b, t, l, cb)
    n0 = jnp.pad(ml_n, ((0, 0), (0, SUBLANES - ML_HEADS), (0, 0)))
    m0 = jnp.pad(ml_m, ((0, 0), (0, LANES - ML_HEADS)))[:, None, :]
    mk_t = proj[:, COL_MQKV + REC_W:COL_MQKV + 2 * REC_W].reshape(b * nc, l, REC_W).transpose(0, 2, 1)
    h_m, c_new, n_new, m_new = _ml_scan(proj, mk_t, gate_t, pk['prow'], pk['pcol'], ml_c, n0, m0, b, t, l, cb)

    x1, x1b = _post(x, o_sb, o_g, h_m, proj, pk['w_out'], e_heads, pk['gnw'], pk['mnw'], *pk['ln1'], cfg['tm'])
    x2, x2b = _memattn(x1, x1b, pk['w_cq'], pk['w_co'], mk, mv, *pk['ln2'], b, t, cfg['mem_tm'])
    x3, x3b = _ffn(x2, x2b, pk['w_up'], pk['w_down'], *pk['ln3'], cfg['tm'], cfg['ffn_tf'])

    new_conv = g_qkv[:, t - (CONV_W - 1):, :]
    return (x3, x3b, sk.reshape(b, t, SB_HEADS, HEAD_DIM), sv.reshape(b, t, SB_HEADS, HEAD_DIM), new_conv,
            gdn_s_new, c_new, n_new[:, :ML_HEADS, :], m_new[:, 0, :ML_HEADS])


def kernel(x_prompt, x_sample, cache_sb_k, cache_sb_v, cache_gdn_conv, state_gdn, state_mlstm_C, state_mlstm_n, state_mlstm_m, cache_mem_k, cache_mem_v, mem_prompt, w_in, gdn_conv_w, gdn_A_log, gdn_dt_bias, gdn_norm_w, mlstm_i_bias, mlstm_f_bias, mlstm_norm_w, w_out, ln1_g, ln1_b, w_cq, w_ckv, w_co, ln2_g, ln2_b, w_up, w_down, ln3_g, ln3_b):
    bp, tp, d = x_prompt.shape
    bs, ts, _ = x_sample.shape
    depth = w_in.shape[0]
    n_mem = mem_prompt.shape[1]
    dff = w_down.shape[1]
    cfg_p = dict(tm=512, sb_tq=1024, sb_tk=256, conv_tt=512, mem_tm=512, ffn_tf=dff // 2, scan_cb=4)
    cfg_s = dict(tm=bs * ts, sb_tq=ts, sb_tk=256, conv_tt=ts, mem_tm=ts, ffn_tf=dff // 2, scan_cb=1)
    e_heads = _head_consts()

    xp = x_prompt.reshape(bp * tp, d)
    xs = x_sample.reshape(bs * ts, d)
    xpb, xsb = xp.astype(BF16), xs.astype(BF16)
    mem_b = mem_prompt.reshape(bp * n_mem, d).astype(BF16)
    p_out = [[] for _ in range(9)]
    s_out = [[] for _ in range(7)]
    for li in range(depth):
        p = {'w_in': w_in[li], 'gdn_conv_w': gdn_conv_w[li], 'gdn_A_log': gdn_A_log[li],
             'gdn_dt_bias': gdn_dt_bias[li], 'gdn_norm_w': gdn_norm_w[li], 'mlstm_i_bias': mlstm_i_bias[li],
             'mlstm_f_bias': mlstm_f_bias[li], 'mlstm_norm_w': mlstm_norm_w[li], 'w_out': w_out[li],
             'ln1_g': ln1_g[li], 'ln1_b': ln1_b[li], 'w_cq': w_cq[li], 'w_co': w_co[li],
             'ln2_g': ln2_g[li], 'ln2_b': ln2_b[li], 'w_up': w_up[li], 'w_down': w_down[li],
             'ln3_g': ln3_g[li], 'ln3_b': ln3_b[li]}
        pk = _pack_layer(p)
        mkv = _matmul(mem_b, w_ckv[li].astype(BF16), n_mem, 1024)
        mk_p = mkv[:, :d].reshape(bp, n_mem, d)
        mv_p = mkv[:, d:].reshape(bp, n_mem, d)
        res = _trunk_layer(xp, xpb, pk, e_heads, mk_p.astype(BF16), mv_p.astype(BF16), bp, tp, cfg_p, None,
                           jnp.zeros((bp, CONV_W - 1, 3 * REC_W), F32),
                           jnp.zeros((bp, GDN_HEADS, HEAD_DIM, HEAD_DIM), F32),
                           jnp.zeros((bp, ML_HEADS, HEAD_DIM, HEAD_DIM), F32),
                           jnp.zeros((bp, ML_HEADS, HEAD_DIM), F32),
                           jnp.zeros((bp, ML_HEADS), F32))
        xp, xpb = res[0], res[1]
        hd = d // MEM_HEADS
        for j, a in enumerate(res[2:] + (mk_p.reshape(bp, n_mem, MEM_HEADS, hd), mv_p.reshape(bp, n_mem, MEM_HEADS, hd))):
            p_out[j].append(a)
        res = _trunk_layer(xs, xsb, pk, e_heads,
                           cache_mem_k[li].reshape(bs, n_mem, d).astype(BF16),
                           cache_mem_v[li].reshape(bs, n_mem, d).astype(BF16),
                           bs, ts, cfg_s, (cache_sb_k[li], cache_sb_v[li]), cache_gdn_conv[li],
                           state_gdn[li], state_mlstm_C[li], state_mlstm_n[li], state_mlstm_m[li])
        xs, xsb = res[0], res[1]
        for j, a in enumerate(res[2:]):
            s_out[j].append(a)
    p_st = [jnp.stack(a) for a in p_out]
    s_st = [jnp.stack(a) for a in s_out]
    return (xp.reshape(bp, tp, d), xs.reshape(bs, ts, d), *p_st, *s_st)
```

```python
import functools

import numpy as np
import jax
import jax.numpy as jnp
from jax import lax
from jax.experimental import pallas as pl
from jax.experimental.pallas import tpu as pltpu

F32 = jnp.float32
BF16 = jnp.bfloat16

HEAD_DIM = 64
SB_HEADS = 4
GDN_HEADS = 6
ML_HEADS = 6
REC_W = GDN_HEADS * HEAD_DIM
SB_W = SB_HEADS * HEAD_DIM
CONV_W = 4
CHUNK = 64
MEM_HEADS = 4
LN_EPS = 1e-5
NORM_EPS = 1e-6
DEPTH = 2
DEEPNORM_ALPHA = (2 * DEPTH) ** 0.25
LOG2E = float(np.log2(np.e))
SOFTPLUS2_CLAMP = 120.0

GROUP = 256
LANES = 128
SUBLANES = 8
VMEM_LIMIT = 56 * 1024 * 1024

PROJ_W = 4096
COL_GQKV = 0
COL_MQKV = 1152
COL_GZ = 2304
COL_MO = 2688
COL_SB = 3072
COL_GATE = 3840
GATE_BETA, GATE_G, GATE_I, GATE_F = 0, 6, 12, 18


def _cparams(*sem):
    return pltpu.CompilerParams(dimension_semantics=sem, vmem_limit_bytes=VMEM_LIMIT)


def _dot(a, b):
    return jnp.dot(a.astype(BF16), b.astype(BF16), preferred_element_type=F32)


def _dot_nt(a, b):
    return lax.dot_general(a.astype(BF16), b.astype(BF16), (((1,), (1,)), ((), ())),
                           preferred_element_type=F32)


def _dot_tn(a, b):
    return lax.dot_general(a.astype(BF16), b.astype(BF16), (((0,), (0,)), ((), ())),
                           preferred_element_type=F32)


def _split(x, parts):
    out = []
    r = x
    for _ in range(parts - 1):
        p = r.astype(BF16)
        out.append(p)
        r = r - p.astype(F32)
    out.append(r.astype(BF16))
    return out


def _dot_exact_lhs(m, x, parts):
    acc = None
    for p in _split(x, parts):
        t = jnp.dot(m, p, preferred_element_type=F32)
        acc = t if acc is None else acc + t
    return acc


def _dot_exact_rhs(x, m, parts):
    acc = None
    for p in _split(x, parts):
        t = jnp.dot(p, m, preferred_element_type=F32)
        acc = t if acc is None else acc + t
    return acc


def _ones_where(mask, dtype):
    return jnp.where(mask, 1.0, 0.0).astype(dtype)


def _sigmoid(x):
    return 1.0 / (1.0 + jnp.exp(-x))


def _layer_norm(y, g, b):
    mu = jnp.mean(y, axis=-1, keepdims=True)
    d = y - mu
    var = jnp.mean(d * d, axis=-1, keepdims=True)
    return d * lax.rsqrt(var + LN_EPS) * g + b


def _mm_kernel(x_ref, w_ref, o_ref):
    o_ref[...] = jnp.dot(x_ref[...], w_ref[...], preferred_element_type=F32).astype(o_ref.dtype)


def _matmul(x, w, tm, tn, out_dtype=F32):
    n, k = x.shape
    m = w.shape[1]
    return pl.pallas_call(
        _mm_kernel,
        grid=(n // tm, m // tn),
        in_specs=[pl.BlockSpec((tm, k), lambda i, j: (i, 0)),
                  pl.BlockSpec((k, tn), lambda i, j: (0, j))],
        out_specs=pl.BlockSpec((tm, tn), lambda i, j: (i, j)),
        out_shape=jax.ShapeDtypeStruct((n, m), out_dtype),
        compiler_params=_cparams("parallel", "parallel"),
        name="proj_matmul",
    )(x, w)


def _proj_kernel(x_ref, w_ref, o_ref, skt_ref, sv_ref, mkt_ref, gtt_ref, *, tm, tn):
    xb = x_ref[...].astype(BF16)
    for j in range(PROJ_W // tn):
        o_ref[:, j * tn:(j + 1) * tn] = jnp.dot(xb, w_ref[:, j * tn:(j + 1) * tn], preferred_element_type=F32)
    for r in range(tm // GROUP):
        rows = slice(r * GROUP, (r + 1) * GROUP)
        skt_ref[r] = o_ref[rows, COL_SB + SB_W:COL_SB + 2 * SB_W].T.astype(BF16)
        for h in range(SB_HEADS):
            c0 = COL_SB + 2 * SB_W + h * HEAD_DIM
            sv_ref[h, r] = o_ref[rows, c0:c0 + HEAD_DIM].astype(BF16)
        _store_chunks_t(mkt_ref, r, o_ref[rows, COL_MQKV + REC_W:COL_MQKV + 2 * REC_W])
        _store_chunks_t(gtt_ref, r, o_ref[rows, COL_GATE:COL_GATE + LANES])


def _store_chunks_t(ref, r, a):
    at = a.T
    per = GROUP // CHUNK
    for j in range(per):
        ref[r * per + j] = at[:, j * CHUNK:(j + 1) * CHUNK]


def _proj(x, w, tm):
    n, d = x.shape
    ng = n // GROUP
    gpt = tm // GROUP
    cpt = tm // CHUNK
    return pl.pallas_call(
        functools.partial(_proj_kernel, tm=tm, tn=1024),
        grid=(n // tm,),
        in_specs=[pl.BlockSpec((tm, d), lambda i: (i, 0)),
                  pl.BlockSpec((d, PROJ_W), lambda i: (0, 0), pipeline_mode=pl.Buffered(1))],
        out_specs=[pl.BlockSpec((tm, PROJ_W), lambda i: (i, 0)),
                   pl.BlockSpec((gpt, SB_W, GROUP), lambda i: (i, 0, 0)),
                   pl.BlockSpec((SB_HEADS, gpt, GROUP, HEAD_DIM), lambda i: (0, i, 0, 0)),
                   pl.BlockSpec((cpt, REC_W, CHUNK), lambda i: (i, 0, 0)),
                   pl.BlockSpec((cpt, LANES, CHUNK), lambda i: (i, 0, 0))],
        out_shape=[jax.ShapeDtypeStruct((n, PROJ_W), F32),
                   jax.ShapeDtypeStruct((ng, SB_W, GROUP), BF16),
                   jax.ShapeDtypeStruct((SB_HEADS, ng, GROUP, HEAD_DIM), BF16),
                   jax.ShapeDtypeStruct((n // CHUNK, REC_W, CHUNK), F32),
                   jax.ShapeDtypeStruct((n // CHUNK, LANES, CHUNK), F32)],
        compiler_params=_cparams("parallel"),
        name="proj_full",
    )(x, w)


def _gdn_pre_kernel(x_ref, ctx_ref, w_ref, e_ref, o_ref, *rest, tt):
    xbuf = rest[-1]
    i = pl.program_id(1)

    @pl.when(i == 0)
    def _():
        xbuf[0:SUBLANES, :] = ctx_ref[0]

    xbuf[SUBLANES:SUBLANES + tt, :] = x_ref[...]
    acc = w_ref[CONV_W - 1:CONV_W, :] * xbuf[SUBLANES:SUBLANES + tt, :]
    for j in range(CONV_W - 1):
        off = SUBLANES - (CONV_W - 1) + j
        acc = acc + w_ref[j:j + 1, :] * xbuf[off:off + tt, :]
    y = acc * _sigmoid(acc)
    q = y[:, 0:REC_W]
    k = y[:, REC_W:2 * REC_W]
    e = e_ref[...]
    qs = _dot_exact_rhs(q * q, e, 2)
    ks = _dot_exact_rhs(k * k, e, 2)
    kn = k * lax.rsqrt(ks + NORM_EPS)
    o_ref[:, 0:REC_W] = q * lax.rsqrt(qs + NORM_EPS) * (HEAD_DIM ** -0.5)
    o_ref[:, REC_W:2 * REC_W] = kn
    o_ref[:, 2 * REC_W:3 * REC_W] = y[:, 2 * REC_W:3 * REC_W]
    if len(rest) == 2:
        for r in range(tt // GROUP):
            _store_chunks_t(rest[0], r, kn[r * GROUP:(r + 1) * GROUP, :])
    xbuf[0:SUBLANES, :] = xbuf[tt:tt + SUBLANES, :]


def _gdn_pre(proj, ctx8, conv_w8, e_heads, b, t, tt):
    w3 = 3 * REC_W
    nt = t // tt
    emit_kt = tt % GROUP == 0
    out_specs = [pl.BlockSpec((tt, w3), lambda bi, i: (bi * nt + i, 0))]
    out_shape = [jax.ShapeDtypeStruct((b * t, w3), F32)]
    if emit_kt:
        out_specs.append(pl.BlockSpec((tt // CHUNK, REC_W, CHUNK), lambda bi, i: (bi * nt + i, 0, 0)))
        out_shape.append(jax.ShapeDtypeStruct((b * t // CHUNK, REC_W, CHUNK), F32))
    res = pl.pallas_call(
        functools.partial(_gdn_pre_kernel, tt=tt),
        grid=(b, nt),
        in_specs=[pl.BlockSpec((tt, w3), lambda bi, i: (bi * nt + i, COL_GQKV // w3)),
                  pl.BlockSpec((1, SUBLANES, w3), lambda bi, i: (bi, 0, 0)),
                  pl.BlockSpec((SUBLANES, w3), lambda bi, i: (0, 0)),
                  pl.BlockSpec((REC_W, REC_W), lambda bi, i: (0, 0))],
        out_specs=out_specs,
        out_shape=out_shape,
        scratch_shapes=[pltpu.VMEM((tt + SUBLANES, w3), F32)],
        compiler_params=_cparams("parallel", "arbitrary"),
        name="gdn_pre",
    )(proj, ctx8, conv_w8, e_heads)
    return (res[0], res[1]) if emit_kt else (res[0], None)


def _sb_kernel(q_ref, kt_ref, v_ref, o_ref, acc_ref, c_ref, *, hb, tq, tk, nkb, q_off):
    qi = pl.program_id(1)
    q_lo = q_off + qi * tq
    nblk = jnp.minimum(nkb, (q_lo + tq - 1 + tk - 1) // tk)
    nfull = jnp.minimum(nblk, q_lo // tk)
    scale = HEAD_DIM ** -0.5 * LOG2E
    qs = [(q_ref[:, h * HEAD_DIM:(h + 1) * HEAD_DIM] * scale).astype(BF16) for h in range(hb)]
    lower_incl = _ones_where(lax.broadcasted_iota(jnp.int32, (tk, tk), 0)
                             >= lax.broadcasted_iota(jnp.int32, (tk, tk), 1), BF16)
    acc_ref[...] = jnp.zeros_like(acc_ref)
    c_ref[...] = jnp.zeros_like(c_ref)

    def block(kb, masked):
        if masked:
            valid = (kb * tk + lax.broadcasted_iota(jnp.int32, (tq, tk), 1)
                     < q_lo + lax.broadcasted_iota(jnp.int32, (tq, tk), 0))
        for h in range(hb):
            kt = kt_ref[0, kb, h * HEAD_DIM:(h + 1) * HEAD_DIM, :]
            z = jnp.dot(qs[h], kt, preferred_element_type=F32)
            lneg = jnp.maximum(z, jnp.log2(1.0 + jnp.exp2(jnp.minimum(z, SOFTPLUS2_CLAMP))))
            if masked:
                lneg = jnp.where(valid, lneg, 0.0)
            incl = jnp.dot(lneg.astype(BF16), lower_incl, preferred_element_type=F32)
            c = c_ref[h]
            a = jnp.exp2(z - incl - jnp.tile(c, (1, tk // LANES)))
            if masked:
                a = jnp.where(valid, a, 0.0)
            acc_ref[h] += jnp.dot(a.astype(BF16), v_ref[h, kb], preferred_element_type=F32)
            c_ref[h] = c + incl[:, 0:1]

    @pl.loop(0, nblk - nfull)
    def _(i):
        block(nblk - 1 - i, True)

    @pl.loop(0, nfull)
    def _(i):
        block(nfull - 1 - i, False)

    for h in range(hb):
        o_ref[:, h * HEAD_DIM:(h + 1) * HEAD_DIM] = acc_ref[h]


def _sb_attn(proj, kt, vh, b, t, hb, tq, q_off):
    _, nkb, w, tk = kt.shape
    d = w // hb
    nq = t // tq
    return pl.pallas_call(
        functools.partial(_sb_kernel, hb=hb, tq=tq, tk=tk, nkb=nkb, q_off=q_off),
        grid=(b, nq),
        in_specs=[pl.BlockSpec((tq, w), lambda bi, i: (bi * nq + i, COL_SB // w)),
                  pl.BlockSpec((1, nkb, w, tk), lambda bi, i: (bi, 0, 0, 0), pipeline_mode=pl.Buffered(1)),
                  pl.BlockSpec((hb, nkb, tk, d), lambda bi, i: (bi, 0, 0, 0), pipeline_mode=pl.Buffered(1))],
        out_specs=pl.BlockSpec((tq, w), lambda bi, i: (bi * nq + i, 0)),
        out_shape=jax.ShapeDtypeStruct((b * t, w), F32),
        scratch_shapes=[pltpu.VMEM((hb, tq, d), F32), pltpu.VMEM((hb, tq, LANES), F32)],
        compiler_params=_cparams("parallel", "parallel"),
        name="sb_attn",
    )(proj, kt, vh)


def _gate_values(pre, neg_a, lane_id):
    sp = jnp.log1p(jnp.exp(-jnp.abs(pre)))
    softplus = jnp.maximum(pre, 0.0) + sp
    log_sig = jnp.minimum(pre, 0.0) - sp
    return jnp.where(lane_id < GATE_G, _sigmoid(pre),
                     jnp.where(lane_id < GATE_I, neg_a * softplus,
                               jnp.where(lane_id < GATE_F, pre, log_sig)))


def _stack(a, b):
    return jnp.concatenate([a, b], axis=0)


def _gates(gt, gtt, prow_ref, pcol_ref, l):
    lane = lax.broadcasted_iota(jnp.int32, (l, LANES), 1)
    val = _gate_values(gt + prow_ref[0:1, :], -jnp.exp(prow_ref[1:2, :]), lane)
    sub = lax.broadcasted_iota(jnp.int32, (LANES, l), 0)
    val_t = _gate_values(gtt + pcol_ref[:, 0:1], -jnp.exp(pcol_ref[:, 1:2]), sub)
    r = lax.broadcasted_iota(jnp.int32, (l, l), 0)
    c = lax.broadcasted_iota(jnp.int32, (l, l), 1)
    csum = _dot_exact_lhs(_ones_where(r >= c, BF16), val, 3)
    csum_t = _dot_exact_rhs(val_t, _ones_where(r <= c, BF16), 3)
    return val, csum, val_t, csum_t


def _gdn_scan_kernel(qkv_ref, kt_ref, gt_ref, gtt_ref, prow_ref, pcol_ref, s0_ref, o_ref, s_out_ref, s_scr,
                     *, l, cb, ng):
    gi = pl.program_id(1)

    @pl.when(gi == 0)
    def _():
        s_scr[...] = s0_ref[0]

    r = lax.broadcasted_iota(jnp.int32, (l, l), 0)
    c = lax.broadcasted_iota(jnp.int32, (l, l), 1)
    tri = r >= c
    stri = r > c
    eye = _ones_where(r == c, F32)
    n_double = max(int(np.ceil(np.log2(l))) - 1, 0)
    hs = range(GDN_HEADS)
    ps = [(j, h) for j in range(cb) for h in hs]
    gates = [_gates(gt_ref[j * l:(j + 1) * l, :], gtt_ref[j], prow_ref, pcol_ref, l) for j in range(cb)]

    def head_cols(j, group, h):
        return qkv_ref[j * l:(j + 1) * l, group * REC_W + h * HEAD_DIM:group * REC_W + (h + 1) * HEAD_DIM]

    q = {p: head_cols(p[0], 0, p[1]) for p in ps}
    k = {p: head_cols(p[0], 1, p[1]) for p in ps}
    v = {p: head_cols(p[0], 2, p[1]) for p in ps}
    kt = {p: kt_ref[p[0], p[1] * HEAD_DIM:(p[1] + 1) * HEAD_DIM, :] for p in ps}
    beta = {p: gates[p[0]][0][:, GATE_BETA + p[1]:GATE_BETA + p[1] + 1] for p in ps}
    g_col = {p: gates[p[0]][1][:, GATE_G + p[1]:GATE_G + p[1] + 1] for p in ps}
    g_row = {p: gates[p[0]][3][GATE_G + p[1]:GATE_G + p[1] + 1, :] for p in ps}
    g_last = {p: g_col[p][l - 1:l, :] for p in ps}
    decay = {p: jnp.where(tri, jnp.exp(jnp.where(tri, g_col[p] - g_row[p], 0.0)), 0.0) for p in ps}
    kb = {p: k[p] * beta[p] for p in ps}
    e_g = {p: jnp.exp(g_col[p]) for p in ps}
    kq = {p: _dot(_stack(kb[p], q[p]), kt[p]) for p in ps}
    x = {p: -jnp.where(stri, kq[p][:l] * decay[p], 0.0) for p in ps}
    tinv = {p: eye + x[p] for p in ps}
    if n_double > 0:
        x = {p: _dot(x[p], x[p]) for p in ps}
        for _ in range(n_double - 1):
            tx = {p: _dot(_stack(tinv[p], x[p]), x[p]) for p in ps}
            tinv = {p: tinv[p] + tx[p][:l] for p in ps}
            x = {p: tx[p][l:] for p in ps}
        tinv = {p: tinv[p] + _dot(tinv[p], x[p]) for p in ps}
    sol_v = {p: _dot(tinv[p], v[p] * beta[p]) for p in ps}
    sol_k = {p: _dot(tinv[p], kb[p] * e_g[p]) for p in ps}
    lhs_s = {p: _stack(q[p] * e_g[p], sol_k[p]) for p in ps}
    lhs_u = {p: _stack(kt[p] * jnp.exp(g_last[p] - g_row[p]), kq[p][l:] * decay[p]) for p in ps}
    e_last = {p: jnp.exp(g_last[p]) for p in ps}
    s = [s_scr[h] for h in hs]
    for j in range(cb):
        t = [_dot(lhs_s[(j, h)], s[h]) for h in hs]
        u = [sol_v[(j, h)] - t[h][l:] for h in hs]
        w = [_dot(lhs_u[(j, h)], u[h]) for h in hs]
        s = [e_last[(j, h)] * s[h] + w[h][:HEAD_DIM] for h in hs]
        for h in hs:
            o_ref[j * l:(j + 1) * l, h * HEAD_DIM:(h + 1) * HEAD_DIM] = t[h][:l] + w[h][HEAD_DIM:]
    for h in hs:
        s_scr[h] = s[h]

    @pl.when(gi == ng - 1)
    def _():
        s_out_ref[0] = s_scr[...]


def _gdn_scan(qkv, k_t, proj, gate_t, prow, pcol, s0, b, t, l, cb):
    ng = t // (l * cb)
    w3 = 3 * REC_W
    return pl.pallas_call(
        functools.partial(_gdn_scan_kernel, l=l, cb=cb, ng=ng),
        grid=(b, ng),
        in_specs=[pl.BlockSpec((cb * l, w3), lambda bi, i: (bi * ng + i, 0)),
                  pl.BlockSpec((cb, REC_W, l), lambda bi, i: (bi * ng + i, 0, 0)),
                  pl.BlockSpec((cb * l, LANES), lambda bi, i: (bi * ng + i, COL_GATE // LANES)),
                  pl.BlockSpec((cb, LANES, l), lambda bi, i: (bi * ng + i, 0, 0)),
                  pl.BlockSpec((SUBLANES, LANES), lambda bi, i: (0, 0)),
                  pl.BlockSpec((LANES, SUBLANES), lambda bi, i: (0, 0)),
                  pl.BlockSpec((1, GDN_HEADS, HEAD_DIM, HEAD_DIM), lambda bi, i: (bi, 0, 0, 0))],
        out_specs=[pl.BlockSpec((cb * l, REC_W), lambda bi, i: (bi * ng + i, 0)),
                   pl.BlockSpec((1, GDN_HEADS, HEAD_DIM, HEAD_DIM), lambda bi, i: (bi, 0, 0, 0))],
        out_shape=[jax.ShapeDtypeStruct((b * t, REC_W), F32),
                   jax.ShapeDtypeStruct((b, GDN_HEADS, HEAD_DIM, HEAD_DIM), F32)],
        scratch_shapes=[pltpu.VMEM((GDN_HEADS, HEAD_DIM, HEAD_DIM), F32)],
        compiler_params=_cparams("parallel", "arbitrary"),
        name="gdn_scan",
    )(qkv, k_t, proj, gate_t, prow, pcol, s0)


def _ml_scan_kernel(qkv_ref, kt_ref, gt_ref, gtt_ref, prow_ref, pcol_ref, c0_ref, n0_ref, m0_ref,
                    o_ref, c_out_ref, n_out_ref, m_out_ref, c_scr, n_scr, m_scr, *, l, cb, ng):
    gi = pl.program_id(1)

    @pl.when(gi == 0)
    def _():
        c_scr[...] = c0_ref[0]
        n_scr[...] = n0_ref[0]
        m_scr[...] = m0_ref[0]

    r = lax.broadcasted_iota(jnp.int32, (l, l), 0)
    c = lax.broadcasted_iota(jnp.int32, (l, l), 1)
    tri = r >= c
    lane = lax.broadcasted_iota(jnp.int32, (1, LANES), 1)
    m_all = m_scr[...]
    hs = range(ML_HEADS)
    ps = [(j, h) for j in range(cb) for h in hs]
    kscale = HEAD_DIM ** -0.5
    gates = [_gates(gt_ref[j * l:(j + 1) * l, :], gtt_ref[j], prow_ref, pcol_ref, l) for j in range(cb)]

    def head_cols(j, group, h):
        return qkv_ref[j * l:(j + 1) * l, group * REC_W + h * HEAD_DIM:group * REC_W + (h + 1) * HEAD_DIM]

    q = {p: head_cols(p[0], 0, p[1]) for p in ps}
    k = {p: head_cols(p[0], 1, p[1]) * kscale for p in ps}
    v = {p: head_cols(p[0], 2, p[1]) for p in ps}
    kt = {p: kt_ref[p[0], p[1] * HEAD_DIM:(p[1] + 1) * HEAD_DIM, :] * kscale for p in ps}
    ig_col = {p: gates[p[0]][0][:, GATE_I + p[1]:GATE_I + p[1] + 1] for p in ps}
    ig_row = {p: gates[p[0]][2][GATE_I + p[1]:GATE_I + p[1] + 1, :] for p in ps}
    f_col = {p: gates[p[0]][1][:, GATE_F + p[1]:GATE_F + p[1] + 1] for p in ps}
    f_row = {p: gates[p[0]][3][GATE_F + p[1]:GATE_F + p[1] + 1, :] for p in ps}
    f_last = {p: f_col[p][l - 1:l, :] for p in ps}
    qk = {p: _dot(q[p], kt[p]) for p in ps}
    d = {p: jnp.where(tri, f_col[p] - f_row[p] + ig_row[p], -jnp.inf) for p in ps}
    d_max = {p: jnp.max(d[p], axis=1, keepdims=True) for p in ps}
    m_prev, m_t = {}, {}
    m_run = [m_all[:, h:h + 1] for h in hs]
    for j in range(cb):
        for h in hs:
            m_prev[(j, h)] = m_run[h]
            m_t[(j, h)] = jnp.maximum(f_col[(j, h)] + m_run[h], d_max[(j, h)])
            m_run[h] = m_t[(j, h)][l - 1:l, :]
    m_new = {p: m_t[p][l - 1:l, :] for p in ps}
    w = {p: jnp.exp(d[p] - m_t[p]) * qk[p] for p in ps}
    c_inter = {p: jnp.exp(f_col[p] + m_prev[p] - m_t[p]) for p in ps}
    wv = {p: _dot(w[p], v[p]) for p in ps}
    w_sum = {p: jnp.sum(w[p], axis=1, keepdims=True) for p in ps}
    e_m = {p: jnp.exp(-m_t[p]) for p in ps}
    w_end_row = {p: jnp.exp(f_last[p] - f_row[p] + ig_row[p] - m_new[p]) for p in ps}
    w_end_col = {p: jnp.exp(f_last[p] - f_col[p] + ig_col[p] - m_new[p]) for p in ps}
    c_prev = {p: jnp.exp(f_last[p] + m_prev[p] - m_new[p]) for p in ps}
    dc = {p: _dot(kt[p] * w_end_row[p], v[p]) for p in ps}
    dn = {p: jnp.sum(w_end_col[p] * k[p], axis=0, keepdims=True) for p in ps}
    c_h = [c_scr[h] for h in hs]
    n_h = [n_scr[h:h + 1, :] for h in hs]
    for j in range(cb):
        for h in hs:
            p = (j, h)
            num = c_inter[p] * _dot(q[p], c_h[h]) + wv[p]
            den = c_inter[p] * jnp.sum(q[p] * n_h[h], axis=1, keepdims=True) + w_sum[p]
            o_ref[j * l:(j + 1) * l, h * HEAD_DIM:(h + 1) * HEAD_DIM] = num / jnp.maximum(jnp.abs(den), e_m[p])
            c_h[h] = c_prev[p] * c_h[h] + dc[p]
            n_h[h] = c_prev[p] * n_h[h] + dn[p]
    m_next = m_all
    for h in hs:
        c_scr[h] = c_h[h]
        n_scr[h:h + 1, :] = n_h[h]
        m_next = jnp.where(lane == h, m_run[h], m_next)
    m_scr[...] = m_next

    @pl.when(gi == ng - 1)
    def _():
        c_out_ref[0] = c_scr[...]
        n_out_ref[0] = n_scr[...]
        m_out_ref[0] = m_scr[...]


def _ml_scan(proj, k_t, gate_t, prow, pcol, c0, n0, m0, b, t, l, cb):
    ng = t // (l * cb)
    w3 = 3 * REC_W
    st = lambda bi, i: (bi, 0, 0, 0)
    st3 = lambda bi, i: (bi, 0, 0)
    return pl.pallas_call(
        functools.partial(_ml_scan_kernel, l=l, cb=cb, ng=ng),
        grid=(b, ng),
        in_specs=[pl.BlockSpec((cb * l, w3), lambda bi, i: (bi * ng + i, COL_MQKV // w3)),
                  pl.BlockSpec((cb, REC_W, l), lambda bi, i: (bi * ng + i, 0, 0)),
                  pl.BlockSpec((cb * l, LANES), lambda bi, i: (bi * ng + i, COL_GATE // LANES)),
                  pl.BlockSpec((cb, LANES, l), lambda bi, i: (bi * ng + i, 0, 0)),
                  pl.BlockSpec((SUBLANES, LANES), lambda bi, i: (0, 0)),
                  pl.BlockSpec((LANES, SUBLANES), lambda bi, i: (0, 0)),
                  pl.BlockSpec((1, ML_HEADS, HEAD_DIM, HEAD_DIM), st),
                  pl.BlockSpec((1, SUBLANES, HEAD_DIM), st3),
                  pl.BlockSpec((1, 1, LANES), st3)],
        out_specs=[pl.BlockSpec((cb * l, REC_W), lambda bi, i: (bi * ng + i, 0)),
                   pl.BlockSpec((1, ML_HEADS, HEAD_DIM, HEAD_DIM), st),
                   pl.BlockSpec((1, SUBLANES, HEAD_DIM), st3),
                   pl.BlockSpec((1, 1, LANES), st3)],
        out_shape=[jax.ShapeDtypeStruct((b * t, REC_W), F32),
                   jax.ShapeDtypeStruct((b, ML_HEADS, HEAD_DIM, HEAD_DIM), F32),
                   jax.ShapeDtypeStruct((b, SUBLANES, HEAD_DIM), F32),
                   jax.ShapeDtypeStruct((b, 1, LANES), F32)],
        scratch_shapes=[pltpu.VMEM((ML_HEADS, HEAD_DIM, HEAD_DIM), F32),
                        pltpu.VMEM((SUBLANES, HEAD_DIM), F32),
                        pltpu.VMEM((1, LANES), F32)],
        compiler_params=_cparams("parallel", "arbitrary"),
        name="ml_scan",
    )(proj, k_t, proj, gate_t, prow, pcol, c0, n0, m0)


def _post_kernel(x_ref, osb_ref, og_ref, hm_ref, gz_ref, mo_ref, w_ref, e_ref, gnw_ref, mnw_ref,
                 lg_ref, lb_ref, o_ref, ob_ref):
    e = e_ref[...]
    og = og_ref[...]
    hm = hm_ref[...]
    og_ms = _dot_exact_rhs(og * og, e, 2) * (1.0 / HEAD_DIM)
    hm_ms = _dot_exact_rhs(hm * hm, e, 2) * (1.0 / HEAD_DIM)
    gz = gz_ref[...]
    o_g = og * lax.rsqrt(og_ms + NORM_EPS) * gnw_ref[...] * (gz * _sigmoid(gz))
    o_m = _sigmoid(mo_ref[...]) * (hm * lax.rsqrt(hm_ms + NORM_EPS) * mnw_ref[...])
    mixed = (_dot(osb_ref[...], w_ref[0:SB_W, :])
             + _dot(o_g, w_ref[SB_W:SB_W + REC_W, :])
             + _dot(o_m, w_ref[SB_W + REC_W:SB_W + 2 * REC_W, :]))
    y = _layer_norm(DEEPNORM_ALPHA * x_ref[...] + mixed, lg_ref[...], lb_ref[...])
    o_ref[...] = y
    ob_ref[...] = y.astype(BF16)


def _post(x, osb, og, hm, proj, w_out, e_heads, gnw, mnw, lg, lb, tm):
    n, d = x.shape
    row = lambda i: (i, 0)
    fixed = lambda i: (0, 0)
    return pl.pallas_call(
        _post_kernel,
        grid=(n // tm,),
        in_specs=[pl.BlockSpec((tm, d), row),
                  pl.BlockSpec((tm, SB_W), row),
                  pl.BlockSpec((tm, REC_W), row),
                  pl.BlockSpec((tm, REC_W), row),
                  pl.BlockSpec((tm, REC_W), lambda i: (i, COL_GZ // REC_W)),
                  pl.BlockSpec((tm, REC_W), lambda i: (i, COL_MO // REC_W)),
                  pl.BlockSpec(w_out.shape, fixed),
                  pl.BlockSpec((REC_W, REC_W), fixed),
                  pl.BlockSpec((1, REC_W), fixed),
                  pl.BlockSpec((1, REC_W), fixed),
                  pl.BlockSpec((1, d), fixed),
                  pl.BlockSpec((1, d), fixed)],
        out_specs=[pl.BlockSpec((tm, d), row), pl.BlockSpec((tm, d), row)],
        out_shape=[jax.ShapeDtypeStruct((n, d), F32), jax.ShapeDtypeStruct((n, d), BF16)],
        compiler_params=_cparams("parallel"),
        name="post",
    )(x, osb, og, hm, proj, proj, w_out, e_heads, gnw, mnw, lg, lb)


def _memattn_kernel(x_ref, xb_ref, wq_ref, wo_ref, mk_ref, mv_ref, lg_ref, lb_ref, o_ref, ob_ref):
    d = x_ref.shape[1]
    hd = d // MEM_HEADS
    q = jnp.dot(xb_ref[...], wq_ref[...], preferred_element_type=F32)
    out = None
    for h in range(MEM_HEADS):
        sl = slice(h * hd, (h + 1) * hd)
        s = _dot_nt(q[:, sl], mk_ref[0, :, sl]) * (hd ** -0.5)
        p = jnp.exp(s - jnp.max(s, axis=1, keepdims=True))
        o_h = _dot(p, mv_ref[0, :, sl]) / jnp.sum(p, axis=1, keepdims=True)
        t = _dot(o_h, wo_ref[sl, :])
        out = t if out is None else out + t
    y = _layer_norm(DEEPNORM_ALPHA * x_ref[...] + out, lg_ref[...], lb_ref[...])
    o_ref[...] = y
    ob_ref[...] = y.astype(BF16)


def _memattn(x, xb, w_cq, w_co, mk, mv, lg, lb, b, t, tm):
    n, d = x.shape
    nt = t // tm
    nm = mk.shape[1]
    row = lambda bi, i: (bi * nt + i, 0)
    fixed = lambda bi, i: (0, 0)
    return pl.pallas_call(
        _memattn_kernel,
        grid=(b, nt),
        in_specs=[pl.BlockSpec((tm, d), row),
                  pl.BlockSpec((tm, d), row),
                  pl.BlockSpec((d, d), fixed),
                  pl.BlockSpec((d, d), fixed),
                  pl.BlockSpec((1, nm, d), lambda bi, i: (bi, 0, 0)),
                  pl.BlockSpec((1, nm, d), lambda bi, i: (bi, 0, 0)),
                  pl.BlockSpec((1, d), fixed),
                  pl.BlockSpec((1, d), fixed)],
        out_specs=[pl.BlockSpec((tm, d), row), pl.BlockSpec((tm, d), row)],
        out_shape=[jax.ShapeDtypeStruct((n, d), F32), jax.ShapeDtypeStruct((n, d), BF16)],
        compiler_params=_cparams("parallel", "parallel"),
        name="memattn",
    )(x, xb, w_cq, w_co, mk, mv, lg, lb)


def _ffn_kernel(x_ref, xb_ref, wg_ref, wu_ref, wd_ref, lg_ref, lb_ref, o_ref, ob_ref, acc_ref, *, nf):
    j = pl.program_id(1)

    @pl.when(j == 0)
    def _():
        acc_ref[...] = jnp.zeros_like(acc_ref)

    xb = xb_ref[...]
    gate = jnp.dot(xb, wg_ref[...], preferred_element_type=F32)
    up = jnp.dot(xb, wu_ref[...], preferred_element_type=F32)
    acc_ref[...] += _dot(gate * _sigmoid(gate) * up, wd_ref[...])

    @pl.when(j == nf - 1)
    def _():
        y = _layer_norm(DEEPNORM_ALPHA * x_ref[...] + acc_ref[...], lg_ref[...], lb_ref[...])
        o_ref[...] = y
        ob_ref[...] = y.astype(BF16)


def _ffn(x, xb, w_up, w_down, lg, lb, tm, tf):
    n, d = x.shape
    dff = w_down.shape[0]
    nf = dff // tf
    row = lambda i, j: (i, 0)
    fixed = lambda i, j: (0, 0)
    return pl.pallas_call(
        functools.partial(_ffn_kernel, nf=nf),
        grid=(n // tm, nf),
        in_specs=[pl.BlockSpec((tm, d), row),
                  pl.BlockSpec((tm, d), row),
                  pl.BlockSpec((d, tf), lambda i, j: (0, j)),
                  pl.BlockSpec((d, tf), lambda i, j: (0, nf + j)),
                  pl.BlockSpec((tf, d), lambda i, j: (j, 0)),
                  pl.BlockSpec((1, d), fixed),
                  pl.BlockSpec((1, d), fixed)],
        out_specs=[pl.BlockSpec((tm, d), row), pl.BlockSpec((tm, d), row)],
        out_shape=[jax.ShapeDtypeStruct((n, d), F32), jax.ShapeDtypeStruct((n, d), BF16)],
        scratch_shapes=[pltpu.VMEM((tm, d), F32)],
        compiler_params=_cparams("parallel", "arbitrary"),
        name="ffn",
    )(x, xb, w_up, w_up, w_down, lg, lb)


def _pack_layer(p):
    w_in = p['w_in']
    d = w_in.shape[0]
    o_sb, o_gqkv, o_gz = 0, 3 * SB_W, 3 * SB_W + 3 * REC_W
    o_gb = o_gz + REC_W
    o_mqkv = o_gb + 2 * GDN_HEADS
    o_mo = o_mqkv + 3 * REC_W
    o_mi = o_mo + REC_W
    gate_cols = jnp.concatenate([w_in[:, o_gb:o_gb + 2 * GDN_HEADS], w_in[:, o_mi:o_mi + 2 * ML_HEADS]], axis=1)
    used = COL_GATE + gate_cols.shape[1]
    w_al = jnp.concatenate([w_in[:, o_gqkv:o_gqkv + 3 * REC_W], w_in[:, o_mqkv:o_mqkv + 3 * REC_W],
                            w_in[:, o_gz:o_gz + REC_W], w_in[:, o_mo:o_mo + REC_W],
                            w_in[:, o_sb:o_sb + 3 * SB_W], gate_cols,
                            jnp.zeros((d, PROJ_W - used), w_in.dtype)], axis=1).astype(BF16)
    prow = jnp.zeros((SUBLANES, LANES), F32)
    prow = prow.at[0, GATE_G:GATE_G + GDN_HEADS].set(p['gdn_dt_bias'])
    prow = prow.at[0, GATE_I:GATE_I + ML_HEADS].set(p['mlstm_i_bias'])
    prow = prow.at[0, GATE_F:GATE_F + ML_HEADS].set(p['mlstm_f_bias'])
    prow = prow.at[1, GATE_G:GATE_G + GDN_HEADS].set(p['gdn_A_log'])
    conv_w8 = jnp.zeros((SUBLANES, 3 * REC_W), F32).at[0:CONV_W].set(p['gdn_conv_w'])
    return dict(
        w_al=w_al, prow=prow, pcol=prow.T, conv_w8=conv_w8,
        gnw=jnp.tile(p['gdn_norm_w'], GDN_HEADS)[None, :], mnw=p['mlstm_norm_w'][None, :],
        w_out=p['w_out'].astype(BF16), w_cq=p['w_cq'].astype(BF16), w_co=p['w_co'].astype(BF16),
        w_up=p['w_up'].astype(BF16), w_down=p['w_down'].astype(BF16),
        ln1=(p['ln1_g'][None, :], p['ln1_b'][None, :]), ln2=(p['ln2_g'][None, :], p['ln2_b'][None, :]),
        ln3=(p['ln3_g'][None, :], p['ln3_b'][None, :]))


def _head_consts():
    hid = np.arange(REC_W) // HEAD_DIM
    return jnp.asarray(hid[:, None] == hid[None, :], BF16)


def _sb_layout(k, v, tk):
    b, tkv, _ = k.shape
    nkb = tkv // tk
    kt = k.reshape(b, nkb, tk, SB_W).transpose(0, 1, 3, 2)
    vh = v.reshape(b, nkb, tk, SB_HEADS, HEAD_DIM).transpose(0, 3, 1, 2, 4)
    vh = vh.reshape(b * SB_HEADS, nkb, tk, HEAD_DIM)
    return kt.astype(BF16), vh.astype(BF16)


def _time_on_lanes(a, groups, width):
    return a.reshape(groups, width, a.shape[1]).transpose(0, 2, 1)


def _trunk_layer(x, xb, pk, e_heads, mk, mv, b, t, cfg, sb_past, conv_ctx, gdn_s, ml_c, ml_n, ml_m):
    n, d = x.shape
    l = min(t, CHUNK)
    nc = t // l
    cb = min(cfg['scan_cb'], nc)
    ng = b * nc // cb
    tk = cfg['sb_tk']
    ctx8 = jnp.pad(conv_ctx, ((0, 0), (SUBLANES - (CONV_W - 1), 0), (0, 0)))
    if sb_past is None:
        proj, kt, vh, mk_t, gate_t = _proj(x if xb is None else xb, pk['w_al'], cfg['tm'])
        kt = kt.reshape(b, t // tk, SB_W, tk)
        q_off = 0
        gqkv_act, gk_t = _gdn_pre(proj, ctx8, pk['conv_w8'], e_heads, b, t, cfg['conv_tt'])
    else:
        proj = _matmul(xb, pk['w_al'], cfg['tm'], 1024)
    sk = proj[:, COL_SB + SB_W:COL_SB + 2 * SB_W].reshape(b, t, SB_W)
    sv = proj[:, COL_SB + 2 * SB_W:COL_SB + 3 * SB_W].reshape(b, t, SB_W)
    g_qkv = proj[:, COL_GQKV:COL_GQKV + 3 * REC_W].reshape(b, t, 3 * REC_W)
    if sb_past is not None:
        past_k, past_v = sb_past
        q_off = past_k.shape[1]
        kv_len = q_off + t
        kv_pad = -(-kv_len // tk) * tk - kv_len
        kk = jnp.pad(jnp.concatenate([past_k.reshape(b, q_off, SB_W), sk], axis=1), ((0, 0), (0, kv_pad), (0, 0)))
        vv = jnp.pad(jnp.concatenate([past_v.reshape(b, q_off, SB_W), sv], axis=1), ((0, 0), (0, kv_pad), (0, 0)))
        kt, vh = _sb_layout(kk, vv, tk)
        gqkv_act, _ = _gdn_pre(proj, ctx8, pk['conv_w8'], e_heads, b, t, cfg['conv_tt'])
        gate_t = _time_on_lanes(proj[:, COL_GATE:COL_GATE + LANES], b * nc, l)
        gk_t = _time_on_lanes(gqkv_act[:, REC_W:2 * REC_W], b * nc, l)
        mk_t = _time_on_lanes(proj[:, COL_MQKV + REC_W:COL_MQKV + 2 * REC_W], b * nc, l)

    o_sb = _sb_attn(proj, kt, vh, b, t, SB_HEADS, cfg['sb_tq'], q_off)
    o_g, gdn_s_new = _gdn_scan(gqkv_act, gk_t, proj, gate_t, pk['prow'], pk['pcol'], gdn_s, b, t, l, cb)
    n0 = jnp.pad(ml_n, ((0, 0), (0, SUBLANES - ML_HEADS), (0, 0)))
    m0 = jnp.pad(ml_m, ((0, 0), (0, LANES - ML_HEADS)))[:, None, :]
    h_m, c_new, n_new, m_new = _ml_scan(proj, mk_t, gate_t, pk['prow'], pk['pcol'], ml_c, n0, m0, b, t, l, cb)

    x1, x1b = _post(x, o_sb, o_g, h_m, proj, pk['w_out'], e_heads, pk['gnw'], pk['mnw'], *pk['ln1'], cfg['tm'])
    x2, x2b = _memattn(x1, x1b, pk['w_cq'], pk['w_co'], mk, mv, *pk['ln2'], b, t, cfg['mem_tm'])
    x3, x3b = _ffn(x2, x2b, pk['w_up'], pk['w_down'], *pk['ln3'], cfg['tm'], cfg['ffn_tf'])

    new_conv = g_qkv[:, t - (CONV_W - 1):, :]
    return (x3, x3b, sk.reshape(b, t, SB_HEADS, HEAD_DIM), sv.reshape(b, t, SB_HEADS, HEAD_DIM), new_conv,
            gdn_s_new, c_new, n_new[:, :ML_HEADS, :], m_new[:, 0, :ML_HEADS])


def kernel(x_prompt, x_sample, cache_sb_k, cache_sb_v, cache_gdn_conv, state_gdn, state_mlstm_C, state_mlstm_n, state_mlstm_m, cache_mem_k, cache_mem_v, mem_prompt, w_in, gdn_conv_w, gdn_A_log, gdn_dt_bias, gdn_norm_w, mlstm_i_bias, mlstm_f_bias, mlstm_norm_w, w_out, ln1_g, ln1_b, w_cq, w_ckv, w_co, ln2_g, ln2_b, w_up, w_down, ln3_g, ln3_b):
    bp, tp, d = x_prompt.shape
    bs, ts, _ = x_sample.shape
    depth = w_in.shape[0]
    n_mem = mem_prompt.shape[1]
    dff = w_down.shape[1]
    cfg_p = dict(tm=512, sb_tq=1024, sb_tk=256, conv_tt=512, mem_tm=512, ffn_tf=dff // 2, scan_cb=4)
    cfg_s = dict(tm=bs * ts, sb_tq=ts, sb_tk=256, conv_tt=ts, mem_tm=ts, ffn_tf=dff // 2, scan_cb=1)
    e_heads = _head_consts()

    xp = x_prompt.reshape(bp * tp, d)
    xs = x_sample.reshape(bs * ts, d)
    xpb, xsb = None, xs.astype(BF16)
    mem_b = mem_prompt.reshape(bp * n_mem, d).astype(BF16)
    p_out = [[] for _ in range(9)]
    s_out = [[] for _ in range(7)]
    for li in range(depth):
        p = {'w_in': w_in[li], 'gdn_conv_w': gdn_conv_w[li], 'gdn_A_log': gdn_A_log[li],
             'gdn_dt_bias': gdn_dt_bias[li], 'gdn_norm_w': gdn_norm_w[li], 'mlstm_i_bias': mlstm_i_bias[li],
             'mlstm_f_bias': mlstm_f_bias[li], 'mlstm_norm_w': mlstm_norm_w[li], 'w_out': w_out[li],
             'ln1_g': ln1_g[li], 'ln1_b': ln1_b[li], 'w_cq': w_cq[li], 'w_co': w_co[li],
             'ln2_g': ln2_g[li], 'ln2_b': ln2_b[li], 'w_up': w_up[li], 'w_down': w_down[li],
             'ln3_g': ln3_g[li], 'ln3_b': ln3_b[li]}
        pk = _pack_layer(p)
        mkv = _matmul(mem_b, w_ckv[li].astype(BF16), n_mem, 1024)
        mk_p = mkv[:, :d].reshape(bp, n_mem, d)
        mv_p = mkv[:, d:].reshape(bp, n_mem, d)
        res = _trunk_layer(xp, xpb, pk, e_heads, mk_p.astype(BF16), mv_p.astype(BF16), bp, tp, cfg_p, None,
                           jnp.zeros((bp, CONV_W - 1, 3 * REC_W), F32),
                           jnp.zeros((bp, GDN_HEADS, HEAD_DIM, HEAD_DIM), F32),
                           jnp.zeros((bp, ML_HEADS, HEAD_DIM, HEAD_DIM), F32),
                           jnp.zeros((bp, ML_HEADS, HEAD_DIM), F32),
                           jnp.zeros((bp, ML_HEADS), F32))
        xp, xpb = res[0], res[1]
        hd = d // MEM_HEADS
        for j, a in enumerate(res[2:] + (mk_p.reshape(bp, n_mem, MEM_HEADS, hd), mv_p.reshape(bp, n_mem, MEM_HEADS, hd))):
            p_out[j].append(a)
        res = _trunk_layer(xs, xsb, pk, e_heads,
                           cache_mem_k[li].reshape(bs, n_mem, d).astype(BF16),
                           cache_mem_v[li].reshape(bs, n_mem, d).astype(BF16),
                           bs, ts, cfg_s, (cache_sb_k[li], cache_sb_v[li]), cache_gdn_conv[li],
                           state_gdn[li], state_mlstm_C[li], state_mlstm_n[li], state_mlstm_m[li])
        xs, xsb = res[0], res[1]
        for j, a in enumerate(res[2:]):
            s_out[j].append(a)
    p_st = [jnp.stack(a) for a in p_out]
    s_st = [jnp.stack(a) for a in s_out]
    return (xp.reshape(bp, tp, d), xs.reshape(bs, ts, d), *p_st, *s_st)
```

```python
import functools

import numpy as np
import jax
import jax.numpy as jnp
from jax import lax
from jax.experimental import pallas as pl
from jax.experimental.pallas import tpu as pltpu

F32 = jnp.float32
BF16 = jnp.bfloat16

HEAD_DIM = 64
SB_HEADS = 4
GDN_HEADS = 6
ML_HEADS = 6
REC_W = GDN_HEADS * HEAD_DIM
SB_W = SB_HEADS * HEAD_DIM
CONV_W = 4
CHUNK = 64
MEM_HEADS = 4
LN_EPS = 1e-5
NORM_EPS = 1e-6
DEPTH = 2
DEEPNORM_ALPHA = (2 * DEPTH) ** 0.25
LOG2E = float(np.log2(np.e))
SOFTPLUS2_CLAMP = 120.0

GROUP = 256
LANES = 128
SUBLANES = 8
VMEM_LIMIT = 56 * 1024 * 1024

PROJ_W = 4096
COL_GQKV = 0
COL_MQKV = 1152
COL_GZ = 2304
COL_MO = 2688
COL_SB = 3072
COL_GATE = 3840
GATE_BETA, GATE_G, GATE_I, GATE_F = 0, 6, 12, 18


def _cparams(*sem):
    return pltpu.CompilerParams(dimension_semantics=sem, vmem_limit_bytes=VMEM_LIMIT)


def _dot(a, b):
    return jnp.dot(a.astype(BF16), b.astype(BF16), preferred_element_type=F32)


def _dot_nt(a, b):
    return lax.dot_general(a.astype(BF16), b.astype(BF16), (((1,), (1,)), ((), ())),
                           preferred_element_type=F32)


def _dot_tn(a, b):
    return lax.dot_general(a.astype(BF16), b.astype(BF16), (((0,), (0,)), ((), ())),
                           preferred_element_type=F32)


def _split(x, parts):
    out = []
    r = x
    for _ in range(parts - 1):
        p = r.astype(BF16)
        out.append(p)
        r = r - p.astype(F32)
    out.append(r.astype(BF16))
    return out


def _dot_exact_lhs(m, x, parts):
    acc = None
    for p in _split(x, parts):
        t = jnp.dot(m, p, preferred_element_type=F32)
        acc = t if acc is None else acc + t
    return acc


def _dot_exact_rhs(x, m, parts):
    acc = None
    for p in _split(x, parts):
        t = jnp.dot(p, m, preferred_element_type=F32)
        acc = t if acc is None else acc + t
    return acc


def _ones_where(mask, dtype):
    return jnp.where(mask, 1.0, 0.0).astype(dtype)


def _sigmoid(x):
    return 1.0 / (1.0 + jnp.exp(-x))


def _layer_norm(y, g, b):
    mu = jnp.mean(y, axis=-1, keepdims=True)
    d = y - mu
    var = jnp.mean(d * d, axis=-1, keepdims=True)
    return d * lax.rsqrt(var + LN_EPS) * g + b


def _mm_kernel(x_ref, w_ref, o_ref):
    o_ref[...] = jnp.dot(x_ref[...], w_ref[...], preferred_element_type=F32).astype(o_ref.dtype)


def _matmul(x, w, tm, tn, out_dtype=F32):
    n, k = x.shape
    m = w.shape[1]
    return pl.pallas_call(
        _mm_kernel,
        grid=(n // tm, m // tn),
        in_specs=[pl.BlockSpec((tm, k), lambda i, j: (i, 0)),
                  pl.BlockSpec((k, tn), lambda i, j: (0, j))],
        out_specs=pl.BlockSpec((tm, tn), lambda i, j: (i, j)),
        out_shape=jax.ShapeDtypeStruct((n, m), out_dtype),
        compiler_params=_cparams("parallel", "parallel"),
        name="proj_matmul",
    )(x, w)


def _proj_kernel(x_ref, w_ref, o_ref, skt_ref, sv_ref, mkt_ref, gtt_ref, *, tm, tn):
    xb = x_ref[...].astype(BF16)
    for j in range(PROJ_W // tn):
        o_ref[:, j * tn:(j + 1) * tn] = jnp.dot(xb, w_ref[:, j * tn:(j + 1) * tn], preferred_element_type=F32)
    for r in range(tm // GROUP):
        rows = slice(r * GROUP, (r + 1) * GROUP)
        skt_ref[r] = o_ref[rows, COL_SB + SB_W:COL_SB + 2 * SB_W].T.astype(BF16)
        for h in range(SB_HEADS):
            c0 = COL_SB + 2 * SB_W + h * HEAD_DIM
            sv_ref[h, r] = o_ref[rows, c0:c0 + HEAD_DIM].astype(BF16)
        _store_chunks_t(mkt_ref, r, o_ref[rows, COL_MQKV:COL_MQKV + 3 * REC_W])
        _store_chunks_t(gtt_ref, r, o_ref[rows, COL_GATE:COL_GATE + LANES])


def _store_chunks_t(ref, r, a):
    at = a.T
    per = GROUP // CHUNK
    for j in range(per):
        ref[r * per + j] = at[:, j * CHUNK:(j + 1) * CHUNK]


def _proj(x, w, tm):
    n, d = x.shape
    ng = n // GROUP
    gpt = tm // GROUP
    cpt = tm // CHUNK
    return pl.pallas_call(
        functools.partial(_proj_kernel, tm=tm, tn=1024),
        grid=(n // tm,),
        in_specs=[pl.BlockSpec((tm, d), lambda i: (i, 0)),
                  pl.BlockSpec((d, PROJ_W), lambda i: (0, 0), pipeline_mode=pl.Buffered(1))],
        out_specs=[pl.BlockSpec((tm, PROJ_W), lambda i: (i, 0)),
                   pl.BlockSpec((gpt, SB_W, GROUP), lambda i: (i, 0, 0)),
                   pl.BlockSpec((SB_HEADS, gpt, GROUP, HEAD_DIM), lambda i: (0, i, 0, 0)),
                   pl.BlockSpec((cpt, 3 * REC_W, CHUNK), lambda i: (i, 0, 0)),
                   pl.BlockSpec((cpt, LANES, CHUNK), lambda i: (i, 0, 0))],
        out_shape=[jax.ShapeDtypeStruct((n, PROJ_W), F32),
                   jax.ShapeDtypeStruct((ng, SB_W, GROUP), BF16),
                   jax.ShapeDtypeStruct((SB_HEADS, ng, GROUP, HEAD_DIM), BF16),
                   jax.ShapeDtypeStruct((n // CHUNK, 3 * REC_W, CHUNK), F32),
                   jax.ShapeDtypeStruct((n // CHUNK, LANES, CHUNK), F32)],
        compiler_params=_cparams("parallel"),
        name="proj_full",
    )(x, w)


def _gdn_pre_kernel(x_ref, ctx_ref, w_ref, e_ref, o_ref, *rest, tt):
    xbuf = rest[-1]
    i = pl.program_id(1)

    @pl.when(i == 0)
    def _():
        xbuf[0:SUBLANES, :] = ctx_ref[0]

    xbuf[SUBLANES:SUBLANES + tt, :] = x_ref[...]
    acc = w_ref[CONV_W - 1:CONV_W, :] * xbuf[SUBLANES:SUBLANES + tt, :]
    for j in range(CONV_W - 1):
        off = SUBLANES - (CONV_W - 1) + j
        acc = acc + w_ref[j:j + 1, :] * xbuf[off:off + tt, :]
    y = acc * _sigmoid(acc)
    q = y[:, 0:REC_W]
    k = y[:, REC_W:2 * REC_W]
    e = e_ref[...]
    qs = _dot_exact_rhs(q * q, e, 2)
    ks = _dot_exact_rhs(k * k, e, 2)
    kn = k * lax.rsqrt(ks + NORM_EPS)
    o_ref[:, 0:REC_W] = q * lax.rsqrt(qs + NORM_EPS) * (HEAD_DIM ** -0.5)
    o_ref[:, REC_W:2 * REC_W] = kn
    o_ref[:, 2 * REC_W:3 * REC_W] = y[:, 2 * REC_W:3 * REC_W]
    if len(rest) == 2:
        for r in range(tt // GROUP):
            _store_chunks_t(rest[0], r, kn[r * GROUP:(r + 1) * GROUP, :])
    xbuf[0:SUBLANES, :] = xbuf[tt:tt + SUBLANES, :]


def _gdn_pre(proj, ctx8, conv_w8, e_heads, b, t, tt):
    w3 = 3 * REC_W
    nt = t // tt
    emit_kt = tt % GROUP == 0
    out_specs = [pl.BlockSpec((tt, w3), lambda bi, i: (bi * nt + i, 0))]
    out_shape = [jax.ShapeDtypeStruct((b * t, w3), F32)]
    if emit_kt:
        out_specs.append(pl.BlockSpec((tt // CHUNK, REC_W, CHUNK), lambda bi, i: (bi * nt + i, 0, 0)))
        out_shape.append(jax.ShapeDtypeStruct((b * t // CHUNK, REC_W, CHUNK), F32))
    res = pl.pallas_call(
        functools.partial(_gdn_pre_kernel, tt=tt),
        grid=(b, nt),
        in_specs=[pl.BlockSpec((tt, w3), lambda bi, i: (bi * nt + i, COL_GQKV // w3)),
                  pl.BlockSpec((1, SUBLANES, w3), lambda bi, i: (bi, 0, 0)),
                  pl.BlockSpec((SUBLANES, w3), lambda bi, i: (0, 0)),
                  pl.BlockSpec((REC_W, REC_W), lambda bi, i: (0, 0))],
        out_specs=out_specs,
        out_shape=out_shape,
        scratch_shapes=[pltpu.VMEM((tt + SUBLANES, w3), F32)],
        compiler_params=_cparams("parallel", "arbitrary"),
        name="gdn_pre",
    )(proj, ctx8, conv_w8, e_heads)
    return (res[0], res[1]) if emit_kt else (res[0], None)


def _sb_kernel(q_ref, kt_ref, v_ref, o_ref, acc_ref, c_ref, *, hb, tq, tk, nkb, q_off, diag_static):
    qi = pl.program_id(1)
    q_lo = q_off + qi * tq
    nblk = jnp.minimum(nkb, (q_lo + tq - 1 + tk - 1) // tk)
    nfull = jnp.minimum(nblk, q_lo // tk)
    scale = HEAD_DIM ** -0.5 * LOG2E
    qs = [(q_ref[:, h * HEAD_DIM:(h + 1) * HEAD_DIM] * scale).astype(BF16) for h in range(hb)]
    lower_incl = _ones_where(lax.broadcasted_iota(jnp.int32, (tk, tk), 0)
                             >= lax.broadcasted_iota(jnp.int32, (tk, tk), 1), BF16)
    acc_ref[...] = jnp.zeros_like(acc_ref)
    c_ref[...] = jnp.zeros_like(c_ref)

    def blocks(kbs, r0=0, valids=None):
        for h in range(hb):
            q = qs[h][r0:, :]
            c = c_ref[h, r0:, :]
            av = None
            for i, kb in enumerate(kbs):
                kt = kt_ref[0, kb, h * HEAD_DIM:(h + 1) * HEAD_DIM, :]
                z = jnp.dot(q, kt, preferred_element_type=F32)
                lneg = jnp.maximum(z, jnp.log2(1.0 + jnp.exp2(jnp.minimum(z, SOFTPLUS2_CLAMP))))
                if valids is not None:
                    lneg = jnp.where(valids[i], lneg, 0.0)
                incl = jnp.dot(lneg.astype(BF16), lower_incl, preferred_element_type=F32)
                a = jnp.exp2(z - incl - jnp.tile(c, (1, tk // LANES)))
                if valids is not None:
                    a = jnp.where(valids[i], a, 0.0)
                t = jnp.dot(a.astype(BF16), v_ref[h, kb], preferred_element_type=F32)
                av = t if av is None else av + t
                c = c + incl[:, 0:1]
            acc_ref[h, r0:, :] += av
            c_ref[h, r0:, :] = c

    if diag_static:
        for r in reversed(range(tq // tk)):
            rows = tq - r * tk
            valid = (lax.broadcasted_iota(jnp.int32, (rows, tk), 1)
                     < lax.broadcasted_iota(jnp.int32, (rows, tk), 0))
            blocks([nfull + r], r0=r * tk, valids=[valid])
    else:
        @pl.loop(0, nblk - nfull)
        def _(i):
            kb = nblk - 1 - i
            valid = (kb * tk + lax.broadcasted_iota(jnp.int32, (tq, tk), 1)
                     < q_lo + lax.broadcasted_iota(jnp.int32, (tq, tk), 0))
            blocks([kb], valids=[valid])

    odd = nfull % 2

    @pl.loop(0, odd)
    def _(i):
        blocks([nfull - 1])

    @pl.loop(0, nfull // 2)
    def _(i):
        kb = nfull - odd - 1 - 2 * i
        blocks([kb, kb - 1])

    for h in range(hb):
        o_ref[:, h * HEAD_DIM:(h + 1) * HEAD_DIM] = acc_ref[h]


def _sb_attn(proj, kt, vh, b, t, hb, tq, q_off):
    _, nkb, w, tk = kt.shape
    d = w // hb
    nq = t // tq
    diag_static = q_off % tk == 0 and tq % tk == 0 and q_off + t <= nkb * tk
    return pl.pallas_call(
        functools.partial(_sb_kernel, hb=hb, tq=tq, tk=tk, nkb=nkb, q_off=q_off, diag_static=diag_static),
        grid=(b, nq),
        in_specs=[pl.BlockSpec((tq, w), lambda bi, i: (bi * nq + i, COL_SB // w)),
                  pl.BlockSpec((1, nkb, w, tk), lambda bi, i: (bi, 0, 0, 0), pipeline_mode=pl.Buffered(1)),
                  pl.BlockSpec((hb, nkb, tk, d), lambda bi, i: (bi, 0, 0, 0), pipeline_mode=pl.Buffered(1))],
        out_specs=pl.BlockSpec((tq, w), lambda bi, i: (bi * nq + i, 0)),
        out_shape=jax.ShapeDtypeStruct((b * t, w), F32),
        scratch_shapes=[pltpu.VMEM((hb, tq, d), F32), pltpu.VMEM((hb, tq, LANES), F32)],
        compiler_params=_cparams("parallel", "parallel"),
        name="sb_attn",
    )(proj, kt, vh)


def _gate_values(pre, neg_a, lane_id):
    sp = jnp.log1p(jnp.exp(-jnp.abs(pre)))
    softplus = jnp.maximum(pre, 0.0) + sp
    log_sig = jnp.minimum(pre, 0.0) - sp
    return jnp.where(lane_id < GATE_G, _sigmoid(pre),
                     jnp.where(lane_id < GATE_I, neg_a * softplus,
                               jnp.where(lane_id < GATE_F, pre, log_sig)))


def _stack(a, b):
    return jnp.concatenate([a, b], axis=0)


def _gates(gt, gtt, prow_ref, pcol_ref, l):
    lane = lax.broadcasted_iota(jnp.int32, (l, LANES), 1)
    val = _gate_values(gt + prow_ref[0:1, :], -jnp.exp(prow_ref[1:2, :]), lane)
    sub = lax.broadcasted_iota(jnp.int32, (LANES, l), 0)
    val_t = _gate_values(gtt + pcol_ref[:, 0:1], -jnp.exp(pcol_ref[:, 1:2]), sub)
    r = lax.broadcasted_iota(jnp.int32, (l, l), 0)
    c = lax.broadcasted_iota(jnp.int32, (l, l), 1)
    csum = _dot_exact_lhs(_ones_where(r >= c, BF16), val, 3)
    csum_t = _dot_exact_rhs(val_t, _ones_where(r <= c, BF16), 3)
    return val, csum, val_t, csum_t


def _gdn_scan_kernel(qkv_ref, kt_ref, gt_ref, gtt_ref, prow_ref, pcol_ref, s0_ref, o_ref, s_out_ref, s_scr,
                     *, l, cb, ng):
    gi = pl.program_id(1)

    @pl.when(gi == 0)
    def _():
        s_scr[...] = s0_ref[0]

    r = lax.broadcasted_iota(jnp.int32, (l, l), 0)
    c = lax.broadcasted_iota(jnp.int32, (l, l), 1)
    tri = r >= c
    stri = r > c
    eye = _ones_where(r == c, F32)
    n_double = max(int(np.ceil(np.log2(l))) - 1, 0)
    hs = range(GDN_HEADS)
    ps = [(j, h) for j in range(cb) for h in hs]
    gates = [_gates(gt_ref[j * l:(j + 1) * l, :], gtt_ref[j], prow_ref, pcol_ref, l) for j in range(cb)]

    def head_cols(j, group, h):
        return qkv_ref[j * l:(j + 1) * l, group * REC_W + h * HEAD_DIM:group * REC_W + (h + 1) * HEAD_DIM]

    q = {p: head_cols(p[0], 0, p[1]) for p in ps}
    k = {p: head_cols(p[0], 1, p[1]) for p in ps}
    v = {p: head_cols(p[0], 2, p[1]) for p in ps}
    kt = {p: kt_ref[p[0], p[1] * HEAD_DIM:(p[1] + 1) * HEAD_DIM, :] for p in ps}
    beta = {p: gates[p[0]][0][:, GATE_BETA + p[1]:GATE_BETA + p[1] + 1] for p in ps}
    g_col = {p: gates[p[0]][1][:, GATE_G + p[1]:GATE_G + p[1] + 1] for p in ps}
    g_row = {p: gates[p[0]][3][GATE_G + p[1]:GATE_G + p[1] + 1, :] for p in ps}
    g_last = {p: g_col[p][l - 1:l, :] for p in ps}
    decay = {p: jnp.where(tri, jnp.exp(jnp.where(tri, g_col[p] - g_row[p], 0.0)), 0.0) for p in ps}
    kb = {p: k[p] * beta[p] for p in ps}
    e_g = {p: jnp.exp(g_col[p]) for p in ps}
    kq = {p: _dot(_stack(kb[p], q[p]), kt[p]) for p in ps}
    x = {p: -jnp.where(stri, kq[p][:l] * decay[p], 0.0) for p in ps}
    tinv = {p: eye + x[p] for p in ps}
    if n_double > 0:
        x = {p: _dot(x[p], x[p]) for p in ps}
        for _ in range(n_double - 1):
            tx = {p: _dot(_stack(tinv[p], x[p]), x[p]) for p in ps}
            tinv = {p: tinv[p] + tx[p][:l] for p in ps}
            x = {p: tx[p][l:] for p in ps}
        tinv = {p: tinv[p] + _dot(tinv[p], x[p]) for p in ps}
    sol_v = {p: _dot(tinv[p], v[p] * beta[p]) for p in ps}
    sol_k = {p: _dot(tinv[p], kb[p] * e_g[p]) for p in ps}
    lhs_s = {p: _stack(q[p] * e_g[p], sol_k[p]) for p in ps}
    lhs_u = {p: _stack(kt[p] * jnp.exp(g_last[p] - g_row[p]), kq[p][l:] * decay[p]) for p in ps}
    e_last = {p: jnp.exp(g_last[p]) for p in ps}
    s = [s_scr[h] for h in hs]
    for j in range(cb):
        t = [_dot(lhs_s[(j, h)], s[h]) for h in hs]
        u = [sol_v[(j, h)] - t[h][l:] for h in hs]
        w = [_dot(lhs_u[(j, h)], u[h]) for h in hs]
        s = [e_last[(j, h)] * s[h] + w[h][:HEAD_DIM] for h in hs]
        for h in hs:
            o_ref[j * l:(j + 1) * l, h * HEAD_DIM:(h + 1) * HEAD_DIM] = t[h][:l] + w[h][HEAD_DIM:]
    for h in hs:
        s_scr[h] = s[h]

    @pl.when(gi == ng - 1)
    def _():
        s_out_ref[0] = s_scr[...]


def _gdn_scan(qkv, k_t, proj, gate_t, prow, pcol, s0, b, t, l, cb):
    ng = t // (l * cb)
    w3 = 3 * REC_W
    return pl.pallas_call(
        functools.partial(_gdn_scan_kernel, l=l, cb=cb, ng=ng),
        grid=(b, ng),
        in_specs=[pl.BlockSpec((cb * l, w3), lambda bi, i: (bi * ng + i, 0)),
                  pl.BlockSpec((cb, REC_W, l), lambda bi, i: (bi * ng + i, 0, 0)),
                  pl.BlockSpec((cb * l, LANES), lambda bi, i: (bi * ng + i, COL_GATE // LANES)),
                  pl.BlockSpec((cb, LANES, l), lambda bi, i: (bi * ng + i, 0, 0)),
                  pl.BlockSpec((SUBLANES, LANES), lambda bi, i: (0, 0)),
                  pl.BlockSpec((LANES, SUBLANES), lambda bi, i: (0, 0)),
                  pl.BlockSpec((1, GDN_HEADS, HEAD_DIM, HEAD_DIM), lambda bi, i: (bi, 0, 0, 0))],
        out_specs=[pl.BlockSpec((cb * l, REC_W), lambda bi, i: (bi * ng + i, 0)),
                   pl.BlockSpec((1, GDN_HEADS, HEAD_DIM, HEAD_DIM), lambda bi, i: (bi, 0, 0, 0))],
        out_shape=[jax.ShapeDtypeStruct((b * t, REC_W), F32),
                   jax.ShapeDtypeStruct((b, GDN_HEADS, HEAD_DIM, HEAD_DIM), F32)],
        scratch_shapes=[pltpu.VMEM((GDN_HEADS, HEAD_DIM, HEAD_DIM), F32)],
        compiler_params=_cparams("parallel", "arbitrary"),
        name="gdn_scan",
    )(qkv, k_t, proj, gate_t, prow, pcol, s0)


def _ml_scan_kernel(qkv_ref, kt_ref, gt_ref, gtt_ref, prow_ref, pcol_ref, c0_ref, n0_ref, m0_ref,
                    o_ref, c_out_ref, n_out_ref, m_out_ref, c_scr, n_scr, m_scr, *, l, cb, ng):
    gi = pl.program_id(1)

    @pl.when(gi == 0)
    def _():
        c_scr[...] = c0_ref[0]
        n_scr[...] = n0_ref[0]
        m_scr[...] = m0_ref[0]

    r = lax.broadcasted_iota(jnp.int32, (l, l), 0)
    c = lax.broadcasted_iota(jnp.int32, (l, l), 1)
    tri = r >= c
    lane = lax.broadcasted_iota(jnp.int32, (1, LANES), 1)
    m_all = m_scr[...]
    hs = range(ML_HEADS)
    ps = [(j, h) for j in range(cb) for h in hs]
    kscale = HEAD_DIM ** -0.5
    gates = [_gates(gt_ref[j * l:(j + 1) * l, :], gtt_ref[j], prow_ref, pcol_ref, l) for j in range(cb)]

    def head_cols(j, group, h):
        return qkv_ref[j * l:(j + 1) * l, group * REC_W + h * HEAD_DIM:group * REC_W + (h + 1) * HEAD_DIM]

    q = {p: head_cols(p[0], 0, p[1]) for p in ps}
    k = {p: head_cols(p[0], 1, p[1]) * kscale for p in ps}
    v = {p: head_cols(p[0], 2, p[1]) for p in ps}
    kt = {p: kt_ref[p[0], p[1] * HEAD_DIM:(p[1] + 1) * HEAD_DIM, :] * kscale for p in ps}
    ig_col = {p: gates[p[0]][0][:, GATE_I + p[1]:GATE_I + p[1] + 1] for p in ps}
    ig_row = {p: gates[p[0]][2][GATE_I + p[1]:GATE_I + p[1] + 1, :] for p in ps}
    f_col = {p: gates[p[0]][1][:, GATE_F + p[1]:GATE_F + p[1] + 1] for p in ps}
    f_row = {p: gates[p[0]][3][GATE_F + p[1]:GATE_F + p[1] + 1, :] for p in ps}
    f_last = {p: f_col[p][l - 1:l, :] for p in ps}
    qk = {p: _dot(q[p], kt[p]) for p in ps}
    d = {p: jnp.where(tri, f_col[p] - f_row[p] + ig_row[p], -jnp.inf) for p in ps}
    d_max = {p: jnp.max(d[p], axis=1, keepdims=True) for p in ps}
    m_prev, m_t = {}, {}
    m_run = [m_all[:, h:h + 1] for h in hs]
    for j in range(cb):
        for h in hs:
            m_prev[(j, h)] = m_run[h]
            m_t[(j, h)] = jnp.maximum(f_col[(j, h)] + m_run[h], d_max[(j, h)])
            m_run[h] = m_t[(j, h)][l - 1:l, :]
    m_new = {p: m_t[p][l - 1:l, :] for p in ps}
    w = {p: jnp.exp(d[p] - m_t[p]) * qk[p] for p in ps}
    c_inter = {p: jnp.exp(f_col[p] + m_prev[p] - m_t[p]) for p in ps}
    wv = {p: _dot(w[p], v[p]) for p in ps}
    w_sum = {p: jnp.sum(w[p], axis=1, keepdims=True) for p in ps}
    e_m = {p: jnp.exp(-m_t[p]) for p in ps}
    w_end_row = {p: jnp.exp(f_last[p] - f_row[p] + ig_row[p] - m_new[p]) for p in ps}
    w_end_col = {p: jnp.exp(f_last[p] - f_col[p] + ig_col[p] - m_new[p]) for p in ps}
    c_prev = {p: jnp.exp(f_last[p] + m_prev[p] - m_new[p]) for p in ps}
    dc = {p: _dot(kt[p] * w_end_row[p], v[p]) for p in ps}
    dn = {p: jnp.sum(w_end_col[p] * k[p], axis=0, keepdims=True) for p in ps}
    c_h = [c_scr[h] for h in hs]
    n_h = [n_scr[h:h + 1, :] for h in hs]
    for j in range(cb):
        for h in hs:
            p = (j, h)
            num = c_inter[p] * _dot(q[p], c_h[h]) + wv[p]
            den = c_inter[p] * jnp.sum(q[p] * n_h[h], axis=1, keepdims=True) + w_sum[p]
            o_ref[j * l:(j + 1) * l, h * HEAD_DIM:(h + 1) * HEAD_DIM] = num / jnp.maximum(jnp.abs(den), e_m[p])
            c_h[h] = c_prev[p] * c_h[h] + dc[p]
            n_h[h] = c_prev[p] * n_h[h] + dn[p]
    m_next = m_all
    for h in hs:
        c_scr[h] = c_h[h]
        n_scr[h:h + 1, :] = n_h[h]
        m_next = jnp.where(lane == h, m_run[h], m_next)
    m_scr[...] = m_next

    @pl.when(gi == ng - 1)
    def _():
        c_out_ref[0] = c_scr[...]
        n_out_ref[0] = n_scr[...]
        m_out_ref[0] = m_scr[...]


def _ml_scan(proj, k_t, gate_t, prow, pcol, c0, n0, m0, b, t, l, cb):
    ng = t // (l * cb)
    w3 = 3 * REC_W
    st = lambda bi, i: (bi, 0, 0, 0)
    st3 = lambda bi, i: (bi, 0, 0)
    return pl.pallas_call(
        functools.partial(_ml_scan_kernel, l=l, cb=cb, ng=ng),
        grid=(b, ng),
        in_specs=[pl.BlockSpec((cb * l, w3), lambda bi, i: (bi * ng + i, COL_MQKV // w3)),
                  pl.BlockSpec((cb, REC_W, l), lambda bi, i: (bi * ng + i, 0, 0)),
                  pl.BlockSpec((cb * l, LANES), lambda bi, i: (bi * ng + i, COL_GATE // LANES)),
                  pl.BlockSpec((cb, LANES, l), lambda bi, i: (bi * ng + i, 0, 0)),
                  pl.BlockSpec((SUBLANES, LANES), lambda bi, i: (0, 0)),
                  pl.BlockSpec((LANES, SUBLANES), lambda bi, i: (0, 0)),
                  pl.BlockSpec((1, ML_HEADS, HEAD_DIM, HEAD_DIM), st),
                  pl.BlockSpec((1, SUBLANES, HEAD_DIM), st3),
                  pl.BlockSpec((1, 1, LANES), st3)],
        out_specs=[pl.BlockSpec((cb * l, REC_W), lambda bi, i: (bi * ng + i, 0)),
                   pl.BlockSpec((1, ML_HEADS, HEAD_DIM, HEAD_DIM), st),
                   pl.BlockSpec((1, SUBLANES, HEAD_DIM), st3),
                   pl.BlockSpec((1, 1, LANES), st3)],
        out_shape=[jax.ShapeDtypeStruct((b * t, REC_W), F32),
                   jax.ShapeDtypeStruct((b, ML_HEADS, HEAD_DIM, HEAD_DIM), F32),
                   jax.ShapeDtypeStruct((b, SUBLANES, HEAD_DIM), F32),
                   jax.ShapeDtypeStruct((b, 1, LANES), F32)],
        scratch_shapes=[pltpu.VMEM((ML_HEADS, HEAD_DIM, HEAD_DIM), F32),
                        pltpu.VMEM((SUBLANES, HEAD_DIM), F32),
                        pltpu.VMEM((1, LANES), F32)],
        compiler_params=_cparams("parallel", "arbitrary"),
        name="ml_scan",
    )(proj, k_t, proj, gate_t, prow, pcol, c0, n0, m0)


def _ml_scan_t_kernel(qkv_ref, qkvt_ref, gt_ref, gtt_ref, prow_ref, pcol_ref, c0_ref, n0_ref, m0_ref,
                      o_ref, c_out_ref, n_out_ref, m_out_ref, c_scr, n_scr, m_scr, *, l, cb, ng):
    gi = pl.program_id(1)

    @pl.when(gi == 0)
    def _():
        c_scr[...] = c0_ref[0]
        n_scr[...] = n0_ref[0]
        m_scr[...] = m0_ref[0]

    r = lax.broadcasted_iota(jnp.int32, (l, l), 0)
    c = lax.broadcasted_iota(jnp.int32, (l, l), 1)
    tri_t = r <= c
    lane = lax.broadcasted_iota(jnp.int32, (1, LANES), 1)
    m_all = m_scr[...]
    hs = range(ML_HEADS)
    ps = [(j, h) for j in range(cb) for h in hs]
    kscale = HEAD_DIM ** -0.5
    zeros_t = jnp.zeros((HEAD_DIM, l), F32)
    gates = [_gates(gt_ref[j * l:(j + 1) * l, :], gtt_ref[j], prow_ref, pcol_ref, l) for j in range(cb)]

    def t_rows(j, group, h):
        return qkvt_ref[j, group * REC_W + h * HEAD_DIM:group * REC_W + (h + 1) * HEAD_DIM, :]

    qt = {p: t_rows(p[0], 0, p[1]) for p in ps}
    qt_pad = {p: (_stack(qt[p], zeros_t) if p[1] % 2 == 0 else _stack(zeros_t, qt[p])) for p in ps}
    vt = {p: t_rows(p[0], 2, p[1]) for p in ps}
    k2 = {p: qkv_ref[p[0] * l:(p[0] + 1) * l, REC_W + (p[1] // 2) * LANES:REC_W + (p[1] // 2 + 1) * LANES] * kscale
          for p in ps}
    ig_col = {p: gates[p[0]][0][:, GATE_I + p[1]:GATE_I + p[1] + 1] for p in ps}
    ig_row = {p: gates[p[0]][2][GATE_I + p[1]:GATE_I + p[1] + 1, :] for p in ps}
    f_col = {p: gates[p[0]][1][:, GATE_F + p[1]:GATE_F + p[1] + 1] for p in ps}
    f_row = {p: gates[p[0]][3][GATE_F + p[1]:GATE_F + p[1] + 1, :] for p in ps}
    f_last = {p: f_row[p][:, l - 1:l] for p in ps}
    kq = {p: _dot(k2[p], qt_pad[p]) for p in ps}
    d = {p: jnp.where(tri_t, f_row[p] + (ig_col[p] - f_col[p]), -jnp.inf) for p in ps}
    d_max = {p: jnp.max(d[p], axis=0, keepdims=True) for p in ps}
    m_prev, m_t = {}, {}
    m_run = [m_all[:, h:h + 1] for h in hs]
    for j in range(cb):
        for h in hs:
            m_prev[(j, h)] = m_run[h]
            m_t[(j, h)] = jnp.maximum(f_row[(j, h)] + m_run[h], d_max[(j, h)])
            m_run[h] = m_t[(j, h)][:, l - 1:l]
    m_new = {p: m_t[p][:, l - 1:l] for p in ps}
    w = {p: jnp.exp(d[p] - m_t[p]) * kq[p] for p in ps}
    c_inter = {p: jnp.exp(f_row[p] + m_prev[p] - m_t[p]) for p in ps}
    vw = {p: _dot(vt[p], w[p]) for p in ps}
    w_sum = {p: jnp.sum(w[p], axis=0, keepdims=True) for p in ps}
    e_m = {p: jnp.exp(-m_t[p]) for p in ps}
    w_end_col = {p: jnp.exp(f_last[p] - f_col[p] + ig_col[p] - m_new[p]) for p in ps}
    c_prev = {p: jnp.exp(f_last[p] + m_prev[p] - m_new[p]) for p in ps}
    wk = {p: w_end_col[p] * k2[p] for p in ps}
    dc = {p: _dot(vt[p], wk[p]) for p in ps}
    dn = {p: jnp.sum(wk[p], axis=0, keepdims=True) for p in ps}
    c_h = [c_scr[h] for h in hs]
    n_h = [n_scr[h:h + 1, :] for h in hs]
    ht = {}
    for j in range(cb):
        for h in hs:
            p = (j, h)
            num = c_inter[p] * _dot(c_h[h], qt_pad[p]) + vw[p]
            den = c_inter[p] * _dot(n_h[h], qt_pad[p]) + w_sum[p]
            ht[p] = num / jnp.maximum(jnp.abs(den), e_m[p])
            c_h[h] = c_prev[p] * c_h[h] + dc[p]
            n_h[h] = c_prev[p] * n_h[h] + dn[p]
    for i in range(ML_HEADS // 2):
        pair_t = _stack(jnp.concatenate([ht[(j, 2 * i)] for j in range(cb)], axis=1),
                        jnp.concatenate([ht[(j, 2 * i + 1)] for j in range(cb)], axis=1))
        o_ref[:, i * LANES:(i + 1) * LANES] = pair_t.T
    m_next = m_all
    for h in hs:
        c_scr[h] = c_h[h]
        n_scr[h:h + 1, :] = n_h[h]
        m_next = jnp.where(lane == h, m_run[h], m_next)
    m_scr[...] = m_next

    @pl.when(gi == ng - 1)
    def _():
        c_out_ref[0] = c_scr[...]
        n_out_ref[0] = n_scr[...]
        m_out_ref[0] = m_scr[...]


def _ml_scan_t(proj, qkv_t, gate_t, prow, pcol, c0, n0, m0, b, t, l, cb):
    ng = t // (l * cb)
    w3 = 3 * REC_W
    st = lambda bi, i: (bi, 0, 0, 0)
    st3 = lambda bi, i: (bi, 0, 0)
    return pl.pallas_call(
        functools.partial(_ml_scan_t_kernel, l=l, cb=cb, ng=ng),
        grid=(b, ng),
        in_specs=[pl.BlockSpec((cb * l, w3), lambda bi, i: (bi * ng + i, COL_MQKV // w3)),
                  pl.BlockSpec((cb, w3, l), lambda bi, i: (bi * ng + i, 0, 0)),
                  pl.BlockSpec((cb * l, LANES), lambda bi, i: (bi * ng + i, COL_GATE // LANES)),
                  pl.BlockSpec((cb, LANES, l), lambda bi, i: (bi * ng + i, 0, 0)),
                  pl.BlockSpec((SUBLANES, LANES), lambda bi, i: (0, 0)),
                  pl.BlockSpec((LANES, SUBLANES), lambda bi, i: (0, 0)),
                  pl.BlockSpec((1, ML_HEADS, HEAD_DIM, LANES), st),
                  pl.BlockSpec((1, SUBLANES, LANES), st3),
                  pl.BlockSpec((1, 1, LANES), st3)],
        out_specs=[pl.BlockSpec((cb * l, REC_W), lambda bi, i: (bi * ng + i, 0)),
                   pl.BlockSpec((1, ML_HEADS, HEAD_DIM, LANES), st),
                   pl.BlockSpec((1, SUBLANES, LANES), st3),
                   pl.BlockSpec((1, 1, LANES), st3)],
        out_shape=[jax.ShapeDtypeStruct((b * t, REC_W), F32),
                   jax.ShapeDtypeStruct((b, ML_HEADS, HEAD_DIM, LANES), F32),
                   jax.ShapeDtypeStruct((b, SUBLANES, LANES), F32),
                   jax.ShapeDtypeStruct((b, 1, LANES), F32)],
        scratch_shapes=[pltpu.VMEM((ML_HEADS, HEAD_DIM, LANES), F32),
                        pltpu.VMEM((SUBLANES, LANES), F32),
                        pltpu.VMEM((1, LANES), F32)],
        compiler_params=_cparams("parallel", "arbitrary"),
        name="ml_scan_t",
    )(proj, qkv_t, proj, gate_t, prow, pcol, c0, n0, m0)


def _post_kernel(x_ref, osb_ref, og_ref, hm_ref, gz_ref, mo_ref, w_ref, e_ref, gnw_ref, mnw_ref,
                 lg_ref, lb_ref, o_ref, ob_ref):
    e = e_ref[...]
    og = og_ref[...]
    hm = hm_ref[...]
    og_ms = _dot_exact_rhs(og * og, e, 2) * (1.0 / HEAD_DIM)
    hm_ms = _dot_exact_rhs(hm * hm, e, 2) * (1.0 / HEAD_DIM)
    gz = gz_ref[...]
    o_g = og * lax.rsqrt(og_ms + NORM_EPS) * gnw_ref[...] * (gz * _sigmoid(gz))
    o_m = _sigmoid(mo_ref[...]) * (hm * lax.rsqrt(hm_ms + NORM_EPS) * mnw_ref[...])
    mixed = (_dot(osb_ref[...], w_ref[0:SB_W, :])
             + _dot(o_g, w_ref[SB_W:SB_W + REC_W, :])
             + _dot(o_m, w_ref[SB_W + REC_W:SB_W + 2 * REC_W, :]))
    y = _layer_norm(DEEPNORM_ALPHA * x_ref[...] + mixed, lg_ref[...], lb_ref[...])
    o_ref[...] = y
    ob_ref[...] = y.astype(BF16)


def _post(x, osb, og, hm, proj, w_out, e_heads, gnw, mnw, lg, lb, tm):
    n, d = x.shape
    row = lambda i: (i, 0)
    fixed = lambda i: (0, 0)
    return pl.pallas_call(
        _post_kernel,
        grid=(n // tm,),
        in_specs=[pl.BlockSpec((tm, d), row),
                  pl.BlockSpec((tm, SB_W), row),
                  pl.BlockSpec((tm, REC_W), row),
                  pl.BlockSpec((tm, REC_W), row),
                  pl.BlockSpec((tm, REC_W), lambda i: (i, COL_GZ // REC_W)),
                  pl.BlockSpec((tm, REC_W), lambda i: (i, COL_MO // REC_W)),
                  pl.BlockSpec(w_out.shape, fixed),
                  pl.BlockSpec((REC_W, REC_W), fixed),
                  pl.BlockSpec((1, REC_W), fixed),
                  pl.BlockSpec((1, REC_W), fixed),
                  pl.BlockSpec((1, d), fixed),
                  pl.BlockSpec((1, d), fixed)],
        out_specs=[pl.BlockSpec((tm, d), row), pl.BlockSpec((tm, d), row)],
        out_shape=[jax.ShapeDtypeStruct((n, d), F32), jax.ShapeDtypeStruct((n, d), BF16)],
        compiler_params=_cparams("parallel"),
        name="post",
    )(x, osb, og, hm, proj, proj, w_out, e_heads, gnw, mnw, lg, lb)


def _memattn_kernel(x_ref, xb_ref, wq_ref, wo_ref, mk_ref, mv_ref, lg_ref, lb_ref, o_ref, ob_ref):
    d = x_ref.shape[1]
    hd = d // MEM_HEADS
    q = jnp.dot(xb_ref[...], wq_ref[...], preferred_element_type=F32)
    out = None
    for h in range(MEM_HEADS):
        sl = slice(h * hd, (h + 1) * hd)
        s = _dot_nt(q[:, sl], mk_ref[0, :, sl]) * (hd ** -0.5)
        p = jnp.exp(s - jnp.max(s, axis=1, keepdims=True))
        o_h = _dot(p, mv_ref[0, :, sl]) / jnp.sum(p, axis=1, keepdims=True)
        t = _dot(o_h, wo_ref[sl, :])
        out = t if out is None else out + t
    y = _layer_norm(DEEPNORM_ALPHA * x_ref[...] + out, lg_ref[...], lb_ref[...])
    o_ref[...] = y
    ob_ref[...] = y.astype(BF16)


def _memattn(x, xb, w_cq, w_co, mk, mv, lg, lb, b, t, tm):
    n, d = x.shape
    nt = t // tm
    nm = mk.shape[1]
    row = lambda bi, i: (bi * nt + i, 0)
    fixed = lambda bi, i: (0, 0)
    return pl.pallas_call(
        _memattn_kernel,
        grid=(b, nt),
        in_specs=[pl.BlockSpec((tm, d), row),
                  pl.BlockSpec((tm, d), row),
                  pl.BlockSpec((d, d), fixed),
                  pl.BlockSpec((d, d), fixed),
                  pl.BlockSpec((1, nm, d), lambda bi, i: (bi, 0, 0)),
                  pl.BlockSpec((1, nm, d), lambda bi, i: (bi, 0, 0)),
                  pl.BlockSpec((1, d), fixed),
                  pl.BlockSpec((1, d), fixed)],
        out_specs=[pl.BlockSpec((tm, d), row), pl.BlockSpec((tm, d), row)],
        out_shape=[jax.ShapeDtypeStruct((n, d), F32), jax.ShapeDtypeStruct((n, d), BF16)],
        compiler_params=_cparams("parallel", "parallel"),
        name="memattn",
    )(x, xb, w_cq, w_co, mk, mv, lg, lb)


def _ffn_kernel(x_ref, xb_ref, wg_ref, wu_ref, wd_ref, lg_ref, lb_ref, o_ref, ob_ref, acc_ref, *, nf):
    j = pl.program_id(1)

    @pl.when(j == 0)
    def _():
        acc_ref[...] = jnp.zeros_like(acc_ref)

    xb = xb_ref[...]
    gate = jnp.dot(xb, wg_ref[...], preferred_element_type=F32)
    up = jnp.dot(xb, wu_ref[...], preferred_element_type=F32)
    acc_ref[...] += _dot(gate * _sigmoid(gate) * up, wd_ref[...])

    @pl.when(j == nf - 1)
    def _():
        y = _layer_norm(DEEPNORM_ALPHA * x_ref[...] + acc_ref[...], lg_ref[...], lb_ref[...])
        o_ref[...] = y
        ob_ref[...] = y.astype(BF16)


def _ffn(x, xb, w_up, w_down, lg, lb, tm, tf):
    n, d = x.shape
    dff = w_down.shape[0]
    nf = dff // tf
    row = lambda i, j: (i, 0)
    fixed = lambda i, j: (0, 0)
    return pl.pallas_call(
        functools.partial(_ffn_kernel, nf=nf),
        grid=(n // tm, nf),
        in_specs=[pl.BlockSpec((tm, d), row),
                  pl.BlockSpec((tm, d), row),
                  pl.BlockSpec((d, tf), lambda i, j: (0, j)),
                  pl.BlockSpec((d, tf), lambda i, j: (0, nf + j)),
                  pl.BlockSpec((tf, d), lambda i, j: (j, 0)),
                  pl.BlockSpec((1, d), fixed),
                  pl.BlockSpec((1, d), fixed)],
        out_specs=[pl.BlockSpec((tm, d), row), pl.BlockSpec((tm, d), row)],
        out_shape=[jax.ShapeDtypeStruct((n, d), F32), jax.ShapeDtypeStruct((n, d), BF16)],
        scratch_shapes=[pltpu.VMEM((tm, d), F32)],
        compiler_params=_cparams("parallel", "arbitrary"),
        name="ffn",
    )(x, xb, w_up, w_up, w_down, lg, lb)


def _pack_layer(p):
    w_in = p['w_in']
    d = w_in.shape[0]
    o_sb, o_gqkv, o_gz = 0, 3 * SB_W, 3 * SB_W + 3 * REC_W
    o_gb = o_gz + REC_W
    o_mqkv = o_gb + 2 * GDN_HEADS
    o_mo = o_mqkv + 3 * REC_W
    o_mi = o_mo + REC_W
    gate_cols = jnp.concatenate([w_in[:, o_gb:o_gb + 2 * GDN_HEADS], w_in[:, o_mi:o_mi + 2 * ML_HEADS]], axis=1)
    used = COL_GATE + gate_cols.shape[1]
    w_al = jnp.concatenate([w_in[:, o_gqkv:o_gqkv + 3 * REC_W], w_in[:, o_mqkv:o_mqkv + 3 * REC_W],
                            w_in[:, o_gz:o_gz + REC_W], w_in[:, o_mo:o_mo + REC_W],
                            w_in[:, o_sb:o_sb + 3 * SB_W], gate_cols,
                            jnp.zeros((d, PROJ_W - used), w_in.dtype)], axis=1).astype(BF16)
    prow = jnp.zeros((SUBLANES, LANES), F32)
    prow = prow.at[0, GATE_G:GATE_G + GDN_HEADS].set(p['gdn_dt_bias'])
    prow = prow.at[0, GATE_I:GATE_I + ML_HEADS].set(p['mlstm_i_bias'])
    prow = prow.at[0, GATE_F:GATE_F + ML_HEADS].set(p['mlstm_f_bias'])
    prow = prow.at[1, GATE_G:GATE_G + GDN_HEADS].set(p['gdn_A_log'])
    conv_w8 = jnp.zeros((SUBLANES, 3 * REC_W), F32).at[0:CONV_W].set(p['gdn_conv_w'])
    return dict(
        w_al=w_al, prow=prow, pcol=prow.T, conv_w8=conv_w8,
        gnw=jnp.tile(p['gdn_norm_w'], GDN_HEADS)[None, :], mnw=p['mlstm_norm_w'][None, :],
        w_out=p['w_out'].astype(BF16), w_cq=p['w_cq'].astype(BF16), w_co=p['w_co'].astype(BF16),
        w_up=p['w_up'].astype(BF16), w_down=p['w_down'].astype(BF16),
        ln1=(p['ln1_g'][None, :], p['ln1_b'][None, :]), ln2=(p['ln2_g'][None, :], p['ln2_b'][None, :]),
        ln3=(p['ln3_g'][None, :], p['ln3_b'][None, :]))


def _head_consts():
    hid = np.arange(REC_W) // HEAD_DIM
    return jnp.asarray(hid[:, None] == hid[None, :], BF16)


def _sb_layout(k, v, tk):
    b, tkv, _ = k.shape
    nkb = tkv // tk
    kt = k.reshape(b, nkb, tk, SB_W).transpose(0, 1, 3, 2)
    vh = v.reshape(b, nkb, tk, SB_HEADS, HEAD_DIM).transpose(0, 3, 1, 2, 4)
    vh = vh.reshape(b * SB_HEADS, nkb, tk, HEAD_DIM)
    return kt.astype(BF16), vh.astype(BF16)


def _to_pair_lanes(a):
    z = jnp.zeros_like(a)
    even = (jnp.arange(a.shape[1]) % 2 == 0)[None, :, None, None]
    return jnp.where(even, jnp.concatenate([a, z], axis=-1), jnp.concatenate([z, a], axis=-1))


def _from_pair_lanes(a):
    even = (jnp.arange(a.shape[1]) % 2 == 0)[None, :, None, None]
    return jnp.where(even, a[..., :HEAD_DIM], a[..., HEAD_DIM:])


def _time_on_lanes(a, groups, width):
    return a.reshape(groups, width, a.shape[1]).transpose(0, 2, 1)


def _trunk_layer(x, xb, pk, e_heads, mk, mv, b, t, cfg, sb_past, conv_ctx, gdn_s, ml_c, ml_n, ml_m):
    n, d = x.shape
    l = min(t, CHUNK)
    nc = t // l
    cb = min(cfg['scan_cb'], nc)
    ng = b * nc // cb
    tk = cfg['sb_tk']
    ctx8 = jnp.pad(conv_ctx, ((0, 0), (SUBLANES - (CONV_W - 1), 0), (0, 0)))
    if sb_past is None:
        proj, kt, vh, mk_t, gate_t = _proj(x if xb is None else xb, pk['w_al'], cfg['tm'])
        kt = kt.reshape(b, t // tk, SB_W, tk)
        q_off = 0
        gqkv_act, gk_t = _gdn_pre(proj, ctx8, pk['conv_w8'], e_heads, b, t, cfg['conv_tt'])
    else:
        proj = _matmul(xb, pk['w_al'], cfg['tm'], 1024)
    sk = proj[:, COL_SB + SB_W:COL_SB + 2 * SB_W].reshape(b, t, SB_W)
    sv = proj[:, COL_SB + 2 * SB_W:COL_SB + 3 * SB_W].reshape(b, t, SB_W)
    if sb_past is not None:
        past_k, past_v = sb_past
        q_off = past_k.shape[1]
        kv_len = q_off + t
        kv_pad = -(-kv_len // tk) * tk - kv_len
        kk = jnp.pad(jnp.concatenate([past_k.reshape(b, q_off, SB_W), sk], axis=1), ((0, 0), (0, kv_pad), (0, 0)))
        vv = jnp.pad(jnp.concatenate([past_v.reshape(b, q_off, SB_W), sv], axis=1), ((0, 0), (0, kv_pad), (0, 0)))
        kt, vh = _sb_layout(kk, vv, tk)
        gqkv_act, _ = _gdn_pre(proj, ctx8, pk['conv_w8'], e_heads, b, t, cfg['conv_tt'])
        gate_t = _time_on_lanes(proj[:, COL_GATE:COL_GATE + LANES], b * nc, l)
        gk_t = _time_on_lanes(gqkv_act[:, REC_W:2 * REC_W], b * nc, l)
        mk_t = _time_on_lanes(proj[:, COL_MQKV + REC_W:COL_MQKV + 2 * REC_W], b * nc, l)

    o_sb = _sb_attn(proj, kt, vh, b, t, SB_HEADS, cfg['sb_tq'], q_off)
    o_g, gdn_s_new = _gdn_scan(gqkv_act, gk_t, proj, gate_t, pk['prow'], pk['pcol'], gdn_s, b, t, l, cb)
    m0 = jnp.pad(ml_m, ((0, 0), (0, LANES - ML_HEADS)))[:, None, :]
    if sb_past is None:
        c0 = _to_pair_lanes(jnp.swapaxes(ml_c, -1, -2))
        n0 = jnp.pad(_to_pair_lanes(ml_n[:, :, None, :])[:, :, 0, :], ((0, 0), (0, SUBLANES - ML_HEADS), (0, 0)))
        h_m, c_new, n_new, m_new = _ml_scan_t(proj, mk_t, gate_t, pk['prow'], pk['pcol'], c0, n0, m0, b, t, l, cb)
        c_new = jnp.swapaxes(_from_pair_lanes(c_new), -1, -2)
        n_new = _from_pair_lanes(n_new[:, :ML_HEADS, None, :])[:, :, 0, :]
    else:
        n0 = jnp.pad(ml_n, ((0, 0), (0, SUBLANES - ML_HEADS), (0, 0)))
        h_m, c_new, n_new, m_new = _ml_scan(proj, mk_t, gate_t, pk['prow'], pk['pcol'], ml_c, n0, m0, b, t, l, cb)
        n_new = n_new[:, :ML_HEADS, :]

    x1, x1b = _post(x, o_sb, o_g, h_m, proj, pk['w_out'], e_heads, pk['gnw'], pk['mnw'], *pk['ln1'], cfg['tm'])
    x2, x2b = _memattn(x1, x1b, pk['w_cq'], pk['w_co'], mk, mv, *pk['ln2'], b, t, cfg['mem_tm'])
    x3, x3b = _ffn(x2, x2b, pk['w_up'], pk['w_down'], *pk['ln3'], cfg['tm'], cfg['ffn_tf'])

    new_conv = proj.reshape(b, t, PROJ_W)[:, t - (CONV_W - 1):, COL_GQKV:COL_GQKV + 3 * REC_W]
    return (x3, x3b, sk.reshape(b, t, SB_HEADS, HEAD_DIM), sv.reshape(b, t, SB_HEADS, HEAD_DIM), new_conv,
            gdn_s_new, c_new, n_new, m_new[:, 0, :ML_HEADS])


def kernel(x_prompt, x_sample, cache_sb_k, cache_sb_v, cache_gdn_conv, state_gdn, state_mlstm_C, state_mlstm_n, state_mlstm_m, cache_mem_k, cache_mem_v, mem_prompt, w_in, gdn_conv_w, gdn_A_log, gdn_dt_bias, gdn_norm_w, mlstm_i_bias, mlstm_f_bias, mlstm_norm_w, w_out, ln1_g, ln1_b, w_cq, w_ckv, w_co, ln2_g, ln2_b, w_up, w_down, ln3_g, ln3_b):
    bp, tp, d = x_prompt.shape
    bs, ts, _ = x_sample.shape
    depth = w_in.shape[0]
    n_mem = mem_prompt.shape[1]
    dff = w_down.shape[1]
    cfg_p = dict(tm=512, sb_tq=1024, sb_tk=256, conv_tt=512, mem_tm=512, ffn_tf=dff // 2, scan_cb=4)
    cfg_s = dict(tm=bs * ts, sb_tq=ts, sb_tk=256, conv_tt=ts, mem_tm=ts, ffn_tf=dff // 2, scan_cb=1)
    e_heads = _head_consts()

    xp = x_prompt.reshape(bp * tp, d)
    xs = x_sample.reshape(bs * ts, d)
    xpb, xsb = None, xs.astype(BF16)
    mem_b = mem_prompt.reshape(bp * n_mem, d).astype(BF16)
    p_out = [[] for _ in range(9)]
    s_out = [[] for _ in range(7)]
    for li in range(depth):
        p = {'w_in': w_in[li], 'gdn_conv_w': gdn_conv_w[li], 'gdn_A_log': gdn_A_log[li],
             'gdn_dt_bias': gdn_dt_bias[li], 'gdn_norm_w': gdn_norm_w[li], 'mlstm_i_bias': mlstm_i_bias[li],
             'mlstm_f_bias': mlstm_f_bias[li], 'mlstm_norm_w': mlstm_norm_w[li], 'w_out': w_out[li],
             'ln1_g': ln1_g[li], 'ln1_b': ln1_b[li], 'w_cq': w_cq[li], 'w_co': w_co[li],
             'ln2_g': ln2_g[li], 'ln2_b': ln2_b[li], 'w_up': w_up[li], 'w_down': w_down[li],
             'ln3_g': ln3_g[li], 'ln3_b': ln3_b[li]}
        pk = _pack_layer(p)
        mkv = _matmul(mem_b, w_ckv[li].astype(BF16), n_mem, 1024)
        mk_p = mkv[:, :d].reshape(bp, n_mem, d)
        mv_p = mkv[:, d:].reshape(bp, n_mem, d)
        res = _trunk_layer(xp, xpb, pk, e_heads, mk_p.astype(BF16), mv_p.astype(BF16), bp, tp, cfg_p, None,
                           jnp.zeros((bp, CONV_W - 1, 3 * REC_W), F32),
                           jnp.zeros((bp, GDN_HEADS, HEAD_DIM, HEAD_DIM), F32),
                           jnp.zeros((bp, ML_HEADS, HEAD_DIM, HEAD_DIM), F32),
                           jnp.zeros((bp, ML_HEADS, HEAD_DIM), F32),
                           jnp.zeros((bp, ML_HEADS), F32))
        xp, xpb = res[0], res[1]
        hd = d // MEM_HEADS
        for j, a in enumerate(res[2:] + (mk_p.reshape(bp, n_mem, MEM_HEADS, hd), mv_p.reshape(bp, n_mem, MEM_HEADS, hd))):
            p_out[j].append(a)
        res = _trunk_layer(xs, xsb, pk, e_heads,
                           cache_mem_k[li].reshape(bs, n_mem, d).astype(BF16),
                           cache_mem_v[li].reshape(bs, n_mem, d).astype(BF16),
                           bs, ts, cfg_s, (cache_sb_k[li], cache_sb_v[li]), cache_gdn_conv[li],
                           state_gdn[li], state_mlstm_C[li], state_mlstm_n[li], state_mlstm_m[li])
        xs, xsb = res[0], res[1]
        for j, a in enumerate(res[2:]):
            s_out[j].append(a)
    p_st = [jnp.stack(a) for a in p_out]
    s_st = [jnp.stack(a) for a in s_out]
    return (xp.reshape(bp, tp, d), xs.reshape(bs, ts, d), *p_st, *s_st)
```

```python
import functools

import numpy as np
import jax
import jax.numpy as jnp
from jax import lax
from jax.experimental import pallas as pl
from jax.experimental.pallas import tpu as pltpu

F32 = jnp.float32
BF16 = jnp.bfloat16

HEAD_DIM = 64
SB_HEADS = 4
GDN_HEADS = 6
ML_HEADS = 6
REC_W = GDN_HEADS * HEAD_DIM
SB_W = SB_HEADS * HEAD_DIM
CONV_W = 4
CHUNK = 64
MEM_HEADS = 4
LN_EPS = 1e-5
NORM_EPS = 1e-6
DEPTH = 2
DEEPNORM_ALPHA = (2 * DEPTH) ** 0.25
LOG2E = float(np.log2(np.e))
SOFTPLUS2_CLAMP = 120.0
F32_EXP2_ZERO = 160.0
BOUND_SLACK = 1.01

GROUP = 256
LANES = 128
SUBLANES = 8
VMEM_LIMIT = 56 * 1024 * 1024

PROJ_W = 4096
COL_GQKV = 0
COL_MQKV = 1152
COL_GZ = 2304
COL_MO = 2688
COL_SB = 3072
COL_GATE = 3840
GATE_BETA, GATE_G, GATE_I, GATE_F = 0, 6, 12, 18


def _cparams(*sem):
    return pltpu.CompilerParams(dimension_semantics=sem, vmem_limit_bytes=VMEM_LIMIT)


def _dot(a, b):
    return jnp.dot(a.astype(BF16), b.astype(BF16), preferred_element_type=F32)


def _dot_nt(a, b):
    return lax.dot_general(a.astype(BF16), b.astype(BF16), (((1,), (1,)), ((), ())),
                           preferred_element_type=F32)


def _dot_tn(a, b):
    return lax.dot_general(a.astype(BF16), b.astype(BF16), (((0,), (0,)), ((), ())),
                           preferred_element_type=F32)


def _split(x, parts):
    out = []
    r = x
    for _ in range(parts - 1):
        p = r.astype(BF16)
        out.append(p)
        r = r - p.astype(F32)
    out.append(r.astype(BF16))
    return out


def _dot_exact_lhs(m, x, parts):
    acc = None
    for p in _split(x, parts):
        t = jnp.dot(m, p, preferred_element_type=F32)
        acc = t if acc is None else acc + t
    return acc


def _dot_exact_rhs(x, m, parts):
    acc = None
    for p in _split(x, parts):
        t = jnp.dot(p, m, preferred_element_type=F32)
        acc = t if acc is None else acc + t
    return acc


def _ones_where(mask, dtype):
    return jnp.where(mask, 1.0, 0.0).astype(dtype)


def _sigmoid(x):
    return 1.0 / (1.0 + jnp.exp(-x))


def _layer_norm(y, g, b):
    mu = jnp.mean(y, axis=-1, keepdims=True)
    d = y - mu
    var = jnp.mean(d * d, axis=-1, keepdims=True)
    return d * lax.rsqrt(var + LN_EPS) * g + b


def _mm_kernel(x_ref, w_ref, o_ref):
    o_ref[...] = jnp.dot(x_ref[...], w_ref[...], preferred_element_type=F32).astype(o_ref.dtype)


def _matmul(x, w, tm, tn, out_dtype=F32):
    n, k = x.shape
    m = w.shape[1]
    return pl.pallas_call(
        _mm_kernel,
        grid=(n // tm, m // tn),
        in_specs=[pl.BlockSpec((tm, k), lambda i, j: (i, 0)),
                  pl.BlockSpec((k, tn), lambda i, j: (0, j))],
        out_specs=pl.BlockSpec((tm, tn), lambda i, j: (i, j)),
        out_shape=jax.ShapeDtypeStruct((n, m), out_dtype),
        compiler_params=_cparams("parallel", "parallel"),
        name="proj_matmul",
    )(x, w)


def _proj_kernel(x_ref, w_ref, o_ref, skt_ref, sv_ref, mkt_ref, gtt_ref, *, tm, tn):
    xb = x_ref[...].astype(BF16)
    for j in range(PROJ_W // tn):
        o_ref[:, j * tn:(j + 1) * tn] = jnp.dot(xb, w_ref[:, j * tn:(j + 1) * tn], preferred_element_type=F32)
    for r in range(tm // GROUP):
        rows = slice(r * GROUP, (r + 1) * GROUP)
        skt_ref[r] = o_ref[rows, COL_SB + SB_W:COL_SB + 2 * SB_W].T.astype(BF16)
        for h in range(SB_HEADS):
            c0 = COL_SB + 2 * SB_W + h * HEAD_DIM
            sv_ref[h, r] = o_ref[rows, c0:c0 + HEAD_DIM].astype(BF16)
        _store_chunks_t(mkt_ref, r, o_ref[rows, COL_MQKV:COL_MQKV + 3 * REC_W])
        _store_chunks_t(gtt_ref, r, o_ref[rows, COL_GATE:COL_GATE + LANES])


def _store_chunks_t(ref, r, a):
    at = a.T
    per = GROUP // CHUNK
    for j in range(per):
        ref[r * per + j] = at[:, j * CHUNK:(j + 1) * CHUNK]


def _proj(x, w, tm):
    n, d = x.shape
    ng = n // GROUP
    gpt = tm // GROUP
    cpt = tm // CHUNK
    return pl.pallas_call(
        functools.partial(_proj_kernel, tm=tm, tn=1024),
        grid=(n // tm,),
        in_specs=[pl.BlockSpec((tm, d), lambda i: (i, 0)),
                  pl.BlockSpec((d, PROJ_W), lambda i: (0, 0), pipeline_mode=pl.Buffered(1))],
        out_specs=[pl.BlockSpec((tm, PROJ_W), lambda i: (i, 0)),
                   pl.BlockSpec((gpt, SB_W, GROUP), lambda i: (i, 0, 0)),
                   pl.BlockSpec((SB_HEADS, gpt, GROUP, HEAD_DIM), lambda i: (0, i, 0, 0)),
                   pl.BlockSpec((cpt, 3 * REC_W, CHUNK), lambda i: (i, 0, 0)),
                   pl.BlockSpec((cpt, LANES, CHUNK), lambda i: (i, 0, 0))],
        out_shape=[jax.ShapeDtypeStruct((n, PROJ_W), F32),
                   jax.ShapeDtypeStruct((ng, SB_W, GROUP), BF16),
                   jax.ShapeDtypeStruct((SB_HEADS, ng, GROUP, HEAD_DIM), BF16),
                   jax.ShapeDtypeStruct((n // CHUNK, 3 * REC_W, CHUNK), F32),
                   jax.ShapeDtypeStruct((n // CHUNK, LANES, CHUNK), F32)],
        compiler_params=_cparams("parallel"),
        name="proj_full",
    )(x, w)


def _gdn_pre_kernel(x_ref, ctx_ref, w_ref, e_ref, o_ref, *rest, tt):
    xbuf = rest[-1]
    i = pl.program_id(1)

    @pl.when(i == 0)
    def _():
        xbuf[0:SUBLANES, :] = ctx_ref[0]

    xbuf[SUBLANES:SUBLANES + tt, :] = x_ref[...]
    acc = w_ref[CONV_W - 1:CONV_W, :] * xbuf[SUBLANES:SUBLANES + tt, :]
    for j in range(CONV_W - 1):
        off = SUBLANES - (CONV_W - 1) + j
        acc = acc + w_ref[j:j + 1, :] * xbuf[off:off + tt, :]
    y = acc * _sigmoid(acc)
    q = y[:, 0:REC_W]
    k = y[:, REC_W:2 * REC_W]
    e = e_ref[...]
    qs = _dot_exact_rhs(q * q, e, 2)
    ks = _dot_exact_rhs(k * k, e, 2)
    kn = k * lax.rsqrt(ks + NORM_EPS)
    o_ref[:, 0:REC_W] = q * lax.rsqrt(qs + NORM_EPS) * (HEAD_DIM ** -0.5)
    o_ref[:, REC_W:2 * REC_W] = kn
    o_ref[:, 2 * REC_W:3 * REC_W] = y[:, 2 * REC_W:3 * REC_W]
    if len(rest) == 2:
        for r in range(tt // GROUP):
            _store_chunks_t(rest[0], r, kn[r * GROUP:(r + 1) * GROUP, :])
    xbuf[0:SUBLANES, :] = xbuf[tt:tt + SUBLANES, :]


def _gdn_pre(proj, ctx8, conv_w8, e_heads, b, t, tt):
    w3 = 3 * REC_W
    nt = t // tt
    emit_kt = tt % GROUP == 0
    out_specs = [pl.BlockSpec((tt, w3), lambda bi, i: (bi * nt + i, 0))]
    out_shape = [jax.ShapeDtypeStruct((b * t, w3), F32)]
    if emit_kt:
        out_specs.append(pl.BlockSpec((tt // CHUNK, REC_W, CHUNK), lambda bi, i: (bi * nt + i, 0, 0)))
        out_shape.append(jax.ShapeDtypeStruct((b * t // CHUNK, REC_W, CHUNK), F32))
    res = pl.pallas_call(
        functools.partial(_gdn_pre_kernel, tt=tt),
        grid=(b, nt),
        in_specs=[pl.BlockSpec((tt, w3), lambda bi, i: (bi * nt + i, COL_GQKV // w3)),
                  pl.BlockSpec((1, SUBLANES, w3), lambda bi, i: (bi, 0, 0)),
                  pl.BlockSpec((SUBLANES, w3), lambda bi, i: (0, 0)),
                  pl.BlockSpec((REC_W, REC_W), lambda bi, i: (0, 0))],
        out_specs=out_specs,
        out_shape=out_shape,
        scratch_shapes=[pltpu.VMEM((tt + SUBLANES, w3), F32)],
        compiler_params=_cparams("parallel", "arbitrary"),
        name="gdn_pre",
    )(proj, ctx8, conv_w8, e_heads)
    return (res[0], res[1]) if emit_kt else (res[0], None)


def _sb_kernel(q_ref, kt_ref, v_ref, o_ref, acc_ref, c_ref, zmax_ref, kmax_ref,
               *, hb, tq, tk, nkb, q_off, diag_static):
    qi = pl.program_id(1)
    q_lo = q_off + qi * tq
    nblk = jnp.minimum(nkb, (q_lo + tq - 1 + tk - 1) // tk)
    nfull = jnp.minimum(nblk, q_lo // tk)
    scale = HEAD_DIM ** -0.5 * LOG2E
    qs = [(q_ref[:, h * HEAD_DIM:(h + 1) * HEAD_DIM] * scale).astype(BF16) for h in range(hb)]
    lower_incl = _ones_where(lax.broadcasted_iota(jnp.int32, (tk, tk), 0)
                             >= lax.broadcasted_iota(jnp.int32, (tk, tk), 1), BF16)
    acc_ref[...] = jnp.zeros_like(acc_ref)
    c_ref[...] = jnp.zeros_like(c_ref)

    @pl.when(qi == 0)
    def _():
        def knorm(kb, best):
            k32 = kt_ref[0, kb].astype(F32)
            sq = k32 * k32
            return tuple(jnp.maximum(best[h], jnp.sum(sq[h * HEAD_DIM:(h + 1) * HEAD_DIM, :], axis=0, keepdims=True))
                         for h in range(hb))
        best = lax.fori_loop(0, nkb, knorm, tuple(jnp.zeros((1, tk), F32) for _ in range(hb)))
        for h in range(hb):
            kmax_ref[h] = jnp.broadcast_to(jnp.sqrt(jnp.max(best[h], axis=1, keepdims=True)), (1, LANES))

    for h in range(hb):
        q32 = qs[h].astype(F32)
        qn = jnp.sqrt(jnp.sum(q32 * q32, axis=1, keepdims=True))
        zmax_ref[h] = qn * kmax_ref[h] * BOUND_SLACK

    def all_weights_vanish():
        margin = c_ref[0] - zmax_ref[0]
        for h in range(1, hb):
            margin = jnp.minimum(margin, c_ref[h] - zmax_ref[h])
        return jnp.min(margin) > F32_EXP2_ZERO

    def blocks(kbs, r0=0, valids=None):
        for h in range(hb):
            q = qs[h][r0:, :]
            c = c_ref[h, r0:, :]
            av = None
            for i, kb in enumerate(kbs):
                kt = kt_ref[0, kb, h * HEAD_DIM:(h + 1) * HEAD_DIM, :]
                z = jnp.dot(q, kt, preferred_element_type=F32)
                lneg = jnp.maximum(z, jnp.log2(1.0 + jnp.exp2(jnp.minimum(z, SOFTPLUS2_CLAMP))))
                if valids is not None:
                    lneg = jnp.where(valids[i], lneg, 0.0)
                incl = jnp.dot(lneg.astype(BF16), lower_incl, preferred_element_type=F32)
                a = jnp.exp2(z - incl - jnp.tile(c, (1, tk // LANES)))
                if valids is not None:
                    a = jnp.where(valids[i], a, 0.0)
                t = jnp.dot(a.astype(BF16), v_ref[h, kb], preferred_element_type=F32)
                av = t if av is None else av + t
                c = c + incl[:, 0:1]
            acc_ref[h, r0:, :] += av
            c_ref[h, r0:, :] = c

    if diag_static:
        for r in reversed(range(tq // tk)):
            rows = tq - r * tk
            valid = (lax.broadcasted_iota(jnp.int32, (rows, tk), 1)
                     < lax.broadcasted_iota(jnp.int32, (rows, tk), 0))
            blocks([nfull + r], r0=r * tk, valids=[valid])
    else:
        @pl.loop(0, nblk - nfull)
        def _(i):
            kb = nblk - 1 - i
            valid = (kb * tk + lax.broadcasted_iota(jnp.int32, (tq, tk), 1)
                     < q_lo + lax.broadcasted_iota(jnp.int32, (tq, tk), 0))
            blocks([kb], valids=[valid])

    odd = nfull % 2

    @pl.loop(0, odd)
    def _(i):
        blocks([nfull - 1])

    def pair_cond(carry):
        i, done = carry
        return jnp.logical_and(i < nfull // 2, done == 0)

    def pair_body(carry):
        i, _ = carry
        kb = nfull - odd - 1 - 2 * i
        blocks([kb, kb - 1])
        return i + 1, all_weights_vanish().astype(jnp.int32)

    lax.while_loop(pair_cond, pair_body, (jnp.int32(0), all_weights_vanish().astype(jnp.int32)))

    for h in range(hb):
        o_ref[:, h * HEAD_DIM:(h + 1) * HEAD_DIM] = acc_ref[h]


def _sb_attn(proj, kt, vh, b, t, hb, tq, q_off):
    _, nkb, w, tk = kt.shape
    d = w // hb
    nq = t // tq
    diag_static = q_off % tk == 0 and tq % tk == 0 and q_off + t <= nkb * tk
    return pl.pallas_call(
        functools.partial(_sb_kernel, hb=hb, tq=tq, tk=tk, nkb=nkb, q_off=q_off, diag_static=diag_static),
        grid=(b, nq),
        in_specs=[pl.BlockSpec((tq, w), lambda bi, i: (bi * nq + i, COL_SB // w)),
                  pl.BlockSpec((1, nkb, w, tk), lambda bi, i: (bi, 0, 0, 0), pipeline_mode=pl.Buffered(1)),
                  pl.BlockSpec((hb, nkb, tk, d), lambda bi, i: (bi, 0, 0, 0), pipeline_mode=pl.Buffered(1))],
        out_specs=pl.BlockSpec((tq, w), lambda bi, i: (bi * nq + i, 0)),
        out_shape=jax.ShapeDtypeStruct((b * t, w), F32),
        scratch_shapes=[pltpu.VMEM((hb, tq, d), F32), pltpu.VMEM((hb, tq, LANES), F32),
                        pltpu.VMEM((hb, tq, LANES), F32), pltpu.VMEM((hb, 1, LANES), F32)],
        compiler_params=_cparams("parallel", "arbitrary"),
        name="sb_attn",
    )(proj, kt, vh)


def _gate_values(pre, neg_a, lane_id):
    sp = jnp.log1p(jnp.exp(-jnp.abs(pre)))
    softplus = jnp.maximum(pre, 0.0) + sp
    log_sig = jnp.minimum(pre, 0.0) - sp
    return jnp.where(lane_id < GATE_G, _sigmoid(pre),
                     jnp.where(lane_id < GATE_I, neg_a * softplus,
                               jnp.where(lane_id < GATE_F, pre, log_sig)))


def _stack(a, b):
    return jnp.concatenate([a, b], axis=0)


def _gates(gt, gtt, prow_ref, pcol_ref, l):
    lane = lax.broadcasted_iota(jnp.int32, (l, LANES), 1)
    val = _gate_values(gt + prow_ref[0:1, :], -jnp.exp(prow_ref[1:2, :]), lane)
    sub = lax.broadcasted_iota(jnp.int32, (LANES, l), 0)
    val_t = _gate_values(gtt + pcol_ref[:, 0:1], -jnp.exp(pcol_ref[:, 1:2]), sub)
    r = lax.broadcasted_iota(jnp.int32, (l, l), 0)
    c = lax.broadcasted_iota(jnp.int32, (l, l), 1)
    csum = _dot_exact_lhs(_ones_where(r >= c, BF16), val, 3)
    csum_t = _dot_exact_rhs(val_t, _ones_where(r <= c, BF16), 3)
    return val, csum, val_t, csum_t


def _gdn_scan_kernel(qkv_ref, kt_ref, gt_ref, gtt_ref, prow_ref, pcol_ref, s0_ref, o_ref, s_out_ref, s_scr,
                     *, l, cb, ng):
    gi = pl.program_id(1)

    @pl.when(gi == 0)
    def _():
        s_scr[...] = s0_ref[0]

    r = lax.broadcasted_iota(jnp.int32, (l, l), 0)
    c = lax.broadcasted_iota(jnp.int32, (l, l), 1)
    tri = r >= c
    stri = r > c
    eye = _ones_where(r == c, F32)
    n_double = max(int(np.ceil(np.log2(l))) - 1, 0)
    hs = range(GDN_HEADS)
    ps = [(j, h) for j in range(cb) for h in hs]
    gates = [_gates(gt_ref[j * l:(j + 1) * l, :], gtt_ref[j], prow_ref, pcol_ref, l) for j in range(cb)]

    def head_cols(j, group, h):
        return qkv_ref[j * l:(j + 1) * l, group * REC_W + h * HEAD_DIM:group * REC_W + (h + 1) * HEAD_DIM]

    q = {p: head_cols(p[0], 0, p[1]) for p in ps}
    k = {p: head_cols(p[0], 1, p[1]) for p in ps}
    v = {p: head_cols(p[0], 2, p[1]) for p in ps}
    kt = {p: kt_ref[p[0], p[1] * HEAD_DIM:(p[1] + 1) * HEAD_DIM, :] for p in ps}
    beta = {p: gates[p[0]][0][:, GATE_BETA + p[1]:GATE_BETA + p[1] + 1] for p in ps}
    g_col = {p: gates[p[0]][1][:, GATE_G + p[1]:GATE_G + p[1] + 1] for p in ps}
    g_row = {p: gates[p[0]][3][GATE_G + p[1]:GATE_G + p[1] + 1, :] for p in ps}
    g_last = {p: g_col[p][l - 1:l, :] for p in ps}
    decay = {p: jnp.where(tri, jnp.exp(jnp.where(tri, g_col[p] - g_row[p], 0.0)), 0.0) for p in ps}
    kb = {p: k[p] * beta[p] for p in ps}
    e_g = {p: jnp.exp(g_col[p]) for p in ps}
    kq = {p: _dot(_stack(kb[p], q[p]), kt[p]) for p in ps}
    x = {p: -jnp.where(stri, kq[p][:l] * decay[p], 0.0) for p in ps}
    tinv = {p: eye + x[p] for p in ps}
    if n_double > 0:
        x = {p: _dot(x[p], x[p]) for p in ps}
        for _ in range(n_double - 1):
            tx = {p: _dot(_stack(tinv[p], x[p]), x[p]) for p in ps}
            tinv = {p: tinv[p] + tx[p][:l] for p in ps}
            x = {p: tx[p][l:] for p in ps}
        tinv = {p: tinv[p] + _dot(tinv[p], x[p]) for p in ps}
    sol_v = {p: _dot(tinv[p], v[p] * beta[p]) for p in ps}
    sol_k = {p: _dot(tinv[p], kb[p] * e_g[p]) for p in ps}
    lhs_s = {p: _stack(q[p] * e_g[p], sol_k[p]) for p in ps}
    lhs_u = {p: _stack(kt[p] * jnp.exp(g_last[p] - g_row[p]), kq[p][l:] * decay[p]) for p in ps}
    e_last = {p: jnp.exp(g_last[p]) for p in ps}
    s = [s_scr[h] for h in hs]
    for j in range(cb):
        t = [_dot(lhs_s[(j, h)], s[h]) for h in hs]
        u = [sol_v[(j, h)] - t[h][l:] for h in hs]
        w = [_dot(lhs_u[(j, h)], u[h]) for h in hs]
        s = [e_last[(j, h)] * s[h] + w[h][:HEAD_DIM] for h in hs]
        for h in hs:
            o_ref[j * l:(j + 1) * l, h * HEAD_DIM:(h + 1) * HEAD_DIM] = t[h][:l] + w[h][HEAD_DIM:]
    for h in hs:
        s_scr[h] = s[h]

    @pl.when(gi == ng - 1)
    def _():
        s_out_ref[0] = s_scr[...]


def _gdn_scan(qkv, k_t, proj, gate_t, prow, pcol, s0, b, t, l, cb):
    ng = t // (l * cb)
    w3 = 3 * REC_W
    return pl.pallas_call(
        functools.partial(_gdn_scan_kernel, l=l, cb=cb, ng=ng),
        grid=(b, ng),
        in_specs=[pl.BlockSpec((cb * l, w3), lambda bi, i: (bi * ng + i, 0)),
                  pl.BlockSpec((cb, REC_W, l), lambda bi, i: (bi * ng + i, 0, 0)),
                  pl.BlockSpec((cb * l, LANES), lambda bi, i: (bi * ng + i, COL_GATE // LANES)),
                  pl.BlockSpec((cb, LANES, l), lambda bi, i: (bi * ng + i, 0, 0)),
                  pl.BlockSpec((SUBLANES, LANES), lambda bi, i: (0, 0)),
                  pl.BlockSpec((LANES, SUBLANES), lambda bi, i: (0, 0)),
                  pl.BlockSpec((1, GDN_HEADS, HEAD_DIM, HEAD_DIM), lambda bi, i: (bi, 0, 0, 0))],
        out_specs=[pl.BlockSpec((cb * l, REC_W), lambda bi, i: (bi * ng + i, 0)),
                   pl.BlockSpec((1, GDN_HEADS, HEAD_DIM, HEAD_DIM), lambda bi, i: (bi, 0, 0, 0))],
        out_shape=[jax.ShapeDtypeStruct((b * t, REC_W), F32),
                   jax.ShapeDtypeStruct((b, GDN_HEADS, HEAD_DIM, HEAD_DIM), F32)],
        scratch_shapes=[pltpu.VMEM((GDN_HEADS, HEAD_DIM, HEAD_DIM), F32)],
        compiler_params=_cparams("parallel", "arbitrary"),
        name="gdn_scan",
    )(qkv, k_t, proj, gate_t, prow, pcol, s0)


def _ml_scan_kernel(qkv_ref, kt_ref, gt_ref, gtt_ref, prow_ref, pcol_ref, c0_ref, n0_ref, m0_ref,
                    o_ref, c_out_ref, n_out_ref, m_out_ref, c_scr, n_scr, m_scr, *, l, cb, ng):
    gi = pl.program_id(1)

    @pl.when(gi == 0)
    def _():
        c_scr[...] = c0_ref[0]
        n_scr[...] = n0_ref[0]
        m_scr[...] = m0_ref[0]

    r = lax.broadcasted_iota(jnp.int32, (l, l), 0)
    c = lax.broadcasted_iota(jnp.int32, (l, l), 1)
    tri = r >= c
    lane = lax.broadcasted_iota(jnp.int32, (1, LANES), 1)
    m_all = m_scr[...]
    hs = range(ML_HEADS)
    ps = [(j, h) for j in range(cb) for h in hs]
    kscale = HEAD_DIM ** -0.5
    gates = [_gates(gt_ref[j * l:(j + 1) * l, :], gtt_ref[j], prow_ref, pcol_ref, l) for j in range(cb)]

    def head_cols(j, group, h):
        return qkv_ref[j * l:(j + 1) * l, group * REC_W + h * HEAD_DIM:group * REC_W + (h + 1) * HEAD_DIM]

    q = {p: head_cols(p[0], 0, p[1]) for p in ps}
    k = {p: head_cols(p[0], 1, p[1]) * kscale for p in ps}
    v = {p: head_cols(p[0], 2, p[1]) for p in ps}
    kt = {p: kt_ref[p[0], p[1] * HEAD_DIM:(p[1] + 1) * HEAD_DIM, :] * kscale for p in ps}
    ig_col = {p: gates[p[0]][0][:, GATE_I + p[1]:GATE_I + p[1] + 1] for p in ps}
    ig_row = {p: gates[p[0]][2][GATE_I + p[1]:GATE_I + p[1] + 1, :] for p in ps}
    f_col = {p: gates[p[0]][1][:, GATE_F + p[1]:GATE_F + p[1] + 1] for p in ps}
    f_row = {p: gates[p[0]][3][GATE_F + p[1]:GATE_F + p[1] + 1, :] for p in ps}
    f_last = {p: f_col[p][l - 1:l, :] for p in ps}
    qk = {p: _dot(q[p], kt[p]) for p in ps}
    d = {p: jnp.where(tri, f_col[p] - f_row[p] + ig_row[p], -jnp.inf) for p in ps}
    d_max = {p: jnp.max(d[p], axis=1, keepdims=True) for p in ps}
    m_prev, m_t = {}, {}
    m_run = [m_all[:, h:h + 1] for h in hs]
    for j in range(cb):
        for h in hs:
            m_prev[(j, h)] = m_run[h]
            m_t[(j, h)] = jnp.maximum(f_col[(j, h)] + m_run[h], d_max[(j, h)])
            m_run[h] = m_t[(j, h)][l - 1:l, :]
    m_new = {p: m_t[p][l - 1:l, :] for p in ps}
    w = {p: jnp.exp(d[p] - m_t[p]) * qk[p] for p in ps}
    c_inter = {p: jnp.exp(f_col[p] + m_prev[p] - m_t[p]) for p in ps}
    wv = {p: _dot(w[p], v[p]) for p in ps}
    w_sum = {p: jnp.sum(w[p], axis=1, keepdims=True) for p in ps}
    e_m = {p: jnp.exp(-m_t[p]) for p in ps}
    w_end_row = {p: jnp.exp(f_last[p] - f_row[p] + ig_row[p] - m_new[p]) for p in ps}
    w_end_col = {p: jnp.exp(f_last[p] - f_col[p] + ig_col[p] - m_new[p]) for p in ps}
    c_prev = {p: jnp.exp(f_last[p] + m_prev[p] - m_new[p]) for p in ps}
    dc = {p: _dot(kt[p] * w_end_row[p], v[p]) for p in ps}
    dn = {p: jnp.sum(w_end_col[p] * k[p], axis=0, keepdims=True) for p in ps}
    c_h = [c_scr[h] for h in hs]
    n_h = [n_scr[h:h + 1, :] for h in hs]
    for j in range(cb):
        for h in hs:
            p = (j, h)
            num = c_inter[p] * _dot(q[p], c_h[h]) + wv[p]
            den = c_inter[p] * jnp.sum(q[p] * n_h[h], axis=1, keepdims=True) + w_sum[p]
            o_ref[j * l:(j + 1) * l, h * HEAD_DIM:(h + 1) * HEAD_DIM] = num / jnp.maximum(jnp.abs(den), e_m[p])
            c_h[h] = c_prev[p] * c_h[h] + dc[p]
            n_h[h] = c_prev[p] * n_h[h] + dn[p]
    m_next = m_all
    for h in hs:
        c_scr[h] = c_h[h]
        n_scr[h:h + 1, :] = n_h[h]
        m_next = jnp.where(lane == h, m_run[h], m_next)
    m_scr[...] = m_next

    @pl.when(gi == ng - 1)
    def _():
        c_out_ref[0] = c_scr[...]
        n_out_ref[0] = n_scr[...]
        m_out_ref[0] = m_scr[...]


def _ml_scan(proj, k_t, gate_t, prow, pcol, c0, n0, m0, b, t, l, cb):
    ng = t // (l * cb)
    w3 = 3 * REC_W
    st = lambda bi, i: (bi, 0, 0, 0)
    st3 = lambda bi, i: (bi, 0, 0)
    return pl.pallas_call(
        functools.partial(_ml_scan_kernel, l=l, cb=cb, ng=ng),
        grid=(b, ng),
        in_specs=[pl.BlockSpec((cb * l, w3), lambda bi, i: (bi * ng + i, COL_MQKV // w3)),
                  pl.BlockSpec((cb, REC_W, l), lambda bi, i: (bi * ng + i, 0, 0)),
                  pl.BlockSpec((cb * l, LANES), lambda bi, i: (bi * ng + i, COL_GATE // LANES)),
                  pl.BlockSpec((cb, LANES, l), lambda bi, i: (bi * ng + i, 0, 0)),
                  pl.BlockSpec((SUBLANES, LANES), lambda bi, i: (0, 0)),
                  pl.BlockSpec((LANES, SUBLANES), lambda bi, i: (0, 0)),
                  pl.BlockSpec((1, ML_HEADS, HEAD_DIM, HEAD_DIM), st),
                  pl.BlockSpec((1, SUBLANES, HEAD_DIM), st3),
                  pl.BlockSpec((1, 1, LANES), st3)],
        out_specs=[pl.BlockSpec((cb * l, REC_W), lambda bi, i: (bi * ng + i, 0)),
                   pl.BlockSpec((1, ML_HEADS, HEAD_DIM, HEAD_DIM), st),
                   pl.BlockSpec((1, SUBLANES, HEAD_DIM), st3),
                   pl.BlockSpec((1, 1, LANES), st3)],
        out_shape=[jax.ShapeDtypeStruct((b * t, REC_W), F32),
                   jax.ShapeDtypeStruct((b, ML_HEADS, HEAD_DIM, HEAD_DIM), F32),
                   jax.ShapeDtypeStruct((b, SUBLANES, HEAD_DIM), F32),
                   jax.ShapeDtypeStruct((b, 1, LANES), F32)],
        scratch_shapes=[pltpu.VMEM((ML_HEADS, HEAD_DIM, HEAD_DIM), F32),
                        pltpu.VMEM((SUBLANES, HEAD_DIM), F32),
                        pltpu.VMEM((1, LANES), F32)],
        compiler_params=_cparams("parallel", "arbitrary"),
        name="ml_scan",
    )(proj, k_t, proj, gate_t, prow, pcol, c0, n0, m0)


def _ml_scan_t_kernel(qkv_ref, qkvt_ref, gt_ref, gtt_ref, prow_ref, pcol_ref, c0_ref, n0_ref, m0_ref,
                      o_ref, c_out_ref, n_out_ref, m_out_ref, c_scr, n_scr, m_scr, *, l, cb, ng):
    gi = pl.program_id(1)

    @pl.when(gi == 0)
    def _():
        c_scr[...] = c0_ref[0]
        n_scr[...] = n0_ref[0]
        m_scr[...] = m0_ref[0]

    r = lax.broadcasted_iota(jnp.int32, (l, l), 0)
    c = lax.broadcasted_iota(jnp.int32, (l, l), 1)
    tri_t = r <= c
    lane = lax.broadcasted_iota(jnp.int32, (1, LANES), 1)
    m_all = m_scr[...]
    hs = range(ML_HEADS)
    ps = [(j, h) for j in range(cb) for h in hs]
    kscale = HEAD_DIM ** -0.5
    zeros_t = jnp.zeros((HEAD_DIM, l), F32)
    gates = [_gates(gt_ref[j * l:(j + 1) * l, :], gtt_ref[j], prow_ref, pcol_ref, l) for j in range(cb)]

    def t_rows(j, group, h):
        return qkvt_ref[j, group * REC_W + h * HEAD_DIM:group * REC_W + (h + 1) * HEAD_DIM, :]

    qt = {p: t_rows(p[0], 0, p[1]) for p in ps}
    qt_pad = {p: (_stack(qt[p], zeros_t) if p[1] % 2 == 0 else _stack(zeros_t, qt[p])) for p in ps}
    vt = {p: t_rows(p[0], 2, p[1]) for p in ps}
    k2 = {p: qkv_ref[p[0] * l:(p[0] + 1) * l, REC_W + (p[1] // 2) * LANES:REC_W + (p[1] // 2 + 1) * LANES] * kscale
          for p in ps}
    ig_col = {p: gates[p[0]][0][:, GATE_I + p[1]:GATE_I + p[1] + 1] for p in ps}
    ig_row = {p: gates[p[0]][2][GATE_I + p[1]:GATE_I + p[1] + 1, :] for p in ps}
    f_col = {p: gates[p[0]][1][:, GATE_F + p[1]:GATE_F + p[1] + 1] for p in ps}
    f_row = {p: gates[p[0]][3][GATE_F + p[1]:GATE_F + p[1] + 1, :] for p in ps}
    f_last = {p: f_row[p][:, l - 1:l] for p in ps}
    kq = {p: _dot(k2[p], qt_pad[p]) for p in ps}
    d = {p: jnp.where(tri_t, f_row[p] + (ig_col[p] - f_col[p]), -jnp.inf) for p in ps}
    d_max = {p: jnp.max(d[p], axis=0, keepdims=True) for p in ps}
    m_prev, m_t = {}, {}
    m_run = [m_all[:, h:h + 1] for h in hs]
    for j in range(cb):
        for h in hs:
            m_prev[(j, h)] = m_run[h]
            m_t[(j, h)] = jnp.maximum(f_row[(j, h)] + m_run[h], d_max[(j, h)])
            m_run[h] = m_t[(j, h)][:, l - 1:l]
    m_new = {p: m_t[p][:, l - 1:l] for p in ps}
    w = {p: jnp.exp(d[p] - m_t[p]) * kq[p] for p in ps}
    c_inter = {p: jnp.exp(f_row[p] + m_prev[p] - m_t[p]) for p in ps}
    vw = {p: _dot(vt[p], w[p]) for p in ps}
    w_sum = {p: jnp.sum(w[p], axis=0, keepdims=True) for p in ps}
    e_m = {p: jnp.exp(-m_t[p]) for p in ps}
    w_end_col = {p: jnp.exp(f_last[p] - f_col[p] + ig_col[p] - m_new[p]) for p in ps}
    c_prev = {p: jnp.exp(f_last[p] + m_prev[p] - m_new[p]) for p in ps}
    wk = {p: w_end_col[p] * k2[p] for p in ps}
    dc = {p: _dot(vt[p], wk[p]) for p in ps}
    dn = {p: jnp.sum(wk[p], axis=0, keepdims=True) for p in ps}
    c_h = [c_scr[h] for h in hs]
    n_h = [n_scr[h:h + 1, :] for h in hs]
    ht = {}
    for j in range(cb):
        for h in hs:
            p = (j, h)
            num = c_inter[p] * _dot(c_h[h], qt_pad[p]) + vw[p]
            den = c_inter[p] * _dot(n_h[h], qt_pad[p]) + w_sum[p]
            ht[p] = num / jnp.maximum(jnp.abs(den), e_m[p])
            c_h[h] = c_prev[p] * c_h[h] + dc[p]
            n_h[h] = c_prev[p] * n_h[h] + dn[p]
    for i in range(ML_HEADS // 2):
        pair_t = _stack(jnp.concatenate([ht[(j, 2 * i)] for j in range(cb)], axis=1),
                        jnp.concatenate([ht[(j, 2 * i + 1)] for j in range(cb)], axis=1))
        o_ref[:, i * LANES:(i + 1) * LANES] = pair_t.T
    m_next = m_all
    for h in hs:
        c_scr[h] = c_h[h]
        n_scr[h:h + 1, :] = n_h[h]
        m_next = jnp.where(lane == h, m_run[h], m_next)
    m_scr[...] = m_next

    @pl.when(gi == ng - 1)
    def _():
        c_out_ref[0] = c_scr[...]
        n_out_ref[0] = n_scr[...]
        m_out_ref[0] = m_scr[...]


def _ml_scan_t(proj, qkv_t, gate_t, prow, pcol, c0, n0, m0, b, t, l, cb):
    ng = t // (l * cb)
    w3 = 3 * REC_W
    st = lambda bi, i: (bi, 0, 0, 0)
    st3 = lambda bi, i: (bi, 0, 0)
    return pl.pallas_call(
        functools.partial(_ml_scan_t_kernel, l=l, cb=cb, ng=ng),
        grid=(b, ng),
        in_specs=[pl.BlockSpec((cb * l, w3), lambda bi, i: (bi * ng + i, COL_MQKV // w3)),
                  pl.BlockSpec((cb, w3, l), lambda bi, i: (bi * ng + i, 0, 0)),
                  pl.BlockSpec((cb * l, LANES), lambda bi, i: (bi * ng + i, COL_GATE // LANES)),
                  pl.BlockSpec((cb, LANES, l), lambda bi, i: (bi * ng + i, 0, 0)),
                  pl.BlockSpec((SUBLANES, LANES), lambda bi, i: (0, 0)),
                  pl.BlockSpec((LANES, SUBLANES), lambda bi, i: (0, 0)),
                  pl.BlockSpec((1, ML_HEADS, HEAD_DIM, LANES), st),
                  pl.BlockSpec((1, SUBLANES, LANES), st3),
                  pl.BlockSpec((1, 1, LANES), st3)],
        out_specs=[pl.BlockSpec((cb * l, REC_W), lambda bi, i: (bi * ng + i, 0)),
                   pl.BlockSpec((1, ML_HEADS, HEAD_DIM, LANES), st),
                   pl.BlockSpec((1, SUBLANES, LANES), st3),
                   pl.BlockSpec((1, 1, LANES), st3)],
        out_shape=[jax.ShapeDtypeStruct((b * t, REC_W), F32),
                   jax.ShapeDtypeStruct((b, ML_HEADS, HEAD_DIM, LANES), F32),
                   jax.ShapeDtypeStruct((b, SUBLANES, LANES), F32),
                   jax.ShapeDtypeStruct((b, 1, LANES), F32)],
        scratch_shapes=[pltpu.VMEM((ML_HEADS, HEAD_DIM, LANES), F32),
                        pltpu.VMEM((SUBLANES, LANES), F32),
                        pltpu.VMEM((1, LANES), F32)],
        compiler_params=_cparams("parallel", "arbitrary"),
        name="ml_scan_t",
    )(proj, qkv_t, proj, gate_t, prow, pcol, c0, n0, m0)


def _post_kernel(x_ref, osb_ref, og_ref, hm_ref, gz_ref, mo_ref, w_ref, e_ref, gnw_ref, mnw_ref,
                 lg_ref, lb_ref, o_ref, ob_ref):
    e = e_ref[...]
    og = og_ref[...]
    hm = hm_ref[...]
    og_ms = _dot_exact_rhs(og * og, e, 2) * (1.0 / HEAD_DIM)
    hm_ms = _dot_exact_rhs(hm * hm, e, 2) * (1.0 / HEAD_DIM)
    gz = gz_ref[...]
    o_g = og * lax.rsqrt(og_ms + NORM_EPS) * gnw_ref[...] * (gz * _sigmoid(gz))
    o_m = _sigmoid(mo_ref[...]) * (hm * lax.rsqrt(hm_ms + NORM_EPS) * mnw_ref[...])
    mixed = (_dot(osb_ref[...], w_ref[0:SB_W, :])
             + _dot(o_g, w_ref[SB_W:SB_W + REC_W, :])
             + _dot(o_m, w_ref[SB_W + REC_W:SB_W + 2 * REC_W, :]))
    y = _layer_norm(DEEPNORM_ALPHA * x_ref[...] + mixed, lg_ref[...], lb_ref[...])
    o_ref[...] = y
    ob_ref[...] = y.astype(BF16)


def _post(x, osb, og, hm, proj, w_out, e_heads, gnw, mnw, lg, lb, tm):
    n, d = x.shape
    row = lambda i: (i, 0)
    fixed = lambda i: (0, 0)
    return pl.pallas_call(
        _post_kernel,
        grid=(n // tm,),
        in_specs=[pl.BlockSpec((tm, d), row),
                  pl.BlockSpec((tm, SB_W), row),
                  pl.BlockSpec((tm, REC_W), row),
                  pl.BlockSpec((tm, REC_W), row),
                  pl.BlockSpec((tm, REC_W), lambda i: (i, COL_GZ // REC_W)),
                  pl.BlockSpec((tm, REC_W), lambda i: (i, COL_MO // REC_W)),
                  pl.BlockSpec(w_out.shape, fixed),
                  pl.BlockSpec((REC_W, REC_W), fixed),
                  pl.BlockSpec((1, REC_W), fixed),
                  pl.BlockSpec((1, REC_W), fixed),
                  pl.BlockSpec((1, d), fixed),
                  pl.BlockSpec((1, d), fixed)],
        out_specs=[pl.BlockSpec((tm, d), row), pl.BlockSpec((tm, d), row)],
        out_shape=[jax.ShapeDtypeStruct((n, d), F32), jax.ShapeDtypeStruct((n, d), BF16)],
        compiler_params=_cparams("parallel"),
        name="post",
    )(x, osb, og, hm, proj, proj, w_out, e_heads, gnw, mnw, lg, lb)


def _memattn_kernel(x_ref, xb_ref, wq_ref, wo_ref, mk_ref, mv_ref, lg_ref, lb_ref, o_ref, ob_ref):
    d = x_ref.shape[1]
    hd = d // MEM_HEADS
    q = jnp.dot(xb_ref[...], wq_ref[...], preferred_element_type=F32)
    out = None
    for h in range(MEM_HEADS):
        sl = slice(h * hd, (h + 1) * hd)
        s = _dot_nt(q[:, sl], mk_ref[0, :, sl]) * (hd ** -0.5)
        p = jnp.exp(s - jnp.max(s, axis=1, keepdims=True))
        o_h = _dot(p, mv_ref[0, :, sl]) / jnp.sum(p, axis=1, keepdims=True)
        t = _dot(o_h, wo_ref[sl, :])
        out = t if out is None else out + t
    y = _layer_norm(DEEPNORM_ALPHA * x_ref[...] + out, lg_ref[...], lb_ref[...])
    o_ref[...] = y
    ob_ref[...] = y.astype(BF16)


def _memattn(x, xb, w_cq, w_co, mk, mv, lg, lb, b, t, tm):
    n, d = x.shape
    nt = t // tm
    nm = mk.shape[1]
    row = lambda bi, i: (bi * nt + i, 0)
    fixed = lambda bi, i: (0, 0)
    return pl.pallas_call(
        _memattn_kernel,
        grid=(b, nt),
        in_specs=[pl.BlockSpec((tm, d), row),
                  pl.BlockSpec((tm, d), row),
                  pl.BlockSpec((d, d), fixed),
                  pl.BlockSpec((d, d), fixed),
                  pl.BlockSpec((1, nm, d), lambda bi, i: (bi, 0, 0)),
                  pl.BlockSpec((1, nm, d), lambda bi, i: (bi, 0, 0)),
                  pl.BlockSpec((1, d), fixed),
                  pl.BlockSpec((1, d), fixed)],
        out_specs=[pl.BlockSpec((tm, d), row), pl.BlockSpec((tm, d), row)],
        out_shape=[jax.ShapeDtypeStruct((n, d), F32), jax.ShapeDtypeStruct((n, d), BF16)],
        compiler_params=_cparams("parallel", "parallel"),
        name="memattn",
    )(x, xb, w_cq, w_co, mk, mv, lg, lb)


def _ffn_kernel(x_ref, xb_ref, wg_ref, wu_ref, wd_ref, lg_ref, lb_ref, o_ref, ob_ref, acc_ref, *, nf):
    j = pl.program_id(1)

    @pl.when(j == 0)
    def _():
        acc_ref[...] = jnp.zeros_like(acc_ref)

    xb = xb_ref[...]
    gate = jnp.dot(xb, wg_ref[...], preferred_element_type=F32)
    up = jnp.dot(xb, wu_ref[...], preferred_element_type=F32)
    acc_ref[...] += _dot(gate * _sigmoid(gate) * up, wd_ref[...])

    @pl.when(j == nf - 1)
    def _():
        y = _layer_norm(DEEPNORM_ALPHA * x_ref[...] + acc_ref[...], lg_ref[...], lb_ref[...])
        o_ref[...] = y
        ob_ref[...] = y.astype(BF16)


def _ffn(x, xb, w_up, w_down, lg, lb, tm, tf):
    n, d = x.shape
    dff = w_down.shape[0]
    nf = dff // tf
    row = lambda i, j: (i, 0)
    fixed = lambda i, j: (0, 0)
    return pl.pallas_call(
        functools.partial(_ffn_kernel, nf=nf),
        grid=(n // tm, nf),
        in_specs=[pl.BlockSpec((tm, d), row),
                  pl.BlockSpec((tm, d), row),
                  pl.BlockSpec((d, tf), lambda i, j: (0, j)),
                  pl.BlockSpec((d, tf), lambda i, j: (0, nf + j)),
                  pl.BlockSpec((tf, d), lambda i, j: (j, 0)),
                  pl.BlockSpec((1, d), fixed),
                  pl.BlockSpec((1, d), fixed)],
        out_specs=[pl.BlockSpec((tm, d), row), pl.BlockSpec((tm, d), row)],
        out_shape=[jax.ShapeDtypeStruct((n, d), F32), jax.ShapeDtypeStruct((n, d), BF16)],
        scratch_shapes=[pltpu.VMEM((tm, d), F32)],
        compiler_params=_cparams("parallel", "arbitrary"),
        name="ffn",
    )(x, xb, w_up, w_up, w_down, lg, lb)


def _pack_layer(p):
    w_in = p['w_in']
    d = w_in.shape[0]
    o_sb, o_gqkv, o_gz = 0, 3 * SB_W, 3 * SB_W + 3 * REC_W
    o_gb = o_gz + REC_W
    o_mqkv = o_gb + 2 * GDN_HEADS
    o_mo = o_mqkv + 3 * REC_W
    o_mi = o_mo + REC_W
    gate_cols = jnp.concatenate([w_in[:, o_gb:o_gb + 2 * GDN_HEADS], w_in[:, o_mi:o_mi + 2 * ML_HEADS]], axis=1)
    used = COL_GATE + gate_cols.shape[1]
    w_al = jnp.concatenate([w_in[:, o_gqkv:o_gqkv + 3 * REC_W], w_in[:, o_mqkv:o_mqkv + 3 * REC_W],
                            w_in[:, o_gz:o_gz + REC_W], w_in[:, o_mo:o_mo + REC_W],
                            w_in[:, o_sb:o_sb + 3 * SB_W], gate_cols,
                            jnp.zeros((d, PROJ_W - used), w_in.dtype)], axis=1).astype(BF16)
    prow = jnp.zeros((SUBLANES, LANES), F32)
    prow = prow.at[0, GATE_G:GATE_G + GDN_HEADS].set(p['gdn_dt_bias'])
    prow = prow.at[0, GATE_I:GATE_I + ML_HEADS].set(p['mlstm_i_bias'])
    prow = prow.at[0, GATE_F:GATE_F + ML_HEADS].set(p['mlstm_f_bias'])
    prow = prow.at[1, GATE_G:GATE_G + GDN_HEADS].set(p['gdn_A_log'])
    conv_w8 = jnp.zeros((SUBLANES, 3 * REC_W), F32).at[0:CONV_W].set(p['gdn_conv_w'])
    return dict(
        w_al=w_al, prow=prow, pcol=prow.T, conv_w8=conv_w8,
        gnw=jnp.tile(p['gdn_norm_w'], GDN_HEADS)[None, :], mnw=p['mlstm_norm_w'][None, :],
        w_out=p['w_out'].astype(BF16), w_cq=p['w_cq'].astype(BF16), w_co=p['w_co'].astype(BF16),
        w_up=p['w_up'].astype(BF16), w_down=p['w_down'].astype(BF16),
        ln1=(p['ln1_g'][None, :], p['ln1_b'][None, :]), ln2=(p['ln2_g'][None, :], p['ln2_b'][None, :]),
        ln3=(p['ln3_g'][None, :], p['ln3_b'][None, :]))


def _head_consts():
    hid = np.arange(REC_W) // HEAD_DIM
    return jnp.asarray(hid[:, None] == hid[None, :], BF16)


def _sb_layout(k, v, tk):
    b, tkv, _ = k.shape
    nkb = tkv // tk
    kt = k.reshape(b, nkb, tk, SB_W).transpose(0, 1, 3, 2)
    vh = v.reshape(b, nkb, tk, SB_HEADS, HEAD_DIM).transpose(0, 3, 1, 2, 4)
    vh = vh.reshape(b * SB_HEADS, nkb, tk, HEAD_DIM)
    return kt.astype(BF16), vh.astype(BF16)


def _to_pair_lanes(a):
    z = jnp.zeros_like(a)
    even = (jnp.arange(a.shape[1]) % 2 == 0)[None, :, None, None]
    return jnp.where(even, jnp.concatenate([a, z], axis=-1), jnp.concatenate([z, a], axis=-1))


def _from_pair_lanes(a):
    even = (jnp.arange(a.shape[1]) % 2 == 0)[None, :, None, None]
    return jnp.where(even, a[..., :HEAD_DIM], a[..., HEAD_DIM:])


def _time_on_lanes(a, groups, width):
    return a.reshape(groups, width, a.shape[1]).transpose(0, 2, 1)


def _trunk_layer(x, xb, pk, e_heads, mk, mv, b, t, cfg, sb_past, conv_ctx, gdn_s, ml_c, ml_n, ml_m):
    n, d = x.shape
    l = min(t, CHUNK)
    nc = t // l
    cb = min(cfg['scan_cb'], nc)
    ng = b * nc // cb
    tk = cfg['sb_tk']
    ctx8 = jnp.pad(conv_ctx, ((0, 0), (SUBLANES - (CONV_W - 1), 0), (0, 0)))
    if sb_past is None:
        proj, kt, vh, mk_t, gate_t = _proj(x if xb is None else xb, pk['w_al'], cfg['tm'])
        kt = kt.reshape(b, t // tk, SB_W, tk)
        q_off = 0
        gqkv_act, gk_t = _gdn_pre(proj, ctx8, pk['conv_w8'], e_heads, b, t, cfg['conv_tt'])
    else:
        proj = _matmul(xb, pk['w_al'], cfg['tm'], 1024)
    sk = proj[:, COL_SB + SB_W:COL_SB + 2 * SB_W].reshape(b, t, SB_W)
    sv = proj[:, COL_SB + 2 * SB_W:COL_SB + 3 * SB_W].reshape(b, t, SB_W)
    if sb_past is not None:
        past_k, past_v = sb_past
        q_off = past_k.shape[1]
        kv_len = q_off + t
        kv_pad = -(-kv_len // tk) * tk - kv_len
        kk = jnp.pad(jnp.concatenate([past_k.reshape(b, q_off, SB_W), sk], axis=1), ((0, 0), (0, kv_pad), (0, 0)))
        vv = jnp.pad(jnp.concatenate([past_v.reshape(b, q_off, SB_W), sv], axis=1), ((0, 0), (0, kv_pad), (0, 0)))
        kt, vh = _sb_layout(kk, vv, tk)
        gqkv_act, _ = _gdn_pre(proj, ctx8, pk['conv_w8'], e_heads, b, t, cfg['conv_tt'])
        gate_t = _time_on_lanes(proj[:, COL_GATE:COL_GATE + LANES], b * nc, l)
        gk_t = _time_on_lanes(gqkv_act[:, REC_W:2 * REC_W], b * nc, l)
        mk_t = _time_on_lanes(proj[:, COL_MQKV + REC_W:COL_MQKV + 2 * REC_W], b * nc, l)

    o_sb = _sb_attn(proj, kt, vh, b, t, SB_HEADS, cfg['sb_tq'], q_off)
    o_g, gdn_s_new = _gdn_scan(gqkv_act, gk_t, proj, gate_t, pk['prow'], pk['pcol'], gdn_s, b, t, l, cb)
    m0 = jnp.pad(ml_m, ((0, 0), (0, LANES - ML_HEADS)))[:, None, :]
    if sb_past is None:
        c0 = _to_pair_lanes(jnp.swapaxes(ml_c, -1, -2))
        n0 = jnp.pad(_to_pair_lanes(ml_n[:, :, None, :])[:, :, 0, :], ((0, 0), (0, SUBLANES - ML_HEADS), (0, 0)))
        h_m, c_new, n_new, m_new = _ml_scan_t(proj, mk_t, gate_t, pk['prow'], pk['pcol'], c0, n0, m0, b, t, l, cb)
        c_new = jnp.swapaxes(_from_pair_lanes(c_new), -1, -2)
        n_new = _from_pair_lanes(n_new[:, :ML_HEADS, None, :])[:, :, 0, :]
    else:
        n0 = jnp.pad(ml_n, ((0, 0), (0, SUBLANES - ML_HEADS), (0, 0)))
        h_m, c_new, n_new, m_new = _ml_scan(proj, mk_t, gate_t, pk['prow'], pk['pcol'], ml_c, n0, m0, b, t, l, cb)
        n_new = n_new[:, :ML_HEADS, :]

    x1, x1b = _post(x, o_sb, o_g, h_m, proj, pk['w_out'], e_heads, pk['gnw'], pk['mnw'], *pk['ln1'], cfg['tm'])
    x2, x2b = _memattn(x1, x1b, pk['w_cq'], pk['w_co'], mk, mv, *pk['ln2'], b, t, cfg['mem_tm'])
    x3, x3b = _ffn(x2, x2b, pk['w_up'], pk['w_down'], *pk['ln3'], cfg['tm'], cfg['ffn_tf'])

    new_conv = proj.reshape(b, t, PROJ_W)[:, t - (CONV_W - 1):, COL_GQKV:COL_GQKV + 3 * REC_W]
    return (x3, x3b, sk.reshape(b, t, SB_HEADS, HEAD_DIM), sv.reshape(b, t, SB_HEADS, HEAD_DIM), new_conv,
            gdn_s_new, c_new, n_new, m_new[:, 0, :ML_HEADS])


def kernel(x_prompt, x_sample, cache_sb_k, cache_sb_v, cache_gdn_conv, state_gdn, state_mlstm_C, state_mlstm_n, state_mlstm_m, cache_mem_k, cache_mem_v, mem_prompt, w_in, gdn_conv_w, gdn_A_log, gdn_dt_bias, gdn_norm_w, mlstm_i_bias, mlstm_f_bias, mlstm_norm_w, w_out, ln1_g, ln1_b, w_cq, w_ckv, w_co, ln2_g, ln2_b, w_up, w_down, ln3_g, ln3_b):
    bp, tp, d = x_prompt.shape
    bs, ts, _ = x_sample.shape
    depth = w_in.shape[0]
    n_mem = mem_prompt.shape[1]
    dff = w_down.shape[1]
    cfg_p = dict(tm=512, sb_tq=1024, sb_tk=256, conv_tt=512, mem_tm=512, ffn_tf=dff // 2, scan_cb=4)
    cfg_s = dict(tm=bs * ts, sb_tq=ts, sb_tk=256, conv_tt=ts, mem_tm=ts, ffn_tf=dff // 2, scan_cb=1)
    e_heads = _head_consts()

    xp = x_prompt.reshape(bp * tp, d)
    xs = x_sample.reshape(bs * ts, d)
    xpb, xsb = None, xs.astype(BF16)
    mem_b = mem_prompt.reshape(bp * n_mem, d).astype(BF16)
    p_out = [[] for _ in range(9)]
    s_out = [[] for _ in range(7)]
    for li in range(depth):
        p = {'w_in': w_in[li], 'gdn_conv_w': gdn_conv_w[li], 'gdn_A_log': gdn_A_log[li],
             'gdn_dt_bias': gdn_dt_bias[li], 'gdn_norm_w': gdn_norm_w[li], 'mlstm_i_bias': mlstm_i_bias[li],
             'mlstm_f_bias': mlstm_f_bias[li], 'mlstm_norm_w': mlstm_norm_w[li], 'w_out': w_out[li],
             'ln1_g': ln1_g[li], 'ln1_b': ln1_b[li], 'w_cq': w_cq[li], 'w_co': w_co[li],
             'ln2_g': ln2_g[li], 'ln2_b': ln2_b[li], 'w_up': w_up[li], 'w_down': w_down[li],
             'ln3_g': ln3_g[li], 'ln3_b': ln3_b[li]}
        pk = _pack_layer(p)
        mkv = _matmul(mem_b, w_ckv[li].astype(BF16), n_mem, 1024)
        mk_p = mkv[:, :d].reshape(bp, n_mem, d)
        mv_p = mkv[:, d:].reshape(bp, n_mem, d)
        res = _trunk_layer(xp, xpb, pk, e_heads, mk_p.astype(BF16), mv_p.astype(BF16), bp, tp, cfg_p, None,
                           jnp.zeros((bp, CONV_W - 1, 3 * REC_W), F32),
                           jnp.zeros((bp, GDN_HEADS, HEAD_DIM, HEAD_DIM), F32),
                           jnp.zeros((bp, ML_HEADS, HEAD_DIM, HEAD_DIM), F32),
                           jnp.zeros((bp, ML_HEADS, HEAD_DIM), F32),
                           jnp.zeros((bp, ML_HEADS), F32))
        xp, xpb = res[0], res[1]
        hd = d // MEM_HEADS
        for j, a in enumerate(res[2:] + (mk_p.reshape(bp, n_mem, MEM_HEADS, hd), mv_p.reshape(bp, n_mem, MEM_HEADS, hd))):
            p_out[j].append(a)
        res = _trunk_layer(xs, xsb, pk, e_heads,
                           cache_mem_k[li].reshape(bs, n_mem, d).astype(BF16),
                           cache_mem_v[li].reshape(bs, n_mem, d).astype(BF16),
                           bs, ts, cfg_s, (cache_sb_k[li], cache_sb_v[li]), cache_gdn_conv[li],
                           state_gdn[li], state_mlstm_C[li], state_mlstm_n[li], state_mlstm_m[li])
        xs, xsb = res[0], res[1]
        for j, a in enumerate(res[2:]):
            s_out[j].append(a)
    p_st = [jnp.stack(a) for a in p_out]
    s_st = [jnp.stack(a) for a in s_out]
    return (xp.reshape(bp, tp, d), xs.reshape(bs, ts, d), *p_st, *s_st)
```

```python
import functools

import numpy as np
import jax
import jax.numpy as jnp
from jax import lax
from jax.experimental import pallas as pl
from jax.experimental.pallas import tpu as pltpu

F32 = jnp.float32
BF16 = jnp.bfloat16

HEAD_DIM = 64
SB_HEADS = 4
GDN_HEADS = 6
ML_HEADS = 6
REC_W = GDN_HEADS * HEAD_DIM
SB_W = SB_HEADS * HEAD_DIM
CONV_W = 4
CHUNK = 64
MEM_HEADS = 4
LN_EPS = 1e-5
NORM_EPS = 1e-6
DEPTH = 2
DEEPNORM_ALPHA = (2 * DEPTH) ** 0.25
LOG2E = float(np.log2(np.e))
SOFTPLUS2_CLAMP = 120.0
F32_EXP2_ZERO = 160.0
BOUND_SLACK = 1.01

GROUP = 256
LANES = 128
SUBLANES = 8
VMEM_LIMIT = 56 * 1024 * 1024

PROJ_W = 4096
COL_GQKV = 0
COL_MQKV = 1152
COL_GZ = 2304
COL_MO = 2688
COL_SB = 3072
COL_GATE = 3840
GATE_BETA, GATE_G, GATE_I, GATE_F = 0, 6, 12, 18


def _cparams(*sem):
    return pltpu.CompilerParams(dimension_semantics=sem, vmem_limit_bytes=VMEM_LIMIT)


def _dot(a, b):
    return jnp.dot(a.astype(BF16), b.astype(BF16), preferred_element_type=F32)


def _dot_nt(a, b):
    return lax.dot_general(a.astype(BF16), b.astype(BF16), (((1,), (1,)), ((), ())),
                           preferred_element_type=F32)


def _dot_tn(a, b):
    return lax.dot_general(a.astype(BF16), b.astype(BF16), (((0,), (0,)), ((), ())),
                           preferred_element_type=F32)


def _split(x, parts):
    out = []
    r = x
    for _ in range(parts - 1):
        p = r.astype(BF16)
        out.append(p)
        r = r - p.astype(F32)
    out.append(r.astype(BF16))
    return out


def _dot_exact_lhs(m, x, parts):
    acc = None
    for p in _split(x, parts):
        t = jnp.dot(m, p, preferred_element_type=F32)
        acc = t if acc is None else acc + t
    return acc


def _dot_exact_rhs(x, m, parts):
    acc = None
    for p in _split(x, parts):
        t = jnp.dot(p, m, preferred_element_type=F32)
        acc = t if acc is None else acc + t
    return acc


def _ones_where(mask, dtype):
    return jnp.where(mask, 1.0, 0.0).astype(dtype)


def _sigmoid(x):
    return 1.0 / (1.0 + jnp.exp(-x))


def _layer_norm(y, g, b):
    mu = jnp.mean(y, axis=-1, keepdims=True)
    d = y - mu
    var = jnp.mean(d * d, axis=-1, keepdims=True)
    return d * lax.rsqrt(var + LN_EPS) * g + b


def _mm_kernel(x_ref, w_ref, o_ref):
    o_ref[...] = jnp.dot(x_ref[...], w_ref[...], preferred_element_type=F32).astype(o_ref.dtype)


def _matmul(x, w, tm, tn, out_dtype=F32):
    n, k = x.shape
    m = w.shape[1]
    return pl.pallas_call(
        _mm_kernel,
        grid=(n // tm, m // tn),
        in_specs=[pl.BlockSpec((tm, k), lambda i, j: (i, 0)),
                  pl.BlockSpec((k, tn), lambda i, j: (0, j))],
        out_specs=pl.BlockSpec((tm, tn), lambda i, j: (i, j)),
        out_shape=jax.ShapeDtypeStruct((n, m), out_dtype),
        compiler_params=_cparams("parallel", "parallel"),
        name="proj_matmul",
    )(x, w)


def _proj_kernel(x_ref, w_ref, o_ref, skt_ref, sv_ref, mkt_ref, gtt_ref, *, tm, tn):
    xb = x_ref[...].astype(BF16)
    for j in range(PROJ_W // tn):
        o_ref[:, j * tn:(j + 1) * tn] = jnp.dot(xb, w_ref[:, j * tn:(j + 1) * tn], preferred_element_type=F32)
    for r in range(tm // GROUP):
        rows = slice(r * GROUP, (r + 1) * GROUP)
        skt_ref[r] = o_ref[rows, COL_SB + SB_W:COL_SB + 2 * SB_W].T.astype(BF16)
        for h in range(SB_HEADS):
            c0 = COL_SB + 2 * SB_W + h * HEAD_DIM
            sv_ref[h, r] = o_ref[rows, c0:c0 + HEAD_DIM].astype(BF16)
        _store_chunks_t(mkt_ref, r, o_ref[rows, COL_MQKV:COL_MQKV + 3 * REC_W])
        _store_chunks_t(gtt_ref, r, o_ref[rows, COL_GATE:COL_GATE + LANES])


def _store_chunks_t(ref, r, a):
    at = a.T
    per = GROUP // CHUNK
    for j in range(per):
        ref[r * per + j] = at[:, j * CHUNK:(j + 1) * CHUNK]


def _proj(x, w, tm):
    n, d = x.shape
    ng = n // GROUP
    gpt = tm // GROUP
    cpt = tm // CHUNK
    return pl.pallas_call(
        functools.partial(_proj_kernel, tm=tm, tn=1024),
        grid=(n // tm,),
        in_specs=[pl.BlockSpec((tm, d), lambda i: (i, 0)),
                  pl.BlockSpec((d, PROJ_W), lambda i: (0, 0), pipeline_mode=pl.Buffered(1))],
        out_specs=[pl.BlockSpec((tm, PROJ_W), lambda i: (i, 0)),
                   pl.BlockSpec((gpt, SB_W, GROUP), lambda i: (i, 0, 0)),
                   pl.BlockSpec((SB_HEADS, gpt, GROUP, HEAD_DIM), lambda i: (0, i, 0, 0)),
                   pl.BlockSpec((cpt, 3 * REC_W, CHUNK), lambda i: (i, 0, 0)),
                   pl.BlockSpec((cpt, LANES, CHUNK), lambda i: (i, 0, 0))],
        out_shape=[jax.ShapeDtypeStruct((n, PROJ_W), F32),
                   jax.ShapeDtypeStruct((ng, SB_W, GROUP), BF16),
                   jax.ShapeDtypeStruct((SB_HEADS, ng, GROUP, HEAD_DIM), BF16),
                   jax.ShapeDtypeStruct((n // CHUNK, 3 * REC_W, CHUNK), F32),
                   jax.ShapeDtypeStruct((n // CHUNK, LANES, CHUNK), F32)],
        compiler_params=_cparams("parallel"),
        name="proj_full",
    )(x, w)


def _gdn_pre_kernel(x_ref, ctx_ref, w_ref, e_ref, o_ref, *rest, tt):
    xbuf = rest[-1]
    i = pl.program_id(1)

    @pl.when(i == 0)
    def _():
        xbuf[0:SUBLANES, :] = ctx_ref[0]

    xbuf[SUBLANES:SUBLANES + tt, :] = x_ref[...]
    acc = w_ref[CONV_W - 1:CONV_W, :] * xbuf[SUBLANES:SUBLANES + tt, :]
    for j in range(CONV_W - 1):
        off = SUBLANES - (CONV_W - 1) + j
        acc = acc + w_ref[j:j + 1, :] * xbuf[off:off + tt, :]
    y = acc * _sigmoid(acc)
    q = y[:, 0:REC_W]
    k = y[:, REC_W:2 * REC_W]
    e = e_ref[...]
    qs = _dot_exact_rhs(q * q, e, 2)
    ks = _dot_exact_rhs(k * k, e, 2)
    kn = k * lax.rsqrt(ks + NORM_EPS)
    o_ref[:, 0:REC_W] = q * lax.rsqrt(qs + NORM_EPS) * (HEAD_DIM ** -0.5)
    o_ref[:, REC_W:2 * REC_W] = kn
    o_ref[:, 2 * REC_W:3 * REC_W] = y[:, 2 * REC_W:3 * REC_W]
    if len(rest) == 2:
        for r in range(tt // GROUP):
            _store_chunks_t(rest[0], r, kn[r * GROUP:(r + 1) * GROUP, :])
    xbuf[0:SUBLANES, :] = xbuf[tt:tt + SUBLANES, :]


def _gdn_pre(proj, ctx8, conv_w8, e_heads, b, t, tt):
    w3 = 3 * REC_W
    nt = t // tt
    emit_kt = tt % GROUP == 0
    out_specs = [pl.BlockSpec((tt, w3), lambda bi, i: (bi * nt + i, 0))]
    out_shape = [jax.ShapeDtypeStruct((b * t, w3), F32)]
    if emit_kt:
        out_specs.append(pl.BlockSpec((tt // CHUNK, REC_W, CHUNK), lambda bi, i: (bi * nt + i, 0, 0)))
        out_shape.append(jax.ShapeDtypeStruct((b * t // CHUNK, REC_W, CHUNK), F32))
    res = pl.pallas_call(
        functools.partial(_gdn_pre_kernel, tt=tt),
        grid=(b, nt),
        in_specs=[pl.BlockSpec((tt, w3), lambda bi, i: (bi * nt + i, COL_GQKV // w3)),
                  pl.BlockSpec((1, SUBLANES, w3), lambda bi, i: (bi, 0, 0)),
                  pl.BlockSpec((SUBLANES, w3), lambda bi, i: (0, 0)),
                  pl.BlockSpec((REC_W, REC_W), lambda bi, i: (0, 0))],
        out_specs=out_specs,
        out_shape=out_shape,
        scratch_shapes=[pltpu.VMEM((tt + SUBLANES, w3), F32)],
        compiler_params=_cparams("parallel", "arbitrary"),
        name="gdn_pre",
    )(proj, ctx8, conv_w8, e_heads)
    return (res[0], res[1]) if emit_kt else (res[0], None)


def _sb_kernel(q_ref, kt_ref, v_ref, o_ref, acc_ref, c_ref, zmax_ref, kmax_ref,
               *, hb, tq, tk, nkb, q_off, diag_static):
    qi = pl.program_id(1)
    q_lo = q_off + qi * tq
    nblk = jnp.minimum(nkb, (q_lo + tq - 1 + tk - 1) // tk)
    nfull = jnp.minimum(nblk, q_lo // tk)
    scale = HEAD_DIM ** -0.5 * LOG2E
    qs = [(q_ref[:, h * HEAD_DIM:(h + 1) * HEAD_DIM] * scale).astype(BF16) for h in range(hb)]
    lower_incl = _ones_where(lax.broadcasted_iota(jnp.int32, (tk, tk), 0)
                             >= lax.broadcasted_iota(jnp.int32, (tk, tk), 1), BF16)
    acc_ref[...] = jnp.zeros_like(acc_ref)
    c_ref[...] = jnp.zeros_like(c_ref)

    @pl.when(qi == 0)
    def _():
        def knorm(kb, best):
            k32 = kt_ref[0, kb].astype(F32)
            sq = k32 * k32
            return tuple(jnp.maximum(best[h], jnp.sum(sq[h * HEAD_DIM:(h + 1) * HEAD_DIM, :], axis=0, keepdims=True))
                         for h in range(hb))
        best = lax.fori_loop(0, nkb, knorm, tuple(jnp.zeros((1, tk), F32) for _ in range(hb)))
        for h in range(hb):
            kmax_ref[h] = jnp.broadcast_to(jnp.sqrt(jnp.max(best[h], axis=1, keepdims=True)), (1, LANES))

    for h in range(hb):
        q32 = qs[h].astype(F32)
        qn = jnp.sqrt(jnp.sum(q32 * q32, axis=1, keepdims=True))
        zmax_ref[h] = qn * kmax_ref[h] * BOUND_SLACK

    def all_weights_vanish():
        margin = c_ref[0] - zmax_ref[0]
        for h in range(1, hb):
            margin = jnp.minimum(margin, c_ref[h] - zmax_ref[h])
        return jnp.min(margin) > F32_EXP2_ZERO

    def blocks(kbs, r0=0, valids=None):
        for h in range(hb):
            q = qs[h][r0:, :]
            c = c_ref[h, r0:, :]
            av = None
            for i, kb in enumerate(kbs):
                kt = kt_ref[0, kb, h * HEAD_DIM:(h + 1) * HEAD_DIM, :]
                z = jnp.dot(q, kt, preferred_element_type=F32)
                lneg = jnp.maximum(z, jnp.log2(1.0 + jnp.exp2(jnp.minimum(z, SOFTPLUS2_CLAMP))))
                if valids is not None:
                    lneg = jnp.where(valids[i], lneg, 0.0)
                incl = jnp.dot(lneg.astype(BF16), lower_incl, preferred_element_type=F32)
                a = jnp.exp2(z - incl - jnp.tile(c, (1, tk // LANES)))
                if valids is not None:
                    a = jnp.where(valids[i], a, 0.0)
                t = jnp.dot(a.astype(BF16), v_ref[h, kb], preferred_element_type=F32)
                av = t if av is None else av + t
                c = c + incl[:, 0:1]
            acc_ref[h, r0:, :] += av
            c_ref[h, r0:, :] = c

    if diag_static:
        for r in reversed(range(tq // tk)):
            rows = tq - r * tk
            valid = (lax.broadcasted_iota(jnp.int32, (rows, tk), 1)
                     < lax.broadcasted_iota(jnp.int32, (rows, tk), 0))
            blocks([nfull + r], r0=r * tk, valids=[valid])
    else:
        @pl.loop(0, nblk - nfull)
        def _(i):
            kb = nblk - 1 - i
            valid = (kb * tk + lax.broadcasted_iota(jnp.int32, (tq, tk), 1)
                     < q_lo + lax.broadcasted_iota(jnp.int32, (tq, tk), 0))
            blocks([kb], valids=[valid])

    def more_blocks(carry):
        i, done = carry
        return jnp.logical_and(i < nfull, done == 0)

    def next_block(carry):
        i, _ = carry
        blocks([nfull - 1 - i])
        return i + 1, all_weights_vanish().astype(jnp.int32)

    lax.while_loop(more_blocks, next_block, (jnp.int32(0), all_weights_vanish().astype(jnp.int32)))

    for h in range(hb):
        o_ref[:, h * HEAD_DIM:(h + 1) * HEAD_DIM] = acc_ref[h]


def _sb_attn(proj, kt, vh, b, t, hb, tq, q_off):
    _, nkb, w, tk = kt.shape
    d = w // hb
    nq = t // tq
    diag_static = q_off % tk == 0 and tq % tk == 0 and q_off + t <= nkb * tk
    return pl.pallas_call(
        functools.partial(_sb_kernel, hb=hb, tq=tq, tk=tk, nkb=nkb, q_off=q_off, diag_static=diag_static),
        grid=(b, nq),
        in_specs=[pl.BlockSpec((tq, w), lambda bi, i: (bi * nq + i, COL_SB // w)),
                  pl.BlockSpec((1, nkb, w, tk), lambda bi, i: (bi, 0, 0, 0), pipeline_mode=pl.Buffered(1)),
                  pl.BlockSpec((hb, nkb, tk, d), lambda bi, i: (bi, 0, 0, 0), pipeline_mode=pl.Buffered(1))],
        out_specs=pl.BlockSpec((tq, w), lambda bi, i: (bi * nq + i, 0)),
        out_shape=jax.ShapeDtypeStruct((b * t, w), F32),
        scratch_shapes=[pltpu.VMEM((hb, tq, d), F32), pltpu.VMEM((hb, tq, LANES), F32),
                        pltpu.VMEM((hb, tq, LANES), F32), pltpu.VMEM((hb, 1, LANES), F32)],
        compiler_params=_cparams("parallel", "arbitrary"),
        name="sb_attn",
    )(proj, kt, vh)


def _gate_values(pre, neg_a, lane_id):
    sp = jnp.log1p(jnp.exp(-jnp.abs(pre)))
    softplus = jnp.maximum(pre, 0.0) + sp
    log_sig = jnp.minimum(pre, 0.0) - sp
    return jnp.where(lane_id < GATE_G, _sigmoid(pre),
                     jnp.where(lane_id < GATE_I, neg_a * softplus,
                               jnp.where(lane_id < GATE_F, pre, log_sig)))


def _stack(a, b):
    return jnp.concatenate([a, b], axis=0)


def _gates(gt, gtt, prow_ref, pcol_ref, l):
    lane = lax.broadcasted_iota(jnp.int32, (l, LANES), 1)
    val = _gate_values(gt + prow_ref[0:1, :], -jnp.exp(prow_ref[1:2, :]), lane)
    sub = lax.broadcasted_iota(jnp.int32, (LANES, l), 0)
    val_t = _gate_values(gtt + pcol_ref[:, 0:1], -jnp.exp(pcol_ref[:, 1:2]), sub)
    r = lax.broadcasted_iota(jnp.int32, (l, l), 0)
    c = lax.broadcasted_iota(jnp.int32, (l, l), 1)
    csum = _dot_exact_lhs(_ones_where(r >= c, BF16), val, 3)
    csum_t = _dot_exact_rhs(val_t, _ones_where(r <= c, BF16), 3)
    return val, csum, val_t, csum_t


def _gdn_scan_kernel(qkv_ref, kt_ref, gt_ref, gtt_ref, prow_ref, pcol_ref, s0_ref, o_ref, s_out_ref, s_scr,
                     *, l, cb, ng):
    gi = pl.program_id(1)

    @pl.when(gi == 0)
    def _():
        s_scr[...] = s0_ref[0]

    r = lax.broadcasted_iota(jnp.int32, (l, l), 0)
    c = lax.broadcasted_iota(jnp.int32, (l, l), 1)
    tri = r >= c
    stri = r > c
    eye = _ones_where(r == c, F32)
    n_double = max(int(np.ceil(np.log2(l))) - 1, 0)
    hs = range(GDN_HEADS)
    ps = [(j, h) for j in range(cb) for h in hs]
    gates = [_gates(gt_ref[j * l:(j + 1) * l, :], gtt_ref[j], prow_ref, pcol_ref, l) for j in range(cb)]

    def head_cols(j, group, h):
        return qkv_ref[j * l:(j + 1) * l, group * REC_W + h * HEAD_DIM:group * REC_W + (h + 1) * HEAD_DIM]

    q = {p: head_cols(p[0], 0, p[1]) for p in ps}
    k = {p: head_cols(p[0], 1, p[1]) for p in ps}
    v = {p: head_cols(p[0], 2, p[1]) for p in ps}
    kt = {p: kt_ref[p[0], p[1] * HEAD_DIM:(p[1] + 1) * HEAD_DIM, :] for p in ps}
    beta = {p: gates[p[0]][0][:, GATE_BETA + p[1]:GATE_BETA + p[1] + 1] for p in ps}
    g_col = {p: gates[p[0]][1][:, GATE_G + p[1]:GATE_G + p[1] + 1] for p in ps}
    g_row = {p: gates[p[0]][3][GATE_G + p[1]:GATE_G + p[1] + 1, :] for p in ps}
    g_last = {p: g_col[p][l - 1:l, :] for p in ps}
    decay = {p: jnp.where(tri, jnp.exp(jnp.where(tri, g_col[p] - g_row[p], 0.0)), 0.0) for p in ps}
    kb = {p: k[p] * beta[p] for p in ps}
    e_g = {p: jnp.exp(g_col[p]) for p in ps}
    kq = {p: _dot(_stack(kb[p], q[p]), kt[p]) for p in ps}
    x = {p: -jnp.where(stri, kq[p][:l] * decay[p], 0.0) for p in ps}
    tinv = {p: eye + x[p] for p in ps}
    if n_double > 0:
        x = {p: _dot(x[p], x[p]) for p in ps}
        for _ in range(n_double - 1):
            tx = {p: _dot(_stack(tinv[p], x[p]), x[p]) for p in ps}
            tinv = {p: tinv[p] + tx[p][:l] for p in ps}
            x = {p: tx[p][l:] for p in ps}
        tinv = {p: tinv[p] + _dot(tinv[p], x[p]) for p in ps}
    sol_v = {p: _dot(tinv[p], v[p] * beta[p]) for p in ps}
    sol_k = {p: _dot(tinv[p], kb[p] * e_g[p]) for p in ps}
    lhs_s = {p: _stack(q[p] * e_g[p], sol_k[p]) for p in ps}
    lhs_u = {p: _stack(kt[p] * jnp.exp(g_last[p] - g_row[p]), kq[p][l:] * decay[p]) for p in ps}
    e_last = {p: jnp.exp(g_last[p]) for p in ps}
    s = [s_scr[h] for h in hs]
    for j in range(cb):
        t = [_dot(lhs_s[(j, h)], s[h]) for h in hs]
        u = [sol_v[(j, h)] - t[h][l:] for h in hs]
        w = [_dot(lhs_u[(j, h)], u[h]) for h in hs]
        s = [e_last[(j, h)] * s[h] + w[h][:HEAD_DIM] for h in hs]
        for h in hs:
            o_ref[j * l:(j + 1) * l, h * HEAD_DIM:(h + 1) * HEAD_DIM] = t[h][:l] + w[h][HEAD_DIM:]
    for h in hs:
        s_scr[h] = s[h]

    @pl.when(gi == ng - 1)
    def _():
        s_out_ref[0] = s_scr[...]


def _gdn_scan(qkv, k_t, proj, gate_t, prow, pcol, s0, b, t, l, cb):
    ng = t // (l * cb)
    w3 = 3 * REC_W
    return pl.pallas_call(
        functools.partial(_gdn_scan_kernel, l=l, cb=cb, ng=ng),
        grid=(b, ng),
        in_specs=[pl.BlockSpec((cb * l, w3), lambda bi, i: (bi * ng + i, 0)),
                  pl.BlockSpec((cb, REC_W, l), lambda bi, i: (bi * ng + i, 0, 0)),
                  pl.BlockSpec((cb * l, LANES), lambda bi, i: (bi * ng + i, COL_GATE // LANES)),
                  pl.BlockSpec((cb, LANES, l), lambda bi, i: (bi * ng + i, 0, 0)),
                  pl.BlockSpec((SUBLANES, LANES), lambda bi, i: (0, 0)),
                  pl.BlockSpec((LANES, SUBLANES), lambda bi, i: (0, 0)),
                  pl.BlockSpec((1, GDN_HEADS, HEAD_DIM, HEAD_DIM), lambda bi, i: (bi, 0, 0, 0))],
        out_specs=[pl.BlockSpec((cb * l, REC_W), lambda bi, i: (bi * ng + i, 0)),
                   pl.BlockSpec((1, GDN_HEADS, HEAD_DIM, HEAD_DIM), lambda bi, i: (bi, 0, 0, 0))],
        out_shape=[jax.ShapeDtypeStruct((b * t, REC_W), F32),
                   jax.ShapeDtypeStruct((b, GDN_HEADS, HEAD_DIM, HEAD_DIM), F32)],
        scratch_shapes=[pltpu.VMEM((GDN_HEADS, HEAD_DIM, HEAD_DIM), F32)],
        compiler_params=_cparams("parallel", "arbitrary"),
        name="gdn_scan",
    )(qkv, k_t, proj, gate_t, prow, pcol, s0)


def _ml_scan_kernel(qkv_ref, kt_ref, gt_ref, gtt_ref, prow_ref, pcol_ref, c0_ref, n0_ref, m0_ref,
                    o_ref, c_out_ref, n_out_ref, m_out_ref, c_scr, n_scr, m_scr, *, l, cb, ng):
    gi = pl.program_id(1)

    @pl.when(gi == 0)
    def _():
        c_scr[...] = c0_ref[0]
        n_scr[...] = n0_ref[0]
        m_scr[...] = m0_ref[0]

    r = lax.broadcasted_iota(jnp.int32, (l, l), 0)
    c = lax.broadcasted_iota(jnp.int32, (l, l), 1)
    tri = r >= c
    lane = lax.broadcasted_iota(jnp.int32, (1, LANES), 1)
    m_all = m_scr[...]
    hs = range(ML_HEADS)
    ps = [(j, h) for j in range(cb) for h in hs]
    kscale = HEAD_DIM ** -0.5
    gates = [_gates(gt_ref[j * l:(j + 1) * l, :], gtt_ref[j], prow_ref, pcol_ref, l) for j in range(cb)]

    def head_cols(j, group, h):
        return qkv_ref[j * l:(j + 1) * l, group * REC_W + h * HEAD_DIM:group * REC_W + (h + 1) * HEAD_DIM]

    q = {p: head_cols(p[0], 0, p[1]) for p in ps}
    k = {p: head_cols(p[0], 1, p[1]) * kscale for p in ps}
    v = {p: head_cols(p[0], 2, p[1]) for p in ps}
    kt = {p: kt_ref[p[0], p[1] * HEAD_DIM:(p[1] + 1) * HEAD_DIM, :] * kscale for p in ps}
    ig_col = {p: gates[p[0]][0][:, GATE_I + p[1]:GATE_I + p[1] + 1] for p in ps}
    ig_row = {p: gates[p[0]][2][GATE_I + p[1]:GATE_I + p[1] + 1, :] for p in ps}
    f_col = {p: gates[p[0]][1][:, GATE_F + p[1]:GATE_F + p[1] + 1] for p in ps}
    f_row = {p: gates[p[0]][3][GATE_F + p[1]:GATE_F + p[1] + 1, :] for p in ps}
    f_last = {p: f_col[p][l - 1:l, :] for p in ps}
    qk = {p: _dot(q[p], kt[p]) for p in ps}
    d = {p: jnp.where(tri, f_col[p] - f_row[p] + ig_row[p], -jnp.inf) for p in ps}
    d_max = {p: jnp.max(d[p], axis=1, keepdims=True) for p in ps}
    m_prev, m_t = {}, {}
    m_run = [m_all[:, h:h + 1] for h in hs]
    for j in range(cb):
        for h in hs:
            m_prev[(j, h)] = m_run[h]
            m_t[(j, h)] = jnp.maximum(f_col[(j, h)] + m_run[h], d_max[(j, h)])
            m_run[h] = m_t[(j, h)][l - 1:l, :]
    m_new = {p: m_t[p][l - 1:l, :] for p in ps}
    w = {p: jnp.exp(d[p] - m_t[p]) * qk[p] for p in ps}
    c_inter = {p: jnp.exp(f_col[p] + m_prev[p] - m_t[p]) for p in ps}
    wv = {p: _dot(w[p], v[p]) for p in ps}
    w_sum = {p: jnp.sum(w[p], axis=1, keepdims=True) for p in ps}
    e_m = {p: jnp.exp(-m_t[p]) for p in ps}
    w_end_row = {p: jnp.exp(f_last[p] - f_row[p] + ig_row[p] - m_new[p]) for p in ps}
    w_end_col = {p: jnp.exp(f_last[p] - f_col[p] + ig_col[p] - m_new[p]) for p in ps}
    c_prev = {p: jnp.exp(f_last[p] + m_prev[p] - m_new[p]) for p in ps}
    dc = {p: _dot(kt[p] * w_end_row[p], v[p]) for p in ps}
    dn = {p: jnp.sum(w_end_col[p] * k[p], axis=0, keepdims=True) for p in ps}
    c_h = [c_scr[h] for h in hs]
    n_h = [n_scr[h:h + 1, :] for h in hs]
    for j in range(cb):
        for h in hs:
            p = (j, h)
            num = c_inter[p] * _dot(q[p], c_h[h]) + wv[p]
            den = c_inter[p] * jnp.sum(q[p] * n_h[h], axis=1, keepdims=True) + w_sum[p]
            o_ref[j * l:(j + 1) * l, h * HEAD_DIM:(h + 1) * HEAD_DIM] = num / jnp.maximum(jnp.abs(den), e_m[p])
            c_h[h] = c_prev[p] * c_h[h] + dc[p]
            n_h[h] = c_prev[p] * n_h[h] + dn[p]
    m_next = m_all
    for h in hs:
        c_scr[h] = c_h[h]
        n_scr[h:h + 1, :] = n_h[h]
        m_next = jnp.where(lane == h, m_run[h], m_next)
    m_scr[...] = m_next

    @pl.when(gi == ng - 1)
    def _():
        c_out_ref[0] = c_scr[...]
        n_out_ref[0] = n_scr[...]
        m_out_ref[0] = m_scr[...]


def _ml_scan(proj, k_t, gate_t, prow, pcol, c0, n0, m0, b, t, l, cb):
    ng = t // (l * cb)
    w3 = 3 * REC_W
    st = lambda bi, i: (bi, 0, 0, 0)
    st3 = lambda bi, i: (bi, 0, 0)
    return pl.pallas_call(
        functools.partial(_ml_scan_kernel, l=l, cb=cb, ng=ng),
        grid=(b, ng),
        in_specs=[pl.BlockSpec((cb * l, w3), lambda bi, i: (bi * ng + i, COL_MQKV // w3)),
                  pl.BlockSpec((cb, REC_W, l), lambda bi, i: (bi * ng + i, 0, 0)),
                  pl.BlockSpec((cb * l, LANES), lambda bi, i: (bi * ng + i, COL_GATE // LANES)),
                  pl.BlockSpec((cb, LANES, l), lambda bi, i: (bi * ng + i, 0, 0)),
                  pl.BlockSpec((SUBLANES, LANES), lambda bi, i: (0, 0)),
                  pl.BlockSpec((LANES, SUBLANES), lambda bi, i: (0, 0)),
                  pl.BlockSpec((1, ML_HEADS, HEAD_DIM, HEAD_DIM), st),
                  pl.BlockSpec((1, SUBLANES, HEAD_DIM), st3),
                  pl.BlockSpec((1, 1, LANES), st3)],
        out_specs=[pl.BlockSpec((cb * l, REC_W), lambda bi, i: (bi * ng + i, 0)),
                   pl.BlockSpec((1, ML_HEADS, HEAD_DIM, HEAD_DIM), st),
                   pl.BlockSpec((1, SUBLANES, HEAD_DIM), st3),
                   pl.BlockSpec((1, 1, LANES), st3)],
        out_shape=[jax.ShapeDtypeStruct((b * t, REC_W), F32),
                   jax.ShapeDtypeStruct((b, ML_HEADS, HEAD_DIM, HEAD_DIM), F32),
                   jax.ShapeDtypeStruct((b, SUBLANES, HEAD_DIM), F32),
                   jax.ShapeDtypeStruct((b, 1, LANES), F32)],
        scratch_shapes=[pltpu.VMEM((ML_HEADS, HEAD_DIM, HEAD_DIM), F32),
                        pltpu.VMEM((SUBLANES, HEAD_DIM), F32),
                        pltpu.VMEM((1, LANES), F32)],
        compiler_params=_cparams("parallel", "arbitrary"),
        name="ml_scan",
    )(proj, k_t, proj, gate_t, prow, pcol, c0, n0, m0)


def _ml_scan_t_kernel(qkv_ref, qkvt_ref, gt_ref, gtt_ref, prow_ref, pcol_ref, c0_ref, n0_ref, m0_ref,
                      o_ref, c_out_ref, n_out_ref, m_out_ref, c_scr, n_scr, m_scr, *, l, cb, ng):
    gi = pl.program_id(1)

    @pl.when(gi == 0)
    def _():
        c_scr[...] = c0_ref[0]
        n_scr[...] = n0_ref[0]
        m_scr[...] = m0_ref[0]

    r = lax.broadcasted_iota(jnp.int32, (l, l), 0)
    c = lax.broadcasted_iota(jnp.int32, (l, l), 1)
    tri_t = r <= c
    lane = lax.broadcasted_iota(jnp.int32, (1, LANES), 1)
    m_all = m_scr[...]
    hs = range(ML_HEADS)
    ps = [(j, h) for j in range(cb) for h in hs]
    kscale = HEAD_DIM ** -0.5
    zeros_t = jnp.zeros((HEAD_DIM, l), F32)
    gates = [_gates(gt_ref[j * l:(j + 1) * l, :], gtt_ref[j], prow_ref, pcol_ref, l) for j in range(cb)]

    def t_rows(j, group, h):
        return qkvt_ref[j, group * REC_W + h * HEAD_DIM:group * REC_W + (h + 1) * HEAD_DIM, :]

    qt = {p: t_rows(p[0], 0, p[1]) for p in ps}
    qt_pad = {p: (_stack(qt[p], zeros_t) if p[1] % 2 == 0 else _stack(zeros_t, qt[p])) for p in ps}
    vt = {p: t_rows(p[0], 2, p[1]) for p in ps}
    k2 = {p: qkv_ref[p[0] * l:(p[0] + 1) * l, REC_W + (p[1] // 2) * LANES:REC_W + (p[1] // 2 + 1) * LANES] * kscale
          for p in ps}
    ig_col = {p: gates[p[0]][0][:, GATE_I + p[1]:GATE_I + p[1] + 1] for p in ps}
    ig_row = {p: gates[p[0]][2][GATE_I + p[1]:GATE_I + p[1] + 1, :] for p in ps}
    f_col = {p: gates[p[0]][1][:, GATE_F + p[1]:GATE_F + p[1] + 1] for p in ps}
    f_row = {p: gates[p[0]][3][GATE_F + p[1]:GATE_F + p[1] + 1, :] for p in ps}
    f_last = {p: f_row[p][:, l - 1:l] for p in ps}
    kq = {p: _dot(k2[p], qt_pad[p]) for p in ps}
    d = {p: jnp.where(tri_t, f_row[p] + (ig_col[p] - f_col[p]), -jnp.inf) for p in ps}
    d_max = {p: jnp.max(d[p], axis=0, keepdims=True) for p in ps}
    m_prev, m_t = {}, {}
    m_run = [m_all[:, h:h + 1] for h in hs]
    for j in range(cb):
        for h in hs:
            m_prev[(j, h)] = m_run[h]
            m_t[(j, h)] = jnp.maximum(f_row[(j, h)] + m_run[h], d_max[(j, h)])
            m_run[h] = m_t[(j, h)][:, l - 1:l]
    m_new = {p: m_t[p][:, l - 1:l] for p in ps}
    w = {p: jnp.exp(d[p] - m_t[p]) * kq[p] for p in ps}
    c_inter = {p: jnp.exp(f_row[p] + m_prev[p] - m_t[p]) for p in ps}
    vw = {p: _dot(vt[p], w[p]) for p in ps}
    w_sum = {p: jnp.sum(w[p], axis=0, keepdims=True) for p in ps}
    e_m = {p: jnp.exp(-m_t[p]) for p in ps}
    w_end_col = {p: jnp.exp(f_last[p] - f_col[p] + ig_col[p] - m_new[p]) for p in ps}
    c_prev = {p: jnp.exp(f_last[p] + m_prev[p] - m_new[p]) for p in ps}
    wk = {p: w_end_col[p] * k2[p] for p in ps}
    dc = {p: _dot(vt[p], wk[p]) for p in ps}
    dn = {p: jnp.sum(wk[p], axis=0, keepdims=True) for p in ps}
    c_h = [c_scr[h] for h in hs]
    n_h = [n_scr[h:h + 1, :] for h in hs]
    ht = {}
    for j in range(cb):
        for h in hs:
            p = (j, h)
            num = c_inter[p] * _dot(c_h[h], qt_pad[p]) + vw[p]
            den = c_inter[p] * _dot(n_h[h], qt_pad[p]) + w_sum[p]
            ht[p] = num / jnp.maximum(jnp.abs(den), e_m[p])
            c_h[h] = c_prev[p] * c_h[h] + dc[p]
            n_h[h] = c_prev[p] * n_h[h] + dn[p]
    for i in range(ML_HEADS // 2):
        pair_t = _stack(jnp.concatenate([ht[(j, 2 * i)] for j in range(cb)], axis=1),
                        jnp.concatenate([ht[(j, 2 * i + 1)] for j in range(cb)], axis=1))
        o_ref[:, i * LANES:(i + 1) * LANES] = pair_t.T
    m_next = m_all
    for h in hs:
        c_scr[h] = c_h[h]
        n_scr[h:h + 1, :] = n_h[h]
        m_next = jnp.where(lane == h, m_run[h], m_next)
    m_scr[...] = m_next

    @pl.when(gi == ng - 1)
    def _():
        c_out_ref[0] = c_scr[...]
        n_out_ref[0] = n_scr[...]
        m_out_ref[0] = m_scr[...]


def _ml_scan_t(proj, qkv_t, gate_t, prow, pcol, c0, n0, m0, b, t, l, cb):
    ng = t // (l * cb)
    w3 = 3 * REC_W
    st = lambda bi, i: (bi, 0, 0, 0)
    st3 = lambda bi, i: (bi, 0, 0)
    return pl.pallas_call(
        functools.partial(_ml_scan_t_kernel, l=l, cb=cb, ng=ng),
        grid=(b, ng),
        in_specs=[pl.BlockSpec((cb * l, w3), lambda bi, i: (bi * ng + i, COL_MQKV // w3)),
                  pl.BlockSpec((cb, w3, l), lambda bi, i: (bi * ng + i, 0, 0)),
                  pl.BlockSpec((cb * l, LANES), lambda bi, i: (bi * ng + i, COL_GATE // LANES)),
                  pl.BlockSpec((cb, LANES, l), lambda bi, i: (bi * ng + i, 0, 0)),
                  pl.BlockSpec((SUBLANES, LANES), lambda bi, i: (0, 0)),
                  pl.BlockSpec((LANES, SUBLANES), lambda bi, i: (0, 0)),
                  pl.BlockSpec((1, ML_HEADS, HEAD_DIM, LANES), st),
                  pl.BlockSpec((1, SUBLANES, LANES), st3),
                  pl.BlockSpec((1, 1, LANES), st3)],
        out_specs=[pl.BlockSpec((cb * l, REC_W), lambda bi, i: (bi * ng + i, 0)),
                   pl.BlockSpec((1, ML_HEADS, HEAD_DIM, LANES), st),
                   pl.BlockSpec((1, SUBLANES, LANES), st3),
                   pl.BlockSpec((1, 1, LANES), st3)],
        out_shape=[jax.ShapeDtypeStruct((b * t, REC_W), F32),
                   jax.ShapeDtypeStruct((b, ML_HEADS, HEAD_DIM, LANES), F32),
                   jax.ShapeDtypeStruct((b, SUBLANES, LANES), F32),
                   jax.ShapeDtypeStruct((b, 1, LANES), F32)],
        scratch_shapes=[pltpu.VMEM((ML_HEADS, HEAD_DIM, LANES), F32),
                        pltpu.VMEM((SUBLANES, LANES), F32),
                        pltpu.VMEM((1, LANES), F32)],
        compiler_params=_cparams("parallel", "arbitrary"),
        name="ml_scan_t",
    )(proj, qkv_t, proj, gate_t, prow, pcol, c0, n0, m0)


def _post_kernel(x_ref, osb_ref, og_ref, hm_ref, gz_ref, mo_ref, w_ref, e_ref, gnw_ref, mnw_ref,
                 lg_ref, lb_ref, o_ref, ob_ref):
    e = e_ref[...]
    og = og_ref[...]
    hm = hm_ref[...]
    og_ms = _dot_exact_rhs(og * og, e, 2) * (1.0 / HEAD_DIM)
    hm_ms = _dot_exact_rhs(hm * hm, e, 2) * (1.0 / HEAD_DIM)
    gz = gz_ref[...]
    o_g = og * lax.rsqrt(og_ms + NORM_EPS) * gnw_ref[...] * (gz * _sigmoid(gz))
    o_m = _sigmoid(mo_ref[...]) * (hm * lax.rsqrt(hm_ms + NORM_EPS) * mnw_ref[...])
    mixed = (_dot(osb_ref[...], w_ref[0:SB_W, :])
             + _dot(o_g, w_ref[SB_W:SB_W + REC_W, :])
             + _dot(o_m, w_ref[SB_W + REC_W:SB_W + 2 * REC_W, :]))
    y = _layer_norm(DEEPNORM_ALPHA * x_ref[...] + mixed, lg_ref[...], lb_ref[...])
    o_ref[...] = y
    ob_ref[...] = y.astype(BF16)


def _post(x, osb, og, hm, proj, w_out, e_heads, gnw, mnw, lg, lb, tm):
    n, d = x.shape
    row = lambda i: (i, 0)
    fixed = lambda i: (0, 0)
    return pl.pallas_call(
        _post_kernel,
        grid=(n // tm,),
        in_specs=[pl.BlockSpec((tm, d), row),
                  pl.BlockSpec((tm, SB_W), row),
                  pl.BlockSpec((tm, REC_W), row),
                  pl.BlockSpec((tm, REC_W), row),
                  pl.BlockSpec((tm, REC_W), lambda i: (i, COL_GZ // REC_W)),
                  pl.BlockSpec((tm, REC_W), lambda i: (i, COL_MO // REC_W)),
                  pl.BlockSpec(w_out.shape, fixed),
                  pl.BlockSpec((REC_W, REC_W), fixed),
                  pl.BlockSpec((1, REC_W), fixed),
                  pl.BlockSpec((1, REC_W), fixed),
                  pl.BlockSpec((1, d), fixed),
                  pl.BlockSpec((1, d), fixed)],
        out_specs=[pl.BlockSpec((tm, d), row), pl.BlockSpec((tm, d), row)],
        out_shape=[jax.ShapeDtypeStruct((n, d), F32), jax.ShapeDtypeStruct((n, d), BF16)],
        compiler_params=_cparams("parallel"),
        name="post",
    )(x, osb, og, hm, proj, proj, w_out, e_heads, gnw, mnw, lg, lb)


def _memattn_kernel(x_ref, xb_ref, wq_ref, wo_ref, mk_ref, mv_ref, lg_ref, lb_ref, o_ref, ob_ref):
    d = x_ref.shape[1]
    hd = d // MEM_HEADS
    q = jnp.dot(xb_ref[...], wq_ref[...], preferred_element_type=F32)
    out = None
    for h in range(MEM_HEADS):
        sl = slice(h * hd, (h + 1) * hd)
        s = _dot_nt(q[:, sl], mk_ref[0, :, sl]) * (hd ** -0.5)
        p = jnp.exp(s - jnp.max(s, axis=1, keepdims=True))
        o_h = _dot(p, mv_ref[0, :, sl]) / jnp.sum(p, axis=1, keepdims=True)
        t = _dot(o_h, wo_ref[sl, :])
        out = t if out is None else out + t
    y = _layer_norm(DEEPNORM_ALPHA * x_ref[...] + out, lg_ref[...], lb_ref[...])
    o_ref[...] = y
    ob_ref[...] = y.astype(BF16)


def _memattn(x, xb, w_cq, w_co, mk, mv, lg, lb, b, t, tm):
    n, d = x.shape
    nt = t // tm
    nm = mk.shape[1]
    row = lambda bi, i: (bi * nt + i, 0)
    fixed = lambda bi, i: (0, 0)
    return pl.pallas_call(
        _memattn_kernel,
        grid=(b, nt),
        in_specs=[pl.BlockSpec((tm, d), row),
                  pl.BlockSpec((tm, d), row),
                  pl.BlockSpec((d, d), fixed),
                  pl.BlockSpec((d, d), fixed),
                  pl.BlockSpec((1, nm, d), lambda bi, i: (bi, 0, 0)),
                  pl.BlockSpec((1, nm, d), lambda bi, i: (bi, 0, 0)),
                  pl.BlockSpec((1, d), fixed),
                  pl.BlockSpec((1, d), fixed)],
        out_specs=[pl.BlockSpec((tm, d), row), pl.BlockSpec((tm, d), row)],
        out_shape=[jax.ShapeDtypeStruct((n, d), F32), jax.ShapeDtypeStruct((n, d), BF16)],
        compiler_params=_cparams("parallel", "parallel"),
        name="memattn",
    )(x, xb, w_cq, w_co, mk, mv, lg, lb)


def _ffn_kernel(x_ref, xb_ref, wu_ref, wd_ref, lg_ref, lb_ref, o_ref, ob_ref, *, tf):
    xb = xb_ref[...]
    dff = wd_ref.shape[0]
    acc = None
    for j in range(dff // tf):
        gate = jnp.dot(xb, wu_ref[:, j * tf:(j + 1) * tf], preferred_element_type=F32)
        up = jnp.dot(xb, wu_ref[:, dff + j * tf:dff + (j + 1) * tf], preferred_element_type=F32)
        t = _dot(gate * _sigmoid(gate) * up, wd_ref[j * tf:(j + 1) * tf, :])
        acc = t if acc is None else acc + t
    y = _layer_norm(DEEPNORM_ALPHA * x_ref[...] + acc, lg_ref[...], lb_ref[...])
    o_ref[...] = y
    ob_ref[...] = y.astype(BF16)


def _ffn(x, xb, w_up, w_down, lg, lb, tm, tf):
    n, d = x.shape
    row = lambda i: (i, 0)
    fixed = lambda i: (0, 0)
    return pl.pallas_call(
        functools.partial(_ffn_kernel, tf=tf),
        grid=(n // tm,),
        in_specs=[pl.BlockSpec((tm, d), row),
                  pl.BlockSpec((tm, d), row),
                  pl.BlockSpec(w_up.shape, fixed, pipeline_mode=pl.Buffered(1)),
                  pl.BlockSpec(w_down.shape, fixed, pipeline_mode=pl.Buffered(1)),
                  pl.BlockSpec((1, d), fixed),
                  pl.BlockSpec((1, d), fixed)],
        out_specs=[pl.BlockSpec((tm, d), row), pl.BlockSpec((tm, d), row)],
        out_shape=[jax.ShapeDtypeStruct((n, d), F32), jax.ShapeDtypeStruct((n, d), BF16)],
        compiler_params=_cparams("parallel"),
        name="ffn",
    )(x, xb, w_up, w_down, lg, lb)


def _pack_layer(p):
    w_in = p['w_in']
    d = w_in.shape[0]
    o_sb, o_gqkv, o_gz = 0, 3 * SB_W, 3 * SB_W + 3 * REC_W
    o_gb = o_gz + REC_W
    o_mqkv = o_gb + 2 * GDN_HEADS
    o_mo = o_mqkv + 3 * REC_W
    o_mi = o_mo + REC_W
    gate_cols = jnp.concatenate([w_in[:, o_gb:o_gb + 2 * GDN_HEADS], w_in[:, o_mi:o_mi + 2 * ML_HEADS]], axis=1)
    used = COL_GATE + gate_cols.shape[1]
    w_al = jnp.concatenate([w_in[:, o_gqkv:o_gqkv + 3 * REC_W], w_in[:, o_mqkv:o_mqkv + 3 * REC_W],
                            w_in[:, o_gz:o_gz + REC_W], w_in[:, o_mo:o_mo + REC_W],
                            w_in[:, o_sb:o_sb + 3 * SB_W], gate_cols,
                            jnp.zeros((d, PROJ_W - used), w_in.dtype)], axis=1).astype(BF16)
    prow = jnp.zeros((SUBLANES, LANES), F32)
    prow = prow.at[0, GATE_G:GATE_G + GDN_HEADS].set(p['gdn_dt_bias'])
    prow = prow.at[0, GATE_I:GATE_I + ML_HEADS].set(p['mlstm_i_bias'])
    prow = prow.at[0, GATE_F:GATE_F + ML_HEADS].set(p['mlstm_f_bias'])
    prow = prow.at[1, GATE_G:GATE_G + GDN_HEADS].set(p['gdn_A_log'])
    conv_w8 = jnp.zeros((SUBLANES, 3 * REC_W), F32).at[0:CONV_W].set(p['gdn_conv_w'])
    return dict(
        w_al=w_al, prow=prow, pcol=prow.T, conv_w8=conv_w8,
        gnw=jnp.tile(p['gdn_norm_w'], GDN_HEADS)[None, :], mnw=p['mlstm_norm_w'][None, :],
        w_out=p['w_out'].astype(BF16), w_cq=p['w_cq'].astype(BF16), w_co=p['w_co'].astype(BF16),
        w_up=p['w_up'].astype(BF16), w_down=p['w_down'].astype(BF16),
        ln1=(p['ln1_g'][None, :], p['ln1_b'][None, :]), ln2=(p['ln2_g'][None, :], p['ln2_b'][None, :]),
        ln3=(p['ln3_g'][None, :], p['ln3_b'][None, :]))


def _head_consts():
    hid = np.arange(REC_W) // HEAD_DIM
    return jnp.asarray(hid[:, None] == hid[None, :], BF16)


def _sb_layout(k, v, tk):
    b, tkv, _ = k.shape
    nkb = tkv // tk
    kt = k.reshape(b, nkb, tk, SB_W).transpose(0, 1, 3, 2)
    vh = v.reshape(b, nkb, tk, SB_HEADS, HEAD_DIM).transpose(0, 3, 1, 2, 4)
    vh = vh.reshape(b * SB_HEADS, nkb, tk, HEAD_DIM)
    return kt.astype(BF16), vh.astype(BF16)


def _to_pair_lanes(a):
    z = jnp.zeros_like(a)
    even = (jnp.arange(a.shape[1]) % 2 == 0)[None, :, None, None]
    return jnp.where(even, jnp.concatenate([a, z], axis=-1), jnp.concatenate([z, a], axis=-1))


def _from_pair_lanes(a):
    even = (jnp.arange(a.shape[1]) % 2 == 0)[None, :, None, None]
    return jnp.where(even, a[..., :HEAD_DIM], a[..., HEAD_DIM:])


def _time_on_lanes(a, groups, width):
    return a.reshape(groups, width, a.shape[1]).transpose(0, 2, 1)


def _trunk_layer(x, xb, pk, e_heads, mk, mv, b, t, cfg, sb_past, conv_ctx, gdn_s, ml_c, ml_n, ml_m):
    n, d = x.shape
    l = min(t, CHUNK)
    nc = t // l
    cb = min(cfg['scan_cb'], nc)
    ng = b * nc // cb
    tk = cfg['sb_tk']
    ctx8 = jnp.pad(conv_ctx, ((0, 0), (SUBLANES - (CONV_W - 1), 0), (0, 0)))
    if sb_past is None:
        proj, kt, vh, mk_t, gate_t = _proj(x if xb is None else xb, pk['w_al'], cfg['tm'])
        kt = kt.reshape(b, t // tk, SB_W, tk)
        q_off = 0
        gqkv_act, gk_t = _gdn_pre(proj, ctx8, pk['conv_w8'], e_heads, b, t, cfg['conv_tt'])
    else:
        proj = _matmul(xb, pk['w_al'], cfg['tm'], 1024)
    sk = proj[:, COL_SB + SB_W:COL_SB + 2 * SB_W].reshape(b, t, SB_W)
    sv = proj[:, COL_SB + 2 * SB_W:COL_SB + 3 * SB_W].reshape(b, t, SB_W)
    if sb_past is not None:
        past_k, past_v = sb_past
        q_off = past_k.shape[1]
        kv_len = q_off + t
        kv_pad = -(-kv_len // tk) * tk - kv_len
        kk = jnp.pad(jnp.concatenate([past_k.reshape(b, q_off, SB_W), sk], axis=1), ((0, 0), (0, kv_pad), (0, 0)))
        vv = jnp.pad(jnp.concatenate([past_v.reshape(b, q_off, SB_W), sv], axis=1), ((0, 0), (0, kv_pad), (0, 0)))
        kt, vh = _sb_layout(kk, vv, tk)
        gqkv_act, _ = _gdn_pre(proj, ctx8, pk['conv_w8'], e_heads, b, t, cfg['conv_tt'])
        gate_t = _time_on_lanes(proj[:, COL_GATE:COL_GATE + LANES], b * nc, l)
        gk_t = _time_on_lanes(gqkv_act[:, REC_W:2 * REC_W], b * nc, l)
        mk_t = _time_on_lanes(proj[:, COL_MQKV + REC_W:COL_MQKV + 2 * REC_W], b * nc, l)

    o_sb = _sb_attn(proj, kt, vh, b, t, SB_HEADS, cfg['sb_tq'], q_off)
    o_g, gdn_s_new = _gdn_scan(gqkv_act, gk_t, proj, gate_t, pk['prow'], pk['pcol'], gdn_s, b, t, l, cb)
    m0 = jnp.pad(ml_m, ((0, 0), (0, LANES - ML_HEADS)))[:, None, :]
    if sb_past is None:
        c0 = _to_pair_lanes(jnp.swapaxes(ml_c, -1, -2))
        n0 = jnp.pad(_to_pair_lanes(ml_n[:, :, None, :])[:, :, 0, :], ((0, 0), (0, SUBLANES - ML_HEADS), (0, 0)))
        h_m, c_new, n_new, m_new = _ml_scan_t(proj, mk_t, gate_t, pk['prow'], pk['pcol'], c0, n0, m0, b, t, l, cb)
        c_new = jnp.swapaxes(_from_pair_lanes(c_new), -1, -2)
        n_new = _from_pair_lanes(n_new[:, :ML_HEADS, None, :])[:, :, 0, :]
    else:
        n0 = jnp.pad(ml_n, ((0, 0), (0, SUBLANES - ML_HEADS), (0, 0)))
        h_m, c_new, n_new, m_new = _ml_scan(proj, mk_t, gate_t, pk['prow'], pk['pcol'], ml_c, n0, m0, b, t, l, cb)
        n_new = n_new[:, :ML_HEADS, :]

    x1, x1b = _post(x, o_sb, o_g, h_m, proj, pk['w_out'], e_heads, pk['gnw'], pk['mnw'], *pk['ln1'], cfg['post_tm'])
    x2, x2b = _memattn(x1, x1b, pk['w_cq'], pk['w_co'], mk, mv, *pk['ln2'], b, t, cfg['mem_tm'])
    x3, x3b = _ffn(x2, x2b, pk['w_up'], pk['w_down'], *pk['ln3'], cfg['tm'], cfg['ffn_tf'])

    new_conv = proj.reshape(b, t, PROJ_W)[:, t - (CONV_W - 1):, COL_GQKV:COL_GQKV + 3 * REC_W]
    return (x3, x3b, sk.reshape(b, t, SB_HEADS, HEAD_DIM), sv.reshape(b, t, SB_HEADS, HEAD_DIM), new_conv,
            gdn_s_new, c_new, n_new, m_new[:, 0, :ML_HEADS])


def kernel(x_prompt, x_sample, cache_sb_k, cache_sb_v, cache_gdn_conv, state_gdn, state_mlstm_C, state_mlstm_n, state_mlstm_m, cache_mem_k, cache_mem_v, mem_prompt, w_in, gdn_conv_w, gdn_A_log, gdn_dt_bias, gdn_norm_w, mlstm_i_bias, mlstm_f_bias, mlstm_norm_w, w_out, ln1_g, ln1_b, w_cq, w_ckv, w_co, ln2_g, ln2_b, w_up, w_down, ln3_g, ln3_b):
    bp, tp, d = x_prompt.shape
    bs, ts, _ = x_sample.shape
    depth = w_in.shape[0]
    n_mem = mem_prompt.shape[1]
    dff = w_down.shape[1]
    cfg_p = dict(tm=512, post_tm=512, sb_tq=512, sb_tk=256, conv_tt=512, mem_tm=512, ffn_tf=dff // 2, scan_cb=4)
    cfg_s = dict(tm=bs * ts, post_tm=bs * ts, sb_tq=ts, sb_tk=256, conv_tt=ts, mem_tm=ts, ffn_tf=dff // 2, scan_cb=1)
    e_heads = _head_consts()

    xp = x_prompt.reshape(bp * tp, d)
    xs = x_sample.reshape(bs * ts, d)
    xpb, xsb = None, xs.astype(BF16)
    mem_b = mem_prompt.reshape(bp * n_mem, d).astype(BF16)
    p_out = [[] for _ in range(9)]
    s_out = [[] for _ in range(7)]
    for li in range(depth):
        p = {'w_in': w_in[li], 'gdn_conv_w': gdn_conv_w[li], 'gdn_A_log': gdn_A_log[li],
             'gdn_dt_bias': gdn_dt_bias[li], 'gdn_norm_w': gdn_norm_w[li], 'mlstm_i_bias': mlstm_i_bias[li],
             'mlstm_f_bias': mlstm_f_bias[li], 'mlstm_norm_w': mlstm_norm_w[li], 'w_out': w_out[li],
             'ln1_g': ln1_g[li], 'ln1_b': ln1_b[li], 'w_cq': w_cq[li], 'w_co': w_co[li],
             'ln2_g': ln2_g[li], 'ln2_b': ln2_b[li], 'w_up': w_up[li], 'w_down': w_down[li],
             'ln3_g': ln3_g[li], 'ln3_b': ln3_b[li]}
        pk = _pack_layer(p)
        mkv = _matmul(mem_b, w_ckv[li].astype(BF16), n_mem, 1024)
        mk_p = mkv[:, :d].reshape(bp, n_mem, d)
        mv_p = mkv[:, d:].reshape(bp, n_mem, d)
        res = _trunk_layer(xp, xpb, pk, e_heads, mk_p.astype(BF16), mv_p.astype(BF16), bp, tp, cfg_p, None,
                           jnp.zeros((bp, CONV_W - 1, 3 * REC_W), F32),
                           jnp.zeros((bp, GDN_HEADS, HEAD_DIM, HEAD_DIM), F32),
                           jnp.zeros((bp, ML_HEADS, HEAD_DIM, HEAD_DIM), F32),
                           jnp.zeros((bp, ML_HEADS, HEAD_DIM), F32),
                           jnp.zeros((bp, ML_HEADS), F32))
        xp, xpb = res[0], res[1]
        hd = d // MEM_HEADS
        for j, a in enumerate(res[2:] + (mk_p.reshape(bp, n_mem, MEM_HEADS, hd), mv_p.reshape(bp, n_mem, MEM_HEADS, hd))):
            p_out[j].append(a)
        res = _trunk_layer(xs, xsb, pk, e_heads,
                           cache_mem_k[li].reshape(bs, n_mem, d).astype(BF16),
                           cache_mem_v[li].reshape(bs, n_mem, d).astype(BF16),
                           bs, ts, cfg_s, (cache_sb_k[li], cache_sb_v[li]), cache_gdn_conv[li],
                           state_gdn[li], state_mlstm_C[li], state_mlstm_n[li], state_mlstm_m[li])
        xs, xsb = res[0], res[1]
        for j, a in enumerate(res[2:]):
            s_out[j].append(a)
    p_st = [jnp.stack(a) for a in p_out]
    s_st = [jnp.stack(a) for a in s_out]
    return (xp.reshape(bp, tp, d), xs.reshape(bs, ts, d), *p_st, *s_st)
```

```python
import functools

import numpy as np
import jax
import jax.numpy as jnp
from jax import lax
from jax.experimental import pallas as pl
from jax.experimental.pallas import tpu as pltpu

F32 = jnp.float32
BF16 = jnp.bfloat16

HEAD_DIM = 64
SB_HEADS = 4
GDN_HEADS = 6
ML_HEADS = 6
REC_W = GDN_HEADS * HEAD_DIM
SB_W = SB_HEADS * HEAD_DIM
CONV_W = 4
CHUNK = 64
MEM_HEADS = 4
LN_EPS = 1e-5
NORM_EPS = 1e-6
DEPTH = 2
DEEPNORM_ALPHA = (2 * DEPTH) ** 0.25
LOG2E = float(np.log2(np.e))
SOFTPLUS2_CLAMP = 120.0
NORM_SUM_PARTS = 1
F32_EXP2_ZERO = 160.0
BOUND_SLACK = 1.01

GROUP = 256
LANES = 128
SUBLANES = 8
VMEM_LIMIT = 56 * 1024 * 1024

PROJ_W = 4096
COL_GQKV = 0
COL_MQKV = 1152
COL_GZ = 2304
COL_MO = 2688
COL_SB = 3072
COL_GATE = 3840
GATE_BETA, GATE_G, GATE_I, GATE_F = 0, 6, 12, 18


def _cparams(*sem):
    return pltpu.CompilerParams(dimension_semantics=sem, vmem_limit_bytes=VMEM_LIMIT)


def _dot(a, b):
    return jnp.dot(a.astype(BF16), b.astype(BF16), preferred_element_type=F32)


def _dot_nt(a, b):
    return lax.dot_general(a.astype(BF16), b.astype(BF16), (((1,), (1,)), ((), ())),
                           preferred_element_type=F32)


def _dot_tn(a, b):
    return lax.dot_general(a.astype(BF16), b.astype(BF16), (((0,), (0,)), ((), ())),
                           preferred_element_type=F32)


def _split(x, parts):
    out = []
    r = x
    for _ in range(parts - 1):
        p = r.astype(BF16)
        out.append(p)
        r = r - p.astype(F32)
    out.append(r.astype(BF16))
    return out


def _dot_exact_lhs(m, x, parts):
    acc = None
    for p in _split(x, parts):
        t = jnp.dot(m, p, preferred_element_type=F32)
        acc = t if acc is None else acc + t
    return acc


def _dot_exact_rhs(x, m, parts):
    acc = None
    for p in _split(x, parts):
        t = jnp.dot(p, m, preferred_element_type=F32)
        acc = t if acc is None else acc + t
    return acc


def _ones_where(mask, dtype):
    return jnp.where(mask, 1.0, 0.0).astype(dtype)


def _sigmoid(x):
    return 1.0 / (1.0 + jnp.exp(-x))


def _layer_norm(y, g, b):
    mu = jnp.mean(y, axis=-1, keepdims=True)
    d = y - mu
    var = jnp.mean(d * d, axis=-1, keepdims=True)
    return d * lax.rsqrt(var + LN_EPS) * g + b


def _mm_kernel(x_ref, w_ref, o_ref):
    o_ref[...] = jnp.dot(x_ref[...], w_ref[...], preferred_element_type=F32).astype(o_ref.dtype)


def _matmul(x, w, tm, tn, out_dtype=F32):
    n, k = x.shape
    m = w.shape[1]
    return pl.pallas_call(
        _mm_kernel,
        grid=(n // tm, m // tn),
        in_specs=[pl.BlockSpec((tm, k), lambda i, j: (i, 0)),
                  pl.BlockSpec((k, tn), lambda i, j: (0, j))],
        out_specs=pl.BlockSpec((tm, tn), lambda i, j: (i, j)),
        out_shape=jax.ShapeDtypeStruct((n, m), out_dtype),
        compiler_params=_cparams("parallel", "parallel"),
        name="proj_matmul",
    )(x, w)


def _proj_kernel(x_ref, w_ref, o_ref, skt_ref, sv_ref, mkt_ref, gtt_ref, *, tm, tn):
    xb = x_ref[...].astype(BF16)
    for j in range(PROJ_W // tn):
        o_ref[:, j * tn:(j + 1) * tn] = jnp.dot(xb, w_ref[:, j * tn:(j + 1) * tn], preferred_element_type=F32)
    for r in range(tm // GROUP):
        rows = slice(r * GROUP, (r + 1) * GROUP)
        skt_ref[r] = o_ref[rows, COL_SB + SB_W:COL_SB + 2 * SB_W].T.astype(BF16)
        for h in range(SB_HEADS):
            c0 = COL_SB + 2 * SB_W + h * HEAD_DIM
            sv_ref[h, r] = o_ref[rows, c0:c0 + HEAD_DIM].astype(BF16)
        _store_chunks_t(mkt_ref, r, o_ref[rows, COL_MQKV:COL_MQKV + 3 * REC_W])
        _store_chunks_t(gtt_ref, r, o_ref[rows, COL_GATE:COL_GATE + LANES])


def _store_chunks_t(ref, r, a):
    at = a.T
    per = GROUP // CHUNK
    for j in range(per):
        ref[r * per + j] = at[:, j * CHUNK:(j + 1) * CHUNK]


def _proj(x, w, tm):
    n, d = x.shape
    ng = n // GROUP
    gpt = tm // GROUP
    cpt = tm // CHUNK
    return pl.pallas_call(
        functools.partial(_proj_kernel, tm=tm, tn=1024),
        grid=(n // tm,),
        in_specs=[pl.BlockSpec((tm, d), lambda i: (i, 0)),
                  pl.BlockSpec((d, PROJ_W), lambda i: (0, 0), pipeline_mode=pl.Buffered(1))],
        out_specs=[pl.BlockSpec((tm, PROJ_W), lambda i: (i, 0)),
                   pl.BlockSpec((gpt, SB_W, GROUP), lambda i: (i, 0, 0)),
                   pl.BlockSpec((SB_HEADS, gpt, GROUP, HEAD_DIM), lambda i: (0, i, 0, 0)),
                   pl.BlockSpec((cpt, 3 * REC_W, CHUNK), lambda i: (i, 0, 0)),
                   pl.BlockSpec((cpt, LANES, CHUNK), lambda i: (i, 0, 0))],
        out_shape=[jax.ShapeDtypeStruct((n, PROJ_W), F32),
                   jax.ShapeDtypeStruct((ng, SB_W, GROUP), BF16),
                   jax.ShapeDtypeStruct((SB_HEADS, ng, GROUP, HEAD_DIM), BF16),
                   jax.ShapeDtypeStruct((n // CHUNK, 3 * REC_W, CHUNK), F32),
                   jax.ShapeDtypeStruct((n // CHUNK, LANES, CHUNK), F32)],
        compiler_params=_cparams("parallel"),
        name="proj_full",
    )(x, w)


def _gdn_pre_kernel(x_ref, ctx_ref, w_ref, e_ref, o_ref, *rest, tt):
    xbuf = rest[-1]
    i = pl.program_id(1)

    @pl.when(i == 0)
    def _():
        xbuf[0:SUBLANES, :] = ctx_ref[0]

    xbuf[SUBLANES:SUBLANES + tt, :] = x_ref[...]
    acc = w_ref[CONV_W - 1:CONV_W, :] * xbuf[SUBLANES:SUBLANES + tt, :]
    for j in range(CONV_W - 1):
        off = SUBLANES - (CONV_W - 1) + j
        acc = acc + w_ref[j:j + 1, :] * xbuf[off:off + tt, :]
    y = acc * _sigmoid(acc)
    q = y[:, 0:REC_W]
    k = y[:, REC_W:2 * REC_W]
    e = e_ref[...]
    qs = _dot_exact_rhs(q * q, e, NORM_SUM_PARTS)
    ks = _dot_exact_rhs(k * k, e, NORM_SUM_PARTS)
    kn = k * lax.rsqrt(ks + NORM_EPS)
    o_ref[:, 0:REC_W] = q * lax.rsqrt(qs + NORM_EPS) * (HEAD_DIM ** -0.5)
    o_ref[:, REC_W:2 * REC_W] = kn
    o_ref[:, 2 * REC_W:3 * REC_W] = y[:, 2 * REC_W:3 * REC_W]
    if len(rest) == 2:
        for r in range(tt // GROUP):
            _store_chunks_t(rest[0], r, kn[r * GROUP:(r + 1) * GROUP, :])
    xbuf[0:SUBLANES, :] = xbuf[tt:tt + SUBLANES, :]


def _gdn_pre(proj, ctx8, conv_w8, e_heads, b, t, tt):
    w3 = 3 * REC_W
    nt = t // tt
    emit_kt = tt % GROUP == 0
    out_specs = [pl.BlockSpec((tt, w3), lambda bi, i: (bi * nt + i, 0))]
    out_shape = [jax.ShapeDtypeStruct((b * t, w3), F32)]
    if emit_kt:
        out_specs.append(pl.BlockSpec((tt // CHUNK, REC_W, CHUNK), lambda bi, i: (bi * nt + i, 0, 0)))
        out_shape.append(jax.ShapeDtypeStruct((b * t // CHUNK, REC_W, CHUNK), F32))
    res = pl.pallas_call(
        functools.partial(_gdn_pre_kernel, tt=tt),
        grid=(b, nt),
        in_specs=[pl.BlockSpec((tt, w3), lambda bi, i: (bi * nt + i, COL_GQKV // w3)),
                  pl.BlockSpec((1, SUBLANES, w3), lambda bi, i: (bi, 0, 0)),
                  pl.BlockSpec((SUBLANES, w3), lambda bi, i: (0, 0)),
                  pl.BlockSpec((REC_W, REC_W), lambda bi, i: (0, 0))],
        out_specs=out_specs,
        out_shape=out_shape,
        scratch_shapes=[pltpu.VMEM((tt + SUBLANES, w3), F32)],
        compiler_params=_cparams("parallel", "arbitrary"),
        name="gdn_pre",
    )(proj, ctx8, conv_w8, e_heads)
    return (res[0], res[1]) if emit_kt else (res[0], None)


def _sb_kernel(q_ref, kt_ref, v_ref, o_ref, acc_ref, c_ref, zmax_ref, kmax_ref,
               *, hb, tq, tk, nkb, q_off, diag_static):
    qi = pl.program_id(1)
    q_lo = q_off + qi * tq
    nblk = jnp.minimum(nkb, (q_lo + tq - 1 + tk - 1) // tk)
    nfull = jnp.minimum(nblk, q_lo // tk)
    scale = HEAD_DIM ** -0.5 * LOG2E
    qs = [(q_ref[:, h * HEAD_DIM:(h + 1) * HEAD_DIM] * scale).astype(BF16) for h in range(hb)]
    lower_incl = _ones_where(lax.broadcasted_iota(jnp.int32, (tk, tk), 0)
                             >= lax.broadcasted_iota(jnp.int32, (tk, tk), 1), BF16)
    acc_ref[...] = jnp.zeros_like(acc_ref)
    c_ref[...] = jnp.zeros_like(c_ref)

    @pl.when(qi == 0)
    def _():
        def knorm(kb, best):
            k32 = kt_ref[0, kb].astype(F32)
            sq = k32 * k32
            return tuple(jnp.maximum(best[h], jnp.sum(sq[h * HEAD_DIM:(h + 1) * HEAD_DIM, :], axis=0, keepdims=True))
                         for h in range(hb))
        best = lax.fori_loop(0, nkb, knorm, tuple(jnp.zeros((1, tk), F32) for _ in range(hb)))
        for h in range(hb):
            kmax_ref[h] = jnp.broadcast_to(jnp.sqrt(jnp.max(best[h], axis=1, keepdims=True)), (1, LANES))

    for h in range(hb):
        q32 = qs[h].astype(F32)
        qn = jnp.sqrt(jnp.sum(q32 * q32, axis=1, keepdims=True))
        zmax_ref[h] = qn * kmax_ref[h] * BOUND_SLACK

    def all_weights_vanish():
        margin = c_ref[0] - zmax_ref[0]
        for h in range(1, hb):
            margin = jnp.minimum(margin, c_ref[h] - zmax_ref[h])
        return jnp.min(margin) > F32_EXP2_ZERO

    def blocks(kbs, r0=0, valids=None):
        for h in range(hb):
            q = qs[h][r0:, :]
            c = c_ref[h, r0:, :]
            av = None
            for i, kb in enumerate(kbs):
                kt = kt_ref[0, kb, h * HEAD_DIM:(h + 1) * HEAD_DIM, :]
                z = jnp.dot(q, kt, preferred_element_type=F32)
                lneg = jnp.maximum(z, jnp.log2(1.0 + jnp.exp2(jnp.minimum(z, SOFTPLUS2_CLAMP))))
                if valids is not None:
                    lneg = jnp.where(valids[i], lneg, 0.0)
                incl = jnp.dot(lneg.astype(BF16), lower_incl, preferred_element_type=F32)
                a = jnp.exp2(z - incl - jnp.tile(c, (1, tk // LANES)))
                if valids is not None:
                    a = jnp.where(valids[i], a, 0.0)
                t = jnp.dot(a.astype(BF16), v_ref[h, kb], preferred_element_type=F32)
                av = t if av is None else av + t
                c = c + incl[:, 0:1]
            acc_ref[h, r0:, :] += av
            c_ref[h, r0:, :] = c

    if diag_static:
        for r in reversed(range(tq // tk)):
            rows = tq - r * tk
            valid = (lax.broadcasted_iota(jnp.int32, (rows, tk), 1)
                     < lax.broadcasted_iota(jnp.int32, (rows, tk), 0))
            blocks([nfull + r], r0=r * tk, valids=[valid])
    else:
        @pl.loop(0, nblk - nfull)
        def _(i):
            kb = nblk - 1 - i
            valid = (kb * tk + lax.broadcasted_iota(jnp.int32, (tq, tk), 1)
                     < q_lo + lax.broadcasted_iota(jnp.int32, (tq, tk), 0))
            blocks([kb], valids=[valid])

    def more_blocks(carry):
        i, done = carry
        return jnp.logical_and(i < nfull, done == 0)

    def next_block(carry):
        i, _ = carry
        blocks([nfull - 1 - i])
        return i + 1, all_weights_vanish().astype(jnp.int32)

    lax.while_loop(more_blocks, next_block, (jnp.int32(0), all_weights_vanish().astype(jnp.int32)))

    for h in range(hb):
        o_ref[:, h * HEAD_DIM:(h + 1) * HEAD_DIM] = acc_ref[h]


def _sb_attn(proj, kt, vh, b, t, hb, tq, q_off):
    _, nkb, w, tk = kt.shape
    d = w // hb
    nq = t // tq
    diag_static = q_off % tk == 0 and tq % tk == 0 and q_off + t <= nkb * tk
    return pl.pallas_call(
        functools.partial(_sb_kernel, hb=hb, tq=tq, tk=tk, nkb=nkb, q_off=q_off, diag_static=diag_static),
        grid=(b, nq),
        in_specs=[pl.BlockSpec((tq, w), lambda bi, i: (bi * nq + i, COL_SB // w)),
                  pl.BlockSpec((1, nkb, w, tk), lambda bi, i: (bi, 0, 0, 0), pipeline_mode=pl.Buffered(1)),
                  pl.BlockSpec((hb, nkb, tk, d), lambda bi, i: (bi, 0, 0, 0), pipeline_mode=pl.Buffered(1))],
        out_specs=pl.BlockSpec((tq, w), lambda bi, i: (bi * nq + i, 0)),
        out_shape=jax.ShapeDtypeStruct((b * t, w), F32),
        scratch_shapes=[pltpu.VMEM((hb, tq, d), F32), pltpu.VMEM((hb, tq, LANES), F32),
                        pltpu.VMEM((hb, tq, LANES), F32), pltpu.VMEM((hb, 1, LANES), F32)],
        compiler_params=_cparams("parallel", "arbitrary"),
        name="sb_attn",
    )(proj, kt, vh)


def _gate_values(pre, neg_a, lane_id):
    sp = jnp.log1p(jnp.exp(-jnp.abs(pre)))
    softplus = jnp.maximum(pre, 0.0) + sp
    log_sig = jnp.minimum(pre, 0.0) - sp
    return jnp.where(lane_id < GATE_G, _sigmoid(pre),
                     jnp.where(lane_id < GATE_I, neg_a * softplus,
                               jnp.where(lane_id < GATE_F, pre, log_sig)))


def _stack(a, b):
    return jnp.concatenate([a, b], axis=0)


def _gates(gt, gtt, prow_ref, pcol_ref, l):
    lane = lax.broadcasted_iota(jnp.int32, (l, LANES), 1)
    val = _gate_values(gt + prow_ref[0:1, :], -jnp.exp(prow_ref[1:2, :]), lane)
    sub = lax.broadcasted_iota(jnp.int32, (LANES, l), 0)
    val_t = _gate_values(gtt + pcol_ref[:, 0:1], -jnp.exp(pcol_ref[:, 1:2]), sub)
    r = lax.broadcasted_iota(jnp.int32, (l, l), 0)
    c = lax.broadcasted_iota(jnp.int32, (l, l), 1)
    csum = _dot_exact_lhs(_ones_where(r >= c, BF16), val, 3)
    csum_t = _dot_exact_rhs(val_t, _ones_where(r <= c, BF16), 3)
    return val, csum, val_t, csum_t


def _gdn_scan_kernel(qkv_ref, kt_ref, gt_ref, gtt_ref, prow_ref, pcol_ref, s0_ref, o_ref, s_out_ref, s_scr,
                     *, l, cb, ng):
    gi = pl.program_id(1)

    @pl.when(gi == 0)
    def _():
        s_scr[...] = s0_ref[0]

    r = lax.broadcasted_iota(jnp.int32, (l, l), 0)
    c = lax.broadcasted_iota(jnp.int32, (l, l), 1)
    tri = r >= c
    stri = r > c
    eye = _ones_where(r == c, F32)
    n_double = max(int(np.ceil(np.log2(l))) - 1, 0)
    hs = range(GDN_HEADS)
    ps = [(j, h) for j in range(cb) for h in hs]
    gates = [_gates(gt_ref[j * l:(j + 1) * l, :], gtt_ref[j], prow_ref, pcol_ref, l) for j in range(cb)]

    def head_cols(j, group, h):
        return qkv_ref[j * l:(j + 1) * l, group * REC_W + h * HEAD_DIM:group * REC_W + (h + 1) * HEAD_DIM]

    q = {p: head_cols(p[0], 0, p[1]) for p in ps}
    k = {p: head_cols(p[0], 1, p[1]) for p in ps}
    v = {p: head_cols(p[0], 2, p[1]) for p in ps}
    kt = {p: kt_ref[p[0], p[1] * HEAD_DIM:(p[1] + 1) * HEAD_DIM, :] for p in ps}
    beta = {p: gates[p[0]][0][:, GATE_BETA + p[1]:GATE_BETA + p[1] + 1] for p in ps}
    g_col = {p: gates[p[0]][1][:, GATE_G + p[1]:GATE_G + p[1] + 1] for p in ps}
    g_row = {p: gates[p[0]][3][GATE_G + p[1]:GATE_G + p[1] + 1, :] for p in ps}
    g_last = {p: g_col[p][l - 1:l, :] for p in ps}
    decay = {p: jnp.where(tri, jnp.exp(jnp.where(tri, g_col[p] - g_row[p], 0.0)), 0.0) for p in ps}
    kb = {p: k[p] * beta[p] for p in ps}
    e_g = {p: jnp.exp(g_col[p]) for p in ps}
    kq = {p: _dot(_stack(kb[p], q[p]), kt[p]) for p in ps}
    x = {p: -jnp.where(stri, kq[p][:l] * decay[p], 0.0) for p in ps}
    tinv = {p: eye + x[p] for p in ps}
    if n_double > 0:
        x = {p: _dot(x[p], x[p]) for p in ps}
        for _ in range(n_double - 1):
            tx = {p: _dot(_stack(tinv[p], x[p]), x[p]) for p in ps}
            tinv = {p: tinv[p] + tx[p][:l] for p in ps}
            x = {p: tx[p][l:] for p in ps}
        tinv = {p: tinv[p] + _dot(tinv[p], x[p]) for p in ps}
    sol_v = {p: _dot(tinv[p], v[p] * beta[p]) for p in ps}
    sol_k = {p: _dot(tinv[p], kb[p] * e_g[p]) for p in ps}
    lhs_s = {p: _stack(q[p] * e_g[p], sol_k[p]) for p in ps}
    lhs_u = {p: _stack(kt[p] * jnp.exp(g_last[p] - g_row[p]), kq[p][l:] * decay[p]) for p in ps}
    e_last = {p: jnp.exp(g_last[p]) for p in ps}
    s = [s_scr[h] for h in hs]
    for j in range(cb):
        t = [_dot(lhs_s[(j, h)], s[h]) for h in hs]
        u = [sol_v[(j, h)] - t[h][l:] for h in hs]
        w = [_dot(lhs_u[(j, h)], u[h]) for h in hs]
        s = [e_last[(j, h)] * s[h] + w[h][:HEAD_DIM] for h in hs]
        for h in hs:
            o_ref[j * l:(j + 1) * l, h * HEAD_DIM:(h + 1) * HEAD_DIM] = t[h][:l] + w[h][HEAD_DIM:]
    for h in hs:
        s_scr[h] = s[h]

    @pl.when(gi == ng - 1)
    def _():
        s_out_ref[0] = s_scr[...]


def _gdn_scan(qkv, k_t, proj, gate_t, prow, pcol, s0, b, t, l, cb):
    ng = t // (l * cb)
    w3 = 3 * REC_W
    return pl.pallas_call(
        functools.partial(_gdn_scan_kernel, l=l, cb=cb, ng=ng),
        grid=(b, ng),
        in_specs=[pl.BlockSpec((cb * l, w3), lambda bi, i: (bi * ng + i, 0)),
                  pl.BlockSpec((cb, REC_W, l), lambda bi, i: (bi * ng + i, 0, 0)),
                  pl.BlockSpec((cb * l, LANES), lambda bi, i: (bi * ng + i, COL_GATE // LANES)),
                  pl.BlockSpec((cb, LANES, l), lambda bi, i: (bi * ng + i, 0, 0)),
                  pl.BlockSpec((SUBLANES, LANES), lambda bi, i: (0, 0)),
                  pl.BlockSpec((LANES, SUBLANES), lambda bi, i: (0, 0)),
                  pl.BlockSpec((1, GDN_HEADS, HEAD_DIM, HEAD_DIM), lambda bi, i: (bi, 0, 0, 0))],
        out_specs=[pl.BlockSpec((cb * l, REC_W), lambda bi, i: (bi * ng + i, 0)),
                   pl.BlockSpec((1, GDN_HEADS, HEAD_DIM, HEAD_DIM), lambda bi, i: (bi, 0, 0, 0))],
        out_shape=[jax.ShapeDtypeStruct((b * t, REC_W), F32),
                   jax.ShapeDtypeStruct((b, GDN_HEADS, HEAD_DIM, HEAD_DIM), F32)],
        scratch_shapes=[pltpu.VMEM((GDN_HEADS, HEAD_DIM, HEAD_DIM), F32)],
        compiler_params=_cparams("parallel", "arbitrary"),
        name="gdn_scan",
    )(qkv, k_t, proj, gate_t, prow, pcol, s0)


def _gdn_scan_pair_kernel(qkv_ref, kt_ref, gt_ref, gtt_ref, prow_ref, pcol_ref, s0_ref, o_ref, s_out_ref, s_scr,
                          *, l, cb, ng):
    gi = pl.program_id(1)
    npair = GDN_HEADS // 2
    row128 = lax.broadcasted_iota(jnp.int32, (LANES, LANES), 0)
    col128 = lax.broadcasted_iota(jnp.int32, (LANES, LANES), 1)
    blockmask = (row128 < HEAD_DIM) == (col128 < HEAD_DIM)

    def bd(a):
        return jnp.where(blockmask, _stack(a, a), 0.0)

    @pl.when(gi == 0)
    def _():
        for p in range(npair):
            s_scr[p] = bd(jnp.concatenate([s0_ref[0, 2 * p], s0_ref[0, 2 * p + 1]], axis=1))

    trow = lax.broadcasted_iota(jnp.int32, (l, LANES), 0)
    tcol = lax.broadcasted_iota(jnp.int32, (l, LANES), 1) % HEAD_DIM
    left = lax.broadcasted_iota(jnp.int32, (l, LANES), 1) < HEAD_DIM
    tri = trow >= tcol
    stri = trow > tcol
    eye = _ones_where(trow == tcol, F32)
    krow_left = lax.broadcasted_iota(jnp.int32, (LANES, l), 0) < HEAD_DIM
    n_double = max(int(np.ceil(np.log2(l))) - 1, 0)
    ps = [(j, p) for j in range(cb) for p in range(npair)]
    gates = [_gates(gt_ref[j * l:(j + 1) * l, :], gtt_ref[j], prow_ref, pcol_ref, l) for j in range(cb)]

    def pair_cols(j, group, p):
        c0 = group * REC_W + p * LANES
        return qkv_ref[j * l:(j + 1) * l, c0:c0 + LANES]

    def col_pair(a, lane0, p):
        return jnp.where(left, a[:, lane0 + 2 * p:lane0 + 2 * p + 1], a[:, lane0 + 2 * p + 1:lane0 + 2 * p + 2])

    q = {x: pair_cols(x[0], 0, x[1]) for x in ps}
    k = {x: pair_cols(x[0], 1, x[1]) for x in ps}
    v = {x: pair_cols(x[0], 2, x[1]) for x in ps}
    ktp = {x: kt_ref[x[0], x[1] * LANES:(x[1] + 1) * LANES, :] for x in ps}
    beta = {x: col_pair(gates[x[0]][0], GATE_BETA, x[1]) for x in ps}
    g_col = {x: col_pair(gates[x[0]][1], GATE_G, x[1]) for x in ps}
    g_row_a = {x: gates[x[0]][3][GATE_G + 2 * x[1]:GATE_G + 2 * x[1] + 1, :] for x in ps}
    g_row_b = {x: gates[x[0]][3][GATE_G + 2 * x[1] + 1:GATE_G + 2 * x[1] + 2, :] for x in ps}
    g_row = {x: jnp.concatenate([g_row_a[x], g_row_b[x]], axis=1) for x in ps}
    g_last = {x: g_col[x][l - 1:l, :] for x in ps}
    decay = {x: jnp.where(tri, jnp.exp(jnp.where(tri, g_col[x] - g_row[x], 0.0)), 0.0) for x in ps}
    kb = {x: k[x] * beta[x] for x in ps}
    e_g = {x: jnp.exp(g_col[x]) for x in ps}
    bd_kt = {x: jnp.concatenate([jnp.where(krow_left, ktp[x], 0.0), jnp.where(krow_left, 0.0, ktp[x])], axis=1)
             for x in ps}
    kq = {x: _dot(_stack(kb[x], q[x]), bd_kt[x]) for x in ps}
    y = {x: -jnp.where(stri, kq[x][:l] * decay[x], 0.0) for x in ps}
    tinv = {x: eye + y[x] for x in ps}
    if n_double > 0:
        y = {x: _dot(y[x], bd(y[x])) for x in ps}
        for _ in range(n_double - 1):
            ty = {x: _dot(_stack(tinv[x], y[x]), bd(y[x])) for x in ps}
            tinv = {x: tinv[x] + ty[x][:l] for x in ps}
            y = {x: ty[x][l:] for x in ps}
        tinv = {x: tinv[x] + _dot(tinv[x], bd(y[x])) for x in ps}
    sol = {x: _dot(tinv[x], jnp.concatenate([bd(v[x] * beta[x]), bd(kb[x] * e_g[x])], axis=1)) for x in ps}
    lhs_s = {x: _stack(q[x] * e_g[x], sol[x][:, LANES:]) for x in ps}
    g_last_a = {x: g_last[x][:, 0:1] for x in ps}
    g_last_b = {x: g_last[x][:, HEAD_DIM:HEAD_DIM + 1] for x in ps}
    ktd = {x: ktp[x] * jnp.where(krow_left, jnp.exp(g_last_a[x] - g_row_a[x]), jnp.exp(g_last_b[x] - g_row_b[x]))
           for x in ps}
    attn = {x: kq[x][l:] * decay[x] for x in ps}
    e_last = {x: jnp.exp(g_last[x]) for x in ps}
    s = [s_scr[p] for p in range(npair)]
    for j in range(cb):
        t = [_dot(lhs_s[(j, p)], s[p]) for p in range(npair)]
        u = [sol[(j, p)][:, :LANES] - t[p][l:] for p in range(npair)]
        du = [_dot(ktd[(j, p)], u[p]) for p in range(npair)]
        ou = [_dot(attn[(j, p)], bd(u[p])) for p in range(npair)]
        s = [e_last[(j, p)] * s[p] + jnp.where(blockmask, du[p], 0.0) for p in range(npair)]
        for p in range(npair):
            o_ref[j * l:(j + 1) * l, p * LANES:(p + 1) * LANES] = t[p][:l] + ou[p]
    for p in range(npair):
        s_scr[p] = s[p]

    @pl.when(gi == ng - 1)
    def _():
        for p in range(npair):
            s_out_ref[0, 2 * p] = s_scr[p, 0:HEAD_DIM, 0:HEAD_DIM]
            s_out_ref[0, 2 * p + 1] = s_scr[p, HEAD_DIM:LANES, HEAD_DIM:LANES]


def _gdn_scan_pair(qkv, k_t, proj, gate_t, prow, pcol, s0, b, t, l, cb):
    ng = t // (l * cb)
    w3 = 3 * REC_W
    return pl.pallas_call(
        functools.partial(_gdn_scan_pair_kernel, l=l, cb=cb, ng=ng),
        grid=(b, ng),
        in_specs=[pl.BlockSpec((cb * l, w3), lambda bi, i: (bi * ng + i, 0)),
                  pl.BlockSpec((cb, REC_W, l), lambda bi, i: (bi * ng + i, 0, 0)),
                  pl.BlockSpec((cb * l, LANES), lambda bi, i: (bi * ng + i, COL_GATE // LANES)),
                  pl.BlockSpec((cb, LANES, l), lambda bi, i: (bi * ng + i, 0, 0)),
                  pl.BlockSpec((SUBLANES, LANES), lambda bi, i: (0, 0)),
                  pl.BlockSpec((LANES, SUBLANES), lambda bi, i: (0, 0)),
                  pl.BlockSpec((1, GDN_HEADS, HEAD_DIM, HEAD_DIM), lambda bi, i: (bi, 0, 0, 0))],
        out_specs=[pl.BlockSpec((cb * l, REC_W), lambda bi, i: (bi * ng + i, 0)),
                   pl.BlockSpec((1, GDN_HEADS, HEAD_DIM, HEAD_DIM), lambda bi, i: (bi, 0, 0, 0))],
        out_shape=[jax.ShapeDtypeStruct((b * t, REC_W), F32),
                   jax.ShapeDtypeStruct((b, GDN_HEADS, HEAD_DIM, HEAD_DIM), F32)],
        scratch_shapes=[pltpu.VMEM((GDN_HEADS // 2, LANES, LANES), F32)],
        compiler_params=_cparams("parallel", "arbitrary"),
        name="gdn_scan_pair",
    )(qkv, k_t, proj, gate_t, prow, pcol, s0)


def _ml_scan_kernel(qkv_ref, kt_ref, gt_ref, gtt_ref, prow_ref, pcol_ref, c0_ref, n0_ref, m0_ref,
                    o_ref, c_out_ref, n_out_ref, m_out_ref, c_scr, n_scr, m_scr, *, l, cb, ng):
    gi = pl.program_id(1)

    @pl.when(gi == 0)
    def _():
        c_scr[...] = c0_ref[0]
        n_scr[...] = n0_ref[0]
        m_scr[...] = m0_ref[0]

    r = lax.broadcasted_iota(jnp.int32, (l, l), 0)
    c = lax.broadcasted_iota(jnp.int32, (l, l), 1)
    tri = r >= c
    lane = lax.broadcasted_iota(jnp.int32, (1, LANES), 1)
    m_all = m_scr[...]
    hs = range(ML_HEADS)
    ps = [(j, h) for j in range(cb) for h in hs]
    kscale = HEAD_DIM ** -0.5
    gates = [_gates(gt_ref[j * l:(j + 1) * l, :], gtt_ref[j], prow_ref, pcol_ref, l) for j in range(cb)]

    def head_cols(j, group, h):
        return qkv_ref[j * l:(j + 1) * l, group * REC_W + h * HEAD_DIM:group * REC_W + (h + 1) * HEAD_DIM]

    q = {p: head_cols(p[0], 0, p[1]) for p in ps}
    k = {p: head_cols(p[0], 1, p[1]) * kscale for p in ps}
    v = {p: head_cols(p[0], 2, p[1]) for p in ps}
    kt = {p: kt_ref[p[0], p[1] * HEAD_DIM:(p[1] + 1) * HEAD_DIM, :] * kscale for p in ps}
    ig_col = {p: gates[p[0]][0][:, GATE_I + p[1]:GATE_I + p[1] + 1] for p in ps}
    ig_row = {p: gates[p[0]][2][GATE_I + p[1]:GATE_I + p[1] + 1, :] for p in ps}
    f_col = {p: gates[p[0]][1][:, GATE_F + p[1]:GATE_F + p[1] + 1] for p in ps}
    f_row = {p: gates[p[0]][3][GATE_F + p[1]:GATE_F + p[1] + 1, :] for p in ps}
    f_last = {p: f_col[p][l - 1:l, :] for p in ps}
    qk = {p: _dot(q[p], kt[p]) for p in ps}
    d = {p: jnp.where(tri, f_col[p] - f_row[p] + ig_row[p], -jnp.inf) for p in ps}
    d_max = {p: jnp.max(d[p], axis=1, keepdims=True) for p in ps}
    m_prev, m_t = {}, {}
    m_run = [m_all[:, h:h + 1] for h in hs]
    for j in range(cb):
        for h in hs:
            m_prev[(j, h)] = m_run[h]
            m_t[(j, h)] = jnp.maximum(f_col[(j, h)] + m_run[h], d_max[(j, h)])
            m_run[h] = m_t[(j, h)][l - 1:l, :]
    m_new = {p: m_t[p][l - 1:l, :] for p in ps}
    w = {p: jnp.exp(d[p] - m_t[p]) * qk[p] for p in ps}
    c_inter = {p: jnp.exp(f_col[p] + m_prev[p] - m_t[p]) for p in ps}
    wv = {p: _dot(w[p], v[p]) for p in ps}
    w_sum = {p: jnp.sum(w[p], axis=1, keepdims=True) for p in ps}
    e_m = {p: jnp.exp(-m_t[p]) for p in ps}
    w_end_row = {p: jnp.exp(f_last[p] - f_row[p] + ig_row[p] - m_new[p]) for p in ps}
    w_end_col = {p: jnp.exp(f_last[p] - f_col[p] + ig_col[p] - m_new[p]) for p in ps}
    c_prev = {p: jnp.exp(f_last[p] + m_prev[p] - m_new[p]) for p in ps}
    dc = {p: _dot(kt[p] * w_end_row[p], v[p]) for p in ps}
    dn = {p: jnp.sum(w_end_col[p] * k[p], axis=0, keepdims=True) for p in ps}
    c_h = [c_scr[h] for h in hs]
    n_h = [n_scr[h:h + 1, :] for h in hs]
    for j in range(cb):
        for h in hs:
            p = (j, h)
            num = c_inter[p] * _dot(q[p], c_h[h]) + wv[p]
            den = c_inter[p] * jnp.sum(q[p] * n_h[h], axis=1, keepdims=True) + w_sum[p]
            o_ref[j * l:(j + 1) * l, h * HEAD_DIM:(h + 1) * HEAD_DIM] = num / jnp.maximum(jnp.abs(den), e_m[p])
            c_h[h] = c_prev[p] * c_h[h] + dc[p]
            n_h[h] = c_prev[p] * n_h[h] + dn[p]
    m_next = m_all
    for h in hs:
        c_scr[h] = c_h[h]
        n_scr[h:h + 1, :] = n_h[h]
        m_next = jnp.where(lane == h, m_run[h], m_next)
    m_scr[...] = m_next

    @pl.when(gi == ng - 1)
    def _():
        c_out_ref[0] = c_scr[...]
        n_out_ref[0] = n_scr[...]
        m_out_ref[0] = m_scr[...]


def _ml_scan(proj, k_t, gate_t, prow, pcol, c0, n0, m0, b, t, l, cb):
    ng = t // (l * cb)
    w3 = 3 * REC_W
    st = lambda bi, i: (bi, 0, 0, 0)
    st3 = lambda bi, i: (bi, 0, 0)
    return pl.pallas_call(
        functools.partial(_ml_scan_kernel, l=l, cb=cb, ng=ng),
        grid=(b, ng),
        in_specs=[pl.BlockSpec((cb * l, w3), lambda bi, i: (bi * ng + i, COL_MQKV // w3)),
                  pl.BlockSpec((cb, REC_W, l), lambda bi, i: (bi * ng + i, 0, 0)),
                  pl.BlockSpec((cb * l, LANES), lambda bi, i: (bi * ng + i, COL_GATE // LANES)),
                  pl.BlockSpec((cb, LANES, l), lambda bi, i: (bi * ng + i, 0, 0)),
                  pl.BlockSpec((SUBLANES, LANES), lambda bi, i: (0, 0)),
                  pl.BlockSpec((LANES, SUBLANES), lambda bi, i: (0, 0)),
                  pl.BlockSpec((1, ML_HEADS, HEAD_DIM, HEAD_DIM), st),
                  pl.BlockSpec((1, SUBLANES, HEAD_DIM), st3),
                  pl.BlockSpec((1, 1, LANES), st3)],
        out_specs=[pl.BlockSpec((cb * l, REC_W), lambda bi, i: (bi * ng + i, 0)),
                   pl.BlockSpec((1, ML_HEADS, HEAD_DIM, HEAD_DIM), st),
                   pl.BlockSpec((1, SUBLANES, HEAD_DIM), st3),
                   pl.BlockSpec((1, 1, LANES), st3)],
        out_shape=[jax.ShapeDtypeStruct((b * t, REC_W), F32),
                   jax.ShapeDtypeStruct((b, ML_HEADS, HEAD_DIM, HEAD_DIM), F32),
                   jax.ShapeDtypeStruct((b, SUBLANES, HEAD_DIM), F32),
                   jax.ShapeDtypeStruct((b, 1, LANES), F32)],
        scratch_shapes=[pltpu.VMEM((ML_HEADS, HEAD_DIM, HEAD_DIM), F32),
                        pltpu.VMEM((SUBLANES, HEAD_DIM), F32),
                        pltpu.VMEM((1, LANES), F32)],
        compiler_params=_cparams("parallel", "arbitrary"),
        name="ml_scan",
    )(proj, k_t, proj, gate_t, prow, pcol, c0, n0, m0)


def _ml_scan_t_kernel(qkv_ref, qkvt_ref, gt_ref, gtt_ref, prow_ref, pcol_ref, c0_ref, n0_ref, m0_ref,
                      o_ref, c_out_ref, n_out_ref, m_out_ref, c_scr, n_scr, m_scr, *, l, cb, ng):
    gi = pl.program_id(1)

    @pl.when(gi == 0)
    def _():
        c_scr[...] = c0_ref[0]
        n_scr[...] = n0_ref[0]
        m_scr[...] = m0_ref[0]

    r = lax.broadcasted_iota(jnp.int32, (l, l), 0)
    c = lax.broadcasted_iota(jnp.int32, (l, l), 1)
    tri_t = r <= c
    lane = lax.broadcasted_iota(jnp.int32, (1, LANES), 1)
    m_all = m_scr[...]
    hs = range(ML_HEADS)
    ps = [(j, h) for j in range(cb) for h in hs]
    kscale = HEAD_DIM ** -0.5
    zeros_t = jnp.zeros((HEAD_DIM, l), F32)
    gates = [_gates(gt_ref[j * l:(j + 1) * l, :], gtt_ref[j], prow_ref, pcol_ref, l) for j in range(cb)]

    def t_rows(j, group, h):
        return qkvt_ref[j, group * REC_W + h * HEAD_DIM:group * REC_W + (h + 1) * HEAD_DIM, :]

    qt = {p: t_rows(p[0], 0, p[1]) for p in ps}
    qt_pad = {p: (_stack(qt[p], zeros_t) if p[1] % 2 == 0 else _stack(zeros_t, qt[p])) for p in ps}
    vt = {p: t_rows(p[0], 2, p[1]) for p in ps}
    k2 = {p: qkv_ref[p[0] * l:(p[0] + 1) * l, REC_W + (p[1] // 2) * LANES:REC_W + (p[1] // 2 + 1) * LANES] * kscale
          for p in ps}
    ig_col = {p: gates[p[0]][0][:, GATE_I + p[1]:GATE_I + p[1] + 1] for p in ps}
    ig_row = {p: gates[p[0]][2][GATE_I + p[1]:GATE_I + p[1] + 1, :] for p in ps}
    f_col = {p: gates[p[0]][1][:, GATE_F + p[1]:GATE_F + p[1] + 1] for p in ps}
    f_row = {p: gates[p[0]][3][GATE_F + p[1]:GATE_F + p[1] + 1, :] for p in ps}
    f_last = {p: f_row[p][:, l - 1:l] for p in ps}
    kq = {p: _dot(k2[p], qt_pad[p]) for p in ps}
    d = {p: jnp.where(tri_t, f_row[p] + (ig_col[p] - f_col[p]), -jnp.inf) for p in ps}
    d_max = {p: jnp.max(d[p], axis=0, keepdims=True) for p in ps}
    m_prev, m_t = {}, {}
    m_run = [m_all[:, h:h + 1] for h in hs]
    for j in range(cb):
        for h in hs:
            m_prev[(j, h)] = m_run[h]
            m_t[(j, h)] = jnp.maximum(f_row[(j, h)] + m_run[h], d_max[(j, h)])
            m_run[h] = m_t[(j, h)][:, l - 1:l]
    m_new = {p: m_t[p][:, l - 1:l] for p in ps}
    w = {p: jnp.exp(d[p] - m_t[p]) * kq[p] for p in ps}
    c_inter = {p: jnp.exp(f_row[p] + m_prev[p] - m_t[p]) for p in ps}
    vw = {p: _dot(vt[p], w[p]) for p in ps}
    w_sum = {p: jnp.sum(w[p], axis=0, keepdims=True) for p in ps}
    e_m = {p: jnp.exp(-m_t[p]) for p in ps}
    w_end_col = {p: jnp.exp(f_last[p] - f_col[p] + ig_col[p] - m_new[p]) for p in ps}
    c_prev = {p: jnp.exp(f_last[p] + m_prev[p] - m_new[p]) for p in ps}
    wk = {p: w_end_col[p] * k2[p] for p in ps}
    dc = {p: _dot(vt[p], wk[p]) for p in ps}
    dn = {p: jnp.sum(wk[p], axis=0, keepdims=True) for p in ps}
    c_h = [c_scr[h] for h in hs]
    n_h = [n_scr[h:h + 1, :] for h in hs]
    ht = {}
    for j in range(cb):
        for h in hs:
            p = (j, h)
            num = c_inter[p] * _dot(c_h[h], qt_pad[p]) + vw[p]
            den = c_inter[p] * _dot(n_h[h], qt_pad[p]) + w_sum[p]
            ht[p] = num / jnp.maximum(jnp.abs(den), e_m[p])
            c_h[h] = c_prev[p] * c_h[h] + dc[p]
            n_h[h] = c_prev[p] * n_h[h] + dn[p]
    for i in range(ML_HEADS // 2):
        pair_t = _stack(jnp.concatenate([ht[(j, 2 * i)] for j in range(cb)], axis=1),
                        jnp.concatenate([ht[(j, 2 * i + 1)] for j in range(cb)], axis=1))
        o_ref[:, i * LANES:(i + 1) * LANES] = pair_t.T
    m_next = m_all
    for h in hs:
        c_scr[h] = c_h[h]
        n_scr[h:h + 1, :] = n_h[h]
        m_next = jnp.where(lane == h, m_run[h], m_next)
    m_scr[...] = m_next

    @pl.when(gi == ng - 1)
    def _():
        c_out_ref[0] = c_scr[...]
        n_out_ref[0] = n_scr[...]
        m_out_ref[0] = m_scr[...]


def _ml_scan_t(proj, qkv_t, gate_t, prow, pcol, c0, n0, m0, b, t, l, cb):
    ng = t // (l * cb)
    w3 = 3 * REC_W
    st = lambda bi, i: (bi, 0, 0, 0)
    st3 = lambda bi, i: (bi, 0, 0)
    return pl.pallas_call(
        functools.partial(_ml_scan_t_kernel, l=l, cb=cb, ng=ng),
        grid=(b, ng),
        in_specs=[pl.BlockSpec((cb * l, w3), lambda bi, i: (bi * ng + i, COL_MQKV // w3)),
                  pl.BlockSpec((cb, w3, l), lambda bi, i: (bi * ng + i, 0, 0)),
                  pl.BlockSpec((cb * l, LANES), lambda bi, i: (bi * ng + i, COL_GATE // LANES)),
                  pl.BlockSpec((cb, LANES, l), lambda bi, i: (bi * ng + i, 0, 0)),
                  pl.BlockSpec((SUBLANES, LANES), lambda bi, i: (0, 0)),
                  pl.BlockSpec((LANES, SUBLANES), lambda bi, i: (0, 0)),
                  pl.BlockSpec((1, ML_HEADS, HEAD_DIM, LANES), st),
                  pl.BlockSpec((1, SUBLANES, LANES), st3),
                  pl.BlockSpec((1, 1, LANES), st3)],
        out_specs=[pl.BlockSpec((cb * l, REC_W), lambda bi, i: (bi * ng + i, 0)),
                   pl.BlockSpec((1, ML_HEADS, HEAD_DIM, LANES), st),
                   pl.BlockSpec((1, SUBLANES, LANES), st3),
                   pl.BlockSpec((1, 1, LANES), st3)],
        out_shape=[jax.ShapeDtypeStruct((b * t, REC_W), F32),
                   jax.ShapeDtypeStruct((b, ML_HEADS, HEAD_DIM, LANES), F32),
                   jax.ShapeDtypeStruct((b, SUBLANES, LANES), F32),
                   jax.ShapeDtypeStruct((b, 1, LANES), F32)],
        scratch_shapes=[pltpu.VMEM((ML_HEADS, HEAD_DIM, LANES), F32),
                        pltpu.VMEM((SUBLANES, LANES), F32),
                        pltpu.VMEM((1, LANES), F32)],
        compiler_params=_cparams("parallel", "arbitrary"),
        name="ml_scan_t",
    )(proj, qkv_t, proj, gate_t, prow, pcol, c0, n0, m0)


def _post_kernel(x_ref, osb_ref, og_ref, hm_ref, gz_ref, mo_ref, w_ref, e_ref, gnw_ref, mnw_ref,
                 lg_ref, lb_ref, o_ref, ob_ref):
    e = e_ref[...]
    og = og_ref[...]
    hm = hm_ref[...]
    og_ms = _dot_exact_rhs(og * og, e, NORM_SUM_PARTS) * (1.0 / HEAD_DIM)
    hm_ms = _dot_exact_rhs(hm * hm, e, NORM_SUM_PARTS) * (1.0 / HEAD_DIM)
    gz = gz_ref[...]
    o_g = og * lax.rsqrt(og_ms + NORM_EPS) * gnw_ref[...] * (gz * _sigmoid(gz))
    o_m = _sigmoid(mo_ref[...]) * (hm * lax.rsqrt(hm_ms + NORM_EPS) * mnw_ref[...])
    mixed = (_dot(osb_ref[...], w_ref[0:SB_W, :])
             + _dot(o_g, w_ref[SB_W:SB_W + REC_W, :])
             + _dot(o_m, w_ref[SB_W + REC_W:SB_W + 2 * REC_W, :]))
    y = _layer_norm(DEEPNORM_ALPHA * x_ref[...] + mixed, lg_ref[...], lb_ref[...])
    o_ref[...] = y
    ob_ref[...] = y.astype(BF16)


def _post(x, osb, og, hm, proj, w_out, e_heads, gnw, mnw, lg, lb, tm):
    n, d = x.shape
    row = lambda i: (i, 0)
    fixed = lambda i: (0, 0)
    return pl.pallas_call(
        _post_kernel,
        grid=(n // tm,),
        in_specs=[pl.BlockSpec((tm, d), row),
                  pl.BlockSpec((tm, SB_W), row),
                  pl.BlockSpec((tm, REC_W), row),
                  pl.BlockSpec((tm, REC_W), row),
                  pl.BlockSpec((tm, REC_W), lambda i: (i, COL_GZ // REC_W)),
                  pl.BlockSpec((tm, REC_W), lambda i: (i, COL_MO // REC_W)),
                  pl.BlockSpec(w_out.shape, fixed),
                  pl.BlockSpec((REC_W, REC_W), fixed),
                  pl.BlockSpec((1, REC_W), fixed),
                  pl.BlockSpec((1, REC_W), fixed),
                  pl.BlockSpec((1, d), fixed),
                  pl.BlockSpec((1, d), fixed)],
        out_specs=[pl.BlockSpec((tm, d), row), pl.BlockSpec((tm, d), row)],
        out_shape=[jax.ShapeDtypeStruct((n, d), F32), jax.ShapeDtypeStruct((n, d), BF16)],
        compiler_params=_cparams("parallel"),
        name="post",
    )(x, osb, og, hm, proj, proj, w_out, e_heads, gnw, mnw, lg, lb)


def _memattn_kernel(x_ref, xb_ref, wq_ref, wo_ref, mk_ref, mv_ref, lg_ref, lb_ref, o_ref, ob_ref):
    d = x_ref.shape[1]
    hd = d // MEM_HEADS
    q = jnp.dot(xb_ref[...], wq_ref[...], preferred_element_type=F32)
    out = None
    for h in range(MEM_HEADS):
        sl = slice(h * hd, (h + 1) * hd)
        s = _dot_nt(q[:, sl], mk_ref[0, :, sl]) * (hd ** -0.5)
        p = jnp.exp(s - jnp.max(s, axis=1, keepdims=True))
        o_h = _dot(p, mv_ref[0, :, sl]) / jnp.sum(p, axis=1, keepdims=True)
        t = _dot(o_h, wo_ref[sl, :])
        out = t if out is None else out + t
    y = _layer_norm(DEEPNORM_ALPHA * x_ref[...] + out, lg_ref[...], lb_ref[...])
    o_ref[...] = y
    ob_ref[...] = y.astype(BF16)


def _memattn(x, xb, w_cq, w_co, mk, mv, lg, lb, b, t, tm):
    n, d = x.shape
    nt = t // tm
    nm = mk.shape[1]
    row = lambda bi, i: (bi * nt + i, 0)
    fixed = lambda bi, i: (0, 0)
    return pl.pallas_call(
        _memattn_kernel,
        grid=(b, nt),
        in_specs=[pl.BlockSpec((tm, d), row),
                  pl.BlockSpec((tm, d), row),
                  pl.BlockSpec((d, d), fixed),
                  pl.BlockSpec((d, d), fixed),
                  pl.BlockSpec((1, nm, d), lambda bi, i: (bi, 0, 0)),
                  pl.BlockSpec((1, nm, d), lambda bi, i: (bi, 0, 0)),
                  pl.BlockSpec((1, d), fixed),
                  pl.BlockSpec((1, d), fixed)],
        out_specs=[pl.BlockSpec((tm, d), row), pl.BlockSpec((tm, d), row)],
        out_shape=[jax.ShapeDtypeStruct((n, d), F32), jax.ShapeDtypeStruct((n, d), BF16)],
        compiler_params=_cparams("parallel", "parallel"),
        name="memattn",
    )(x, xb, w_cq, w_co, mk, mv, lg, lb)


def _ffn_kernel(x_ref, xb_ref, wu_ref, wd_ref, lg_ref, lb_ref, o_ref, ob_ref, *, tf):
    xb = xb_ref[...]
    dff = wd_ref.shape[0]
    acc = None
    for j in range(dff // tf):
        gate = jnp.dot(xb, wu_ref[:, j * tf:(j + 1) * tf], preferred_element_type=F32)
        up = jnp.dot(xb, wu_ref[:, dff + j * tf:dff + (j + 1) * tf], preferred_element_type=F32)
        t = _dot(gate * _sigmoid(gate) * up, wd_ref[j * tf:(j + 1) * tf, :])
        acc = t if acc is None else acc + t
    y = _layer_norm(DEEPNORM_ALPHA * x_ref[...] + acc, lg_ref[...], lb_ref[...])
    o_ref[...] = y
    ob_ref[...] = y.astype(BF16)


def _ffn(x, xb, w_up, w_down, lg, lb, tm, tf):
    n, d = x.shape
    row = lambda i: (i, 0)
    fixed = lambda i: (0, 0)
    return pl.pallas_call(
        functools.partial(_ffn_kernel, tf=tf),
        grid=(n // tm,),
        in_specs=[pl.BlockSpec((tm, d), row),
                  pl.BlockSpec((tm, d), row),
                  pl.BlockSpec(w_up.shape, fixed, pipeline_mode=pl.Buffered(1)),
                  pl.BlockSpec(w_down.shape, fixed, pipeline_mode=pl.Buffered(1)),
                  pl.BlockSpec((1, d), fixed),
                  pl.BlockSpec((1, d), fixed)],
        out_specs=[pl.BlockSpec((tm, d), row), pl.BlockSpec((tm, d), row)],
        out_shape=[jax.ShapeDtypeStruct((n, d), F32), jax.ShapeDtypeStruct((n, d), BF16)],
        compiler_params=_cparams("parallel"),
        name="ffn",
    )(x, xb, w_up, w_down, lg, lb)


def _pack_layer(p):
    w_in = p['w_in']
    d = w_in.shape[0]
    o_sb, o_gqkv, o_gz = 0, 3 * SB_W, 3 * SB_W + 3 * REC_W
    o_gb = o_gz + REC_W
    o_mqkv = o_gb + 2 * GDN_HEADS
    o_mo = o_mqkv + 3 * REC_W
    o_mi = o_mo + REC_W
    gate_cols = jnp.concatenate([w_in[:, o_gb:o_gb + 2 * GDN_HEADS], w_in[:, o_mi:o_mi + 2 * ML_HEADS]], axis=1)
    used = COL_GATE + gate_cols.shape[1]
    w_al = jnp.concatenate([w_in[:, o_gqkv:o_gqkv + 3 * REC_W], w_in[:, o_mqkv:o_mqkv + 3 * REC_W],
                            w_in[:, o_gz:o_gz + REC_W], w_in[:, o_mo:o_mo + REC_W],
                            w_in[:, o_sb:o_sb + 3 * SB_W], gate_cols,
                            jnp.zeros((d, PROJ_W - used), w_in.dtype)], axis=1).astype(BF16)
    prow = jnp.zeros((SUBLANES, LANES), F32)
    prow = prow.at[0, GATE_G:GATE_G + GDN_HEADS].set(p['gdn_dt_bias'])
    prow = prow.at[0, GATE_I:GATE_I + ML_HEADS].set(p['mlstm_i_bias'])
    prow = prow.at[0, GATE_F:GATE_F + ML_HEADS].set(p['mlstm_f_bias'])
    prow = prow.at[1, GATE_G:GATE_G + GDN_HEADS].set(p['gdn_A_log'])
    conv_w8 = jnp.zeros((SUBLANES, 3 * REC_W), F32).at[0:CONV_W].set(p['gdn_conv_w'])
    return dict(
        w_al=w_al, prow=prow, pcol=prow.T, conv_w8=conv_w8,
        gnw=jnp.tile(p['gdn_norm_w'], GDN_HEADS)[None, :], mnw=p['mlstm_norm_w'][None, :],
        w_out=p['w_out'].astype(BF16), w_cq=p['w_cq'].astype(BF16), w_co=p['w_co'].astype(BF16),
        w_up=p['w_up'].astype(BF16), w_down=p['w_down'].astype(BF16),
        ln1=(p['ln1_g'][None, :], p['ln1_b'][None, :]), ln2=(p['ln2_g'][None, :], p['ln2_b'][None, :]),
        ln3=(p['ln3_g'][None, :], p['ln3_b'][None, :]))


def _head_consts():
    hid = np.arange(REC_W) // HEAD_DIM
    return jnp.asarray(hid[:, None] == hid[None, :], BF16)


def _sb_layout(k, v, tk):
    b, tkv, _ = k.shape
    nkb = tkv // tk
    kt = k.reshape(b, nkb, tk, SB_W).transpose(0, 1, 3, 2)
    vh = v.reshape(b, nkb, tk, SB_HEADS, HEAD_DIM).transpose(0, 3, 1, 2, 4)
    vh = vh.reshape(b * SB_HEADS, nkb, tk, HEAD_DIM)
    return kt.astype(BF16), vh.astype(BF16)


def _to_pair_lanes(a):
    z = jnp.zeros_like(a)
    even = (jnp.arange(a.shape[1]) % 2 == 0)[None, :, None, None]
    return jnp.where(even, jnp.concatenate([a, z], axis=-1), jnp.concatenate([z, a], axis=-1))


def _from_pair_lanes(a):
    even = (jnp.arange(a.shape[1]) % 2 == 0)[None, :, None, None]
    return jnp.where(even, a[..., :HEAD_DIM], a[..., HEAD_DIM:])


def _time_on_lanes(a, groups, width):
    return a.reshape(groups, width, a.shape[1]).transpose(0, 2, 1)


def _trunk_layer(x, xb, pk, e_heads, mk, mv, b, t, cfg, sb_past, conv_ctx, gdn_s, ml_c, ml_n, ml_m):
    n, d = x.shape
    l = min(t, CHUNK)
    nc = t // l
    cb = min(cfg['scan_cb'], nc)
    ng = b * nc // cb
    tk = cfg['sb_tk']
    ctx8 = jnp.pad(conv_ctx, ((0, 0), (SUBLANES - (CONV_W - 1), 0), (0, 0)))
    if sb_past is None:
        proj, kt, vh, mk_t, gate_t = _proj(x if xb is None else xb, pk['w_al'], cfg['tm'])
        kt = kt.reshape(b, t // tk, SB_W, tk)
        q_off = 0
        gqkv_act, gk_t = _gdn_pre(proj, ctx8, pk['conv_w8'], e_heads, b, t, cfg['conv_tt'])
    else:
        proj = _matmul(xb, pk['w_al'], cfg['tm'], 1024)
    sk = proj[:, COL_SB + SB_W:COL_SB + 2 * SB_W].reshape(b, t, SB_W)
    sv = proj[:, COL_SB + 2 * SB_W:COL_SB + 3 * SB_W].reshape(b, t, SB_W)
    if sb_past is not None:
        past_k, past_v = sb_past
        q_off = past_k.shape[1]
        kv_len = q_off + t
        kv_pad = -(-kv_len // tk) * tk - kv_len
        kk = jnp.pad(jnp.concatenate([past_k.reshape(b, q_off, SB_W), sk], axis=1), ((0, 0), (0, kv_pad), (0, 0)))
        vv = jnp.pad(jnp.concatenate([past_v.reshape(b, q_off, SB_W), sv], axis=1), ((0, 0), (0, kv_pad), (0, 0)))
        kt, vh = _sb_layout(kk, vv, tk)
        gqkv_act, _ = _gdn_pre(proj, ctx8, pk['conv_w8'], e_heads, b, t, cfg['conv_tt'])
        gate_t = _time_on_lanes(proj[:, COL_GATE:COL_GATE + LANES], b * nc, l)
        gk_t = _time_on_lanes(gqkv_act[:, REC_W:2 * REC_W], b * nc, l)
        mk_t = _time_on_lanes(proj[:, COL_MQKV + REC_W:COL_MQKV + 2 * REC_W], b * nc, l)

    o_sb = _sb_attn(proj, kt, vh, b, t, SB_HEADS, cfg['sb_tq'], q_off)
    gdn_scan = _gdn_scan_pair if l == HEAD_DIM else _gdn_scan
    o_g, gdn_s_new = gdn_scan(gqkv_act, gk_t, proj, gate_t, pk['prow'], pk['pcol'], gdn_s, b, t, l,
                              min(cfg['gdn_cb'], nc))
    m0 = jnp.pad(ml_m, ((0, 0), (0, LANES - ML_HEADS)))[:, None, :]
    if sb_past is None:
        c0 = _to_pair_lanes(jnp.swapaxes(ml_c, -1, -2))
        n0 = jnp.pad(_to_pair_lanes(ml_n[:, :, None, :])[:, :, 0, :], ((0, 0), (0, SUBLANES - ML_HEADS), (0, 0)))
        h_m, c_new, n_new, m_new = _ml_scan_t(proj, mk_t, gate_t, pk['prow'], pk['pcol'], c0, n0, m0, b, t, l, cb)
        c_new = jnp.swapaxes(_from_pair_lanes(c_new), -1, -2)
        n_new = _from_pair_lanes(n_new[:, :ML_HEADS, None, :])[:, :, 0, :]
    else:
        n0 = jnp.pad(ml_n, ((0, 0), (0, SUBLANES - ML_HEADS), (0, 0)))
        h_m, c_new, n_new, m_new = _ml_scan(proj, mk_t, gate_t, pk['prow'], pk['pcol'], ml_c, n0, m0, b, t, l, cb)
        n_new = n_new[:, :ML_HEADS, :]

    x1, x1b = _post(x, o_sb, o_g, h_m, proj, pk['w_out'], e_heads, pk['gnw'], pk['mnw'], *pk['ln1'], cfg['post_tm'])
    x2, x2b = _memattn(x1, x1b, pk['w_cq'], pk['w_co'], mk, mv, *pk['ln2'], b, t, cfg['mem_tm'])
    x3, x3b = _ffn(x2, x2b, pk['w_up'], pk['w_down'], *pk['ln3'], cfg['tm'], cfg['ffn_tf'])

    new_conv = proj.reshape(b, t, PROJ_W)[:, t - (CONV_W - 1):, COL_GQKV:COL_GQKV + 3 * REC_W]
    return (x3, x3b, sk.reshape(b, t, SB_HEADS, HEAD_DIM), sv.reshape(b, t, SB_HEADS, HEAD_DIM), new_conv,
            gdn_s_new, c_new, n_new, m_new[:, 0, :ML_HEADS])


def kernel(x_prompt, x_sample, cache_sb_k, cache_sb_v, cache_gdn_conv, state_gdn, state_mlstm_C, state_mlstm_n, state_mlstm_m, cache_mem_k, cache_mem_v, mem_prompt, w_in, gdn_conv_w, gdn_A_log, gdn_dt_bias, gdn_norm_w, mlstm_i_bias, mlstm_f_bias, mlstm_norm_w, w_out, ln1_g, ln1_b, w_cq, w_ckv, w_co, ln2_g, ln2_b, w_up, w_down, ln3_g, ln3_b):
    bp, tp, d = x_prompt.shape
    bs, ts, _ = x_sample.shape
    depth = w_in.shape[0]
    n_mem = mem_prompt.shape[1]
    dff = w_down.shape[1]
    cfg_p = dict(tm=512, post_tm=512, sb_tq=512, sb_tk=256, conv_tt=512, mem_tm=512, ffn_tf=dff // 2, scan_cb=4, gdn_cb=8)
    cfg_s = dict(tm=bs * ts, post_tm=bs * ts, sb_tq=ts, sb_tk=256, conv_tt=ts, mem_tm=ts, ffn_tf=dff // 2, scan_cb=1, gdn_cb=1)
    e_heads = _head_consts()

    xp = x_prompt.reshape(bp * tp, d)
    xs = x_sample.reshape(bs * ts, d)
    xpb, xsb = None, xs.astype(BF16)
    mem_b = mem_prompt.reshape(bp * n_mem, d).astype(BF16)
    p_out = [[] for _ in range(9)]
    s_out = [[] for _ in range(7)]
    for li in range(depth):
        p = {'w_in': w_in[li], 'gdn_conv_w': gdn_conv_w[li], 'gdn_A_log': gdn_A_log[li],
             'gdn_dt_bias': gdn_dt_bias[li], 'gdn_norm_w': gdn_norm_w[li], 'mlstm_i_bias': mlstm_i_bias[li],
             'mlstm_f_bias': mlstm_f_bias[li], 'mlstm_norm_w': mlstm_norm_w[li], 'w_out': w_out[li],
             'ln1_g': ln1_g[li], 'ln1_b': ln1_b[li], 'w_cq': w_cq[li], 'w_co': w_co[li],
             'ln2_g': ln2_g[li], 'ln2_b': ln2_b[li], 'w_up': w_up[li], 'w_down': w_down[li],
             'ln3_g': ln3_g[li], 'ln3_b': ln3_b[li]}
        pk = _pack_layer(p)
        mkv = _matmul(mem_b, w_ckv[li].astype(BF16), n_mem, 1024)
        mk_p = mkv[:, :d].reshape(bp, n_mem, d)
        mv_p = mkv[:, d:].reshape(bp, n_mem, d)
        res = _trunk_layer(xp, xpb, pk, e_heads, mk_p.astype(BF16), mv_p.astype(BF16), bp, tp, cfg_p, None,
                           jnp.zeros((bp, CONV_W - 1, 3 * REC_W), F32),
                           jnp.zeros((bp, GDN_HEADS, HEAD_DIM, HEAD_DIM), F32),
                           jnp.zeros((bp, ML_HEADS, HEAD_DIM, HEAD_DIM), F32),
                           jnp.zeros((bp, ML_HEADS, HEAD_DIM), F32),
                           jnp.zeros((bp, ML_HEADS), F32))
        xp, xpb = res[0], res[1]
        hd = d // MEM_HEADS
        for j, a in enumerate(res[2:] + (mk_p.reshape(bp, n_mem, MEM_HEADS, hd), mv_p.reshape(bp, n_mem, MEM_HEADS, hd))):
            p_out[j].append(a)
        res = _trunk_layer(xs, xsb, pk, e_heads,
                           cache_mem_k[li].reshape(bs, n_mem, d).astype(BF16),
                           cache_mem_v[li].reshape(bs, n_mem, d).astype(BF16),
                           bs, ts, cfg_s, (cache_sb_k[li], cache_sb_v[li]), cache_gdn_conv[li],
                           state_gdn[li], state_mlstm_C[li], state_mlstm_n[li], state_mlstm_m[li])
        xs, xsb = res[0], res[1]
        for j, a in enumerate(res[2:]):
            s_out[j].append(a)
    p_st = [jnp.stack(a) for a in p_out]
    s_st = [jnp.stack(a) for a in s_out]
    return (xp.reshape(bp, tp, d), xs.reshape(bs, ts, d), *p_st, *s_st)
```

```python
import functools

import numpy as np
import jax
import jax.numpy as jnp
from jax import lax
from jax.experimental import pallas as pl
from jax.experimental.pallas import tpu as pltpu

F32 = jnp.float32
BF16 = jnp.bfloat16

HEAD_DIM = 64
SB_HEADS = 4
GDN_HEADS = 6
ML_HEADS = 6
REC_W = GDN_HEADS * HEAD_DIM
SB_W = SB_HEADS * HEAD_DIM
CONV_W = 4
CHUNK = 64
MEM_HEADS = 4
LN_EPS = 1e-5
NORM_EPS = 1e-6
DEPTH = 2
DEEPNORM_ALPHA = (2 * DEPTH) ** 0.25
LOG2E = float(np.log2(np.e))
SOFTPLUS2_CLAMP = 120.0
NORM_SUM_PARTS = 1
F32_EXP2_ZERO = 160.0
BOUND_SLACK = 1.01

GROUP = 256
LANES = 128
SUBLANES = 8
VMEM_LIMIT = 56 * 1024 * 1024

PROJ_W = 4096
COL_GQKV = 0
COL_MQKV = 1152
COL_GZ = 2304
COL_MO = 2688
COL_SB = 3072
COL_GATE = 3840
GATE_BETA, GATE_G, GATE_I, GATE_F = 0, 6, 12, 18


def _cparams(*sem):
    return pltpu.CompilerParams(dimension_semantics=sem, vmem_limit_bytes=VMEM_LIMIT)


def _dot(a, b):
    return jnp.dot(a.astype(BF16), b.astype(BF16), preferred_element_type=F32)


def _dot_nt(a, b):
    return lax.dot_general(a.astype(BF16), b.astype(BF16), (((1,), (1,)), ((), ())),
                           preferred_element_type=F32)


def _split(x, parts):
    out = []
    r = x
    for _ in range(parts - 1):
        p = r.astype(BF16)
        out.append(p)
        r = r - p.astype(F32)
    out.append(r.astype(BF16))
    return out


def _dot_exact_lhs(m, x, parts):
    acc = None
    for p in _split(x, parts):
        t = jnp.dot(m, p, preferred_element_type=F32)
        acc = t if acc is None else acc + t
    return acc


def _dot_exact_rhs(x, m, parts):
    acc = None
    for p in _split(x, parts):
        t = jnp.dot(p, m, preferred_element_type=F32)
        acc = t if acc is None else acc + t
    return acc


def _ones_where(mask, dtype):
    return jnp.where(mask, 1.0, 0.0).astype(dtype)


def _sigmoid(x):
    return 1.0 / (1.0 + jnp.exp(-x))


def _layer_norm(y, g, b):
    mu = jnp.mean(y, axis=-1, keepdims=True)
    d = y - mu
    var = jnp.mean(d * d, axis=-1, keepdims=True)
    return d * lax.rsqrt(var + LN_EPS) * g + b


def _mm_kernel(x_ref, w_ref, o_ref):
    o_ref[...] = jnp.dot(x_ref[...], w_ref[...], preferred_element_type=F32).astype(o_ref.dtype)


def _matmul(x, w, tm, tn, out_dtype=F32):
    n, k = x.shape
    m = w.shape[1]
    return pl.pallas_call(
        _mm_kernel,
        grid=(n // tm, m // tn),
        in_specs=[pl.BlockSpec((tm, k), lambda i, j: (i, 0)),
                  pl.BlockSpec((k, tn), lambda i, j: (0, j))],
        out_specs=pl.BlockSpec((tm, tn), lambda i, j: (i, j)),
        out_shape=jax.ShapeDtypeStruct((n, m), out_dtype),
        compiler_params=_cparams("parallel", "parallel"),
        name="proj_matmul",
    )(x, w)


def _gdn_activate(x, first, ctx_ref, w_ref, e_ref, o_ref, kt_ref, xbuf, tt):
    @pl.when(first)
    def _():
        xbuf[0:SUBLANES, :] = ctx_ref[0]

    xbuf[SUBLANES:SUBLANES + tt, :] = x
    acc = w_ref[CONV_W - 1:CONV_W, :] * xbuf[SUBLANES:SUBLANES + tt, :]
    for j in range(CONV_W - 1):
        off = SUBLANES - (CONV_W - 1) + j
        acc = acc + w_ref[j:j + 1, :] * xbuf[off:off + tt, :]
    y = acc * _sigmoid(acc)
    q = y[:, 0:REC_W]
    k = y[:, REC_W:2 * REC_W]
    e = e_ref[...]
    qs = _dot_exact_rhs(q * q, e, NORM_SUM_PARTS)
    ks = _dot_exact_rhs(k * k, e, NORM_SUM_PARTS)
    kn = k * lax.rsqrt(ks + NORM_EPS)
    o_ref[:, 0:REC_W] = q * lax.rsqrt(qs + NORM_EPS) * (HEAD_DIM ** -0.5)
    o_ref[:, REC_W:2 * REC_W] = kn
    o_ref[:, 2 * REC_W:3 * REC_W] = y[:, 2 * REC_W:3 * REC_W]
    if kt_ref is not None:
        for r in range(tt // GROUP):
            _store_chunks_t(kt_ref, r, kn[r * GROUP:(r + 1) * GROUP, :])
    xbuf[0:SUBLANES, :] = xbuf[tt:tt + SUBLANES, :]


def _proj_kernel(x_ref, w_ref, ctx_ref, cw_ref, e_ref, o_ref, skt_ref, sv_ref, mkt_ref, gtt_ref, act_ref, gkt_ref,
                 xbuf, *, tm, tn):
    xb = x_ref[...].astype(BF16)
    for j in range(PROJ_W // tn):
        o_ref[:, j * tn:(j + 1) * tn] = jnp.dot(xb, w_ref[:, j * tn:(j + 1) * tn], preferred_element_type=F32)
    for r in range(tm // GROUP):
        rows = slice(r * GROUP, (r + 1) * GROUP)
        skt_ref[r] = o_ref[rows, COL_SB + SB_W:COL_SB + 2 * SB_W].T.astype(BF16)
        for h in range(SB_HEADS):
            c0 = COL_SB + 2 * SB_W + h * HEAD_DIM
            sv_ref[h, r] = o_ref[rows, c0:c0 + HEAD_DIM].astype(BF16)
        _store_chunks_t(mkt_ref, r, o_ref[rows, COL_MQKV:COL_MQKV + 3 * REC_W])
        _store_chunks_t(gtt_ref, r, o_ref[rows, COL_GATE:COL_GATE + LANES])
    _gdn_activate(o_ref[:, COL_GQKV:COL_GQKV + 3 * REC_W], pl.program_id(1) == 0, ctx_ref, cw_ref, e_ref,
                  act_ref, gkt_ref, xbuf, tm)


def _store_chunks_t(ref, r, a):
    at = a.T
    per = GROUP // CHUNK
    for j in range(per):
        ref[r * per + j] = at[:, j * CHUNK:(j + 1) * CHUNK]


def _proj(x, w, ctx8, conv_w8, e_heads, b, t, tm):
    n, d = x.shape
    nt = t // tm
    ng = n // GROUP
    gpt = tm // GROUP
    cpt = tm // CHUNK
    w3 = 3 * REC_W
    row = lambda bi, i: (bi * nt + i, 0)
    row3 = lambda bi, i: (bi * nt + i, 0, 0)
    fixed = lambda bi, i: (0, 0)
    return pl.pallas_call(
        functools.partial(_proj_kernel, tm=tm, tn=1024),
        grid=(b, nt),
        in_specs=[pl.BlockSpec((tm, d), row),
                  pl.BlockSpec((d, PROJ_W), fixed, pipeline_mode=pl.Buffered(1)),
                  pl.BlockSpec((1, SUBLANES, w3), lambda bi, i: (bi, 0, 0)),
                  pl.BlockSpec((SUBLANES, w3), fixed),
                  pl.BlockSpec((REC_W, REC_W), fixed)],
        out_specs=[pl.BlockSpec((tm, PROJ_W), row),
                   pl.BlockSpec((gpt, SB_W, GROUP), row3),
                   pl.BlockSpec((SB_HEADS, gpt, GROUP, HEAD_DIM), lambda bi, i: (0, bi * nt + i, 0, 0)),
                   pl.BlockSpec((cpt, w3, CHUNK), row3),
                   pl.BlockSpec((cpt, LANES, CHUNK), row3),
                   pl.BlockSpec((tm, w3), row),
                   pl.BlockSpec((cpt, REC_W, CHUNK), row3)],
        out_shape=[jax.ShapeDtypeStruct((n, PROJ_W), F32),
                   jax.ShapeDtypeStruct((ng, SB_W, GROUP), BF16),
                   jax.ShapeDtypeStruct((SB_HEADS, ng, GROUP, HEAD_DIM), BF16),
                   jax.ShapeDtypeStruct((n // CHUNK, w3, CHUNK), F32),
                   jax.ShapeDtypeStruct((n // CHUNK, LANES, CHUNK), F32),
                   jax.ShapeDtypeStruct((n, w3), F32),
                   jax.ShapeDtypeStruct((n // CHUNK, REC_W, CHUNK), F32)],
        scratch_shapes=[pltpu.VMEM((tm + SUBLANES, w3), F32)],
        compiler_params=_cparams("parallel", "arbitrary"),
        name="proj_full",
    )(x, w, ctx8, conv_w8, e_heads)


def _gdn_pre_kernel(x_ref, ctx_ref, w_ref, e_ref, o_ref, xbuf, *, tt):
    _gdn_activate(x_ref[...], pl.program_id(1) == 0, ctx_ref, w_ref, e_ref, o_ref, None, xbuf, tt)


def _gdn_pre(proj, ctx8, conv_w8, e_heads, b, t, tt):
    w3 = 3 * REC_W
    nt = t // tt
    return pl.pallas_call(
        functools.partial(_gdn_pre_kernel, tt=tt),
        grid=(b, nt),
        in_specs=[pl.BlockSpec((tt, w3), lambda bi, i: (bi * nt + i, COL_GQKV // w3)),
                  pl.BlockSpec((1, SUBLANES, w3), lambda bi, i: (bi, 0, 0)),
                  pl.BlockSpec((SUBLANES, w3), lambda bi, i: (0, 0)),
                  pl.BlockSpec((REC_W, REC_W), lambda bi, i: (0, 0))],
        out_specs=pl.BlockSpec((tt, w3), lambda bi, i: (bi * nt + i, 0)),
        out_shape=jax.ShapeDtypeStruct((b * t, w3), F32),
        scratch_shapes=[pltpu.VMEM((tt + SUBLANES, w3), F32)],
        compiler_params=_cparams("parallel", "arbitrary"),
        name="gdn_pre",
    )(proj, ctx8, conv_w8, e_heads)


def _sb_kernel(q_ref, kt_ref, v_ref, o_ref, acc_ref, c_ref, zmax_ref, kmax_ref,
               *, hb, tq, tk, nkb, q_off, diag_static):
    qi = pl.program_id(1)
    q_lo = q_off + qi * tq
    nblk = jnp.minimum(nkb, (q_lo + tq - 1 + tk - 1) // tk)
    nfull = jnp.minimum(nblk, q_lo // tk)
    scale = HEAD_DIM ** -0.5 * LOG2E
    qs = [(q_ref[:, h * HEAD_DIM:(h + 1) * HEAD_DIM] * scale).astype(BF16) for h in range(hb)]
    lower_incl = _ones_where(lax.broadcasted_iota(jnp.int32, (tk, tk), 0)
                             >= lax.broadcasted_iota(jnp.int32, (tk, tk), 1), BF16)
    acc_ref[...] = jnp.zeros_like(acc_ref)
    c_ref[...] = jnp.zeros_like(c_ref)

    @pl.when(qi == 0)
    def _():
        def knorm(kb, best):
            k32 = kt_ref[0, kb].astype(F32)
            sq = k32 * k32
            return tuple(jnp.maximum(best[h], jnp.sum(sq[h * HEAD_DIM:(h + 1) * HEAD_DIM, :], axis=0, keepdims=True))
                         for h in range(hb))
        best = lax.fori_loop(0, nkb, knorm, tuple(jnp.zeros((1, tk), F32) for _ in range(hb)))
        for h in range(hb):
            kmax_ref[h] = jnp.broadcast_to(jnp.sqrt(jnp.max(best[h], axis=1, keepdims=True)), (1, LANES))

    for h in range(hb):
        q32 = qs[h].astype(F32)
        qn = jnp.sqrt(jnp.sum(q32 * q32, axis=1, keepdims=True))
        zmax_ref[h] = qn * kmax_ref[h] * BOUND_SLACK

    def all_weights_vanish():
        margin = c_ref[0] - zmax_ref[0]
        for h in range(1, hb):
            margin = jnp.minimum(margin, c_ref[h] - zmax_ref[h])
        return jnp.min(margin) > F32_EXP2_ZERO

    def blocks(kbs, r0=0, valids=None):
        for h in range(hb):
            q = qs[h][r0:, :]
            c = c_ref[h, r0:, :]
            av = None
            for i, kb in enumerate(kbs):
                kt = kt_ref[0, kb, h * HEAD_DIM:(h + 1) * HEAD_DIM, :]
                z = jnp.dot(q, kt, preferred_element_type=F32)
                lneg = jnp.maximum(z, jnp.log2(1.0 + jnp.exp2(jnp.minimum(z, SOFTPLUS2_CLAMP))))
                if valids is not None:
                    lneg = jnp.where(valids[i], lneg, 0.0)
                incl = jnp.dot(lneg.astype(BF16), lower_incl, preferred_element_type=F32)
                a = jnp.exp2(z - incl - jnp.tile(c, (1, tk // LANES)))
                if valids is not None:
                    a = jnp.where(valids[i], a, 0.0)
                t = jnp.dot(a.astype(BF16), v_ref[h, kb], preferred_element_type=F32)
                av = t if av is None else av + t
                c = c + incl[:, 0:1]
            acc_ref[h, r0:, :] += av
            c_ref[h, r0:, :] = c

    if diag_static:
        for r in reversed(range(tq // tk)):
            rows = tq - r * tk
            valid = (lax.broadcasted_iota(jnp.int32, (rows, tk), 1)
                     < lax.broadcasted_iota(jnp.int32, (rows, tk), 0))
            blocks([nfull + r], r0=r * tk, valids=[valid])
    else:
        @pl.loop(0, nblk - nfull)
        def _(i):
            kb = nblk - 1 - i
            valid = (kb * tk + lax.broadcasted_iota(jnp.int32, (tq, tk), 1)
                     < q_lo + lax.broadcasted_iota(jnp.int32, (tq, tk), 0))
            blocks([kb], valids=[valid])

    def more_blocks(carry):
        i, done = carry
        return jnp.logical_and(i < nfull, done == 0)

    def next_block(carry):
        i, _ = carry
        blocks([nfull - 1 - i])
        return i + 1, all_weights_vanish().astype(jnp.int32)

    lax.while_loop(more_blocks, next_block, (jnp.int32(0), all_weights_vanish().astype(jnp.int32)))

    for h in range(hb):
        o_ref[:, h * HEAD_DIM:(h + 1) * HEAD_DIM] = acc_ref[h]


def _sb_attn(proj, kt, vh, b, t, hb, tq, q_off):
    _, nkb, w, tk = kt.shape
    d = w // hb
    nq = t // tq
    diag_static = q_off % tk == 0 and tq % tk == 0 and q_off + t <= nkb * tk
    return pl.pallas_call(
        functools.partial(_sb_kernel, hb=hb, tq=tq, tk=tk, nkb=nkb, q_off=q_off, diag_static=diag_static),
        grid=(b, nq),
        in_specs=[pl.BlockSpec((tq, w), lambda bi, i: (bi * nq + i, COL_SB // w)),
                  pl.BlockSpec((1, nkb, w, tk), lambda bi, i: (bi, 0, 0, 0), pipeline_mode=pl.Buffered(1)),
                  pl.BlockSpec((hb, nkb, tk, d), lambda bi, i: (bi, 0, 0, 0), pipeline_mode=pl.Buffered(1))],
        out_specs=pl.BlockSpec((tq, w), lambda bi, i: (bi * nq + i, 0)),
        out_shape=jax.ShapeDtypeStruct((b * t, w), F32),
        scratch_shapes=[pltpu.VMEM((hb, tq, d), F32), pltpu.VMEM((hb, tq, LANES), F32),
                        pltpu.VMEM((hb, tq, LANES), F32), pltpu.VMEM((hb, 1, LANES), F32)],
        compiler_params=_cparams("parallel", "arbitrary"),
        name="sb_attn",
    )(proj, kt, vh)


def _gate_values(pre, neg_a, lane_id):
    sp = jnp.log1p(jnp.exp(-jnp.abs(pre)))
    softplus = jnp.maximum(pre, 0.0) + sp
    log_sig = jnp.minimum(pre, 0.0) - sp
    return jnp.where(lane_id < GATE_G, _sigmoid(pre),
                     jnp.where(lane_id < GATE_I, neg_a * softplus,
                               jnp.where(lane_id < GATE_F, pre, log_sig)))


def _stack(a, b):
    return jnp.concatenate([a, b], axis=0)


def _gates(gt, gtt, prow_ref, pcol_ref, l):
    lane = lax.broadcasted_iota(jnp.int32, (l, LANES), 1)
    val = _gate_values(gt + prow_ref[0:1, :], -jnp.exp(prow_ref[1:2, :]), lane)
    sub = lax.broadcasted_iota(jnp.int32, (LANES, l), 0)
    val_t = _gate_values(gtt + pcol_ref[:, 0:1], -jnp.exp(pcol_ref[:, 1:2]), sub)
    r = lax.broadcasted_iota(jnp.int32, (l, l), 0)
    c = lax.broadcasted_iota(jnp.int32, (l, l), 1)
    csum = _dot_exact_lhs(_ones_where(r >= c, BF16), val, 3)
    csum_t = _dot_exact_rhs(val_t, _ones_where(r <= c, BF16), 3)
    return val, csum, val_t, csum_t


def _gdn_scan_kernel(qkv_ref, kt_ref, gt_ref, gtt_ref, prow_ref, pcol_ref, s0_ref, o_ref, s_out_ref, s_scr,
                     *, l, cb, ng):
    gi = pl.program_id(1)

    @pl.when(gi == 0)
    def _():
        s_scr[...] = s0_ref[0]

    r = lax.broadcasted_iota(jnp.int32, (l, l), 0)
    c = lax.broadcasted_iota(jnp.int32, (l, l), 1)
    tri = r >= c
    stri = r > c
    eye = _ones_where(r == c, F32)
    n_double = max(int(np.ceil(np.log2(l))) - 1, 0)
    hs = range(GDN_HEADS)
    ps = [(j, h) for j in range(cb) for h in hs]
    gates = [_gates(gt_ref[j * l:(j + 1) * l, :], gtt_ref[j], prow_ref, pcol_ref, l) for j in range(cb)]

    def head_cols(j, group, h):
        return qkv_ref[j * l:(j + 1) * l, group * REC_W + h * HEAD_DIM:group * REC_W + (h + 1) * HEAD_DIM]

    q = {p: head_cols(p[0], 0, p[1]) for p in ps}
    k = {p: head_cols(p[0], 1, p[1]) for p in ps}
    v = {p: head_cols(p[0], 2, p[1]) for p in ps}
    kt = {p: kt_ref[p[0], p[1] * HEAD_DIM:(p[1] + 1) * HEAD_DIM, :] for p in ps}
    beta = {p: gates[p[0]][0][:, GATE_BETA + p[1]:GATE_BETA + p[1] + 1] for p in ps}
    g_col = {p: gates[p[0]][1][:, GATE_G + p[1]:GATE_G + p[1] + 1] for p in ps}
    g_row = {p: gates[p[0]][3][GATE_G + p[1]:GATE_G + p[1] + 1, :] for p in ps}
    g_last = {p: g_col[p][l - 1:l, :] for p in ps}
    decay = {p: jnp.where(tri, jnp.exp(jnp.where(tri, g_col[p] - g_row[p], 0.0)), 0.0) for p in ps}
    kb = {p: k[p] * beta[p] for p in ps}
    e_g = {p: jnp.exp(g_col[p]) for p in ps}
    kq = {p: _dot(_stack(kb[p], q[p]), kt[p]) for p in ps}
    x = {p: -jnp.where(stri, kq[p][:l] * decay[p], 0.0) for p in ps}
    tinv = {p: eye + x[p] for p in ps}
    if n_double > 0:
        x = {p: _dot(x[p], x[p]) for p in ps}
        for _ in range(n_double - 1):
            tx = {p: _dot(_stack(tinv[p], x[p]), x[p]) for p in ps}
            tinv = {p: tinv[p] + tx[p][:l] for p in ps}
            x = {p: tx[p][l:] for p in ps}
        tinv = {p: tinv[p] + _dot(tinv[p], x[p]) for p in ps}
    sol_v = {p: _dot(tinv[p], v[p] * beta[p]) for p in ps}
    sol_k = {p: _dot(tinv[p], kb[p] * e_g[p]) for p in ps}
    lhs_s = {p: _stack(q[p] * e_g[p], sol_k[p]) for p in ps}
    lhs_u = {p: _stack(kt[p] * jnp.exp(g_last[p] - g_row[p]), kq[p][l:] * decay[p]) for p in ps}
    e_last = {p: jnp.exp(g_last[p]) for p in ps}
    s = [s_scr[h] for h in hs]
    for j in range(cb):
        t = [_dot(lhs_s[(j, h)], s[h]) for h in hs]
        u = [sol_v[(j, h)] - t[h][l:] for h in hs]
        w = [_dot(lhs_u[(j, h)], u[h]) for h in hs]
        s = [e_last[(j, h)] * s[h] + w[h][:HEAD_DIM] for h in hs]
        for h in hs:
            o_ref[j * l:(j + 1) * l, h * HEAD_DIM:(h + 1) * HEAD_DIM] = t[h][:l] + w[h][HEAD_DIM:]
    for h in hs:
        s_scr[h] = s[h]

    @pl.when(gi == ng - 1)
    def _():
        s_out_ref[0] = s_scr[...]


def _gdn_scan(qkv, k_t, proj, gate_t, prow, pcol, s0, b, t, l, cb):
    ng = t // (l * cb)
    w3 = 3 * REC_W
    return pl.pallas_call(
        functools.partial(_gdn_scan_kernel, l=l, cb=cb, ng=ng),
        grid=(b, ng),
        in_specs=[pl.BlockSpec((cb * l, w3), lambda bi, i: (bi * ng + i, 0)),
                  pl.BlockSpec((cb, REC_W, l), lambda bi, i: (bi * ng + i, 0, 0)),
                  pl.BlockSpec((cb * l, LANES), lambda bi, i: (bi * ng + i, COL_GATE // LANES)),
                  pl.BlockSpec((cb, LANES, l), lambda bi, i: (bi * ng + i, 0, 0)),
                  pl.BlockSpec((SUBLANES, LANES), lambda bi, i: (0, 0)),
                  pl.BlockSpec((LANES, SUBLANES), lambda bi, i: (0, 0)),
                  pl.BlockSpec((1, GDN_HEADS, HEAD_DIM, HEAD_DIM), lambda bi, i: (bi, 0, 0, 0))],
        out_specs=[pl.BlockSpec((cb * l, REC_W), lambda bi, i: (bi * ng + i, 0)),
                   pl.BlockSpec((1, GDN_HEADS, HEAD_DIM, HEAD_DIM), lambda bi, i: (bi, 0, 0, 0))],
        out_shape=[jax.ShapeDtypeStruct((b * t, REC_W), F32),
                   jax.ShapeDtypeStruct((b, GDN_HEADS, HEAD_DIM, HEAD_DIM), F32)],
        scratch_shapes=[pltpu.VMEM((GDN_HEADS, HEAD_DIM, HEAD_DIM), F32)],
        compiler_params=_cparams("parallel", "arbitrary"),
        name="gdn_scan",
    )(qkv, k_t, proj, gate_t, prow, pcol, s0)


def _gdn_scan_pair_kernel(qkv_ref, kt_ref, gt_ref, gtt_ref, prow_ref, pcol_ref, s0_ref, o_ref, s_out_ref, s_scr,
                          *, l, cb, ng):
    gi = pl.program_id(1)
    npair = GDN_HEADS // 2
    row128 = lax.broadcasted_iota(jnp.int32, (LANES, LANES), 0)
    col128 = lax.broadcasted_iota(jnp.int32, (LANES, LANES), 1)
    blockmask = (row128 < HEAD_DIM) == (col128 < HEAD_DIM)

    def bd(a):
        return jnp.where(blockmask, _stack(a, a), 0.0)

    @pl.when(gi == 0)
    def _():
        for p in range(npair):
            s_scr[p] = bd(jnp.concatenate([s0_ref[0, 2 * p], s0_ref[0, 2 * p + 1]], axis=1))

    trow = lax.broadcasted_iota(jnp.int32, (l, LANES), 0)
    tcol = lax.broadcasted_iota(jnp.int32, (l, LANES), 1) % HEAD_DIM
    left = lax.broadcasted_iota(jnp.int32, (l, LANES), 1) < HEAD_DIM
    tri = trow >= tcol
    stri = trow > tcol
    eye = _ones_where(trow == tcol, F32)
    krow_left = lax.broadcasted_iota(jnp.int32, (LANES, l), 0) < HEAD_DIM
    n_double = max(int(np.ceil(np.log2(l))) - 1, 0)
    ps = [(j, p) for j in range(cb) for p in range(npair)]
    gates = [_gates(gt_ref[j * l:(j + 1) * l, :], gtt_ref[j], prow_ref, pcol_ref, l) for j in range(cb)]

    def pair_cols(j, group, p):
        c0 = group * REC_W + p * LANES
        return qkv_ref[j * l:(j + 1) * l, c0:c0 + LANES]

    def col_pair(a, lane0, p):
        return jnp.where(left, a[:, lane0 + 2 * p:lane0 + 2 * p + 1], a[:, lane0 + 2 * p + 1:lane0 + 2 * p + 2])

    q = {x: pair_cols(x[0], 0, x[1]) for x in ps}
    k = {x: pair_cols(x[0], 1, x[1]) for x in ps}
    v = {x: pair_cols(x[0], 2, x[1]) for x in ps}
    ktp = {x: kt_ref[x[0], x[1] * LANES:(x[1] + 1) * LANES, :] for x in ps}
    beta = {x: col_pair(gates[x[0]][0], GATE_BETA, x[1]) for x in ps}
    g_col = {x: col_pair(gates[x[0]][1], GATE_G, x[1]) for x in ps}
    g_row_a = {x: gates[x[0]][3][GATE_G + 2 * x[1]:GATE_G + 2 * x[1] + 1, :] for x in ps}
    g_row_b = {x: gates[x[0]][3][GATE_G + 2 * x[1] + 1:GATE_G + 2 * x[1] + 2, :] for x in ps}
    g_row = {x: jnp.concatenate([g_row_a[x], g_row_b[x]], axis=1) for x in ps}
    g_last = {x: g_col[x][l - 1:l, :] for x in ps}
    decay = {x: jnp.where(tri, jnp.exp(jnp.where(tri, g_col[x] - g_row[x], 0.0)), 0.0) for x in ps}
    kb = {x: k[x] * beta[x] for x in ps}
    e_g = {x: jnp.exp(g_col[x]) for x in ps}
    bd_kt = {x: jnp.concatenate([jnp.where(krow_left, ktp[x], 0.0), jnp.where(krow_left, 0.0, ktp[x])], axis=1)
             for x in ps}
    kq = {x: _dot(_stack(kb[x], q[x]), bd_kt[x]) for x in ps}
    y = {x: -jnp.where(stri, kq[x][:l] * decay[x], 0.0) for x in ps}
    tinv = {x: eye + y[x] for x in ps}
    if n_double > 0:
        y = {x: _dot(y[x], bd(y[x])) for x in ps}
        for _ in range(n_double - 1):
            ty = {x: _dot(_stack(tinv[x], y[x]), bd(y[x])) for x in ps}
            tinv = {x: tinv[x] + ty[x][:l] for x in ps}
            y = {x: ty[x][l:] for x in ps}
        tinv = {x: tinv[x] + _dot(tinv[x], bd(y[x])) for x in ps}
    sol = {x: _dot(tinv[x], jnp.concatenate([bd(v[x] * beta[x]), bd(kb[x] * e_g[x])], axis=1)) for x in ps}
    lhs_s = {x: _stack(q[x] * e_g[x], sol[x][:, LANES:]) for x in ps}
    g_last_a = {x: g_last[x][:, 0:1] for x in ps}
    g_last_b = {x: g_last[x][:, HEAD_DIM:HEAD_DIM + 1] for x in ps}
    ktd = {x: ktp[x] * jnp.where(krow_left, jnp.exp(g_last_a[x] - g_row_a[x]), jnp.exp(g_last_b[x] - g_row_b[x]))
           for x in ps}
    attn = {x: kq[x][l:] * decay[x] for x in ps}
    e_last = {x: jnp.exp(g_last[x]) for x in ps}
    s = [s_scr[p] for p in range(npair)]
    for j in range(cb):
        t = [_dot(lhs_s[(j, p)], s[p]) for p in range(npair)]
        u = [sol[(j, p)][:, :LANES] - t[p][l:] for p in range(npair)]
        du = [_dot(ktd[(j, p)], u[p]) for p in range(npair)]
        ou = [_dot(attn[(j, p)], bd(u[p])) for p in range(npair)]
        s = [e_last[(j, p)] * s[p] + jnp.where(blockmask, du[p], 0.0) for p in range(npair)]
        for p in range(npair):
            o_ref[j * l:(j + 1) * l, p * LANES:(p + 1) * LANES] = t[p][:l] + ou[p]
    for p in range(npair):
        s_scr[p] = s[p]

    @pl.when(gi == ng - 1)
    def _():
        for p in range(npair):
            s_out_ref[0, 2 * p] = s_scr[p, 0:HEAD_DIM, 0:HEAD_DIM]
            s_out_ref[0, 2 * p + 1] = s_scr[p, HEAD_DIM:LANES, HEAD_DIM:LANES]


def _gdn_scan_pair(qkv, k_t, proj, gate_t, prow, pcol, s0, b, t, l, cb):
    ng = t // (l * cb)
    w3 = 3 * REC_W
    return pl.pallas_call(
        functools.partial(_gdn_scan_pair_kernel, l=l, cb=cb, ng=ng),
        grid=(b, ng),
        in_specs=[pl.BlockSpec((cb * l, w3), lambda bi, i: (bi * ng + i, 0)),
                  pl.BlockSpec((cb, REC_W, l), lambda bi, i: (bi * ng + i, 0, 0)),
                  pl.BlockSpec((cb * l, LANES), lambda bi, i: (bi * ng + i, COL_GATE // LANES)),
                  pl.BlockSpec((cb, LANES, l), lambda bi, i: (bi * ng + i, 0, 0)),
                  pl.BlockSpec((SUBLANES, LANES), lambda bi, i: (0, 0)),
                  pl.BlockSpec((LANES, SUBLANES), lambda bi, i: (0, 0)),
                  pl.BlockSpec((1, GDN_HEADS, HEAD_DIM, HEAD_DIM), lambda bi, i: (bi, 0, 0, 0))],
        out_specs=[pl.BlockSpec((cb * l, REC_W), lambda bi, i: (bi * ng + i, 0)),
                   pl.BlockSpec((1, GDN_HEADS, HEAD_DIM, HEAD_DIM), lambda bi, i: (bi, 0, 0, 0))],
        out_shape=[jax.ShapeDtypeStruct((b * t, REC_W), F32),
                   jax.ShapeDtypeStruct((b, GDN_HEADS, HEAD_DIM, HEAD_DIM), F32)],
        scratch_shapes=[pltpu.VMEM((GDN_HEADS // 2, LANES, LANES), F32)],
        compiler_params=_cparams("parallel", "arbitrary"),
        name="gdn_scan_pair",
    )(qkv, k_t, proj, gate_t, prow, pcol, s0)


def _ml_scan_kernel(qkv_ref, kt_ref, gt_ref, gtt_ref, prow_ref, pcol_ref, c0_ref, n0_ref, m0_ref,
                    o_ref, c_out_ref, n_out_ref, m_out_ref, c_scr, n_scr, m_scr, *, l, cb, ng):
    gi = pl.program_id(1)

    @pl.when(gi == 0)
    def _():
        c_scr[...] = c0_ref[0]
        n_scr[...] = n0_ref[0]
        m_scr[...] = m0_ref[0]

    r = lax.broadcasted_iota(jnp.int32, (l, l), 0)
    c = lax.broadcasted_iota(jnp.int32, (l, l), 1)
    tri = r >= c
    lane = lax.broadcasted_iota(jnp.int32, (1, LANES), 1)
    m_all = m_scr[...]
    hs = range(ML_HEADS)
    ps = [(j, h) for j in range(cb) for h in hs]
    kscale = HEAD_DIM ** -0.5
    gates = [_gates(gt_ref[j * l:(j + 1) * l, :], gtt_ref[j], prow_ref, pcol_ref, l) for j in range(cb)]

    def head_cols(j, group, h):
        return qkv_ref[j * l:(j + 1) * l, group * REC_W + h * HEAD_DIM:group * REC_W + (h + 1) * HEAD_DIM]

    q = {p: head_cols(p[0], 0, p[1]) for p in ps}
    k = {p: head_cols(p[0], 1, p[1]) * kscale for p in ps}
    v = {p: head_cols(p[0], 2, p[1]) for p in ps}
    kt = {p: kt_ref[p[0], p[1] * HEAD_DIM:(p[1] + 1) * HEAD_DIM, :] * kscale for p in ps}
    ig_col = {p: gates[p[0]][0][:, GATE_I + p[1]:GATE_I + p[1] + 1] for p in ps}
    ig_row = {p: gates[p[0]][2][GATE_I + p[1]:GATE_I + p[1] + 1, :] for p in ps}
    f_col = {p: gates[p[0]][1][:, GATE_F + p[1]:GATE_F + p[1] + 1] for p in ps}
    f_row = {p: gates[p[0]][3][GATE_F + p[1]:GATE_F + p[1] + 1, :] for p in ps}
    f_last = {p: f_col[p][l - 1:l, :] for p in ps}
    qk = {p: _dot(q[p], kt[p]) for p in ps}
    d = {p: jnp.where(tri, f_col[p] - f_row[p] + ig_row[p], -jnp.inf) for p in ps}
    d_max = {p: jnp.max(d[p], axis=1, keepdims=True) for p in ps}
    m_prev, m_t = {}, {}
    m_run = [m_all[:, h:h + 1] for h in hs]
    for j in range(cb):
        for h in hs:
            m_prev[(j, h)] = m_run[h]
            m_t[(j, h)] = jnp.maximum(f_col[(j, h)] + m_run[h], d_max[(j, h)])
            m_run[h] = m_t[(j, h)][l - 1:l, :]
    m_new = {p: m_t[p][l - 1:l, :] for p in ps}
    w = {p: jnp.exp(d[p] - m_t[p]) * qk[p] for p in ps}
    c_inter = {p: jnp.exp(f_col[p] + m_prev[p] - m_t[p]) for p in ps}
    wv = {p: _dot(w[p], v[p]) for p in ps}
    w_sum = {p: jnp.sum(w[p], axis=1, keepdims=True) for p in ps}
    e_m = {p: jnp.exp(-m_t[p]) for p in ps}
    w_end_row = {p: jnp.exp(f_last[p] - f_row[p] + ig_row[p] - m_new[p]) for p in ps}
    w_end_col = {p: jnp.exp(f_last[p] - f_col[p] + ig_col[p] - m_new[p]) for p in ps}
    c_prev = {p: jnp.exp(f_last[p] + m_prev[p] - m_new[p]) for p in ps}
    dc = {p: _dot(kt[p] * w_end_row[p], v[p]) for p in ps}
    dn = {p: jnp.sum(w_end_col[p] * k[p], axis=0, keepdims=True) for p in ps}
    c_h = [c_scr[h] for h in hs]
    n_h = [n_scr[h:h + 1, :] for h in hs]
    for j in range(cb):
        for h in hs:
            p = (j, h)
            num = c_inter[p] * _dot(q[p], c_h[h]) + wv[p]
            den = c_inter[p] * jnp.sum(q[p] * n_h[h], axis=1, keepdims=True) + w_sum[p]
            o_ref[j * l:(j + 1) * l, h * HEAD_DIM:(h + 1) * HEAD_DIM] = num / jnp.maximum(jnp.abs(den), e_m[p])
            c_h[h] = c_prev[p] * c_h[h] + dc[p]
            n_h[h] = c_prev[p] * n_h[h] + dn[p]
    m_next = m_all
    for h in hs:
        c_scr[h] = c_h[h]
        n_scr[h:h + 1, :] = n_h[h]
        m_next = jnp.where(lane == h, m_run[h], m_next)
    m_scr[...] = m_next

    @pl.when(gi == ng - 1)
    def _():
        c_out_ref[0] = c_scr[...]
        n_out_ref[0] = n_scr[...]
        m_out_ref[0] = m_scr[...]


def _ml_scan(proj, k_t, gate_t, prow, pcol, c0, n0, m0, b, t, l, cb):
    ng = t // (l * cb)
    w3 = 3 * REC_W
    st = lambda bi, i: (bi, 0, 0, 0)
    st3 = lambda bi, i: (bi, 0, 0)
    return pl.pallas_call(
        functools.partial(_ml_scan_kernel, l=l, cb=cb, ng=ng),
        grid=(b, ng),
        in_specs=[pl.BlockSpec((cb * l, w3), lambda bi, i: (bi * ng + i, COL_MQKV // w3)),
                  pl.BlockSpec((cb, REC_W, l), lambda bi, i: (bi * ng + i, 0, 0)),
                  pl.BlockSpec((cb * l, LANES), lambda bi, i: (bi * ng + i, COL_GATE // LANES)),
                  pl.BlockSpec((cb, LANES, l), lambda bi, i: (bi * ng + i, 0, 0)),
                  pl.BlockSpec((SUBLANES, LANES), lambda bi, i: (0, 0)),
                  pl.BlockSpec((LANES, SUBLANES), lambda bi, i: (0, 0)),
                  pl.BlockSpec((1, ML_HEADS, HEAD_DIM, HEAD_DIM), st),
                  pl.BlockSpec((1, SUBLANES, HEAD_DIM), st3),
                  pl.BlockSpec((1, 1, LANES), st3)],
        out_specs=[pl.BlockSpec((cb * l, REC_W), lambda bi, i: (bi * ng + i, 0)),
                   pl.BlockSpec((1, ML_HEADS, HEAD_DIM, HEAD_DIM), st),
                   pl.BlockSpec((1, SUBLANES, HEAD_DIM), st3),
                   pl.BlockSpec((1, 1, LANES), st3)],
        out_shape=[jax.ShapeDtypeStruct((b * t, REC_W), F32),
                   jax.ShapeDtypeStruct((b, ML_HEADS, HEAD_DIM, HEAD_DIM), F32),
                   jax.ShapeDtypeStruct((b, SUBLANES, HEAD_DIM), F32),
                   jax.ShapeDtypeStruct((b, 1, LANES), F32)],
        scratch_shapes=[pltpu.VMEM((ML_HEADS, HEAD_DIM, HEAD_DIM), F32),
                        pltpu.VMEM((SUBLANES, HEAD_DIM), F32),
                        pltpu.VMEM((1, LANES), F32)],
        compiler_params=_cparams("parallel", "arbitrary"),
        name="ml_scan",
    )(proj, k_t, proj, gate_t, prow, pcol, c0, n0, m0)


def _ml_scan_t_kernel(qkv_ref, qkvt_ref, gt_ref, gtt_ref, prow_ref, pcol_ref, c0_ref, n0_ref, m0_ref,
                      o_ref, c_out_ref, n_out_ref, m_out_ref, c_scr, n_scr, m_scr, *, l, cb, ng):
    gi = pl.program_id(1)

    @pl.when(gi == 0)
    def _():
        c_scr[...] = c0_ref[0]
        n_scr[...] = n0_ref[0]
        m_scr[...] = m0_ref[0]

    r = lax.broadcasted_iota(jnp.int32, (l, l), 0)
    c = lax.broadcasted_iota(jnp.int32, (l, l), 1)
    tri_t = r <= c
    lane = lax.broadcasted_iota(jnp.int32, (1, LANES), 1)
    m_all = m_scr[...]
    hs = range(ML_HEADS)
    ps = [(j, h) for j in range(cb) for h in hs]
    kscale = HEAD_DIM ** -0.5
    zeros_t = jnp.zeros((HEAD_DIM, l), F32)
    gates = [_gates(gt_ref[j * l:(j + 1) * l, :], gtt_ref[j], prow_ref, pcol_ref, l) for j in range(cb)]

    def t_rows(j, group, h):
        return qkvt_ref[j, group * REC_W + h * HEAD_DIM:group * REC_W + (h + 1) * HEAD_DIM, :]

    qt = {p: t_rows(p[0], 0, p[1]) for p in ps}
    qt_pad = {p: (_stack(qt[p], zeros_t) if p[1] % 2 == 0 else _stack(zeros_t, qt[p])) for p in ps}
    vt = {p: t_rows(p[0], 2, p[1]) for p in ps}
    k2 = {p: qkv_ref[p[0] * l:(p[0] + 1) * l, REC_W + (p[1] // 2) * LANES:REC_W + (p[1] // 2 + 1) * LANES] * kscale
          for p in ps}
    ig_col = {p: gates[p[0]][0][:, GATE_I + p[1]:GATE_I + p[1] + 1] for p in ps}
    ig_row = {p: gates[p[0]][2][GATE_I + p[1]:GATE_I + p[1] + 1, :] for p in ps}
    f_col = {p: gates[p[0]][1][:, GATE_F + p[1]:GATE_F + p[1] + 1] for p in ps}
    f_row = {p: gates[p[0]][3][GATE_F + p[1]:GATE_F + p[1] + 1, :] for p in ps}
    f_last = {p: f_row[p][:, l - 1:l] for p in ps}
    kq = {p: _dot(k2[p], qt_pad[p]) for p in ps}
    d = {p: jnp.where(tri_t, f_row[p] + (ig_col[p] - f_col[p]), -jnp.inf) for p in ps}
    d_max = {p: jnp.max(d[p], axis=0, keepdims=True) for p in ps}
    m_prev, m_t = {}, {}
    m_run = [m_all[:, h:h + 1] for h in hs]
    for j in range(cb):
        for h in hs:
            m_prev[(j, h)] = m_run[h]
            m_t[(j, h)] = jnp.maximum(f_row[(j, h)] + m_run[h], d_max[(j, h)])
            m_run[h] = m_t[(j, h)][:, l - 1:l]
    m_new = {p: m_t[p][:, l - 1:l] for p in ps}
    w = {p: jnp.exp(d[p] - m_t[p]) * kq[p] for p in ps}
    c_inter = {p: jnp.exp(f_row[p] + m_prev[p] - m_t[p]) for p in ps}
    vw = {p: _dot(vt[p], w[p]) for p in ps}
    w_sum = {p: jnp.sum(w[p], axis=0, keepdims=True) for p in ps}
    e_m = {p: jnp.exp(-m_t[p]) for p in ps}
    w_end_col = {p: jnp.exp(f_last[p] - f_col[p] + ig_col[p] - m_new[p]) for p in ps}
    c_prev = {p: jnp.exp(f_last[p] + m_prev[p] - m_new[p]) for p in ps}
    wk = {p: w_end_col[p] * k2[p] for p in ps}
    dc = {p: _dot(vt[p], wk[p]) for p in ps}
    dn = {p: jnp.sum(wk[p], axis=0, keepdims=True) for p in ps}
    c_h = [c_scr[h] for h in hs]
    n_h = [n_scr[h:h + 1, :] for h in hs]
    ht = {}
    for j in range(cb):
        for h in hs:
            p = (j, h)
            num = c_inter[p] * _dot(c_h[h], qt_pad[p]) + vw[p]
            den = c_inter[p] * _dot(n_h[h], qt_pad[p]) + w_sum[p]
            ht[p] = num / jnp.maximum(jnp.abs(den), e_m[p])
            c_h[h] = c_prev[p] * c_h[h] + dc[p]
            n_h[h] = c_prev[p] * n_h[h] + dn[p]
    for i in range(ML_HEADS // 2):
        pair_t = _stack(jnp.concatenate([ht[(j, 2 * i)] for j in range(cb)], axis=1),
                        jnp.concatenate([ht[(j, 2 * i + 1)] for j in range(cb)], axis=1))
        o_ref[:, i * LANES:(i + 1) * LANES] = pair_t.T
    m_next = m_all
    for h in hs:
        c_scr[h] = c_h[h]
        n_scr[h:h + 1, :] = n_h[h]
        m_next = jnp.where(lane == h, m_run[h], m_next)
    m_scr[...] = m_next

    @pl.when(gi == ng - 1)
    def _():
        c_out_ref[0] = c_scr[...]
        n_out_ref[0] = n_scr[...]
        m_out_ref[0] = m_scr[...]


def _ml_scan_t(proj, qkv_t, gate_t, prow, pcol, c0, n0, m0, b, t, l, cb):
    ng = t // (l * cb)
    w3 = 3 * REC_W
    st = lambda bi, i: (bi, 0, 0, 0)
    st3 = lambda bi, i: (bi, 0, 0)
    return pl.pallas_call(
        functools.partial(_ml_scan_t_kernel, l=l, cb=cb, ng=ng),
        grid=(b, ng),
        in_specs=[pl.BlockSpec((cb * l, w3), lambda bi, i: (bi * ng + i, COL_MQKV // w3)),
                  pl.BlockSpec((cb, w3, l), lambda bi, i: (bi * ng + i, 0, 0)),
                  pl.BlockSpec((cb * l, LANES), lambda bi, i: (bi * ng + i, COL_GATE // LANES)),
                  pl.BlockSpec((cb, LANES, l), lambda bi, i: (bi * ng + i, 0, 0)),
                  pl.BlockSpec((SUBLANES, LANES), lambda bi, i: (0, 0)),
                  pl.BlockSpec((LANES, SUBLANES), lambda bi, i: (0, 0)),
                  pl.BlockSpec((1, ML_HEADS, HEAD_DIM, LANES), st),
                  pl.BlockSpec((1, SUBLANES, LANES), st3),
                  pl.BlockSpec((1, 1, LANES), st3)],
        out_specs=[pl.BlockSpec((cb * l, REC_W), lambda bi, i: (bi * ng + i, 0)),
                   pl.BlockSpec((1, ML_HEADS, HEAD_DIM, LANES), st),
                   pl.BlockSpec((1, SUBLANES, LANES), st3),
                   pl.BlockSpec((1, 1, LANES), st3)],
        out_shape=[jax.ShapeDtypeStruct((b * t, REC_W), F32),
                   jax.ShapeDtypeStruct((b, ML_HEADS, HEAD_DIM, LANES), F32),
                   jax.ShapeDtypeStruct((b, SUBLANES, LANES), F32),
                   jax.ShapeDtypeStruct((b, 1, LANES), F32)],
        scratch_shapes=[pltpu.VMEM((ML_HEADS, HEAD_DIM, LANES), F32),
                        pltpu.VMEM((SUBLANES, LANES), F32),
                        pltpu.VMEM((1, LANES), F32)],
        compiler_params=_cparams("parallel", "arbitrary"),
        name="ml_scan_t",
    )(proj, qkv_t, proj, gate_t, prow, pcol, c0, n0, m0)


def _post_kernel(x_ref, osb_ref, og_ref, hm_ref, gz_ref, mo_ref, w_ref, e_ref, gnw_ref, mnw_ref,
                 lg_ref, lb_ref, o_ref, ob_ref):
    e = e_ref[...]
    og = og_ref[...]
    hm = hm_ref[...]
    og_ms = _dot_exact_rhs(og * og, e, NORM_SUM_PARTS) * (1.0 / HEAD_DIM)
    hm_ms = _dot_exact_rhs(hm * hm, e, NORM_SUM_PARTS) * (1.0 / HEAD_DIM)
    gz = gz_ref[...]
    o_g = og * lax.rsqrt(og_ms + NORM_EPS) * gnw_ref[...] * (gz * _sigmoid(gz))
    o_m = _sigmoid(mo_ref[...]) * (hm * lax.rsqrt(hm_ms + NORM_EPS) * mnw_ref[...])
    mixed = (_dot(osb_ref[...], w_ref[0:SB_W, :])
             + _dot(o_g, w_ref[SB_W:SB_W + REC_W, :])
             + _dot(o_m, w_ref[SB_W + REC_W:SB_W + 2 * REC_W, :]))
    y = _layer_norm(DEEPNORM_ALPHA * x_ref[...] + mixed, lg_ref[...], lb_ref[...])
    o_ref[...] = y
    ob_ref[...] = y.astype(BF16)


def _post(x, osb, og, hm, proj, w_out, e_heads, gnw, mnw, lg, lb, tm):
    n, d = x.shape
    row = lambda i: (i, 0)
    fixed = lambda i: (0, 0)
    return pl.pallas_call(
        _post_kernel,
        grid=(n // tm,),
        in_specs=[pl.BlockSpec((tm, d), row),
                  pl.BlockSpec((tm, SB_W), row),
                  pl.BlockSpec((tm, REC_W), row),
                  pl.BlockSpec((tm, REC_W), row),
                  pl.BlockSpec((tm, REC_W), lambda i: (i, COL_GZ // REC_W)),
                  pl.BlockSpec((tm, REC_W), lambda i: (i, COL_MO // REC_W)),
                  pl.BlockSpec(w_out.shape, fixed),
                  pl.BlockSpec((REC_W, REC_W), fixed),
                  pl.BlockSpec((1, REC_W), fixed),
                  pl.BlockSpec((1, REC_W), fixed),
                  pl.BlockSpec((1, d), fixed),
                  pl.BlockSpec((1, d), fixed)],
        out_specs=[pl.BlockSpec((tm, d), row), pl.BlockSpec((tm, d), row)],
        out_shape=[jax.ShapeDtypeStruct((n, d), F32), jax.ShapeDtypeStruct((n, d), BF16)],
        compiler_params=_cparams("parallel"),
        name="post",
    )(x, osb, og, hm, proj, proj, w_out, e_heads, gnw, mnw, lg, lb)


def _memattn_kernel(x_ref, xb_ref, wq_ref, wo_ref, mk_ref, mv_ref, lg_ref, lb_ref, o_ref, ob_ref):
    d = x_ref.shape[1]
    hd = d // MEM_HEADS
    q = jnp.dot(xb_ref[...], wq_ref[...], preferred_element_type=F32)
    out = None
    for h in range(MEM_HEADS):
        sl = slice(h * hd, (h + 1) * hd)
        s = _dot_nt(q[:, sl], mk_ref[0, :, sl]) * (hd ** -0.5)
        p = jnp.exp(s - jnp.max(s, axis=1, keepdims=True))
        o_h = _dot(p, mv_ref[0, :, sl]) / jnp.sum(p, axis=1, keepdims=True)
        t = _dot(o_h, wo_ref[sl, :])
        out = t if out is None else out + t
    y = _layer_norm(DEEPNORM_ALPHA * x_ref[...] + out, lg_ref[...], lb_ref[...])
    o_ref[...] = y
    ob_ref[...] = y.astype(BF16)


def _memattn(x, xb, w_cq, w_co, mk, mv, lg, lb, b, t, tm):
    n, d = x.shape
    nt = t // tm
    nm = mk.shape[1]
    row = lambda bi, i: (bi * nt + i, 0)
    fixed = lambda bi, i: (0, 0)
    return pl.pallas_call(
        _memattn_kernel,
        grid=(b, nt),
        in_specs=[pl.BlockSpec((tm, d), row),
                  pl.BlockSpec((tm, d), row),
                  pl.BlockSpec((d, d), fixed),
                  pl.BlockSpec((d, d), fixed),
                  pl.BlockSpec((1, nm, d), lambda bi, i: (bi, 0, 0)),
                  pl.BlockSpec((1, nm, d), lambda bi, i: (bi, 0, 0)),
                  pl.BlockSpec((1, d), fixed),
                  pl.BlockSpec((1, d), fixed)],
        out_specs=[pl.BlockSpec((tm, d), row), pl.BlockSpec((tm, d), row)],
        out_shape=[jax.ShapeDtypeStruct((n, d), F32), jax.ShapeDtypeStruct((n, d), BF16)],
        compiler_params=_cparams("parallel", "parallel"),
        name="memattn",
    )(x, xb, w_cq, w_co, mk, mv, lg, lb)


def _ffn_kernel(x_ref, xb_ref, wu_ref, wd_ref, lg_ref, lb_ref, o_ref, ob_ref, *, tf):
    xb = xb_ref[...]
    dff = wd_ref.shape[0]
    acc = None
    for j in range(dff // tf):
        gate = jnp.dot(xb, wu_ref[:, j * tf:(j + 1) * tf], preferred_element_type=F32)
        up = jnp.dot(xb, wu_ref[:, dff + j * tf:dff + (j + 1) * tf], preferred_element_type=F32)
        t = _dot(gate * _sigmoid(gate) * up, wd_ref[j * tf:(j + 1) * tf, :])
        acc = t if acc is None else acc + t
    y = _layer_norm(DEEPNORM_ALPHA * x_ref[...] + acc, lg_ref[...], lb_ref[...])
    o_ref[...] = y
    ob_ref[...] = y.astype(BF16)


def _ffn(x, xb, w_up, w_down, lg, lb, tm, tf):
    n, d = x.shape
    row = lambda i: (i, 0)
    fixed = lambda i: (0, 0)
    return pl.pallas_call(
        functools.partial(_ffn_kernel, tf=tf),
        grid=(n // tm,),
        in_specs=[pl.BlockSpec((tm, d), row),
                  pl.BlockSpec((tm, d), row),
                  pl.BlockSpec(w_up.shape, fixed, pipeline_mode=pl.Buffered(1)),
                  pl.BlockSpec(w_down.shape, fixed, pipeline_mode=pl.Buffered(1)),
                  pl.BlockSpec((1, d), fixed),
                  pl.BlockSpec((1, d), fixed)],
        out_specs=[pl.BlockSpec((tm, d), row), pl.BlockSpec((tm, d), row)],
        out_shape=[jax.ShapeDtypeStruct((n, d), F32), jax.ShapeDtypeStruct((n, d), BF16)],
        compiler_params=_cparams("parallel"),
        name="ffn",
    )(x, xb, w_up, w_down, lg, lb)


def _pack_layer(p):
    w_in = p['w_in']
    d = w_in.shape[0]
    o_sb, o_gqkv, o_gz = 0, 3 * SB_W, 3 * SB_W + 3 * REC_W
    o_gb = o_gz + REC_W
    o_mqkv = o_gb + 2 * GDN_HEADS
    o_mo = o_mqkv + 3 * REC_W
    o_mi = o_mo + REC_W
    gate_cols = jnp.concatenate([w_in[:, o_gb:o_gb + 2 * GDN_HEADS], w_in[:, o_mi:o_mi + 2 * ML_HEADS]], axis=1)
    used = COL_GATE + gate_cols.shape[1]
    w_al = jnp.concatenate([w_in[:, o_gqkv:o_gqkv + 3 * REC_W], w_in[:, o_mqkv:o_mqkv + 3 * REC_W],
                            w_in[:, o_gz:o_gz + REC_W], w_in[:, o_mo:o_mo + REC_W],
                            w_in[:, o_sb:o_sb + 3 * SB_W], gate_cols,
                            jnp.zeros((d, PROJ_W - used), w_in.dtype)], axis=1).astype(BF16)
    prow = jnp.zeros((SUBLANES, LANES), F32)
    prow = prow.at[0, GATE_G:GATE_G + GDN_HEADS].set(p['gdn_dt_bias'])
    prow = prow.at[0, GATE_I:GATE_I + ML_HEADS].set(p['mlstm_i_bias'])
    prow = prow.at[0, GATE_F:GATE_F + ML_HEADS].set(p['mlstm_f_bias'])
    prow = prow.at[1, GATE_G:GATE_G + GDN_HEADS].set(p['gdn_A_log'])
    conv_w8 = jnp.zeros((SUBLANES, 3 * REC_W), F32).at[0:CONV_W].set(p['gdn_conv_w'])
    return dict(
        w_al=w_al, prow=prow, pcol=prow.T, conv_w8=conv_w8,
        gnw=jnp.tile(p['gdn_norm_w'], GDN_HEADS)[None, :], mnw=p['mlstm_norm_w'][None, :],
        w_out=p['w_out'].astype(BF16), w_cq=p['w_cq'].astype(BF16), w_co=p['w_co'].astype(BF16),
        w_up=p['w_up'].astype(BF16), w_down=p['w_down'].astype(BF16),
        ln1=(p['ln1_g'][None, :], p['ln1_b'][None, :]), ln2=(p['ln2_g'][None, :], p['ln2_b'][None, :]),
        ln3=(p['ln3_g'][None, :], p['ln3_b'][None, :]))


def _head_consts():
    hid = np.arange(REC_W) // HEAD_DIM
    return jnp.asarray(hid[:, None] == hid[None, :], BF16)


def _sb_layout(k, v, tk):
    b, tkv, _ = k.shape
    nkb = tkv // tk
    kt = k.reshape(b, nkb, tk, SB_W).transpose(0, 1, 3, 2)
    vh = v.reshape(b, nkb, tk, SB_HEADS, HEAD_DIM).transpose(0, 3, 1, 2, 4)
    vh = vh.reshape(b * SB_HEADS, nkb, tk, HEAD_DIM)
    return kt.astype(BF16), vh.astype(BF16)


def _to_pair_lanes(a):
    z = jnp.zeros_like(a)
    even = (jnp.arange(a.shape[1]) % 2 == 0)[None, :, None, None]
    return jnp.where(even, jnp.concatenate([a, z], axis=-1), jnp.concatenate([z, a], axis=-1))


def _from_pair_lanes(a):
    even = (jnp.arange(a.shape[1]) % 2 == 0)[None, :, None, None]
    return jnp.where(even, a[..., :HEAD_DIM], a[..., HEAD_DIM:])


def _time_on_lanes(a, groups, width):
    return a.reshape(groups, width, a.shape[1]).transpose(0, 2, 1)


def _trunk_layer(x, xb, pk, e_heads, mk, mv, b, t, cfg, sb_past, conv_ctx, gdn_s, ml_c, ml_n, ml_m):
    n, d = x.shape
    l = min(t, CHUNK)
    nc = t // l
    cb = min(cfg['scan_cb'], nc)
    ng = b * nc // cb
    tk = cfg['sb_tk']
    ctx8 = jnp.pad(conv_ctx, ((0, 0), (SUBLANES - (CONV_W - 1), 0), (0, 0)))
    if sb_past is None:
        proj, kt, vh, mk_t, gate_t, gqkv_act, gk_t = _proj(x if xb is None else xb, pk['w_al'], ctx8,
                                                           pk['conv_w8'], e_heads, b, t, cfg['tm'])
        kt = kt.reshape(b, t // tk, SB_W, tk)
        q_off = 0
    else:
        proj = _matmul(xb, pk['w_al'], cfg['tm'], 1024)
    sk = proj[:, COL_SB + SB_W:COL_SB + 2 * SB_W].reshape(b, t, SB_W)
    sv = proj[:, COL_SB + 2 * SB_W:COL_SB + 3 * SB_W].reshape(b, t, SB_W)
    if sb_past is not None:
        past_k, past_v = sb_past
        q_off = past_k.shape[1]
        kv_len = q_off + t
        kv_pad = -(-kv_len // tk) * tk - kv_len
        kk = jnp.pad(jnp.concatenate([past_k.reshape(b, q_off, SB_W), sk], axis=1), ((0, 0), (0, kv_pad), (0, 0)))
        vv = jnp.pad(jnp.concatenate([past_v.reshape(b, q_off, SB_W), sv], axis=1), ((0, 0), (0, kv_pad), (0, 0)))
        kt, vh = _sb_layout(kk, vv, tk)
        gqkv_act = _gdn_pre(proj, ctx8, pk['conv_w8'], e_heads, b, t, cfg['conv_tt'])
        gate_t = _time_on_lanes(proj[:, COL_GATE:COL_GATE + LANES], b * nc, l)
        gk_t = _time_on_lanes(gqkv_act[:, REC_W:2 * REC_W], b * nc, l)
        mk_t = _time_on_lanes(proj[:, COL_MQKV + REC_W:COL_MQKV + 2 * REC_W], b * nc, l)

    o_sb = _sb_attn(proj, kt, vh, b, t, SB_HEADS, cfg['sb_tq'], q_off)
    gdn_scan = _gdn_scan_pair if l == HEAD_DIM else _gdn_scan
    o_g, gdn_s_new = gdn_scan(gqkv_act, gk_t, proj, gate_t, pk['prow'], pk['pcol'], gdn_s, b, t, l,
                              min(cfg['gdn_cb'], nc))
    m0 = jnp.pad(ml_m, ((0, 0), (0, LANES - ML_HEADS)))[:, None, :]
    if sb_past is None:
        c0 = _to_pair_lanes(jnp.swapaxes(ml_c, -1, -2))
        n0 = jnp.pad(_to_pair_lanes(ml_n[:, :, None, :])[:, :, 0, :], ((0, 0), (0, SUBLANES - ML_HEADS), (0, 0)))
        h_m, c_new, n_new, m_new = _ml_scan_t(proj, mk_t, gate_t, pk['prow'], pk['pcol'], c0, n0, m0, b, t, l, cb)
        c_new = jnp.swapaxes(_from_pair_lanes(c_new), -1, -2)
        n_new = _from_pair_lanes(n_new[:, :ML_HEADS, None, :])[:, :, 0, :]
    else:
        n0 = jnp.pad(ml_n, ((0, 0), (0, SUBLANES - ML_HEADS), (0, 0)))
        h_m, c_new, n_new, m_new = _ml_scan(proj, mk_t, gate_t, pk['prow'], pk['pcol'], ml_c, n0, m0, b, t, l, cb)
        n_new = n_new[:, :ML_HEADS, :]

    x1, x1b = _post(x, o_sb, o_g, h_m, proj, pk['w_out'], e_heads, pk['gnw'], pk['mnw'], *pk['ln1'], cfg['post_tm'])
    x2, x2b = _memattn(x1, x1b, pk['w_cq'], pk['w_co'], mk, mv, *pk['ln2'], b, t, cfg['mem_tm'])
    x3, x3b = _ffn(x2, x2b, pk['w_up'], pk['w_down'], *pk['ln3'], cfg['tm'], cfg['ffn_tf'])

    new_conv = proj.reshape(b, t, PROJ_W)[:, t - (CONV_W - 1):, COL_GQKV:COL_GQKV + 3 * REC_W]
    return (x3, x3b, sk.reshape(b, t, SB_HEADS, HEAD_DIM), sv.reshape(b, t, SB_HEADS, HEAD_DIM), new_conv,
            gdn_s_new, c_new, n_new, m_new[:, 0, :ML_HEADS])


def kernel(x_prompt, x_sample, cache_sb_k, cache_sb_v, cache_gdn_conv, state_gdn, state_mlstm_C, state_mlstm_n, state_mlstm_m, cache_mem_k, cache_mem_v, mem_prompt, w_in, gdn_conv_w, gdn_A_log, gdn_dt_bias, gdn_norm_w, mlstm_i_bias, mlstm_f_bias, mlstm_norm_w, w_out, ln1_g, ln1_b, w_cq, w_ckv, w_co, ln2_g, ln2_b, w_up, w_down, ln3_g, ln3_b):
    bp, tp, d = x_prompt.shape
    bs, ts, _ = x_sample.shape
    depth = w_in.shape[0]
    n_mem = mem_prompt.shape[1]
    dff = w_down.shape[1]
    cfg_p = dict(tm=512, post_tm=512, sb_tq=512, sb_tk=256, conv_tt=512, mem_tm=512, ffn_tf=dff // 2, scan_cb=4, gdn_cb=8)
    cfg_s = dict(tm=bs * ts, post_tm=bs * ts, sb_tq=ts, sb_tk=256, conv_tt=ts, mem_tm=ts, ffn_tf=dff // 2, scan_cb=1, gdn_cb=1)
    e_heads = _head_consts()

    xp = x_prompt.reshape(bp * tp, d)
    xs = x_sample.reshape(bs * ts, d)
    xpb, xsb = None, xs.astype(BF16)
    mem_b = mem_prompt.reshape(bp * n_mem, d).astype(BF16)
    p_out = [[] for _ in range(9)]
    s_out = [[] for _ in range(7)]
    for li in range(depth):
        p = {'w_in': w_in[li], 'gdn_conv_w': gdn_conv_w[li], 'gdn_A_log': gdn_A_log[li],
             'gdn_dt_bias': gdn_dt_bias[li], 'gdn_norm_w': gdn_norm_w[li], 'mlstm_i_bias': mlstm_i_bias[li],
             'mlstm_f_bias': mlstm_f_bias[li], 'mlstm_norm_w': mlstm_norm_w[li], 'w_out': w_out[li],
             'ln1_g': ln1_g[li], 'ln1_b': ln1_b[li], 'w_cq': w_cq[li], 'w_co': w_co[li],
             'ln2_g': ln2_g[li], 'ln2_b': ln2_b[li], 'w_up': w_up[li], 'w_down': w_down[li],
             'ln3_g': ln3_g[li], 'ln3_b': ln3_b[li]}
        pk = _pack_layer(p)
        mkv = _matmul(mem_b, w_ckv[li].astype(BF16), n_mem, 1024)
        mk_p = mkv[:, :d].reshape(bp, n_mem, d)
        mv_p = mkv[:, d:].reshape(bp, n_mem, d)
        res = _trunk_layer(xp, xpb, pk, e_heads, mk_p.astype(BF16), mv_p.astype(BF16), bp, tp, cfg_p, None,
                           jnp.zeros((bp, CONV_W - 1, 3 * REC_W), F32),
                           jnp.zeros((bp, GDN_HEADS, HEAD_DIM, HEAD_DIM), F32),
                           jnp.zeros((bp, ML_HEADS, HEAD_DIM, HEAD_DIM), F32),
                           jnp.zeros((bp, ML_HEADS, HEAD_DIM), F32),
                           jnp.zeros((bp, ML_HEADS), F32))
        xp, xpb = res[0], res[1]
        hd = d // MEM_HEADS
        for j, a in enumerate(res[2:] + (mk_p.reshape(bp, n_mem, MEM_HEADS, hd), mv_p.reshape(bp, n_mem, MEM_HEADS, hd))):
            p_out[j].append(a)
        res = _trunk_layer(xs, xsb, pk, e_heads,
                           cache_mem_k[li].reshape(bs, n_mem, d).astype(BF16),
                           cache_mem_v[li].reshape(bs, n_mem, d).astype(BF16),
                           bs, ts, cfg_s, (cache_sb_k[li], cache_sb_v[li]), cache_gdn_conv[li],
                           state_gdn[li], state_mlstm_C[li], state_mlstm_n[li], state_mlstm_m[li])
        xs, xsb = res[0], res[1]
        for j, a in enumerate(res[2:]):
            s_out[j].append(a)
    p_st = [jnp.stack(a) for a in p_out]
    s_st = [jnp.stack(a) for a in s_out]
    return (xp.reshape(bp, tp, d), xs.reshape(bs, ts, d), *p_st, *s_st)
```

```python
import functools

import numpy as np
import jax
import jax.numpy as jnp
from jax import lax
from jax.experimental import pallas as pl
from jax.experimental.pallas import tpu as pltpu

F32 = jnp.float32
BF16 = jnp.bfloat16

HEAD_DIM = 64
SB_HEADS = 4
GDN_HEADS = 6
ML_HEADS = 6
REC_W = GDN_HEADS * HEAD_DIM
SB_W = SB_HEADS * HEAD_DIM
CONV_W = 4
CHUNK = 64
MEM_HEADS = 4
LN_EPS = 1e-5
NORM_EPS = 1e-6
DEPTH = 2
DEEPNORM_ALPHA = (2 * DEPTH) ** 0.25
LOG2E = float(np.log2(np.e))
SOFTPLUS2_CLAMP = 120.0
NORM_SUM_PARTS = 1
F32_EXP2_ZERO = 160.0
BOUND_SLACK = 1.01

GROUP = 256
LANES = 128
SUBLANES = 8
VMEM_LIMIT = 56 * 1024 * 1024

PROJ_W = 4096
COL_GQKV = 0
COL_MQKV = 1152
COL_GZ = 2304
COL_MO = 2688
COL_SB = 3072
COL_GATE = 3840
GATE_BETA, GATE_G, GATE_I, GATE_F = 0, 6, 12, 18


def _cparams(*sem):
    return pltpu.CompilerParams(dimension_semantics=sem, vmem_limit_bytes=VMEM_LIMIT)


def _dot(a, b):
    return jnp.dot(a.astype(BF16), b.astype(BF16), preferred_element_type=F32)


def _dot_nt(a, b):
    return lax.dot_general(a.astype(BF16), b.astype(BF16), (((1,), (1,)), ((), ())),
                           preferred_element_type=F32)


def _split(x, parts):
    out = []
    r = x
    for _ in range(parts - 1):
        p = r.astype(BF16)
        out.append(p)
        r = r - p.astype(F32)
    out.append(r.astype(BF16))
    return out


def _dot_exact_lhs(m, x, parts):
    acc = None
    for p in _split(x, parts):
        t = jnp.dot(m, p, preferred_element_type=F32)
        acc = t if acc is None else acc + t
    return acc


def _dot_exact_rhs(x, m, parts):
    acc = None
    for p in _split(x, parts):
        t = jnp.dot(p, m, preferred_element_type=F32)
        acc = t if acc is None else acc + t
    return acc


def _ones_where(mask, dtype):
    return jnp.where(mask, 1.0, 0.0).astype(dtype)


def _sigmoid(x):
    return 1.0 / (1.0 + jnp.exp(-x))


def _layer_norm(y, g, b):
    mu = jnp.mean(y, axis=-1, keepdims=True)
    d = y - mu
    var = jnp.mean(d * d, axis=-1, keepdims=True)
    return d * lax.rsqrt(var + LN_EPS) * g + b


def _mm_kernel(x_ref, w_ref, o_ref):
    o_ref[...] = jnp.dot(x_ref[...], w_ref[...], preferred_element_type=F32).astype(o_ref.dtype)


def _matmul(x, w, tm, tn, out_dtype=F32):
    n, k = x.shape
    m = w.shape[1]
    return pl.pallas_call(
        _mm_kernel,
        grid=(n // tm, m // tn),
        in_specs=[pl.BlockSpec((tm, k), lambda i, j: (i, 0)),
                  pl.BlockSpec((k, tn), lambda i, j: (0, j))],
        out_specs=pl.BlockSpec((tm, tn), lambda i, j: (i, j)),
        out_shape=jax.ShapeDtypeStruct((n, m), out_dtype),
        compiler_params=_cparams("parallel", "parallel"),
        name="proj_matmul",
    )(x, w)


def _gdn_activate(x, first, ctx_ref, w_ref, e_ref, o_ref, kt_ref, xbuf, tt):
    @pl.when(first)
    def _():
        xbuf[0:SUBLANES, :] = ctx_ref[0]

    xbuf[SUBLANES:SUBLANES + tt, :] = x
    acc = w_ref[CONV_W - 1:CONV_W, :] * xbuf[SUBLANES:SUBLANES + tt, :]
    for j in range(CONV_W - 1):
        off = SUBLANES - (CONV_W - 1) + j
        acc = acc + w_ref[j:j + 1, :] * xbuf[off:off + tt, :]
    y = acc * _sigmoid(acc)
    q = y[:, 0:REC_W]
    k = y[:, REC_W:2 * REC_W]
    e = e_ref[...]
    qs = _dot_exact_rhs(q * q, e, NORM_SUM_PARTS)
    ks = _dot_exact_rhs(k * k, e, NORM_SUM_PARTS)
    kn = k * lax.rsqrt(ks + NORM_EPS)
    o_ref[:, 0:REC_W] = q * lax.rsqrt(qs + NORM_EPS) * (HEAD_DIM ** -0.5)
    o_ref[:, REC_W:2 * REC_W] = kn
    o_ref[:, 2 * REC_W:3 * REC_W] = y[:, 2 * REC_W:3 * REC_W]
    if kt_ref is not None:
        for r in range(tt // GROUP):
            _store_chunks_t(kt_ref, r, kn[r * GROUP:(r + 1) * GROUP, :])
    xbuf[0:SUBLANES, :] = xbuf[tt:tt + SUBLANES, :]


def _proj_kernel(x_ref, w_ref, ctx_ref, cw_ref, e_ref, o_ref, skt_ref, sv_ref, mkt_ref, gtt_ref, act_ref, gkt_ref,
                 xbuf, *, tm, tn):
    xb = x_ref[...].astype(BF16)
    for j in range(PROJ_W // tn):
        o_ref[:, j * tn:(j + 1) * tn] = jnp.dot(xb, w_ref[:, j * tn:(j + 1) * tn], preferred_element_type=F32)
    for r in range(tm // GROUP):
        rows = slice(r * GROUP, (r + 1) * GROUP)
        skt_ref[r] = o_ref[rows, COL_SB + SB_W:COL_SB + 2 * SB_W].T.astype(BF16)
        for h in range(SB_HEADS):
            c0 = COL_SB + 2 * SB_W + h * HEAD_DIM
            sv_ref[h, r] = o_ref[rows, c0:c0 + HEAD_DIM].astype(BF16)
        _store_chunks_t(mkt_ref, r, o_ref[rows, COL_MQKV:COL_MQKV + 3 * REC_W])
        _store_chunks_t(gtt_ref, r, o_ref[rows, COL_GATE:COL_GATE + LANES])
    _gdn_activate(o_ref[:, COL_GQKV:COL_GQKV + 3 * REC_W], pl.program_id(1) == 0, ctx_ref, cw_ref, e_ref,
                  act_ref, gkt_ref, xbuf, tm)


def _store_chunks_t(ref, r, a):
    at = a.T
    per = GROUP // CHUNK
    for j in range(per):
        ref[r * per + j] = at[:, j * CHUNK:(j + 1) * CHUNK]


def _proj(x, w, ctx8, conv_w8, e_heads, b, t, tm):
    n, d = x.shape
    nt = t // tm
    ng = n // GROUP
    gpt = tm // GROUP
    cpt = tm // CHUNK
    w3 = 3 * REC_W
    row = lambda bi, i: (bi * nt + i, 0)
    row3 = lambda bi, i: (bi * nt + i, 0, 0)
    fixed = lambda bi, i: (0, 0)
    return pl.pallas_call(
        functools.partial(_proj_kernel, tm=tm, tn=1024),
        grid=(b, nt),
        in_specs=[pl.BlockSpec((tm, d), row),
                  pl.BlockSpec((d, PROJ_W), fixed, pipeline_mode=pl.Buffered(1)),
                  pl.BlockSpec((1, SUBLANES, w3), lambda bi, i: (bi, 0, 0)),
                  pl.BlockSpec((SUBLANES, w3), fixed),
                  pl.BlockSpec((REC_W, REC_W), fixed)],
        out_specs=[pl.BlockSpec((tm, PROJ_W), row),
                   pl.BlockSpec((gpt, SB_W, GROUP), row3),
                   pl.BlockSpec((SB_HEADS, gpt, GROUP, HEAD_DIM), lambda bi, i: (0, bi * nt + i, 0, 0)),
                   pl.BlockSpec((cpt, w3, CHUNK), row3),
                   pl.BlockSpec((cpt, LANES, CHUNK), row3),
                   pl.BlockSpec((tm, w3), row),
                   pl.BlockSpec((cpt, REC_W, CHUNK), row3)],
        out_shape=[jax.ShapeDtypeStruct((n, PROJ_W), F32),
                   jax.ShapeDtypeStruct((ng, SB_W, GROUP), BF16),
                   jax.ShapeDtypeStruct((SB_HEADS, ng, GROUP, HEAD_DIM), BF16),
                   jax.ShapeDtypeStruct((n // CHUNK, w3, CHUNK), F32),
                   jax.ShapeDtypeStruct((n // CHUNK, LANES, CHUNK), F32),
                   jax.ShapeDtypeStruct((n, w3), F32),
                   jax.ShapeDtypeStruct((n // CHUNK, REC_W, CHUNK), F32)],
        scratch_shapes=[pltpu.VMEM((tm + SUBLANES, w3), F32)],
        compiler_params=_cparams("parallel", "arbitrary"),
        name="proj_full",
    )(x, w, ctx8, conv_w8, e_heads)


def _gdn_pre_kernel(x_ref, ctx_ref, w_ref, e_ref, o_ref, xbuf, *, tt):
    _gdn_activate(x_ref[...], pl.program_id(1) == 0, ctx_ref, w_ref, e_ref, o_ref, None, xbuf, tt)


def _gdn_pre(proj, ctx8, conv_w8, e_heads, b, t, tt):
    w3 = 3 * REC_W
    nt = t // tt
    return pl.pallas_call(
        functools.partial(_gdn_pre_kernel, tt=tt),
        grid=(b, nt),
        in_specs=[pl.BlockSpec((tt, w3), lambda bi, i: (bi * nt + i, COL_GQKV // w3)),
                  pl.BlockSpec((1, SUBLANES, w3), lambda bi, i: (bi, 0, 0)),
                  pl.BlockSpec((SUBLANES, w3), lambda bi, i: (0, 0)),
                  pl.BlockSpec((REC_W, REC_W), lambda bi, i: (0, 0))],
        out_specs=pl.BlockSpec((tt, w3), lambda bi, i: (bi * nt + i, 0)),
        out_shape=jax.ShapeDtypeStruct((b * t, w3), F32),
        scratch_shapes=[pltpu.VMEM((tt + SUBLANES, w3), F32)],
        compiler_params=_cparams("parallel", "arbitrary"),
        name="gdn_pre",
    )(proj, ctx8, conv_w8, e_heads)


def _sb_kernel(q_ref, kt_ref, v_ref, o_ref, acc_ref, c_ref, zmax_ref, kmax_ref,
               *, hb, tq, tk, nkb, q_off, diag_static):
    qi = pl.program_id(1)
    q_lo = q_off + qi * tq
    nblk = jnp.minimum(nkb, (q_lo + tq - 1 + tk - 1) // tk)
    nfull = jnp.minimum(nblk, q_lo // tk)
    scale = HEAD_DIM ** -0.5 * LOG2E
    qs = [(q_ref[:, h * HEAD_DIM:(h + 1) * HEAD_DIM] * scale).astype(BF16) for h in range(hb)]
    lower_incl = _ones_where(lax.broadcasted_iota(jnp.int32, (tk, tk), 0)
                             >= lax.broadcasted_iota(jnp.int32, (tk, tk), 1), BF16)
    acc_ref[...] = jnp.zeros_like(acc_ref)
    c_ref[...] = jnp.zeros_like(c_ref)

    @pl.when(qi == 0)
    def _():
        def knorm(kb, best):
            k32 = kt_ref[0, kb].astype(F32)
            sq = k32 * k32
            return tuple(jnp.maximum(best[h], jnp.sum(sq[h * HEAD_DIM:(h + 1) * HEAD_DIM, :], axis=0, keepdims=True))
                         for h in range(hb))
        best = lax.fori_loop(0, nkb, knorm, tuple(jnp.zeros((1, tk), F32) for _ in range(hb)))
        for h in range(hb):
            kmax_ref[h] = jnp.broadcast_to(jnp.sqrt(jnp.max(best[h], axis=1, keepdims=True)), (1, LANES))

    for h in range(hb):
        q32 = qs[h].astype(F32)
        qn = jnp.sqrt(jnp.sum(q32 * q32, axis=1, keepdims=True))
        zmax_ref[h] = qn * kmax_ref[h] * BOUND_SLACK

    def all_weights_vanish():
        margin = c_ref[0] - zmax_ref[0]
        for h in range(1, hb):
            margin = jnp.minimum(margin, c_ref[h] - zmax_ref[h])
        return jnp.min(margin) > F32_EXP2_ZERO

    def blocks(kbs, r0=0, valids=None):
        for h in range(hb):
            q = qs[h][r0:, :]
            c = c_ref[h, r0:, :]
            av = None
            for i, kb in enumerate(kbs):
                kt = kt_ref[0, kb, h * HEAD_DIM:(h + 1) * HEAD_DIM, :]
                z = jnp.dot(q, kt, preferred_element_type=F32)
                lneg = jnp.maximum(z, jnp.log2(1.0 + jnp.exp2(jnp.minimum(z, SOFTPLUS2_CLAMP))))
                if valids is not None:
                    lneg = jnp.where(valids[i], lneg, 0.0)
                incl = jnp.dot(lneg.astype(BF16), lower_incl, preferred_element_type=F32)
                a = jnp.exp2(z - incl - jnp.tile(c, (1, tk // LANES)))
                if valids is not None:
                    a = jnp.where(valids[i], a, 0.0)
                t = jnp.dot(a.astype(BF16), v_ref[h, kb], preferred_element_type=F32)
                av = t if av is None else av + t
                c = c + incl[:, 0:1]
            acc_ref[h, r0:, :] += av
            c_ref[h, r0:, :] = c

    if diag_static:
        for r in reversed(range(tq // tk)):
            rows = tq - r * tk
            valid = (lax.broadcasted_iota(jnp.int32, (rows, tk), 1)
                     < lax.broadcasted_iota(jnp.int32, (rows, tk), 0))
            blocks([nfull + r], r0=r * tk, valids=[valid])
    else:
        @pl.loop(0, nblk - nfull)
        def _(i):
            kb = nblk - 1 - i
            valid = (kb * tk + lax.broadcasted_iota(jnp.int32, (tq, tk), 1)
                     < q_lo + lax.broadcasted_iota(jnp.int32, (tq, tk), 0))
            blocks([kb], valids=[valid])

    def more_blocks(carry):
        i, done = carry
        return jnp.logical_and(i < nfull, done == 0)

    def next_block(carry):
        i, _ = carry
        blocks([nfull - 1 - i])
        return i + 1, all_weights_vanish().astype(jnp.int32)

    lax.while_loop(more_blocks, next_block, (jnp.int32(0), all_weights_vanish().astype(jnp.int32)))

    for h in range(hb):
        o_ref[:, h * HEAD_DIM:(h + 1) * HEAD_DIM] = acc_ref[h]


def _sb_attn(proj, kt, vh, b, t, hb, tq, q_off):
    _, nkb, w, tk = kt.shape
    d = w // hb
    nq = t // tq
    diag_static = q_off % tk == 0 and tq % tk == 0 and q_off + t <= nkb * tk
    return pl.pallas_call(
        functools.partial(_sb_kernel, hb=hb, tq=tq, tk=tk, nkb=nkb, q_off=q_off, diag_static=diag_static),
        grid=(b, nq),
        in_specs=[pl.BlockSpec((tq, w), lambda bi, i: (bi * nq + i, COL_SB // w)),
                  pl.BlockSpec((1, nkb, w, tk), lambda bi, i: (bi, 0, 0, 0), pipeline_mode=pl.Buffered(1)),
                  pl.BlockSpec((hb, nkb, tk, d), lambda bi, i: (bi, 0, 0, 0), pipeline_mode=pl.Buffered(1))],
        out_specs=pl.BlockSpec((tq, w), lambda bi, i: (bi * nq + i, 0)),
        out_shape=jax.ShapeDtypeStruct((b * t, w), F32),
        scratch_shapes=[pltpu.VMEM((hb, tq, d), F32), pltpu.VMEM((hb, tq, LANES), F32),
                        pltpu.VMEM((hb, tq, LANES), F32), pltpu.VMEM((hb, 1, LANES), F32)],
        compiler_params=_cparams("parallel", "arbitrary"),
        name="sb_attn",
    )(proj, kt, vh)


def _gate_values(pre, neg_a, lane_id):
    sp = jnp.log1p(jnp.exp(-jnp.abs(pre)))
    softplus = jnp.maximum(pre, 0.0) + sp
    log_sig = jnp.minimum(pre, 0.0) - sp
    return jnp.where(lane_id < GATE_G, _sigmoid(pre),
                     jnp.where(lane_id < GATE_I, neg_a * softplus,
                               jnp.where(lane_id < GATE_F, pre, log_sig)))


def _stack(a, b):
    return jnp.concatenate([a, b], axis=0)


def _gates(gt, gtt, prow_ref, pcol_ref, l):
    lane = lax.broadcasted_iota(jnp.int32, (l, LANES), 1)
    val = _gate_values(gt + prow_ref[0:1, :], -jnp.exp(prow_ref[1:2, :]), lane)
    sub = lax.broadcasted_iota(jnp.int32, (LANES, l), 0)
    val_t = _gate_values(gtt + pcol_ref[:, 0:1], -jnp.exp(pcol_ref[:, 1:2]), sub)
    r = lax.broadcasted_iota(jnp.int32, (l, l), 0)
    c = lax.broadcasted_iota(jnp.int32, (l, l), 1)
    csum = _dot_exact_lhs(_ones_where(r >= c, BF16), val, 3)
    csum_t = _dot_exact_rhs(val_t, _ones_where(r <= c, BF16), 3)
    return val, csum, val_t, csum_t


def _gdn_scan_kernel(qkv_ref, kt_ref, gt_ref, gtt_ref, prow_ref, pcol_ref, s0_ref, o_ref, s_out_ref, s_scr,
                     *, l, cb, ng):
    gi = pl.program_id(1)

    @pl.when(gi == 0)
    def _():
        s_scr[...] = s0_ref[0]

    r = lax.broadcasted_iota(jnp.int32, (l, l), 0)
    c = lax.broadcasted_iota(jnp.int32, (l, l), 1)
    tri = r >= c
    stri = r > c
    eye = _ones_where(r == c, F32)
    n_double = max(int(np.ceil(np.log2(l))) - 1, 0)
    hs = range(GDN_HEADS)
    ps = [(j, h) for j in range(cb) for h in hs]
    gates = [_gates(gt_ref[j * l:(j + 1) * l, :], gtt_ref[j], prow_ref, pcol_ref, l) for j in range(cb)]

    def head_cols(j, group, h):
        return qkv_ref[j * l:(j + 1) * l, group * REC_W + h * HEAD_DIM:group * REC_W + (h + 1) * HEAD_DIM]

    q = {p: head_cols(p[0], 0, p[1]) for p in ps}
    k = {p: head_cols(p[0], 1, p[1]) for p in ps}
    v = {p: head_cols(p[0], 2, p[1]) for p in ps}
    kt = {p: kt_ref[p[0], p[1] * HEAD_DIM:(p[1] + 1) * HEAD_DIM, :] for p in ps}
    beta = {p: gates[p[0]][0][:, GATE_BETA + p[1]:GATE_BETA + p[1] + 1] for p in ps}
    g_col = {p: gates[p[0]][1][:, GATE_G + p[1]:GATE_G + p[1] + 1] for p in ps}
    g_row = {p: gates[p[0]][3][GATE_G + p[1]:GATE_G + p[1] + 1, :] for p in ps}
    g_last = {p: g_col[p][l - 1:l, :] for p in ps}
    decay = {p: jnp.where(tri, jnp.exp(jnp.where(tri, g_col[p] - g_row[p], 0.0)), 0.0) for p in ps}
    kb = {p: k[p] * beta[p] for p in ps}
    e_g = {p: jnp.exp(g_col[p]) for p in ps}
    kq = {p: _dot(_stack(kb[p], q[p]), kt[p]) for p in ps}
    x = {p: -jnp.where(stri, kq[p][:l] * decay[p], 0.0) for p in ps}
    tinv = {p: eye + x[p] for p in ps}
    if n_double > 0:
        x = {p: _dot(x[p], x[p]) for p in ps}
        for _ in range(n_double - 1):
            tx = {p: _dot(_stack(tinv[p], x[p]), x[p]) for p in ps}
            tinv = {p: tinv[p] + tx[p][:l] for p in ps}
            x = {p: tx[p][l:] for p in ps}
        tinv = {p: tinv[p] + _dot(tinv[p], x[p]) for p in ps}
    sol_v = {p: _dot(tinv[p], v[p] * beta[p]) for p in ps}
    sol_k = {p: _dot(tinv[p], kb[p] * e_g[p]) for p in ps}
    lhs_s = {p: _stack(q[p] * e_g[p], sol_k[p]) for p in ps}
    lhs_u = {p: _stack(kt[p] * jnp.exp(g_last[p] - g_row[p]), kq[p][l:] * decay[p]) for p in ps}
    e_last = {p: jnp.exp(g_last[p]) for p in ps}
    s = [s_scr[h] for h in hs]
    for j in range(cb):
        t = [_dot(lhs_s[(j, h)], s[h]) for h in hs]
        u = [sol_v[(j, h)] - t[h][l:] for h in hs]
        w = [_dot(lhs_u[(j, h)], u[h]) for h in hs]
        s = [e_last[(j, h)] * s[h] + w[h][:HEAD_DIM] for h in hs]
        for h in hs:
            o_ref[j * l:(j + 1) * l, h * HEAD_DIM:(h + 1) * HEAD_DIM] = t[h][:l] + w[h][HEAD_DIM:]
    for h in hs:
        s_scr[h] = s[h]

    @pl.when(gi == ng - 1)
    def _():
        s_out_ref[0] = s_scr[...]


def _gdn_scan(qkv, k_t, proj, gate_t, prow, pcol, s0, b, t, l, cb):
    ng = t // (l * cb)
    w3 = 3 * REC_W
    return pl.pallas_call(
        functools.partial(_gdn_scan_kernel, l=l, cb=cb, ng=ng),
        grid=(b, ng),
        in_specs=[pl.BlockSpec((cb * l, w3), lambda bi, i: (bi * ng + i, 0)),
                  pl.BlockSpec((cb, REC_W, l), lambda bi, i: (bi * ng + i, 0, 0)),
                  pl.BlockSpec((cb * l, LANES), lambda bi, i: (bi * ng + i, COL_GATE // LANES)),
                  pl.BlockSpec((cb, LANES, l), lambda bi, i: (bi * ng + i, 0, 0)),
                  pl.BlockSpec((SUBLANES, LANES), lambda bi, i: (0, 0)),
                  pl.BlockSpec((LANES, SUBLANES), lambda bi, i: (0, 0)),
                  pl.BlockSpec((1, GDN_HEADS, HEAD_DIM, HEAD_DIM), lambda bi, i: (bi, 0, 0, 0))],
        out_specs=[pl.BlockSpec((cb * l, REC_W), lambda bi, i: (bi * ng + i, 0)),
                   pl.BlockSpec((1, GDN_HEADS, HEAD_DIM, HEAD_DIM), lambda bi, i: (bi, 0, 0, 0))],
        out_shape=[jax.ShapeDtypeStruct((b * t, REC_W), F32),
                   jax.ShapeDtypeStruct((b, GDN_HEADS, HEAD_DIM, HEAD_DIM), F32)],
        scratch_shapes=[pltpu.VMEM((GDN_HEADS, HEAD_DIM, HEAD_DIM), F32)],
        compiler_params=_cparams("parallel", "arbitrary"),
        name="gdn_scan",
    )(qkv, k_t, proj, gate_t, prow, pcol, s0)


def _gdn_scan_pair_kernel(qkv_ref, kt_ref, gt_ref, gtt_ref, prow_ref, pcol_ref, s0_ref, o_ref, s_out_ref, s_scr,
                          *, l, cb, ng):
    gi = pl.program_id(1)
    npair = GDN_HEADS // 2
    row128 = lax.broadcasted_iota(jnp.int32, (LANES, LANES), 0)
    col128 = lax.broadcasted_iota(jnp.int32, (LANES, LANES), 1)
    blockmask = (row128 < HEAD_DIM) == (col128 < HEAD_DIM)

    def bd(a):
        return jnp.where(blockmask, _stack(a, a), 0.0)

    @pl.when(gi == 0)
    def _():
        for p in range(npair):
            s_scr[p] = bd(jnp.concatenate([s0_ref[0, 2 * p], s0_ref[0, 2 * p + 1]], axis=1))

    trow = lax.broadcasted_iota(jnp.int32, (l, LANES), 0)
    tcol = lax.broadcasted_iota(jnp.int32, (l, LANES), 1) % HEAD_DIM
    left = lax.broadcasted_iota(jnp.int32, (l, LANES), 1) < HEAD_DIM
    tri = trow >= tcol
    stri = trow > tcol
    eye = _ones_where(trow == tcol, F32)
    krow_left = lax.broadcasted_iota(jnp.int32, (LANES, l), 0) < HEAD_DIM
    n_double = max(int(np.ceil(np.log2(l))) - 1, 0)
    ps = [(j, p) for j in range(cb) for p in range(npair)]
    gates = [_gates(gt_ref[j * l:(j + 1) * l, :], gtt_ref[j], prow_ref, pcol_ref, l) for j in range(cb)]

    def pair_cols(j, group, p):
        c0 = group * REC_W + p * LANES
        return qkv_ref[j * l:(j + 1) * l, c0:c0 + LANES]

    def col_pair(a, lane0, p):
        return jnp.where(left, a[:, lane0 + 2 * p:lane0 + 2 * p + 1], a[:, lane0 + 2 * p + 1:lane0 + 2 * p + 2])

    q = {x: pair_cols(x[0], 0, x[1]) for x in ps}
    k = {x: pair_cols(x[0], 1, x[1]) for x in ps}
    v = {x: pair_cols(x[0], 2, x[1]) for x in ps}
    ktp = {x: kt_ref[x[0], x[1] * LANES:(x[1] + 1) * LANES, :] for x in ps}
    beta = {x: col_pair(gates[x[0]][0], GATE_BETA, x[1]) for x in ps}
    g_col = {x: col_pair(gates[x[0]][1], GATE_G, x[1]) for x in ps}
    g_row_a = {x: gates[x[0]][3][GATE_G + 2 * x[1]:GATE_G + 2 * x[1] + 1, :] for x in ps}
    g_row_b = {x: gates[x[0]][3][GATE_G + 2 * x[1] + 1:GATE_G + 2 * x[1] + 2, :] for x in ps}
    g_row = {x: jnp.concatenate([g_row_a[x], g_row_b[x]], axis=1) for x in ps}
    g_last = {x: g_col[x][l - 1:l, :] for x in ps}
    decay = {x: jnp.where(tri, jnp.exp(jnp.where(tri, g_col[x] - g_row[x], 0.0)), 0.0) for x in ps}
    kb = {x: k[x] * beta[x] for x in ps}
    e_g = {x: jnp.exp(g_col[x]) for x in ps}
    bd_kt = {x: jnp.concatenate([jnp.where(krow_left, ktp[x], 0.0), jnp.where(krow_left, 0.0, ktp[x])], axis=1)
             for x in ps}
    kq = {x: _dot(_stack(kb[x], q[x]), bd_kt[x]) for x in ps}
    y = {x: -jnp.where(stri, kq[x][:l] * decay[x], 0.0) for x in ps}
    tinv = {x: eye + y[x] for x in ps}
    if n_double > 0:
        y = {x: _dot(y[x], bd(y[x])) for x in ps}
        for _ in range(n_double - 1):
            ty = {x: _dot(_stack(tinv[x], y[x]), bd(y[x])) for x in ps}
            tinv = {x: tinv[x] + ty[x][:l] for x in ps}
            y = {x: ty[x][l:] for x in ps}
        tinv = {x: tinv[x] + _dot(tinv[x], bd(y[x])) for x in ps}
    sol = {x: _dot(tinv[x], jnp.concatenate([bd(v[x] * beta[x]), bd(kb[x] * e_g[x])], axis=1)) for x in ps}
    lhs_s = {x: _stack(q[x] * e_g[x], sol[x][:, LANES:]) for x in ps}
    g_last_a = {x: g_last[x][:, 0:1] for x in ps}
    g_last_b = {x: g_last[x][:, HEAD_DIM:HEAD_DIM + 1] for x in ps}
    ktd = {x: ktp[x] * jnp.where(krow_left, jnp.exp(g_last_a[x] - g_row_a[x]), jnp.exp(g_last_b[x] - g_row_b[x]))
           for x in ps}
    attn = {x: kq[x][l:] * decay[x] for x in ps}
    e_last = {x: jnp.exp(g_last[x]) for x in ps}
    s = [s_scr[p] for p in range(npair)]
    for j in range(cb):
        t = [_dot(lhs_s[(j, p)], s[p]) for p in range(npair)]
        u = [sol[(j, p)][:, :LANES] - t[p][l:] for p in range(npair)]
        du = [_dot(ktd[(j, p)], u[p]) for p in range(npair)]
        ou = [_dot(attn[(j, p)], bd(u[p])) for p in range(npair)]
        s = [e_last[(j, p)] * s[p] + jnp.where(blockmask, du[p], 0.0) for p in range(npair)]
        for p in range(npair):
            o_ref[j * l:(j + 1) * l, p * LANES:(p + 1) * LANES] = t[p][:l] + ou[p]
    for p in range(npair):
        s_scr[p] = s[p]

    @pl.when(gi == ng - 1)
    def _():
        for p in range(npair):
            s_out_ref[0, 2 * p] = s_scr[p, 0:HEAD_DIM, 0:HEAD_DIM]
            s_out_ref[0, 2 * p + 1] = s_scr[p, HEAD_DIM:LANES, HEAD_DIM:LANES]


def _gdn_scan_pair(qkv, k_t, proj, gate_t, prow, pcol, s0, b, t, l, cb):
    ng = t // (l * cb)
    w3 = 3 * REC_W
    return pl.pallas_call(
        functools.partial(_gdn_scan_pair_kernel, l=l, cb=cb, ng=ng),
        grid=(b, ng),
        in_specs=[pl.BlockSpec((cb * l, w3), lambda bi, i: (bi * ng + i, 0)),
                  pl.BlockSpec((cb, REC_W, l), lambda bi, i: (bi * ng + i, 0, 0)),
                  pl.BlockSpec((cb * l, LANES), lambda bi, i: (bi * ng + i, COL_GATE // LANES)),
                  pl.BlockSpec((cb, LANES, l), lambda bi, i: (bi * ng + i, 0, 0)),
                  pl.BlockSpec((SUBLANES, LANES), lambda bi, i: (0, 0)),
                  pl.BlockSpec((LANES, SUBLANES), lambda bi, i: (0, 0)),
                  pl.BlockSpec((1, GDN_HEADS, HEAD_DIM, HEAD_DIM), lambda bi, i: (bi, 0, 0, 0))],
        out_specs=[pl.BlockSpec((cb * l, REC_W), lambda bi, i: (bi * ng + i, 0)),
                   pl.BlockSpec((1, GDN_HEADS, HEAD_DIM, HEAD_DIM), lambda bi, i: (bi, 0, 0, 0))],
        out_shape=[jax.ShapeDtypeStruct((b * t, REC_W), F32),
                   jax.ShapeDtypeStruct((b, GDN_HEADS, HEAD_DIM, HEAD_DIM), F32)],
        scratch_shapes=[pltpu.VMEM((GDN_HEADS // 2, LANES, LANES), F32)],
        compiler_params=_cparams("parallel", "arbitrary"),
        name="gdn_scan_pair",
    )(qkv, k_t, proj, gate_t, prow, pcol, s0)


def _ml_scan_kernel(qkv_ref, kt_ref, gt_ref, gtt_ref, prow_ref, pcol_ref, c0_ref, n0_ref, m0_ref,
                    o_ref, c_out_ref, n_out_ref, m_out_ref, c_scr, n_scr, m_scr, *, l, cb, ng):
    gi = pl.program_id(1)

    @pl.when(gi == 0)
    def _():
        c_scr[...] = c0_ref[0]
        n_scr[...] = n0_ref[0]
        m_scr[...] = m0_ref[0]

    r = lax.broadcasted_iota(jnp.int32, (l, l), 0)
    c = lax.broadcasted_iota(jnp.int32, (l, l), 1)
    tri = r >= c
    lane = lax.broadcasted_iota(jnp.int32, (1, LANES), 1)
    m_all = m_scr[...]
    hs = range(ML_HEADS)
    ps = [(j, h) for j in range(cb) for h in hs]
    kscale = HEAD_DIM ** -0.5
    gates = [_gates(gt_ref[j * l:(j + 1) * l, :], gtt_ref[j], prow_ref, pcol_ref, l) for j in range(cb)]

    def head_cols(j, group, h):
        return qkv_ref[j * l:(j + 1) * l, group * REC_W + h * HEAD_DIM:group * REC_W + (h + 1) * HEAD_DIM]

    q = {p: head_cols(p[0], 0, p[1]) for p in ps}
    k = {p: head_cols(p[0], 1, p[1]) * kscale for p in ps}
    v = {p: head_cols(p[0], 2, p[1]) for p in ps}
    kt = {p: kt_ref[p[0], p[1] * HEAD_DIM:(p[1] + 1) * HEAD_DIM, :] * kscale for p in ps}
    ig_col = {p: gates[p[0]][0][:, GATE_I + p[1]:GATE_I + p[1] + 1] for p in ps}
    ig_row = {p: gates[p[0]][2][GATE_I + p[1]:GATE_I + p[1] + 1, :] for p in ps}
    f_col = {p: gates[p[0]][1][:, GATE_F + p[1]:GATE_F + p[1] + 1] for p in ps}
    f_row = {p: gates[p[0]][3][GATE_F + p[1]:GATE_F + p[1] + 1, :] for p in ps}
    f_last = {p: f_col[p][l - 1:l, :] for p in ps}
    qk = {p: _dot(q[p], kt[p]) for p in ps}
    d = {p: jnp.where(tri, f_col[p] - f_row[p] + ig_row[p], -jnp.inf) for p in ps}
    d_max = {p: jnp.max(d[p], axis=1, keepdims=True) for p in ps}
    m_prev, m_t = {}, {}
    m_run = [m_all[:, h:h + 1] for h in hs]
    for j in range(cb):
        for h in hs:
            m_prev[(j, h)] = m_run[h]
            m_t[(j, h)] = jnp.maximum(f_col[(j, h)] + m_run[h], d_max[(j, h)])
            m_run[h] = m_t[(j, h)][l - 1:l, :]
    m_new = {p: m_t[p][l - 1:l, :] for p in ps}
    w = {p: jnp.exp(d[p] - m_t[p]) * qk[p] for p in ps}
    c_inter = {p: jnp.exp(f_col[p] + m_prev[p] - m_t[p]) for p in ps}
    wv = {p: _dot(w[p], v[p]) for p in ps}
    w_sum = {p: jnp.sum(w[p], axis=1, keepdims=True) for p in ps}
    e_m = {p: jnp.exp(-m_t[p]) for p in ps}
    w_end_row = {p: jnp.exp(f_last[p] - f_row[p] + ig_row[p] - m_new[p]) for p in ps}
    w_end_col = {p: jnp.exp(f_last[p] - f_col[p] + ig_col[p] - m_new[p]) for p in ps}
    c_prev = {p: jnp.exp(f_last[p] + m_prev[p] - m_new[p]) for p in ps}
    dc = {p: _dot(kt[p] * w_end_row[p], v[p]) for p in ps}
    dn = {p: jnp.sum(w_end_col[p] * k[p], axis=0, keepdims=True) for p in ps}
    c_h = [c_scr[h] for h in hs]
    n_h = [n_scr[h:h + 1, :] for h in hs]
    for j in range(cb):
        for h in hs:
            p = (j, h)
            num = c_inter[p] * _dot(q[p], c_h[h]) + wv[p]
            den = c_inter[p] * jnp.sum(q[p] * n_h[h], axis=1, keepdims=True) + w_sum[p]
            o_ref[j * l:(j + 1) * l, h * HEAD_DIM:(h + 1) * HEAD_DIM] = num / jnp.maximum(jnp.abs(den), e_m[p])
            c_h[h] = c_prev[p] * c_h[h] + dc[p]
            n_h[h] = c_prev[p] * n_h[h] + dn[p]
    m_next = m_all
    for h in hs:
        c_scr[h] = c_h[h]
        n_scr[h:h + 1, :] = n_h[h]
        m_next = jnp.where(lane == h, m_run[h], m_next)
    m_scr[...] = m_next

    @pl.when(gi == ng - 1)
    def _():
        c_out_ref[0] = c_scr[...]
        n_out_ref[0] = n_scr[...]
        m_out_ref[0] = m_scr[...]


def _ml_scan(proj, k_t, gate_t, prow, pcol, c0, n0, m0, b, t, l, cb):
    ng = t // (l * cb)
    w3 = 3 * REC_W
    st = lambda bi, i: (bi, 0, 0, 0)
    st3 = lambda bi, i: (bi, 0, 0)
    return pl.pallas_call(
        functools.partial(_ml_scan_kernel, l=l, cb=cb, ng=ng),
        grid=(b, ng),
        in_specs=[pl.BlockSpec((cb * l, w3), lambda bi, i: (bi * ng + i, COL_MQKV // w3)),
                  pl.BlockSpec((cb, REC_W, l), lambda bi, i: (bi * ng + i, 0, 0)),
                  pl.BlockSpec((cb * l, LANES), lambda bi, i: (bi * ng + i, COL_GATE // LANES)),
                  pl.BlockSpec((cb, LANES, l), lambda bi, i: (bi * ng + i, 0, 0)),
                  pl.BlockSpec((SUBLANES, LANES), lambda bi, i: (0, 0)),
                  pl.BlockSpec((LANES, SUBLANES), lambda bi, i: (0, 0)),
                  pl.BlockSpec((1, ML_HEADS, HEAD_DIM, HEAD_DIM), st),
                  pl.BlockSpec((1, SUBLANES, HEAD_DIM), st3),
                  pl.BlockSpec((1, 1, LANES), st3)],
        out_specs=[pl.BlockSpec((cb * l, REC_W), lambda bi, i: (bi * ng + i, 0)),
                   pl.BlockSpec((1, ML_HEADS, HEAD_DIM, HEAD_DIM), st),
                   pl.BlockSpec((1, SUBLANES, HEAD_DIM), st3),
                   pl.BlockSpec((1, 1, LANES), st3)],
        out_shape=[jax.ShapeDtypeStruct((b * t, REC_W), F32),
                   jax.ShapeDtypeStruct((b, ML_HEADS, HEAD_DIM, HEAD_DIM), F32),
                   jax.ShapeDtypeStruct((b, SUBLANES, HEAD_DIM), F32),
                   jax.ShapeDtypeStruct((b, 1, LANES), F32)],
        scratch_shapes=[pltpu.VMEM((ML_HEADS, HEAD_DIM, HEAD_DIM), F32),
                        pltpu.VMEM((SUBLANES, HEAD_DIM), F32),
                        pltpu.VMEM((1, LANES), F32)],
        compiler_params=_cparams("parallel", "arbitrary"),
        name="ml_scan",
    )(proj, k_t, proj, gate_t, prow, pcol, c0, n0, m0)


def _ml_scan_t_kernel(qkv_ref, qkvt_ref, gt_ref, gtt_ref, prow_ref, pcol_ref, c0_ref, n0_ref, m0_ref,
                      o_ref, c_out_ref, n_out_ref, m_out_ref, c_scr, n_scr, m_scr, *, l, cb, ng):
    gi = pl.program_id(1)

    @pl.when(gi == 0)
    def _():
        c_scr[...] = c0_ref[0]
        n_scr[...] = n0_ref[0]
        m_scr[...] = m0_ref[0]

    r = lax.broadcasted_iota(jnp.int32, (l, l), 0)
    c = lax.broadcasted_iota(jnp.int32, (l, l), 1)
    tri_t = r <= c
    lane = lax.broadcasted_iota(jnp.int32, (1, LANES), 1)
    m_all = m_scr[...]
    hs = range(ML_HEADS)
    ps = [(j, h) for j in range(cb) for h in hs]
    kscale = HEAD_DIM ** -0.5
    zeros_t = jnp.zeros((HEAD_DIM, l), F32)
    gates = [_gates(gt_ref[j * l:(j + 1) * l, :], gtt_ref[j], prow_ref, pcol_ref, l) for j in range(cb)]

    def t_rows(j, group, h):
        return qkvt_ref[j, group * REC_W + h * HEAD_DIM:group * REC_W + (h + 1) * HEAD_DIM, :]

    qt = {p: t_rows(p[0], 0, p[1]) for p in ps}
    qt_pad = {p: (_stack(qt[p], zeros_t) if p[1] % 2 == 0 else _stack(zeros_t, qt[p])) for p in ps}
    vt = {p: t_rows(p[0], 2, p[1]) for p in ps}
    k2 = {p: qkv_ref[p[0] * l:(p[0] + 1) * l, REC_W + (p[1] // 2) * LANES:REC_W + (p[1] // 2 + 1) * LANES] * kscale
          for p in ps}
    ig_col = {p: gates[p[0]][0][:, GATE_I + p[1]:GATE_I + p[1] + 1] for p in ps}
    ig_row = {p: gates[p[0]][2][GATE_I + p[1]:GATE_I + p[1] + 1, :] for p in ps}
    f_col = {p: gates[p[0]][1][:, GATE_F + p[1]:GATE_F + p[1] + 1] for p in ps}
    f_row = {p: gates[p[0]][3][GATE_F + p[1]:GATE_F + p[1] + 1, :] for p in ps}
    f_last = {p: f_row[p][:, l - 1:l] for p in ps}
    kq = {p: _dot(k2[p], qt_pad[p]) for p in ps}
    d = {p: jnp.where(tri_t, f_row[p] + (ig_col[p] - f_col[p]), -jnp.inf) for p in ps}
    d_max = {p: jnp.max(d[p], axis=0, keepdims=True) for p in ps}
    m_prev, m_t = {}, {}
    m_run = [m_all[:, h:h + 1] for h in hs]
    for j in range(cb):
        for h in hs:
            m_prev[(j, h)] = m_run[h]
            m_t[(j, h)] = jnp.maximum(f_row[(j, h)] + m_run[h], d_max[(j, h)])
            m_run[h] = m_t[(j, h)][:, l - 1:l]
    m_new = {p: m_t[p][:, l - 1:l] for p in ps}
    w = {p: jnp.exp(d[p] - m_t[p]) * kq[p] for p in ps}
    c_inter = {p: jnp.exp(f_row[p] + m_prev[p] - m_t[p]) for p in ps}
    vw = {p: _dot(vt[p], w[p]) for p in ps}
    w_sum = {p: jnp.sum(w[p], axis=0, keepdims=True) for p in ps}
    e_m = {p: jnp.exp(-m_t[p]) for p in ps}
    w_end_col = {p: jnp.exp(f_last[p] - f_col[p] + ig_col[p] - m_new[p]) for p in ps}
    c_prev = {p: jnp.exp(f_last[p] + m_prev[p] - m_new[p]) for p in ps}
    wk = {p: w_end_col[p] * k2[p] for p in ps}
    dc = {p: _dot(vt[p], wk[p]) for p in ps}
    dn = {p: jnp.sum(wk[p], axis=0, keepdims=True) for p in ps}
    c_h = [c_scr[h] for h in hs]
    n_h = [n_scr[h:h + 1, :] for h in hs]
    ht = {}
    for j in range(cb):
        for h in hs:
            p = (j, h)
            num = c_inter[p] * _dot(c_h[h], qt_pad[p]) + vw[p]
            den = c_inter[p] * _dot(n_h[h], qt_pad[p]) + w_sum[p]
            ht[p] = num / jnp.maximum(jnp.abs(den), e_m[p])
            c_h[h] = c_prev[p] * c_h[h] + dc[p]
            n_h[h] = c_prev[p] * n_h[h] + dn[p]
    for i in range(ML_HEADS // 2):
        pair_t = _stack(jnp.concatenate([ht[(j, 2 * i)] for j in range(cb)], axis=1),
                        jnp.concatenate([ht[(j, 2 * i + 1)] for j in range(cb)], axis=1))
        o_ref[:, i * LANES:(i + 1) * LANES] = pair_t.T
    m_next = m_all
    for h in hs:
        c_scr[h] = c_h[h]
        n_scr[h:h + 1, :] = n_h[h]
        m_next = jnp.where(lane == h, m_run[h], m_next)
    m_scr[...] = m_next

    @pl.when(gi == ng - 1)
    def _():
        c_out_ref[0] = c_scr[...]
        n_out_ref[0] = n_scr[...]
        m_out_ref[0] = m_scr[...]


def _ml_scan_t(proj, qkv_t, gate_t, prow, pcol, c0, n0, m0, b, t, l, cb):
    ng = t // (l * cb)
    w3 = 3 * REC_W
    st = lambda bi, i: (bi, 0, 0, 0)
    st3 = lambda bi, i: (bi, 0, 0)
    return pl.pallas_call(
        functools.partial(_ml_scan_t_kernel, l=l, cb=cb, ng=ng),
        grid=(b, ng),
        in_specs=[pl.BlockSpec((cb * l, w3), lambda bi, i: (bi * ng + i, COL_MQKV // w3)),
                  pl.BlockSpec((cb, w3, l), lambda bi, i: (bi * ng + i, 0, 0)),
                  pl.BlockSpec((cb * l, LANES), lambda bi, i: (bi * ng + i, COL_GATE // LANES)),
                  pl.BlockSpec((cb, LANES, l), lambda bi, i: (bi * ng + i, 0, 0)),
                  pl.BlockSpec((SUBLANES, LANES), lambda bi, i: (0, 0)),
                  pl.BlockSpec((LANES, SUBLANES), lambda bi, i: (0, 0)),
                  pl.BlockSpec((1, ML_HEADS, HEAD_DIM, LANES), st),
                  pl.BlockSpec((1, SUBLANES, LANES), st3),
                  pl.BlockSpec((1, 1, LANES), st3)],
        out_specs=[pl.BlockSpec((cb * l, REC_W), lambda bi, i: (bi * ng + i, 0)),
                   pl.BlockSpec((1, ML_HEADS, HEAD_DIM, LANES), st),
                   pl.BlockSpec((1, SUBLANES, LANES), st3),
                   pl.BlockSpec((1, 1, LANES), st3)],
        out_shape=[jax.ShapeDtypeStruct((b * t, REC_W), F32),
                   jax.ShapeDtypeStruct((b, ML_HEADS, HEAD_DIM, LANES), F32),
                   jax.ShapeDtypeStruct((b, SUBLANES, LANES), F32),
                   jax.ShapeDtypeStruct((b, 1, LANES), F32)],
        scratch_shapes=[pltpu.VMEM((ML_HEADS, HEAD_DIM, LANES), F32),
                        pltpu.VMEM((SUBLANES, LANES), F32),
                        pltpu.VMEM((1, LANES), F32)],
        compiler_params=_cparams("parallel", "arbitrary"),
        name="ml_scan_t",
    )(proj, qkv_t, proj, gate_t, prow, pcol, c0, n0, m0)


def _post_kernel(x_ref, osb_ref, og_ref, hm_ref, gz_ref, mo_ref, w_ref, e_ref, gnw_ref, mnw_ref,
                 lg_ref, lb_ref, o_ref, ob_ref):
    e = e_ref[...]
    og = og_ref[...]
    hm = hm_ref[...]
    og_ms = _dot_exact_rhs(og * og, e, NORM_SUM_PARTS) * (1.0 / HEAD_DIM)
    hm_ms = _dot_exact_rhs(hm * hm, e, NORM_SUM_PARTS) * (1.0 / HEAD_DIM)
    gz = gz_ref[...]
    o_g = og * lax.rsqrt(og_ms + NORM_EPS) * gnw_ref[...] * (gz * _sigmoid(gz))
    o_m = _sigmoid(mo_ref[...]) * (hm * lax.rsqrt(hm_ms + NORM_EPS) * mnw_ref[...])
    mixed = (_dot(osb_ref[...], w_ref[0:SB_W, :])
             + _dot(o_g, w_ref[SB_W:SB_W + REC_W, :])
             + _dot(o_m, w_ref[SB_W + REC_W:SB_W + 2 * REC_W, :]))
    y = _layer_norm(DEEPNORM_ALPHA * x_ref[...] + mixed, lg_ref[...], lb_ref[...])
    o_ref[...] = y
    ob_ref[...] = y.astype(BF16)


def _post(x, osb, og, hm, proj, w_out, e_heads, gnw, mnw, lg, lb, tm):
    n, d = x.shape
    row = lambda i: (i, 0)
    fixed = lambda i: (0, 0)
    return pl.pallas_call(
        _post_kernel,
        grid=(n // tm,),
        in_specs=[pl.BlockSpec((tm, d), row),
                  pl.BlockSpec((tm, SB_W), row),
                  pl.BlockSpec((tm, REC_W), row),
                  pl.BlockSpec((tm, REC_W), row),
                  pl.BlockSpec((tm, REC_W), lambda i: (i, COL_GZ // REC_W)),
                  pl.BlockSpec((tm, REC_W), lambda i: (i, COL_MO // REC_W)),
                  pl.BlockSpec(w_out.shape, fixed),
                  pl.BlockSpec((REC_W, REC_W), fixed),
                  pl.BlockSpec((1, REC_W), fixed),
                  pl.BlockSpec((1, REC_W), fixed),
                  pl.BlockSpec((1, d), fixed),
                  pl.BlockSpec((1, d), fixed)],
        out_specs=[pl.BlockSpec((tm, d), row), pl.BlockSpec((tm, d), row)],
        out_shape=[jax.ShapeDtypeStruct((n, d), F32), jax.ShapeDtypeStruct((n, d), BF16)],
        compiler_params=_cparams("parallel"),
        name="post",
    )(x, osb, og, hm, proj, proj, w_out, e_heads, gnw, mnw, lg, lb)


def _memattn_kernel(x_ref, xb_ref, wq_ref, wo_ref, mk_ref, mv_ref, lg_ref, lb_ref, o_ref, ob_ref):
    d = x_ref.shape[1]
    hd = d // MEM_HEADS
    q = jnp.dot(xb_ref[...], wq_ref[...], preferred_element_type=F32)
    out = None
    for h in range(MEM_HEADS):
        sl = slice(h * hd, (h + 1) * hd)
        s = _dot_nt(q[:, sl], mk_ref[0, :, sl]) * (hd ** -0.5)
        p = jnp.exp(s - jnp.max(s, axis=1, keepdims=True))
        o_h = _dot(p, mv_ref[0, :, sl]) / jnp.sum(p, axis=1, keepdims=True)
        t = _dot(o_h, wo_ref[sl, :])
        out = t if out is None else out + t
    y = _layer_norm(DEEPNORM_ALPHA * x_ref[...] + out, lg_ref[...], lb_ref[...])
    o_ref[...] = y
    ob_ref[...] = y.astype(BF16)


def _memattn(x, xb, w_cq, w_co, mk, mv, lg, lb, b, t, tm):
    n, d = x.shape
    nt = t // tm
    nm = mk.shape[1]
    row = lambda bi, i: (bi * nt + i, 0)
    fixed = lambda bi, i: (0, 0)
    return pl.pallas_call(
        _memattn_kernel,
        grid=(b, nt),
        in_specs=[pl.BlockSpec((tm, d), row),
                  pl.BlockSpec((tm, d), row),
                  pl.BlockSpec((d, d), fixed),
                  pl.BlockSpec((d, d), fixed),
                  pl.BlockSpec((1, nm, d), lambda bi, i: (bi, 0, 0)),
                  pl.BlockSpec((1, nm, d), lambda bi, i: (bi, 0, 0)),
                  pl.BlockSpec((1, d), fixed),
                  pl.BlockSpec((1, d), fixed)],
        out_specs=[pl.BlockSpec((tm, d), row), pl.BlockSpec((tm, d), row)],
        out_shape=[jax.ShapeDtypeStruct((n, d), F32), jax.ShapeDtypeStruct((n, d), BF16)],
        compiler_params=_cparams("parallel", "parallel"),
        name="memattn",
    )(x, xb, w_cq, w_co, mk, mv, lg, lb)


def _ffn_kernel(x_ref, xb_ref, wu_ref, wd_ref, lg_ref, lb_ref, o_ref, ob_ref, *, tf):
    xb = xb_ref[...]
    dff = wd_ref.shape[0]
    acc = None
    for j in range(dff // tf):
        gate = jnp.dot(xb, wu_ref[:, j * tf:(j + 1) * tf], preferred_element_type=F32)
        up = jnp.dot(xb, wu_ref[:, dff + j * tf:dff + (j + 1) * tf], preferred_element_type=F32)
        t = _dot(gate * _sigmoid(gate) * up, wd_ref[j * tf:(j + 1) * tf, :])
        acc = t if acc is None else acc + t
    y = _layer_norm(DEEPNORM_ALPHA * x_ref[...] + acc, lg_ref[...], lb_ref[...])
    o_ref[...] = y
    ob_ref[...] = y.astype(BF16)


def _ffn(x, xb, w_up, w_down, lg, lb, tm, tf):
    n, d = x.shape
    row = lambda i: (i, 0)
    fixed = lambda i: (0, 0)
    return pl.pallas_call(
        functools.partial(_ffn_kernel, tf=tf),
        grid=(n // tm,),
        in_specs=[pl.BlockSpec((tm, d), row),
                  pl.BlockSpec((tm, d), row),
                  pl.BlockSpec(w_up.shape, fixed, pipeline_mode=pl.Buffered(1)),
                  pl.BlockSpec(w_down.shape, fixed, pipeline_mode=pl.Buffered(1)),
                  pl.BlockSpec((1, d), fixed),
                  pl.BlockSpec((1, d), fixed)],
        out_specs=[pl.BlockSpec((tm, d), row), pl.BlockSpec((tm, d), row)],
        out_shape=[jax.ShapeDtypeStruct((n, d), F32), jax.ShapeDtypeStruct((n, d), BF16)],
        compiler_params=_cparams("parallel"),
        name="ffn",
    )(x, xb, w_up, w_down, lg, lb)


def _pack_layer(p):
    w_in = p['w_in']
    d = w_in.shape[0]
    o_sb, o_gqkv, o_gz = 0, 3 * SB_W, 3 * SB_W + 3 * REC_W
    o_gb = o_gz + REC_W
    o_mqkv = o_gb + 2 * GDN_HEADS
    o_mo = o_mqkv + 3 * REC_W
    o_mi = o_mo + REC_W
    gate_cols = jnp.concatenate([w_in[:, o_gb:o_gb + 2 * GDN_HEADS], w_in[:, o_mi:o_mi + 2 * ML_HEADS]], axis=1)
    used = COL_GATE + gate_cols.shape[1]
    w_al = jnp.concatenate([w_in[:, o_gqkv:o_gqkv + 3 * REC_W], w_in[:, o_mqkv:o_mqkv + 3 * REC_W],
                            w_in[:, o_gz:o_gz + REC_W], w_in[:, o_mo:o_mo + REC_W],
                            w_in[:, o_sb:o_sb + 3 * SB_W], gate_cols,
                            jnp.zeros((d, PROJ_W - used), w_in.dtype)], axis=1).astype(BF16)
    prow = jnp.zeros((SUBLANES, LANES), F32)
    prow = prow.at[0, GATE_G:GATE_G + GDN_HEADS].set(p['gdn_dt_bias'])
    prow = prow.at[0, GATE_I:GATE_I + ML_HEADS].set(p['mlstm_i_bias'])
    prow = prow.at[0, GATE_F:GATE_F + ML_HEADS].set(p['mlstm_f_bias'])
    prow = prow.at[1, GATE_G:GATE_G + GDN_HEADS].set(p['gdn_A_log'])
    conv_w8 = jnp.zeros((SUBLANES, 3 * REC_W), F32).at[0:CONV_W].set(p['gdn_conv_w'])
    return dict(
        w_al=w_al, prow=prow, pcol=prow.T, conv_w8=conv_w8,
        gnw=jnp.tile(p['gdn_norm_w'], GDN_HEADS)[None, :], mnw=p['mlstm_norm_w'][None, :],
        w_out=p['w_out'].astype(BF16), w_cq=p['w_cq'].astype(BF16), w_co=p['w_co'].astype(BF16),
        w_up=p['w_up'].astype(BF16), w_down=p['w_down'].astype(BF16),
        ln1=(p['ln1_g'][None, :], p['ln1_b'][None, :]), ln2=(p['ln2_g'][None, :], p['ln2_b'][None, :]),
        ln3=(p['ln3_g'][None, :], p['ln3_b'][None, :]))


def _head_consts():
    hid = np.arange(REC_W) // HEAD_DIM
    return jnp.asarray(hid[:, None] == hid[None, :], BF16)


def _sb_layout(k, v, tk):
    b, tkv, _ = k.shape
    nkb = tkv // tk
    kt = k.reshape(b, nkb, tk, SB_W).transpose(0, 1, 3, 2)
    vh = v.reshape(b, nkb, tk, SB_HEADS, HEAD_DIM).transpose(0, 3, 1, 2, 4)
    vh = vh.reshape(b * SB_HEADS, nkb, tk, HEAD_DIM)
    return kt.astype(BF16), vh.astype(BF16)


def _to_pair_lanes(a):
    z = jnp.zeros_like(a)
    even = (jnp.arange(a.shape[1]) % 2 == 0)[None, :, None, None]
    return jnp.where(even, jnp.concatenate([a, z], axis=-1), jnp.concatenate([z, a], axis=-1))


def _from_pair_lanes(a):
    even = (jnp.arange(a.shape[1]) % 2 == 0)[None, :, None, None]
    return jnp.where(even, a[..., :HEAD_DIM], a[..., HEAD_DIM:])


def _time_on_lanes(a, groups, width):
    return a.reshape(groups, width, a.shape[1]).transpose(0, 2, 1)


def _trunk_layer(x, xb, pk, e_heads, mk, mv, b, t, cfg, sb_past, conv_ctx, gdn_s, ml_c, ml_n, ml_m):
    n, d = x.shape
    l = min(t, CHUNK)
    nc = t // l
    cb = min(cfg['scan_cb'], nc)
    ng = b * nc // cb
    tk = cfg['sb_tk']
    ctx8 = jnp.pad(conv_ctx, ((0, 0), (SUBLANES - (CONV_W - 1), 0), (0, 0)))
    if sb_past is None:
        proj, kt, vh, mk_t, gate_t, gqkv_act, gk_t = _proj(x if xb is None else xb, pk['w_al'], ctx8,
                                                           pk['conv_w8'], e_heads, b, t, cfg['tm'])
        kt = kt.reshape(b, t // tk, SB_W, tk)
        q_off = 0
    else:
        proj = _matmul(xb, pk['w_al'], cfg['tm'], 1024)
    sk = proj[:, COL_SB + SB_W:COL_SB + 2 * SB_W].reshape(b, t, SB_W)
    sv = proj[:, COL_SB + 2 * SB_W:COL_SB + 3 * SB_W].reshape(b, t, SB_W)
    if sb_past is not None:
        past_k, past_v = sb_past
        q_off = past_k.shape[1]
        kv_len = q_off + t
        kv_pad = -(-kv_len // tk) * tk - kv_len
        kk = jnp.pad(jnp.concatenate([past_k.reshape(b, q_off, SB_W), sk], axis=1), ((0, 0), (0, kv_pad), (0, 0)))
        vv = jnp.pad(jnp.concatenate([past_v.reshape(b, q_off, SB_W), sv], axis=1), ((0, 0), (0, kv_pad), (0, 0)))
        kt, vh = _sb_layout(kk, vv, tk)
        gqkv_act = _gdn_pre(proj, ctx8, pk['conv_w8'], e_heads, b, t, cfg['conv_tt'])
        gate_t = _time_on_lanes(proj[:, COL_GATE:COL_GATE + LANES], b * nc, l)
        gk_t = _time_on_lanes(gqkv_act[:, REC_W:2 * REC_W], b * nc, l)
        mk_t = _time_on_lanes(proj[:, COL_MQKV + REC_W:COL_MQKV + 2 * REC_W], b * nc, l)

    o_sb = _sb_attn(proj, kt, vh, b, t, SB_HEADS, cfg['sb_tq'], q_off)
    gdn_scan = _gdn_scan_pair if l == HEAD_DIM else _gdn_scan
    o_g, gdn_s_new = gdn_scan(gqkv_act, gk_t, proj, gate_t, pk['prow'], pk['pcol'], gdn_s, b, t, l,
                              min(cfg['gdn_cb'], nc))
    m0 = jnp.pad(ml_m, ((0, 0), (0, LANES - ML_HEADS)))[:, None, :]
    if sb_past is None:
        c0 = _to_pair_lanes(jnp.swapaxes(ml_c, -1, -2))
        n0 = jnp.pad(_to_pair_lanes(ml_n[:, :, None, :])[:, :, 0, :], ((0, 0), (0, SUBLANES - ML_HEADS), (0, 0)))
        h_m, c_new, n_new, m_new = _ml_scan_t(proj, mk_t, gate_t, pk['prow'], pk['pcol'], c0, n0, m0, b, t, l, cb)
        c_new = jnp.swapaxes(_from_pair_lanes(c_new), -1, -2)
        n_new = _from_pair_lanes(n_new[:, :ML_HEADS, None, :])[:, :, 0, :]
    else:
        n0 = jnp.pad(ml_n, ((0, 0), (0, SUBLANES - ML_HEADS), (0, 0)))
        h_m, c_new, n_new, m_new = _ml_scan(proj, mk_t, gate_t, pk['prow'], pk['pcol'], ml_c, n0, m0, b, t, l, cb)
        n_new = n_new[:, :ML_HEADS, :]

    x1, x1b = _post(x, o_sb, o_g, h_m, proj, pk['w_out'], e_heads, pk['gnw'], pk['mnw'], *pk['ln1'], cfg['post_tm'])
    x2, x2b = _memattn(x1, x1b, pk['w_cq'], pk['w_co'], mk, mv, *pk['ln2'], b, t, cfg['mem_tm'])
    x3, x3b = _ffn(x2, x2b, pk['w_up'], pk['w_down'], *pk['ln3'], cfg['tm'], cfg['ffn_tf'])

    new_conv = proj.reshape(b, t, PROJ_W)[:, t - (CONV_W - 1):, COL_GQKV:COL_GQKV + 3 * REC_W]
    return (x3, x3b, sk.reshape(b, t, SB_HEADS, HEAD_DIM), sv.reshape(b, t, SB_HEADS, HEAD_DIM), new_conv,
            gdn_s_new, c_new, n_new, m_new[:, 0, :ML_HEADS])


def kernel(x_prompt, x_sample, cache_sb_k, cache_sb_v, cache_gdn_conv, state_gdn, state_mlstm_C, state_mlstm_n, state_mlstm_m, cache_mem_k, cache_mem_v, mem_prompt, w_in, gdn_conv_w, gdn_A_log, gdn_dt_bias, gdn_norm_w, mlstm_i_bias, mlstm_f_bias, mlstm_norm_w, w_out, ln1_g, ln1_b, w_cq, w_ckv, w_co, ln2_g, ln2_b, w_up, w_down, ln3_g, ln3_b):
    bp, tp, d = x_prompt.shape
    bs, ts, _ = x_sample.shape
    depth = w_in.shape[0]
    n_mem = mem_prompt.shape[1]
    dff = w_down.shape[1]
    cfg_p = dict(tm=512, post_tm=1024, sb_tq=512, sb_tk=256, conv_tt=512, mem_tm=1024, ffn_tf=dff // 2, scan_cb=4, gdn_cb=8)
    cfg_s = dict(tm=bs * ts, post_tm=bs * ts, sb_tq=ts, sb_tk=256, conv_tt=ts, mem_tm=ts, ffn_tf=dff // 2, scan_cb=1, gdn_cb=1)
    e_heads = _head_consts()

    xp = x_prompt.reshape(bp * tp, d)
    xs = x_sample.reshape(bs * ts, d)
    xpb, xsb = None, xs.astype(BF16)
    mem_b = mem_prompt.reshape(bp * n_mem, d).astype(BF16)
    p_out = [[] for _ in range(9)]
    s_out = [[] for _ in range(7)]
    for li in range(depth):
        p = {'w_in': w_in[li], 'gdn_conv_w': gdn_conv_w[li], 'gdn_A_log': gdn_A_log[li],
             'gdn_dt_bias': gdn_dt_bias[li], 'gdn_norm_w': gdn_norm_w[li], 'mlstm_i_bias': mlstm_i_bias[li],
             'mlstm_f_bias': mlstm_f_bias[li], 'mlstm_norm_w': mlstm_norm_w[li], 'w_out': w_out[li],
             'ln1_g': ln1_g[li], 'ln1_b': ln1_b[li], 'w_cq': w_cq[li], 'w_co': w_co[li],
             'ln2_g': ln2_g[li], 'ln2_b': ln2_b[li], 'w_up': w_up[li], 'w_down': w_down[li],
             'ln3_g': ln3_g[li], 'ln3_b': ln3_b[li]}
        pk = _pack_layer(p)
        mkv = _matmul(mem_b, w_ckv[li].astype(BF16), n_mem, 1024)
        mk_p = mkv[:, :d].reshape(bp, n_mem, d)
        mv_p = mkv[:, d:].reshape(bp, n_mem, d)
        res = _trunk_layer(xp, xpb, pk, e_heads, mk_p.astype(BF16), mv_p.astype(BF16), bp, tp, cfg_p, None,
                           jnp.zeros((bp, CONV_W - 1, 3 * REC_W), F32),
                           jnp.zeros((bp, GDN_HEADS, HEAD_DIM, HEAD_DIM), F32),
                           jnp.zeros((bp, ML_HEADS, HEAD_DIM, HEAD_DIM), F32),
                           jnp.zeros((bp, ML_HEADS, HEAD_DIM), F32),
                           jnp.zeros((bp, ML_HEADS), F32))
        xp, xpb = res[0], res[1]
        hd = d // MEM_HEADS
        for j, a in enumerate(res[2:] + (mk_p.reshape(bp, n_mem, MEM_HEADS, hd), mv_p.reshape(bp, n_mem, MEM_HEADS, hd))):
            p_out[j].append(a)
        res = _trunk_layer(xs, xsb, pk, e_heads,
                           cache_mem_k[li].reshape(bs, n_mem, d).astype(BF16),
                           cache_mem_v[li].reshape(bs, n_mem, d).astype(BF16),
                           bs, ts, cfg_s, (cache_sb_k[li], cache_sb_v[li]), cache_gdn_conv[li],
                           state_gdn[li], state_mlstm_C[li], state_mlstm_n[li], state_mlstm_m[li])
        xs, xsb = res[0], res[1]
        for j, a in enumerate(res[2:]):
            s_out[j].append(a)
    p_st = [jnp.stack(a) for a in p_out]
    s_st = [jnp.stack(a) for a in s_out]
    return (xp.reshape(bp, tp, d), xs.reshape(bs, ts, d), *p_st, *s_st)
```

```python
import functools

import numpy as np
import jax
import jax.numpy as jnp
from jax import lax
from jax.experimental import pallas as pl
from jax.experimental.pallas import tpu as pltpu

F32 = jnp.float32
BF16 = jnp.bfloat16

HEAD_DIM = 64
SB_HEADS = 4
GDN_HEADS = 6
ML_HEADS = 6
REC_W = GDN_HEADS * HEAD_DIM
SB_W = SB_HEADS * HEAD_DIM
CONV_W = 4
CHUNK = 64
MEM_HEADS = 4
LN_EPS = 1e-5
NORM_EPS = 1e-6
DEPTH = 2
DEEPNORM_ALPHA = (2 * DEPTH) ** 0.25
LOG2E = float(np.log2(np.e))
SOFTPLUS2_CLAMP = 120.0
NORM_SUM_PARTS = 1
F32_EXP2_ZERO = 160.0
BOUND_SLACK = 1.01

GROUP = 256
LANES = 128
SUBLANES = 8
VMEM_LIMIT = 56 * 1024 * 1024

PROJ_W = 4096
COL_GQKV = 0
COL_MQKV = 1152
COL_GZ = 2304
COL_MO = 2688
COL_SB = 3072
COL_GATE = 3840
GATE_BETA, GATE_G, GATE_I, GATE_F = 0, 6, 12, 18


def _cparams(*sem):
    return pltpu.CompilerParams(dimension_semantics=sem, vmem_limit_bytes=VMEM_LIMIT)


def _dot(a, b):
    return jnp.dot(a.astype(BF16), b.astype(BF16), preferred_element_type=F32)


def _dot_nt(a, b):
    return lax.dot_general(a.astype(BF16), b.astype(BF16), (((1,), (1,)), ((), ())),
                           preferred_element_type=F32)


def _split(x, parts):
    out = []
    r = x
    for _ in range(parts - 1):
        p = r.astype(BF16)
        out.append(p)
        r = r - p.astype(F32)
    out.append(r.astype(BF16))
    return out


def _dot_exact_lhs(m, x, parts):
    acc = None
    for p in _split(x, parts):
        t = jnp.dot(m, p, preferred_element_type=F32)
        acc = t if acc is None else acc + t
    return acc


def _dot_exact_rhs(x, m, parts):
    acc = None
    for p in _split(x, parts):
        t = jnp.dot(p, m, preferred_element_type=F32)
        acc = t if acc is None else acc + t
    return acc


def _ones_where(mask, dtype):
    return jnp.where(mask, 1.0, 0.0).astype(dtype)


def _sigmoid(x):
    return 1.0 / (1.0 + jnp.exp(-x))


def _layer_norm(y, g, b):
    mu = jnp.mean(y, axis=-1, keepdims=True)
    d = y - mu
    var = jnp.mean(d * d, axis=-1, keepdims=True)
    return d * lax.rsqrt(var + LN_EPS) * g + b


def _mm_kernel(x_ref, w_ref, o_ref):
    o_ref[...] = jnp.dot(x_ref[...], w_ref[...], preferred_element_type=F32).astype(o_ref.dtype)


def _matmul(x, w, tm, tn, out_dtype=F32):
    n, k = x.shape
    m = w.shape[1]
    return pl.pallas_call(
        _mm_kernel,
        grid=(n // tm, m // tn),
        in_specs=[pl.BlockSpec((tm, k), lambda i, j: (i, 0)),
                  pl.BlockSpec((k, tn), lambda i, j: (0, j))],
        out_specs=pl.BlockSpec((tm, tn), lambda i, j: (i, j)),
        out_shape=jax.ShapeDtypeStruct((n, m), out_dtype),
        compiler_params=_cparams("parallel", "parallel"),
        name="proj_matmul",
    )(x, w)


def _gdn_activate(x, first, ctx_ref, w_ref, e_ref, o_ref, kt_ref, xbuf, tt):
    @pl.when(first)
    def _():
        xbuf[0:SUBLANES, :] = ctx_ref[0]

    xbuf[SUBLANES:SUBLANES + tt, :] = x
    acc = w_ref[CONV_W - 1:CONV_W, :] * xbuf[SUBLANES:SUBLANES + tt, :]
    for j in range(CONV_W - 1):
        off = SUBLANES - (CONV_W - 1) + j
        acc = acc + w_ref[j:j + 1, :] * xbuf[off:off + tt, :]
    y = acc * _sigmoid(acc)
    q = y[:, 0:REC_W]
    k = y[:, REC_W:2 * REC_W]
    e = e_ref[...]
    qs = _dot_exact_rhs(q * q, e, NORM_SUM_PARTS)
    ks = _dot_exact_rhs(k * k, e, NORM_SUM_PARTS)
    kn = k * lax.rsqrt(ks + NORM_EPS)
    o_ref[:, 0:REC_W] = q * lax.rsqrt(qs + NORM_EPS) * (HEAD_DIM ** -0.5)
    o_ref[:, REC_W:2 * REC_W] = kn
    o_ref[:, 2 * REC_W:3 * REC_W] = y[:, 2 * REC_W:3 * REC_W]
    if kt_ref is not None:
        for r in range(tt // GROUP):
            _store_chunks_t(kt_ref, r, kn[r * GROUP:(r + 1) * GROUP, :])
    xbuf[0:SUBLANES, :] = xbuf[tt:tt + SUBLANES, :]


def _proj_kernel(x_ref, w_ref, ctx_ref, cw_ref, e_ref, o_ref, skt_ref, sv_ref, mkt_ref, gtt_ref, act_ref, gkt_ref,
                 xbuf, *, tm, tn):
    xb = x_ref[...].astype(BF16)
    for j in range(PROJ_W // tn):
        o_ref[:, j * tn:(j + 1) * tn] = jnp.dot(xb, w_ref[:, j * tn:(j + 1) * tn], preferred_element_type=F32)
    for r in range(tm // GROUP):
        rows = slice(r * GROUP, (r + 1) * GROUP)
        skt_ref[r] = o_ref[rows, COL_SB + SB_W:COL_SB + 2 * SB_W].T.astype(BF16)
        for h in range(SB_HEADS):
            c0 = COL_SB + 2 * SB_W + h * HEAD_DIM
            sv_ref[h, r] = o_ref[rows, c0:c0 + HEAD_DIM].astype(BF16)
        _store_chunks_t(mkt_ref, r, o_ref[rows, COL_MQKV:COL_MQKV + 3 * REC_W])
        _store_chunks_t(gtt_ref, r, o_ref[rows, COL_GATE:COL_GATE + LANES])
    _gdn_activate(o_ref[:, COL_GQKV:COL_GQKV + 3 * REC_W], pl.program_id(1) == 0, ctx_ref, cw_ref, e_ref,
                  act_ref, gkt_ref, xbuf, tm)


def _store_chunks_t(ref, r, a):
    at = a.T
    per = GROUP // CHUNK
    for j in range(per):
        ref[r * per + j] = at[:, j * CHUNK:(j + 1) * CHUNK]


def _proj(x, w, ctx8, conv_w8, e_heads, b, t, tm):
    n, d = x.shape
    nt = t // tm
    ng = n // GROUP
    gpt = tm // GROUP
    cpt = tm // CHUNK
    w3 = 3 * REC_W
    row = lambda bi, i: (bi * nt + i, 0)
    row3 = lambda bi, i: (bi * nt + i, 0, 0)
    fixed = lambda bi, i: (0, 0)
    return pl.pallas_call(
        functools.partial(_proj_kernel, tm=tm, tn=1024),
        grid=(b, nt),
        in_specs=[pl.BlockSpec((tm, d), row),
                  pl.BlockSpec((d, PROJ_W), fixed, pipeline_mode=pl.Buffered(1)),
                  pl.BlockSpec((1, SUBLANES, w3), lambda bi, i: (bi, 0, 0)),
                  pl.BlockSpec((SUBLANES, w3), fixed),
                  pl.BlockSpec((REC_W, REC_W), fixed)],
        out_specs=[pl.BlockSpec((tm, PROJ_W), row),
                   pl.BlockSpec((gpt, SB_W, GROUP), row3),
                   pl.BlockSpec((SB_HEADS, gpt, GROUP, HEAD_DIM), lambda bi, i: (0, bi * nt + i, 0, 0)),
                   pl.BlockSpec((cpt, w3, CHUNK), row3),
                   pl.BlockSpec((cpt, LANES, CHUNK), row3),
                   pl.BlockSpec((tm, w3), row),
                   pl.BlockSpec((cpt, REC_W, CHUNK), row3)],
        out_shape=[jax.ShapeDtypeStruct((n, PROJ_W), F32),
                   jax.ShapeDtypeStruct((ng, SB_W, GROUP), BF16),
                   jax.ShapeDtypeStruct((SB_HEADS, ng, GROUP, HEAD_DIM), BF16),
                   jax.ShapeDtypeStruct((n // CHUNK, w3, CHUNK), F32),
                   jax.ShapeDtypeStruct((n // CHUNK, LANES, CHUNK), F32),
                   jax.ShapeDtypeStruct((n, w3), F32),
                   jax.ShapeDtypeStruct((n // CHUNK, REC_W, CHUNK), F32)],
        scratch_shapes=[pltpu.VMEM((tm + SUBLANES, w3), F32)],
        compiler_params=_cparams("parallel", "arbitrary"),
        name="proj_full",
    )(x, w, ctx8, conv_w8, e_heads)


def _gdn_pre_kernel(x_ref, ctx_ref, w_ref, e_ref, o_ref, xbuf, *, tt):
    _gdn_activate(x_ref[...], pl.program_id(1) == 0, ctx_ref, w_ref, e_ref, o_ref, None, xbuf, tt)


def _gdn_pre(proj, ctx8, conv_w8, e_heads, b, t, tt):
    w3 = 3 * REC_W
    nt = t // tt
    return pl.pallas_call(
        functools.partial(_gdn_pre_kernel, tt=tt),
        grid=(b, nt),
        in_specs=[pl.BlockSpec((tt, w3), lambda bi, i: (bi * nt + i, COL_GQKV // w3)),
                  pl.BlockSpec((1, SUBLANES, w3), lambda bi, i: (bi, 0, 0)),
                  pl.BlockSpec((SUBLANES, w3), lambda bi, i: (0, 0)),
                  pl.BlockSpec((REC_W, REC_W), lambda bi, i: (0, 0))],
        out_specs=pl.BlockSpec((tt, w3), lambda bi, i: (bi * nt + i, 0)),
        out_shape=jax.ShapeDtypeStruct((b * t, w3), F32),
        scratch_shapes=[pltpu.VMEM((tt + SUBLANES, w3), F32)],
        compiler_params=_cparams("parallel", "arbitrary"),
        name="gdn_pre",
    )(proj, ctx8, conv_w8, e_heads)


def _sb_kernel(q_ref, kt_ref, v_ref, o_ref, acc_ref, c_ref, zmax_ref, kmax_ref,
               *, hb, tq, tk, nkb, q_off, diag_static):
    qi = pl.program_id(1)
    q_lo = q_off + qi * tq
    nblk = jnp.minimum(nkb, (q_lo + tq - 1 + tk - 1) // tk)
    nfull = jnp.minimum(nblk, q_lo // tk)
    scale = HEAD_DIM ** -0.5 * LOG2E
    qs = [(q_ref[:, h * HEAD_DIM:(h + 1) * HEAD_DIM] * scale).astype(BF16) for h in range(hb)]
    lower_incl = _ones_where(lax.broadcasted_iota(jnp.int32, (tk, tk), 0)
                             >= lax.broadcasted_iota(jnp.int32, (tk, tk), 1), BF16)
    acc_ref[...] = jnp.zeros_like(acc_ref)
    c_ref[...] = jnp.zeros_like(c_ref)

    @pl.when(qi == 0)
    def _():
        def knorm(kb, best):
            k32 = kt_ref[0, kb].astype(F32)
            sq = k32 * k32
            return tuple(jnp.maximum(best[h], jnp.sum(sq[h * HEAD_DIM:(h + 1) * HEAD_DIM, :], axis=0, keepdims=True))
                         for h in range(hb))
        best = lax.fori_loop(0, nkb, knorm, tuple(jnp.zeros((1, tk), F32) for _ in range(hb)))
        for h in range(hb):
            kmax_ref[h] = jnp.broadcast_to(jnp.sqrt(jnp.max(best[h], axis=1, keepdims=True)), (1, LANES))

    for h in range(hb):
        q32 = qs[h].astype(F32)
        qn = jnp.sqrt(jnp.sum(q32 * q32, axis=1, keepdims=True))
        zmax_ref[h] = qn * kmax_ref[h] * BOUND_SLACK

    def all_weights_vanish():
        margin = c_ref[0] - zmax_ref[0]
        for h in range(1, hb):
            margin = jnp.minimum(margin, c_ref[h] - zmax_ref[h])
        return jnp.min(margin) > F32_EXP2_ZERO

    def blocks(kbs, r0=0, valids=None):
        for h in range(hb):
            q = qs[h][r0:, :]
            c = c_ref[h, r0:, :]
            av = None
            for i, kb in enumerate(kbs):
                kt = kt_ref[0, kb, h * HEAD_DIM:(h + 1) * HEAD_DIM, :]
                z = jnp.dot(q, kt, preferred_element_type=F32)
                lneg = jnp.maximum(z, jnp.log2(1.0 + jnp.exp2(jnp.minimum(z, SOFTPLUS2_CLAMP))))
                if valids is not None:
                    lneg = jnp.where(valids[i], lneg, 0.0)
                incl = jnp.dot(lneg.astype(BF16), lower_incl, preferred_element_type=F32)
                a = jnp.exp2(z - incl - jnp.tile(c, (1, tk // LANES)))
                if valids is not None:
                    a = jnp.where(valids[i], a, 0.0)
                t = jnp.dot(a.astype(BF16), v_ref[h, kb], preferred_element_type=F32)
                av = t if av is None else av + t
                c = c + incl[:, 0:1]
            acc_ref[h, r0:, :] += av
            c_ref[h, r0:, :] = c

    if diag_static:
        for r in reversed(range(tq // tk)):
            rows = tq - r * tk
            valid = (lax.broadcasted_iota(jnp.int32, (rows, tk), 1)
                     < lax.broadcasted_iota(jnp.int32, (rows, tk), 0))
            blocks([nfull + r], r0=r * tk, valids=[valid])
    else:
        @pl.loop(0, nblk - nfull)
        def _(i):
            kb = nblk - 1 - i
            valid = (kb * tk + lax.broadcasted_iota(jnp.int32, (tq, tk), 1)
                     < q_lo + lax.broadcasted_iota(jnp.int32, (tq, tk), 0))
            blocks([kb], valids=[valid])

    def more_blocks(carry):
        i, done = carry
        return jnp.logical_and(i < nfull, done == 0)

    def next_block(carry):
        i, _ = carry
        blocks([nfull - 1 - i])
        return i + 1, all_weights_vanish().astype(jnp.int32)

    lax.while_loop(more_blocks, next_block, (jnp.int32(0), all_weights_vanish().astype(jnp.int32)))

    for h in range(hb):
        o_ref[:, h * HEAD_DIM:(h + 1) * HEAD_DIM] = acc_ref[h]


def _sb_attn(proj, kt, vh, b, t, hb, tq, q_off):
    _, nkb, w, tk = kt.shape
    d = w // hb
    nq = t // tq
    diag_static = q_off % tk == 0 and tq % tk == 0 and q_off + t <= nkb * tk
    kv_mode = dict(pipeline_mode=pl.Buffered(1)) if b == 1 else {}
    return pl.pallas_call(
        functools.partial(_sb_kernel, hb=hb, tq=tq, tk=tk, nkb=nkb, q_off=q_off, diag_static=diag_static),
        grid=(b, nq),
        in_specs=[pl.BlockSpec((tq, w), lambda bi, i: (bi * nq + i, COL_SB // w)),
                  pl.BlockSpec((1, nkb, w, tk), lambda bi, i: (bi, 0, 0, 0), **kv_mode),
                  pl.BlockSpec((hb, nkb, tk, d), lambda bi, i: (bi, 0, 0, 0), **kv_mode)],
        out_specs=pl.BlockSpec((tq, w), lambda bi, i: (bi * nq + i, 0)),
        out_shape=jax.ShapeDtypeStruct((b * t, w), F32),
        scratch_shapes=[pltpu.VMEM((hb, tq, d), F32), pltpu.VMEM((hb, tq, LANES), F32),
                        pltpu.VMEM((hb, tq, LANES), F32), pltpu.VMEM((hb, 1, LANES), F32)],
        compiler_params=_cparams("parallel", "arbitrary"),
        name="sb_attn",
    )(proj, kt, vh)


def _gate_values(pre, neg_a, lane_id):
    sp = jnp.log1p(jnp.exp(-jnp.abs(pre)))
    softplus = jnp.maximum(pre, 0.0) + sp
    log_sig = jnp.minimum(pre, 0.0) - sp
    return jnp.where(lane_id < GATE_G, _sigmoid(pre),
                     jnp.where(lane_id < GATE_I, neg_a * softplus,
                               jnp.where(lane_id < GATE_F, pre, log_sig)))


def _stack(a, b):
    return jnp.concatenate([a, b], axis=0)


def _gates(gt, gtt, prow_ref, pcol_ref, l):
    lane = lax.broadcasted_iota(jnp.int32, (l, LANES), 1)
    val = _gate_values(gt + prow_ref[0:1, :], -jnp.exp(prow_ref[1:2, :]), lane)
    sub = lax.broadcasted_iota(jnp.int32, (LANES, l), 0)
    val_t = _gate_values(gtt + pcol_ref[:, 0:1], -jnp.exp(pcol_ref[:, 1:2]), sub)
    r = lax.broadcasted_iota(jnp.int32, (l, l), 0)
    c = lax.broadcasted_iota(jnp.int32, (l, l), 1)
    csum = _dot_exact_lhs(_ones_where(r >= c, BF16), val, 3)
    csum_t = _dot_exact_rhs(val_t, _ones_where(r <= c, BF16), 3)
    return val, csum, val_t, csum_t


def _gdn_scan_kernel(qkv_ref, kt_ref, gt_ref, gtt_ref, prow_ref, pcol_ref, s0_ref, o_ref, s_out_ref, s_scr,
                     *, l, cb, ng):
    gi = pl.program_id(1)

    @pl.when(gi == 0)
    def _():
        s_scr[...] = s0_ref[0]

    r = lax.broadcasted_iota(jnp.int32, (l, l), 0)
    c = lax.broadcasted_iota(jnp.int32, (l, l), 1)
    tri = r >= c
    stri = r > c
    eye = _ones_where(r == c, F32)
    n_double = max(int(np.ceil(np.log2(l))) - 1, 0)
    hs = range(GDN_HEADS)
    ps = [(j, h) for j in range(cb) for h in hs]
    gates = [_gates(gt_ref[j * l:(j + 1) * l, :], gtt_ref[j], prow_ref, pcol_ref, l) for j in range(cb)]

    def head_cols(j, group, h):
        return qkv_ref[j * l:(j + 1) * l, group * REC_W + h * HEAD_DIM:group * REC_W + (h + 1) * HEAD_DIM]

    q = {p: head_cols(p[0], 0, p[1]) for p in ps}
    k = {p: head_cols(p[0], 1, p[1]) for p in ps}
    v = {p: head_cols(p[0], 2, p[1]) for p in ps}
    kt = {p: kt_ref[p[0], p[1] * HEAD_DIM:(p[1] + 1) * HEAD_DIM, :] for p in ps}
    beta = {p: gates[p[0]][0][:, GATE_BETA + p[1]:GATE_BETA + p[1] + 1] for p in ps}
    g_col = {p: gates[p[0]][1][:, GATE_G + p[1]:GATE_G + p[1] + 1] for p in ps}
    g_row = {p: gates[p[0]][3][GATE_G + p[1]:GATE_G + p[1] + 1, :] for p in ps}
    g_last = {p: g_col[p][l - 1:l, :] for p in ps}
    decay = {p: jnp.where(tri, jnp.exp(jnp.where(tri, g_col[p] - g_row[p], 0.0)), 0.0) for p in ps}
    kb = {p: k[p] * beta[p] for p in ps}
    e_g = {p: jnp.exp(g_col[p]) for p in ps}
    kq = {p: _dot(_stack(kb[p], q[p]), kt[p]) for p in ps}
    x = {p: -jnp.where(stri, kq[p][:l] * decay[p], 0.0) for p in ps}
    tinv = {p: eye + x[p] for p in ps}
    if n_double > 0:
        x = {p: _dot(x[p], x[p]) for p in ps}
        for _ in range(n_double - 1):
            tx = {p: _dot(_stack(tinv[p], x[p]), x[p]) for p in ps}
            tinv = {p: tinv[p] + tx[p][:l] for p in ps}
            x = {p: tx[p][l:] for p in ps}
        tinv = {p: tinv[p] + _dot(tinv[p], x[p]) for p in ps}
    sol_v = {p: _dot(tinv[p], v[p] * beta[p]) for p in ps}
    sol_k = {p: _dot(tinv[p], kb[p] * e_g[p]) for p in ps}
    lhs_s = {p: _stack(q[p] * e_g[p], sol_k[p]) for p in ps}
    lhs_u = {p: _stack(kt[p] * jnp.exp(g_last[p] - g_row[p]), kq[p][l:] * decay[p]) for p in ps}
    e_last = {p: jnp.exp(g_last[p]) for p in ps}
    s = [s_scr[h] for h in hs]
    for j in range(cb):
        t = [_dot(lhs_s[(j, h)], s[h]) for h in hs]
        u = [sol_v[(j, h)] - t[h][l:] for h in hs]
        w = [_dot(lhs_u[(j, h)], u[h]) for h in hs]
        s = [e_last[(j, h)] * s[h] + w[h][:HEAD_DIM] for h in hs]
        for h in hs:
            o_ref[j * l:(j + 1) * l, h * HEAD_DIM:(h + 1) * HEAD_DIM] = t[h][:l] + w[h][HEAD_DIM:]
    for h in hs:
        s_scr[h] = s[h]

    @pl.when(gi == ng - 1)
    def _():
        s_out_ref[0] = s_scr[...]


def _gdn_scan(qkv, k_t, proj, gate_t, prow, pcol, s0, b, t, l, cb):
    ng = t // (l * cb)
    w3 = 3 * REC_W
    return pl.pallas_call(
        functools.partial(_gdn_scan_kernel, l=l, cb=cb, ng=ng),
        grid=(b, ng),
        in_specs=[pl.BlockSpec((cb * l, w3), lambda bi, i: (bi * ng + i, 0)),
                  pl.BlockSpec((cb, REC_W, l), lambda bi, i: (bi * ng + i, 0, 0)),
                  pl.BlockSpec((cb * l, LANES), lambda bi, i: (bi * ng + i, COL_GATE // LANES)),
                  pl.BlockSpec((cb, LANES, l), lambda bi, i: (bi * ng + i, 0, 0)),
                  pl.BlockSpec((SUBLANES, LANES), lambda bi, i: (0, 0)),
                  pl.BlockSpec((LANES, SUBLANES), lambda bi, i: (0, 0)),
                  pl.BlockSpec((1, GDN_HEADS, HEAD_DIM, HEAD_DIM), lambda bi, i: (bi, 0, 0, 0))],
        out_specs=[pl.BlockSpec((cb * l, REC_W), lambda bi, i: (bi * ng + i, 0)),
                   pl.BlockSpec((1, GDN_HEADS, HEAD_DIM, HEAD_DIM), lambda bi, i: (bi, 0, 0, 0))],
        out_shape=[jax.ShapeDtypeStruct((b * t, REC_W), F32),
                   jax.ShapeDtypeStruct((b, GDN_HEADS, HEAD_DIM, HEAD_DIM), F32)],
        scratch_shapes=[pltpu.VMEM((GDN_HEADS, HEAD_DIM, HEAD_DIM), F32)],
        compiler_params=_cparams("parallel", "arbitrary"),
        name="gdn_scan",
    )(qkv, k_t, proj, gate_t, prow, pcol, s0)


def _gdn_scan_pair_kernel(qkv_ref, kt_ref, gt_ref, gtt_ref, prow_ref, pcol_ref, s0_ref, o_ref, s_out_ref, s_scr,
                          *, l, cb, ng):
    gi = pl.program_id(1)
    npair = GDN_HEADS // 2
    row128 = lax.broadcasted_iota(jnp.int32, (LANES, LANES), 0)
    col128 = lax.broadcasted_iota(jnp.int32, (LANES, LANES), 1)
    blockmask = (row128 < HEAD_DIM) == (col128 < HEAD_DIM)

    def bd(a):
        return jnp.where(blockmask, _stack(a, a), 0.0)

    @pl.when(gi == 0)
    def _():
        for p in range(npair):
            s_scr[p] = bd(jnp.concatenate([s0_ref[0, 2 * p], s0_ref[0, 2 * p + 1]], axis=1))

    trow = lax.broadcasted_iota(jnp.int32, (l, LANES), 0)
    tcol = lax.broadcasted_iota(jnp.int32, (l, LANES), 1) % HEAD_DIM
    left = lax.broadcasted_iota(jnp.int32, (l, LANES), 1) < HEAD_DIM
    tri = trow >= tcol
    stri = trow > tcol
    eye = _ones_where(trow == tcol, F32)
    krow_left = lax.broadcasted_iota(jnp.int32, (LANES, l), 0) < HEAD_DIM
    n_double = max(int(np.ceil(np.log2(l))) - 1, 0)
    ps = [(j, p) for j in range(cb) for p in range(npair)]
    gates = [_gates(gt_ref[j * l:(j + 1) * l, :], gtt_ref[j], prow_ref, pcol_ref, l) for j in range(cb)]

    def pair_cols(j, group, p):
        c0 = group * REC_W + p * LANES
        return qkv_ref[j * l:(j + 1) * l, c0:c0 + LANES]

    def col_pair(a, lane0, p):
        return jnp.where(left, a[:, lane0 + 2 * p:lane0 + 2 * p + 1], a[:, lane0 + 2 * p + 1:lane0 + 2 * p + 2])

    q = {x: pair_cols(x[0], 0, x[1]) for x in ps}
    k = {x: pair_cols(x[0], 1, x[1]) for x in ps}
    v = {x: pair_cols(x[0], 2, x[1]) for x in ps}
    ktp = {x: kt_ref[x[0], x[1] * LANES:(x[1] + 1) * LANES, :] for x in ps}
    beta = {x: col_pair(gates[x[0]][0], GATE_BETA, x[1]) for x in ps}
    g_col = {x: col_pair(gates[x[0]][1], GATE_G, x[1]) for x in ps}
    g_row_a = {x: gates[x[0]][3][GATE_G + 2 * x[1]:GATE_G + 2 * x[1] + 1, :] for x in ps}
    g_row_b = {x: gates[x[0]][3][GATE_G + 2 * x[1] + 1:GATE_G + 2 * x[1] + 2, :] for x in ps}
    g_row = {x: jnp.concatenate([g_row_a[x], g_row_b[x]], axis=1) for x in ps}
    g_last = {x: g_col[x][l - 1:l, :] for x in ps}
    decay = {x: jnp.where(tri, jnp.exp(jnp.where(tri, g_col[x] - g_row[x], 0.0)), 0.0) for x in ps}
    kb = {x: k[x] * beta[x] for x in ps}
    e_g = {x: jnp.exp(g_col[x]) for x in ps}
    bd_kt = {x: jnp.concatenate([jnp.where(krow_left, ktp[x], 0.0), jnp.where(krow_left, 0.0, ktp[x])], axis=1)
             for x in ps}
    kq = {x: _dot(_stack(kb[x], q[x]), bd_kt[x]) for x in ps}
    y = {x: -jnp.where(stri, kq[x][:l] * decay[x], 0.0) for x in ps}
    tinv = {x: eye + y[x] for x in ps}
    if n_double > 0:
        y = {x: _dot(y[x], bd(y[x])) for x in ps}
        for _ in range(n_double - 1):
            ty = {x: _dot(_stack(tinv[x], y[x]), bd(y[x])) for x in ps}
            tinv = {x: tinv[x] + ty[x][:l] for x in ps}
            y = {x: ty[x][l:] for x in ps}
        tinv = {x: tinv[x] + _dot(tinv[x], bd(y[x])) for x in ps}
    sol = {x: _dot(tinv[x], jnp.concatenate([bd(v[x] * beta[x]), bd(kb[x] * e_g[x])], axis=1)) for x in ps}
    lhs_s = {x: _stack(q[x] * e_g[x], sol[x][:, LANES:]) for x in ps}
    g_last_a = {x: g_last[x][:, 0:1] for x in ps}
    g_last_b = {x: g_last[x][:, HEAD_DIM:HEAD_DIM + 1] for x in ps}
    ktd = {x: ktp[x] * jnp.where(krow_left, jnp.exp(g_last_a[x] - g_row_a[x]), jnp.exp(g_last_b[x] - g_row_b[x]))
           for x in ps}
    attn = {x: kq[x][l:] * decay[x] for x in ps}
    e_last = {x: jnp.exp(g_last[x]) for x in ps}
    s = [s_scr[p] for p in range(npair)]
    for j in range(cb):
        t = [_dot(lhs_s[(j, p)], s[p]) for p in range(npair)]
        u = [sol[(j, p)][:, :LANES] - t[p][l:] for p in range(npair)]
        du = [_dot(ktd[(j, p)], u[p]) for p in range(npair)]
        ou = [_dot(attn[(j, p)], bd(u[p])) for p in range(npair)]
        s = [e_last[(j, p)] * s[p] + jnp.where(blockmask, du[p], 0.0) for p in range(npair)]
        for p in range(npair):
            o_ref[j * l:(j + 1) * l, p * LANES:(p + 1) * LANES] = t[p][:l] + ou[p]
    for p in range(npair):
        s_scr[p] = s[p]

    @pl.when(gi == ng - 1)
    def _():
        for p in range(npair):
            s_out_ref[0, 2 * p] = s_scr[p, 0:HEAD_DIM, 0:HEAD_DIM]
            s_out_ref[0, 2 * p + 1] = s_scr[p, HEAD_DIM:LANES, HEAD_DIM:LANES]


def _gdn_scan_pair(qkv, k_t, proj, gate_t, prow, pcol, s0, b, t, l, cb):
    ng = t // (l * cb)
    w3 = 3 * REC_W
    return pl.pallas_call(
        functools.partial(_gdn_scan_pair_kernel, l=l, cb=cb, ng=ng),
        grid=(b, ng),
        in_specs=[pl.BlockSpec((cb * l, w3), lambda bi, i: (bi * ng + i, 0)),
                  pl.BlockSpec((cb, REC_W, l), lambda bi, i: (bi * ng + i, 0, 0)),
                  pl.BlockSpec((cb * l, LANES), lambda bi, i: (bi * ng + i, COL_GATE // LANES)),
                  pl.BlockSpec((cb, LANES, l), lambda bi, i: (bi * ng + i, 0, 0)),
                  pl.BlockSpec((SUBLANES, LANES), lambda bi, i: (0, 0)),
                  pl.BlockSpec((LANES, SUBLANES), lambda bi, i: (0, 0)),
                  pl.BlockSpec((1, GDN_HEADS, HEAD_DIM, HEAD_DIM), lambda bi, i: (bi, 0, 0, 0))],
        out_specs=[pl.BlockSpec((cb * l, REC_W), lambda bi, i: (bi * ng + i, 0)),
                   pl.BlockSpec((1, GDN_HEADS, HEAD_DIM, HEAD_DIM), lambda bi, i: (bi, 0, 0, 0))],
        out_shape=[jax.ShapeDtypeStruct((b * t, REC_W), F32),
                   jax.ShapeDtypeStruct((b, GDN_HEADS, HEAD_DIM, HEAD_DIM), F32)],
        scratch_shapes=[pltpu.VMEM((GDN_HEADS // 2, LANES, LANES), F32)],
        compiler_params=_cparams("parallel", "arbitrary"),
        name="gdn_scan_pair",
    )(qkv, k_t, proj, gate_t, prow, pcol, s0)


def _ml_scan_kernel(qkv_ref, kt_ref, gt_ref, gtt_ref, prow_ref, pcol_ref, c0_ref, n0_ref, m0_ref,
                    o_ref, c_out_ref, n_out_ref, m_out_ref, c_scr, n_scr, m_scr, *, l, cb, ng):
    gi = pl.program_id(1)

    @pl.when(gi == 0)
    def _():
        c_scr[...] = c0_ref[0]
        n_scr[...] = n0_ref[0]
        m_scr[...] = m0_ref[0]

    r = lax.broadcasted_iota(jnp.int32, (l, l), 0)
    c = lax.broadcasted_iota(jnp.int32, (l, l), 1)
    tri = r >= c
    lane = lax.broadcasted_iota(jnp.int32, (1, LANES), 1)
    m_all = m_scr[...]
    hs = range(ML_HEADS)
    ps = [(j, h) for j in range(cb) for h in hs]
    kscale = HEAD_DIM ** -0.5
    gates = [_gates(gt_ref[j * l:(j + 1) * l, :], gtt_ref[j], prow_ref, pcol_ref, l) for j in range(cb)]

    def head_cols(j, group, h):
        return qkv_ref[j * l:(j + 1) * l, group * REC_W + h * HEAD_DIM:group * REC_W + (h + 1) * HEAD_DIM]

    q = {p: head_cols(p[0], 0, p[1]) for p in ps}
    k = {p: head_cols(p[0], 1, p[1]) * kscale for p in ps}
    v = {p: head_cols(p[0], 2, p[1]) for p in ps}
    kt = {p: kt_ref[p[0], p[1] * HEAD_DIM:(p[1] + 1) * HEAD_DIM, :] * kscale for p in ps}
    ig_col = {p: gates[p[0]][0][:, GATE_I + p[1]:GATE_I + p[1] + 1] for p in ps}
    ig_row = {p: gates[p[0]][2][GATE_I + p[1]:GATE_I + p[1] + 1, :] for p in ps}
    f_col = {p: gates[p[0]][1][:, GATE_F + p[1]:GATE_F + p[1] + 1] for p in ps}
    f_row = {p: gates[p[0]][3][GATE_F + p[1]:GATE_F + p[1] + 1, :] for p in ps}
    f_last = {p: f_col[p][l - 1:l, :] for p in ps}
    qk = {p: _dot(q[p], kt[p]) for p in ps}
    d = {p: jnp.where(tri, f_col[p] - f_row[p] + ig_row[p], -jnp.inf) for p in ps}
    d_max = {p: jnp.max(d[p], axis=1, keepdims=True) for p in ps}
    m_prev, m_t = {}, {}
    m_run = [m_all[:, h:h + 1] for h in hs]
    for j in range(cb):
        for h in hs:
            m_prev[(j, h)] = m_run[h]
            m_t[(j, h)] = jnp.maximum(f_col[(j, h)] + m_run[h], d_max[(j, h)])
            m_run[h] = m_t[(j, h)][l - 1:l, :]
    m_new = {p: m_t[p][l - 1:l, :] for p in ps}
    w = {p: jnp.exp(d[p] - m_t[p]) * qk[p] for p in ps}
    c_inter = {p: jnp.exp(f_col[p] + m_prev[p] - m_t[p]) for p in ps}
    wv = {p: _dot(w[p], v[p]) for p in ps}
    w_sum = {p: jnp.sum(w[p], axis=1, keepdims=True) for p in ps}
    e_m = {p: jnp.exp(-m_t[p]) for p in ps}
    w_end_row = {p: jnp.exp(f_last[p] - f_row[p] + ig_row[p] - m_new[p]) for p in ps}
    w_end_col = {p: jnp.exp(f_last[p] - f_col[p] + ig_col[p] - m_new[p]) for p in ps}
    c_prev = {p: jnp.exp(f_last[p] + m_prev[p] - m_new[p]) for p in ps}
    dc = {p: _dot(kt[p] * w_end_row[p], v[p]) for p in ps}
    dn = {p: jnp.sum(w_end_col[p] * k[p], axis=0, keepdims=True) for p in ps}
    c_h = [c_scr[h] for h in hs]
    n_h = [n_scr[h:h + 1, :] for h in hs]
    for j in range(cb):
        for h in hs:
            p = (j, h)
            num = c_inter[p] * _dot(q[p], c_h[h]) + wv[p]
            den = c_inter[p] * jnp.sum(q[p] * n_h[h], axis=1, keepdims=True) + w_sum[p]
            o_ref[j * l:(j + 1) * l, h * HEAD_DIM:(h + 1) * HEAD_DIM] = num / jnp.maximum(jnp.abs(den), e_m[p])
            c_h[h] = c_prev[p] * c_h[h] + dc[p]
            n_h[h] = c_prev[p] * n_h[h] + dn[p]
    m_next = m_all
    for h in hs:
        c_scr[h] = c_h[h]
        n_scr[h:h + 1, :] = n_h[h]
        m_next = jnp.where(lane == h, m_run[h], m_next)
    m_scr[...] = m_next

    @pl.when(gi == ng - 1)
    def _():
        c_out_ref[0] = c_scr[...]
        n_out_ref[0] = n_scr[...]
        m_out_ref[0] = m_scr[...]


def _ml_scan(proj, k_t, gate_t, prow, pcol, c0, n0, m0, b, t, l, cb):
    ng = t // (l * cb)
    w3 = 3 * REC_W
    st = lambda bi, i: (bi, 0, 0, 0)
    st3 = lambda bi, i: (bi, 0, 0)
    return pl.pallas_call(
        functools.partial(_ml_scan_kernel, l=l, cb=cb, ng=ng),
        grid=(b, ng),
        in_specs=[pl.BlockSpec((cb * l, w3), lambda bi, i: (bi * ng + i, COL_MQKV // w3)),
                  pl.BlockSpec((cb, REC_W, l), lambda bi, i: (bi * ng + i, 0, 0)),
                  pl.BlockSpec((cb * l, LANES), lambda bi, i: (bi * ng + i, COL_GATE // LANES)),
                  pl.BlockSpec((cb, LANES, l), lambda bi, i: (bi * ng + i, 0, 0)),
                  pl.BlockSpec((SUBLANES, LANES), lambda bi, i: (0, 0)),
                  pl.BlockSpec((LANES, SUBLANES), lambda bi, i: (0, 0)),
                  pl.BlockSpec((1, ML_HEADS, HEAD_DIM, HEAD_DIM), st),
                  pl.BlockSpec((1, SUBLANES, HEAD_DIM), st3),
                  pl.BlockSpec((1, 1, LANES), st3)],
        out_specs=[pl.BlockSpec((cb * l, REC_W), lambda bi, i: (bi * ng + i, 0)),
                   pl.BlockSpec((1, ML_HEADS, HEAD_DIM, HEAD_DIM), st),
                   pl.BlockSpec((1, SUBLANES, HEAD_DIM), st3),
                   pl.BlockSpec((1, 1, LANES), st3)],
        out_shape=[jax.ShapeDtypeStruct((b * t, REC_W), F32),
                   jax.ShapeDtypeStruct((b, ML_HEADS, HEAD_DIM, HEAD_DIM), F32),
                   jax.ShapeDtypeStruct((b, SUBLANES, HEAD_DIM), F32),
                   jax.ShapeDtypeStruct((b, 1, LANES), F32)],
        scratch_shapes=[pltpu.VMEM((ML_HEADS, HEAD_DIM, HEAD_DIM), F32),
                        pltpu.VMEM((SUBLANES, HEAD_DIM), F32),
                        pltpu.VMEM((1, LANES), F32)],
        compiler_params=_cparams("parallel", "arbitrary"),
        name="ml_scan",
    )(proj, k_t, proj, gate_t, prow, pcol, c0, n0, m0)


def _ml_scan_t_kernel(qkv_ref, qkvt_ref, gt_ref, gtt_ref, prow_ref, pcol_ref, c0_ref, n0_ref, m0_ref,
                      o_ref, c_out_ref, n_out_ref, m_out_ref, c_scr, n_scr, m_scr, *, l, cb, ng):
    gi = pl.program_id(1)

    @pl.when(gi == 0)
    def _():
        c_scr[...] = c0_ref[0]
        n_scr[...] = n0_ref[0]
        m_scr[...] = m0_ref[0]

    r = lax.broadcasted_iota(jnp.int32, (l, l), 0)
    c = lax.broadcasted_iota(jnp.int32, (l, l), 1)
    tri_t = r <= c
    lane = lax.broadcasted_iota(jnp.int32, (1, LANES), 1)
    m_all = m_scr[...]
    hs = range(ML_HEADS)
    ps = [(j, h) for j in range(cb) for h in hs]
    kscale = HEAD_DIM ** -0.5
    zeros_t = jnp.zeros((HEAD_DIM, l), F32)
    gates = [_gates(gt_ref[j * l:(j + 1) * l, :], gtt_ref[j], prow_ref, pcol_ref, l) for j in range(cb)]

    def t_rows(j, group, h):
        return qkvt_ref[j, group * REC_W + h * HEAD_DIM:group * REC_W + (h + 1) * HEAD_DIM, :]

    qt = {p: t_rows(p[0], 0, p[1]) for p in ps}
    qt_pad = {p: (_stack(qt[p], zeros_t) if p[1] % 2 == 0 else _stack(zeros_t, qt[p])) for p in ps}
    vt = {p: t_rows(p[0], 2, p[1]) for p in ps}
    k2 = {p: qkv_ref[p[0] * l:(p[0] + 1) * l, REC_W + (p[1] // 2) * LANES:REC_W + (p[1] // 2 + 1) * LANES] * kscale
          for p in ps}
    ig_col = {p: gates[p[0]][0][:, GATE_I + p[1]:GATE_I + p[1] + 1] for p in ps}
    ig_row = {p: gates[p[0]][2][GATE_I + p[1]:GATE_I + p[1] + 1, :] for p in ps}
    f_col = {p: gates[p[0]][1][:, GATE_F + p[1]:GATE_F + p[1] + 1] for p in ps}
    f_row = {p: gates[p[0]][3][GATE_F + p[1]:GATE_F + p[1] + 1, :] for p in ps}
    f_last = {p: f_row[p][:, l - 1:l] for p in ps}
    kq = {p: _dot(k2[p], qt_pad[p]) for p in ps}
    d = {p: jnp.where(tri_t, f_row[p] + (ig_col[p] - f_col[p]), -jnp.inf) for p in ps}
    d_max = {p: jnp.max(d[p], axis=0, keepdims=True) for p in ps}
    m_prev, m_t = {}, {}
    m_run = [m_all[:, h:h + 1] for h in hs]
    for j in range(cb):
        for h in hs:
            m_prev[(j, h)] = m_run[h]
            m_t[(j, h)] = jnp.maximum(f_row[(j, h)] + m_run[h], d_max[(j, h)])
            m_run[h] = m_t[(j, h)][:, l - 1:l]
    m_new = {p: m_t[p][:, l - 1:l] for p in ps}
    w = {p: jnp.exp(d[p] - m_t[p]) * kq[p] for p in ps}
    c_inter = {p: jnp.exp(f_row[p] + m_prev[p] - m_t[p]) for p in ps}
    vw = {p: _dot(vt[p], w[p]) for p in ps}
    w_sum = {p: jnp.sum(w[p], axis=0, keepdims=True) for p in ps}
    e_m = {p: jnp.exp(-m_t[p]) for p in ps}
    w_end_col = {p: jnp.exp(f_last[p] - f_col[p] + ig_col[p] - m_new[p]) for p in ps}
    c_prev = {p: jnp.exp(f_last[p] + m_prev[p] - m_new[p]) for p in ps}
    wk = {p: w_end_col[p] * k2[p] for p in ps}
    dc = {p: _dot(vt[p], wk[p]) for p in ps}
    dn = {p: jnp.sum(wk[p], axis=0, keepdims=True) for p in ps}
    c_h = [c_scr[h] for h in hs]
    n_h = [n_scr[h:h + 1, :] for h in hs]
    ht = {}
    for j in range(cb):
        for h in hs:
            p = (j, h)
            num = c_inter[p] * _dot(c_h[h], qt_pad[p]) + vw[p]
            den = c_inter[p] * _dot(n_h[h], qt_pad[p]) + w_sum[p]
            ht[p] = num / jnp.maximum(jnp.abs(den), e_m[p])
            c_h[h] = c_prev[p] * c_h[h] + dc[p]
            n_h[h] = c_prev[p] * n_h[h] + dn[p]
    for i in range(ML_HEADS // 2):
        pair_t = _stack(jnp.concatenate([ht[(j, 2 * i)] for j in range(cb)], axis=1),
                        jnp.concatenate([ht[(j, 2 * i + 1)] for j in range(cb)], axis=1))
        o_ref[:, i * LANES:(i + 1) * LANES] = pair_t.T
    m_next = m_all
    for h in hs:
        c_scr[h] = c_h[h]
        n_scr[h:h + 1, :] = n_h[h]
        m_next = jnp.where(lane == h, m_run[h], m_next)
    m_scr[...] = m_next

    @pl.when(gi == ng - 1)
    def _():
        c_out_ref[0] = c_scr[...]
        n_out_ref[0] = n_scr[...]
        m_out_ref[0] = m_scr[...]


def _ml_scan_t(proj, qkv_t, gate_t, prow, pcol, c0, n0, m0, b, t, l, cb):
    ng = t // (l * cb)
    w3 = 3 * REC_W
    st = lambda bi, i: (bi, 0, 0, 0)
    st3 = lambda bi, i: (bi, 0, 0)
    return pl.pallas_call(
        functools.partial(_ml_scan_t_kernel, l=l, cb=cb, ng=ng),
        grid=(b, ng),
        in_specs=[pl.BlockSpec((cb * l, w3), lambda bi, i: (bi * ng + i, COL_MQKV // w3)),
                  pl.BlockSpec((cb, w3, l), lambda bi, i: (bi * ng + i, 0, 0)),
                  pl.BlockSpec((cb * l, LANES), lambda bi, i: (bi * ng + i, COL_GATE // LANES)),
                  pl.BlockSpec((cb, LANES, l), lambda bi, i: (bi * ng + i, 0, 0)),
                  pl.BlockSpec((SUBLANES, LANES), lambda bi, i: (0, 0)),
                  pl.BlockSpec((LANES, SUBLANES), lambda bi, i: (0, 0)),
                  pl.BlockSpec((1, ML_HEADS, HEAD_DIM, LANES), st),
                  pl.BlockSpec((1, SUBLANES, LANES), st3),
                  pl.BlockSpec((1, 1, LANES), st3)],
        out_specs=[pl.BlockSpec((cb * l, REC_W), lambda bi, i: (bi * ng + i, 0)),
                   pl.BlockSpec((1, ML_HEADS, HEAD_DIM, LANES), st),
                   pl.BlockSpec((1, SUBLANES, LANES), st3),
                   pl.BlockSpec((1, 1, LANES), st3)],
        out_shape=[jax.ShapeDtypeStruct((b * t, REC_W), F32),
                   jax.ShapeDtypeStruct((b, ML_HEADS, HEAD_DIM, LANES), F32),
                   jax.ShapeDtypeStruct((b, SUBLANES, LANES), F32),
                   jax.ShapeDtypeStruct((b, 1, LANES), F32)],
        scratch_shapes=[pltpu.VMEM((ML_HEADS, HEAD_DIM, LANES), F32),
                        pltpu.VMEM((SUBLANES, LANES), F32),
                        pltpu.VMEM((1, LANES), F32)],
        compiler_params=_cparams("parallel", "arbitrary"),
        name="ml_scan_t",
    )(proj, qkv_t, proj, gate_t, prow, pcol, c0, n0, m0)


def _post_kernel(x_ref, osb_ref, og_ref, hm_ref, gz_ref, mo_ref, w_ref, e_ref, gnw_ref, mnw_ref,
                 lg_ref, lb_ref, o_ref, ob_ref):
    e = e_ref[...]
    og = og_ref[...]
    hm = hm_ref[...]
    og_ms = _dot_exact_rhs(og * og, e, NORM_SUM_PARTS) * (1.0 / HEAD_DIM)
    hm_ms = _dot_exact_rhs(hm * hm, e, NORM_SUM_PARTS) * (1.0 / HEAD_DIM)
    gz = gz_ref[...]
    o_g = og * lax.rsqrt(og_ms + NORM_EPS) * gnw_ref[...] * (gz * _sigmoid(gz))
    o_m = _sigmoid(mo_ref[...]) * (hm * lax.rsqrt(hm_ms + NORM_EPS) * mnw_ref[...])
    mixed = (_dot(osb_ref[...], w_ref[0:SB_W, :])
             + _dot(o_g, w_ref[SB_W:SB_W + REC_W, :])
             + _dot(o_m, w_ref[SB_W + REC_W:SB_W + 2 * REC_W, :]))
    y = _layer_norm(DEEPNORM_ALPHA * x_ref[...] + mixed, lg_ref[...], lb_ref[...])
    o_ref[...] = y
    ob_ref[...] = y.astype(BF16)


def _post(x, osb, og, hm, proj, w_out, e_heads, gnw, mnw, lg, lb, tm):
    n, d = x.shape
    row = lambda i: (i, 0)
    fixed = lambda i: (0, 0)
    return pl.pallas_call(
        _post_kernel,
        grid=(n // tm,),
        in_specs=[pl.BlockSpec((tm, d), row),
                  pl.BlockSpec((tm, SB_W), row),
                  pl.BlockSpec((tm, REC_W), row),
                  pl.BlockSpec((tm, REC_W), row),
                  pl.BlockSpec((tm, REC_W), lambda i: (i, COL_GZ // REC_W)),
                  pl.BlockSpec((tm, REC_W), lambda i: (i, COL_MO // REC_W)),
                  pl.BlockSpec(w_out.shape, fixed),
                  pl.BlockSpec((REC_W, REC_W), fixed),
                  pl.BlockSpec((1, REC_W), fixed),
                  pl.BlockSpec((1, REC_W), fixed),
                  pl.BlockSpec((1, d), fixed),
                  pl.BlockSpec((1, d), fixed)],
        out_specs=[pl.BlockSpec((tm, d), row), pl.BlockSpec((tm, d), row)],
        out_shape=[jax.ShapeDtypeStruct((n, d), F32), jax.ShapeDtypeStruct((n, d), BF16)],
        compiler_params=_cparams("parallel"),
        name="post",
    )(x, osb, og, hm, proj, proj, w_out, e_heads, gnw, mnw, lg, lb)


def _memattn_kernel(x_ref, xb_ref, wq_ref, wo_ref, mk_ref, mv_ref, lg_ref, lb_ref, o_ref, ob_ref):
    d = x_ref.shape[1]
    hd = d // MEM_HEADS
    q = jnp.dot(xb_ref[...], wq_ref[...], preferred_element_type=F32)
    out = None
    for h in range(MEM_HEADS):
        sl = slice(h * hd, (h + 1) * hd)
        s = _dot_nt(q[:, sl], mk_ref[0, :, sl]) * (hd ** -0.5)
        p = jnp.exp(s - jnp.max(s, axis=1, keepdims=True))
        o_h = _dot(p, mv_ref[0, :, sl]) / jnp.sum(p, axis=1, keepdims=True)
        t = _dot(o_h, wo_ref[sl, :])
        out = t if out is None else out + t
    y = _layer_norm(DEEPNORM_ALPHA * x_ref[...] + out, lg_ref[...], lb_ref[...])
    o_ref[...] = y
    ob_ref[...] = y.astype(BF16)


def _memattn(x, xb, w_cq, w_co, mk, mv, lg, lb, b, t, tm):
    n, d = x.shape
    nt = t // tm
    nm = mk.shape[1]
    row = lambda bi, i: (bi * nt + i, 0)
    fixed = lambda bi, i: (0, 0)
    return pl.pallas_call(
        _memattn_kernel,
        grid=(b, nt),
        in_specs=[pl.BlockSpec((tm, d), row),
                  pl.BlockSpec((tm, d), row),
                  pl.BlockSpec((d, d), fixed),
                  pl.BlockSpec((d, d), fixed),
                  pl.BlockSpec((1, nm, d), lambda bi, i: (bi, 0, 0)),
                  pl.BlockSpec((1, nm, d), lambda bi, i: (bi, 0, 0)),
                  pl.BlockSpec((1, d), fixed),
                  pl.BlockSpec((1, d), fixed)],
        out_specs=[pl.BlockSpec((tm, d), row), pl.BlockSpec((tm, d), row)],
        out_shape=[jax.ShapeDtypeStruct((n, d), F32), jax.ShapeDtypeStruct((n, d), BF16)],
        compiler_params=_cparams("parallel", "parallel"),
        name="memattn",
    )(x, xb, w_cq, w_co, mk, mv, lg, lb)


def _ffn_kernel(x_ref, xb_ref, wu_ref, wd_ref, lg_ref, lb_ref, o_ref, ob_ref, *, tf):
    xb = xb_ref[...]
    dff = wd_ref.shape[0]
    acc = None
    for j in range(dff // tf):
        gate = jnp.dot(xb, wu_ref[:, j * tf:(j + 1) * tf], preferred_element_type=F32)
        up = jnp.dot(xb, wu_ref[:, dff + j * tf:dff + (j + 1) * tf], preferred_element_type=F32)
        t = _dot(gate * _sigmoid(gate) * up, wd_ref[j * tf:(j + 1) * tf, :])
        acc = t if acc is None else acc + t
    y = _layer_norm(DEEPNORM_ALPHA * x_ref[...] + acc, lg_ref[...], lb_ref[...])
    o_ref[...] = y
    ob_ref[...] = y.astype(BF16)


def _ffn(x, xb, w_up, w_down, lg, lb, tm, tf):
    n, d = x.shape
    row = lambda i: (i, 0)
    fixed = lambda i: (0, 0)
    return pl.pallas_call(
        functools.partial(_ffn_kernel, tf=tf),
        grid=(n // tm,),
        in_specs=[pl.BlockSpec((tm, d), row),
                  pl.BlockSpec((tm, d), row),
                  pl.BlockSpec(w_up.shape, fixed, pipeline_mode=pl.Buffered(1)),
                  pl.BlockSpec(w_down.shape, fixed, pipeline_mode=pl.Buffered(1)),
                  pl.BlockSpec((1, d), fixed),
                  pl.BlockSpec((1, d), fixed)],
        out_specs=[pl.BlockSpec((tm, d), row), pl.BlockSpec((tm, d), row)],
        out_shape=[jax.ShapeDtypeStruct((n, d), F32), jax.ShapeDtypeStruct((n, d), BF16)],
        compiler_params=_cparams("parallel"),
        name="ffn",
    )(x, xb, w_up, w_down, lg, lb)


def _pack_layer(p):
    w_in = p['w_in']
    d = w_in.shape[0]
    o_sb, o_gqkv, o_gz = 0, 3 * SB_W, 3 * SB_W + 3 * REC_W
    o_gb = o_gz + REC_W
    o_mqkv = o_gb + 2 * GDN_HEADS
    o_mo = o_mqkv + 3 * REC_W
    o_mi = o_mo + REC_W
    gate_cols = jnp.concatenate([w_in[:, o_gb:o_gb + 2 * GDN_HEADS], w_in[:, o_mi:o_mi + 2 * ML_HEADS]], axis=1)
    used = COL_GATE + gate_cols.shape[1]
    w_al = jnp.concatenate([w_in[:, o_gqkv:o_gqkv + 3 * REC_W], w_in[:, o_mqkv:o_mqkv + 3 * REC_W],
                            w_in[:, o_gz:o_gz + REC_W], w_in[:, o_mo:o_mo + REC_W],
                            w_in[:, o_sb:o_sb + 3 * SB_W], gate_cols,
                            jnp.zeros((d, PROJ_W - used), w_in.dtype)], axis=1).astype(BF16)
    prow = jnp.zeros((SUBLANES, LANES), F32)
    prow = prow.at[0, GATE_G:GATE_G + GDN_HEADS].set(p['gdn_dt_bias'])
    prow = prow.at[0, GATE_I:GATE_I + ML_HEADS].set(p['mlstm_i_bias'])
    prow = prow.at[0, GATE_F:GATE_F + ML_HEADS].set(p['mlstm_f_bias'])
    prow = prow.at[1, GATE_G:GATE_G + GDN_HEADS].set(p['gdn_A_log'])
    conv_w8 = jnp.zeros((SUBLANES, 3 * REC_W), F32).at[0:CONV_W].set(p['gdn_conv_w'])
    return dict(
        w_al=w_al, prow=prow, pcol=prow.T, conv_w8=conv_w8,
        gnw=jnp.tile(p['gdn_norm_w'], GDN_HEADS)[None, :], mnw=p['mlstm_norm_w'][None, :],
        w_out=p['w_out'].astype(BF16), w_cq=p['w_cq'].astype(BF16), w_co=p['w_co'].astype(BF16),
        w_up=p['w_up'].astype(BF16), w_down=p['w_down'].astype(BF16),
        ln1=(p['ln1_g'][None, :], p['ln1_b'][None, :]), ln2=(p['ln2_g'][None, :], p['ln2_b'][None, :]),
        ln3=(p['ln3_g'][None, :], p['ln3_b'][None, :]))


def _head_consts():
    hid = np.arange(REC_W) // HEAD_DIM
    return jnp.asarray(hid[:, None] == hid[None, :], BF16)


def _sb_layout(k, v, tk):
    b, tkv, _ = k.shape
    nkb = tkv // tk
    kt = k.reshape(b, nkb, tk, SB_W).transpose(0, 1, 3, 2)
    vh = v.reshape(b, nkb, tk, SB_HEADS, HEAD_DIM).transpose(0, 3, 1, 2, 4)
    vh = vh.reshape(b * SB_HEADS, nkb, tk, HEAD_DIM)
    return kt.astype(BF16), vh.astype(BF16)


def _to_pair_lanes(a):
    z = jnp.zeros_like(a)
    even = (jnp.arange(a.shape[1]) % 2 == 0)[None, :, None, None]
    return jnp.where(even, jnp.concatenate([a, z], axis=-1), jnp.concatenate([z, a], axis=-1))


def _from_pair_lanes(a):
    even = (jnp.arange(a.shape[1]) % 2 == 0)[None, :, None, None]
    return jnp.where(even, a[..., :HEAD_DIM], a[..., HEAD_DIM:])


def _time_on_lanes(a, groups, width):
    return a.reshape(groups, width, a.shape[1]).transpose(0, 2, 1)


def _trunk_layer(x, xb, pk, e_heads, mk, mv, b, t, cfg, sb_past, conv_ctx, gdn_s, ml_c, ml_n, ml_m):
    n, d = x.shape
    l = min(t, CHUNK)
    nc = t // l
    cb = min(cfg['scan_cb'], nc)
    ng = b * nc // cb
    tk = cfg['sb_tk']
    ctx8 = jnp.pad(conv_ctx, ((0, 0), (SUBLANES - (CONV_W - 1), 0), (0, 0)))
    if sb_past is None:
        proj, kt, vh, mk_t, gate_t, gqkv_act, gk_t = _proj(x if xb is None else xb, pk['w_al'], ctx8,
                                                           pk['conv_w8'], e_heads, b, t, cfg['tm'])
        kt = kt.reshape(b, t // tk, SB_W, tk)
        q_off = 0
    else:
        proj = _matmul(xb, pk['w_al'], cfg['tm'], 1024)
    sk = proj[:, COL_SB + SB_W:COL_SB + 2 * SB_W].reshape(b, t, SB_W)
    sv = proj[:, COL_SB + 2 * SB_W:COL_SB + 3 * SB_W].reshape(b, t, SB_W)
    if sb_past is not None:
        past_k, past_v = sb_past
        q_off = past_k.shape[1]
        kv_len = q_off + t
        kv_pad = -(-kv_len // tk) * tk - kv_len
        kk = jnp.pad(jnp.concatenate([past_k.reshape(b, q_off, SB_W), sk], axis=1), ((0, 0), (0, kv_pad), (0, 0)))
        vv = jnp.pad(jnp.concatenate([past_v.reshape(b, q_off, SB_W), sv], axis=1), ((0, 0), (0, kv_pad), (0, 0)))
        kt, vh = _sb_layout(kk, vv, tk)
        gqkv_act = _gdn_pre(proj, ctx8, pk['conv_w8'], e_heads, b, t, cfg['conv_tt'])
        gate_t = _time_on_lanes(proj[:, COL_GATE:COL_GATE + LANES], b * nc, l)
        gk_t = _time_on_lanes(gqkv_act[:, REC_W:2 * REC_W], b * nc, l)
        mk_t = _time_on_lanes(proj[:, COL_MQKV + REC_W:COL_MQKV + 2 * REC_W], b * nc, l)

    o_sb = _sb_attn(proj, kt, vh, b, t, SB_HEADS, cfg['sb_tq'], q_off)
    gdn_scan = _gdn_scan_pair if l == HEAD_DIM else _gdn_scan
    o_g, gdn_s_new = gdn_scan(gqkv_act, gk_t, proj, gate_t, pk['prow'], pk['pcol'], gdn_s, b, t, l,
                              min(cfg['gdn_cb'], nc))
    m0 = jnp.pad(ml_m, ((0, 0), (0, LANES - ML_HEADS)))[:, None, :]
    if sb_past is None:
        c0 = _to_pair_lanes(jnp.swapaxes(ml_c, -1, -2))
        n0 = jnp.pad(_to_pair_lanes(ml_n[:, :, None, :])[:, :, 0, :], ((0, 0), (0, SUBLANES - ML_HEADS), (0, 0)))
        h_m, c_new, n_new, m_new = _ml_scan_t(proj, mk_t, gate_t, pk['prow'], pk['pcol'], c0, n0, m0, b, t, l, cb)
        c_new = jnp.swapaxes(_from_pair_lanes(c_new), -1, -2)
        n_new = _from_pair_lanes(n_new[:, :ML_HEADS, None, :])[:, :, 0, :]
    else:
        n0 = jnp.pad(ml_n, ((0, 0), (0, SUBLANES - ML_HEADS), (0, 0)))
        h_m, c_new, n_new, m_new = _ml_scan(proj, mk_t, gate_t, pk['prow'], pk['pcol'], ml_c, n0, m0, b, t, l, cb)
        n_new = n_new[:, :ML_HEADS, :]

    x1, x1b = _post(x, o_sb, o_g, h_m, proj, pk['w_out'], e_heads, pk['gnw'], pk['mnw'], *pk['ln1'], cfg['post_tm'])
    x2, x2b = _memattn(x1, x1b, pk['w_cq'], pk['w_co'], mk, mv, *pk['ln2'], b, t, cfg['mem_tm'])
    x3, x3b = _ffn(x2, x2b, pk['w_up'], pk['w_down'], *pk['ln3'], cfg['tm'], cfg['ffn_tf'])

    new_conv = proj.reshape(b, t, PROJ_W)[:, t - (CONV_W - 1):, COL_GQKV:COL_GQKV + 3 * REC_W]
    return (x3, x3b, sk.reshape(b, t, SB_HEADS, HEAD_DIM), sv.reshape(b, t, SB_HEADS, HEAD_DIM), new_conv,
            gdn_s_new, c_new, n_new, m_new[:, 0, :ML_HEADS])


def kernel(x_prompt, x_sample, cache_sb_k, cache_sb_v, cache_gdn_conv, state_gdn, state_mlstm_C, state_mlstm_n, state_mlstm_m, cache_mem_k, cache_mem_v, mem_prompt, w_in, gdn_conv_w, gdn_A_log, gdn_dt_bias, gdn_norm_w, mlstm_i_bias, mlstm_f_bias, mlstm_norm_w, w_out, ln1_g, ln1_b, w_cq, w_ckv, w_co, ln2_g, ln2_b, w_up, w_down, ln3_g, ln3_b):
    bp, tp, d = x_prompt.shape
    bs, ts, _ = x_sample.shape
    depth = w_in.shape[0]
    n_mem = mem_prompt.shape[1]
    dff = w_down.shape[1]
    cfg_p = dict(tm=512, post_tm=1024, sb_tq=512, sb_tk=256, conv_tt=512, mem_tm=1024, ffn_tf=dff // 2, scan_cb=4, gdn_cb=16)
    cfg_s = dict(tm=bs * ts, post_tm=bs * ts, sb_tq=ts, sb_tk=256, conv_tt=ts, mem_tm=ts, ffn_tf=dff // 2, scan_cb=1, gdn_cb=1)
    e_heads = _head_consts()

    xp = x_prompt.reshape(bp * tp, d)
    xs = x_sample.reshape(bs * ts, d)
    xpb, xsb = None, xs.astype(BF16)
    mem_b = mem_prompt.reshape(bp * n_mem, d).astype(BF16)
    p_out = [[] for _ in range(9)]
    s_out = [[] for _ in range(7)]
    for li in range(depth):
        p = {'w_in': w_in[li], 'gdn_conv_w': gdn_conv_w[li], 'gdn_A_log': gdn_A_log[li],
             'gdn_dt_bias': gdn_dt_bias[li], 'gdn_norm_w': gdn_norm_w[li], 'mlstm_i_bias': mlstm_i_bias[li],
             'mlstm_f_bias': mlstm_f_bias[li], 'mlstm_norm_w': mlstm_norm_w[li], 'w_out': w_out[li],
             'ln1_g': ln1_g[li], 'ln1_b': ln1_b[li], 'w_cq': w_cq[li], 'w_co': w_co[li],
             'ln2_g': ln2_g[li], 'ln2_b': ln2_b[li], 'w_up': w_up[li], 'w_down': w_down[li],
             'ln3_g': ln3_g[li], 'ln3_b': ln3_b[li]}
        pk = _pack_layer(p)
        mkv = _matmul(mem_b, w_ckv[li].astype(BF16), n_mem, 1024)
        mk_p = mkv[:, :d].reshape(bp, n_mem, d)
        mv_p = mkv[:, d:].reshape(bp, n_mem, d)
        res = _trunk_layer(xp, xpb, pk, e_heads, mk_p.astype(BF16), mv_p.astype(BF16), bp, tp, cfg_p, None,
                           jnp.zeros((bp, CONV_W - 1, 3 * REC_W), F32),
                           jnp.zeros((bp, GDN_HEADS, HEAD_DIM, HEAD_DIM), F32),
                           jnp.zeros((bp, ML_HEADS, HEAD_DIM, HEAD_DIM), F32),
                           jnp.zeros((bp, ML_HEADS, HEAD_DIM), F32),
                           jnp.zeros((bp, ML_HEADS), F32))
        xp, xpb = res[0], res[1]
        hd = d // MEM_HEADS
        for j, a in enumerate(res[2:] + (mk_p.reshape(bp, n_mem, MEM_HEADS, hd), mv_p.reshape(bp, n_mem, MEM_HEADS, hd))):
            p_out[j].append(a)
        res = _trunk_layer(xs, xsb, pk, e_heads,
                           cache_mem_k[li].reshape(bs, n_mem, d).astype(BF16),
                           cache_mem_v[li].reshape(bs, n_mem, d).astype(BF16),
                           bs, ts, cfg_s, (cache_sb_k[li], cache_sb_v[li]), cache_gdn_conv[li],
                           state_gdn[li], state_mlstm_C[li], state_mlstm_n[li], state_mlstm_m[li])
        xs, xsb = res[0], res[1]
        for j, a in enumerate(res[2:]):
            s_out[j].append(a)
    p_st = [jnp.stack(a) for a in p_out]
    s_st = [jnp.stack(a) for a in s_out]
    return (xp.reshape(bp, tp, d), xs.reshape(bs, ts, d), *p_st, *s_st)
```

```python
import functools

import numpy as np
import jax
import jax.numpy as jnp
from jax import lax
from jax.experimental import pallas as pl
from jax.experimental.pallas import tpu as pltpu

F32 = jnp.float32
BF16 = jnp.bfloat16

HEAD_DIM = 64
SB_HEADS = 4
GDN_HEADS = 6
ML_HEADS = 6
REC_W = GDN_HEADS * HEAD_DIM
SB_W = SB_HEADS * HEAD_DIM
CONV_W = 4
CHUNK = 64
MEM_HEADS = 4
LN_EPS = 1e-5
NORM_EPS = 1e-6
DEPTH = 2
DEEPNORM_ALPHA = (2 * DEPTH) ** 0.25
LOG2E = float(np.log2(np.e))
SOFTPLUS2_CLAMP = 120.0
NORM_SUM_PARTS = 1
F32_EXP2_ZERO = 160.0
BOUND_SLACK = 1.01

GROUP = 256
LANES = 128
SUBLANES = 8
VMEM_LIMIT = 56 * 1024 * 1024

PROJ_W = 4096
COL_GQKV = 0
COL_MQKV = 1152
COL_GZ = 2304
COL_MO = 2688
COL_SB = 3072
COL_GATE = 3840
GATE_BETA, GATE_G, GATE_I, GATE_F = 0, 6, 12, 18


def _cparams(*sem):
    return pltpu.CompilerParams(dimension_semantics=sem, vmem_limit_bytes=VMEM_LIMIT)


def _dot(a, b):
    return jnp.dot(a.astype(BF16), b.astype(BF16), preferred_element_type=F32)


def _dot_nt(a, b):
    return lax.dot_general(a.astype(BF16), b.astype(BF16), (((1,), (1,)), ((), ())),
                           preferred_element_type=F32)


def _split(x, parts):
    out = []
    r = x
    for _ in range(parts - 1):
        p = r.astype(BF16)
        out.append(p)
        r = r - p.astype(F32)
    out.append(r.astype(BF16))
    return out


def _dot_exact_lhs(m, x, parts):
    acc = None
    for p in _split(x, parts):
        t = jnp.dot(m, p, preferred_element_type=F32)
        acc = t if acc is None else acc + t
    return acc


def _dot_exact_rhs(x, m, parts):
    acc = None
    for p in _split(x, parts):
        t = jnp.dot(p, m, preferred_element_type=F32)
        acc = t if acc is None else acc + t
    return acc


def _ones_where(mask, dtype):
    return jnp.where(mask, 1.0, 0.0).astype(dtype)


def _sigmoid(x):
    return 1.0 / (1.0 + jnp.exp(-x))


def _layer_norm(y, g, b):
    mu = jnp.mean(y, axis=-1, keepdims=True)
    d = y - mu
    var = jnp.mean(d * d, axis=-1, keepdims=True)
    return d * lax.rsqrt(var + LN_EPS) * g + b


def _mm_kernel(x_ref, w_ref, o_ref):
    o_ref[...] = jnp.dot(x_ref[...], w_ref[...], preferred_element_type=F32).astype(o_ref.dtype)


def _matmul(x, w, tm, tn, out_dtype=F32):
    n, k = x.shape
    m = w.shape[1]
    return pl.pallas_call(
        _mm_kernel,
        grid=(n // tm, m // tn),
        in_specs=[pl.BlockSpec((tm, k), lambda i, j: (i, 0)),
                  pl.BlockSpec((k, tn), lambda i, j: (0, j))],
        out_specs=pl.BlockSpec((tm, tn), lambda i, j: (i, j)),
        out_shape=jax.ShapeDtypeStruct((n, m), out_dtype),
        compiler_params=_cparams("parallel", "parallel"),
        name="proj_matmul",
    )(x, w)


def _gdn_activate(x, first, ctx_ref, w_ref, e_ref, o_ref, kt_ref, xbuf, tt):
    @pl.when(first)
    def _():
        xbuf[0:SUBLANES, :] = ctx_ref[0]

    xbuf[SUBLANES:SUBLANES + tt, :] = x
    acc = w_ref[CONV_W - 1:CONV_W, :] * xbuf[SUBLANES:SUBLANES + tt, :]
    for j in range(CONV_W - 1):
        off = SUBLANES - (CONV_W - 1) + j
        acc = acc + w_ref[j:j + 1, :] * xbuf[off:off + tt, :]
    y = acc * _sigmoid(acc)
    q = y[:, 0:REC_W]
    k = y[:, REC_W:2 * REC_W]
    e = e_ref[...]
    qs = _dot_exact_rhs(q * q, e, NORM_SUM_PARTS)
    ks = _dot_exact_rhs(k * k, e, NORM_SUM_PARTS)
    kn = k * lax.rsqrt(ks + NORM_EPS)
    o_ref[:, 0:REC_W] = q * lax.rsqrt(qs + NORM_EPS) * (HEAD_DIM ** -0.5)
    o_ref[:, REC_W:2 * REC_W] = kn
    o_ref[:, 2 * REC_W:3 * REC_W] = y[:, 2 * REC_W:3 * REC_W]
    if kt_ref is not None:
        for r in range(tt // GROUP):
            _store_chunks_t(kt_ref, r, kn[r * GROUP:(r + 1) * GROUP, :])
    xbuf[0:SUBLANES, :] = xbuf[tt:tt + SUBLANES, :]


def _proj_kernel(x_ref, w_ref, ctx_ref, cw_ref, e_ref, o_ref, skt_ref, sv_ref, mkt_ref, gtt_ref, act_ref, gkt_ref,
                 xbuf, *, tm, tn):
    xb = x_ref[...].astype(BF16)
    for j in range(PROJ_W // tn):
        o_ref[:, j * tn:(j + 1) * tn] = jnp.dot(xb, w_ref[:, j * tn:(j + 1) * tn], preferred_element_type=F32)
    for r in range(tm // GROUP):
        rows = slice(r * GROUP, (r + 1) * GROUP)
        skt_ref[r] = o_ref[rows, COL_SB + SB_W:COL_SB + 2 * SB_W].T.astype(BF16)
        for h in range(SB_HEADS):
            c0 = COL_SB + 2 * SB_W + h * HEAD_DIM
            sv_ref[h, r] = o_ref[rows, c0:c0 + HEAD_DIM].astype(BF16)
        _store_chunks_t(mkt_ref, r, o_ref[rows, COL_MQKV:COL_MQKV + 3 * REC_W])
        _store_chunks_t(gtt_ref, r, o_ref[rows, COL_GATE:COL_GATE + LANES])
    _gdn_activate(o_ref[:, COL_GQKV:COL_GQKV + 3 * REC_W], pl.program_id(1) == 0, ctx_ref, cw_ref, e_ref,
                  act_ref, gkt_ref, xbuf, tm)


def _store_chunks_t(ref, r, a):
    at = a.T
    per = GROUP // CHUNK
    for j in range(per):
        ref[r * per + j] = at[:, j * CHUNK:(j + 1) * CHUNK]


def _proj(x, w, ctx8, conv_w8, e_heads, b, t, tm):
    n, d = x.shape
    nt = t // tm
    ng = n // GROUP
    gpt = tm // GROUP
    cpt = tm // CHUNK
    w3 = 3 * REC_W
    row = lambda bi, i: (bi * nt + i, 0)
    row3 = lambda bi, i: (bi * nt + i, 0, 0)
    fixed = lambda bi, i: (0, 0)
    return pl.pallas_call(
        functools.partial(_proj_kernel, tm=tm, tn=1024),
        grid=(b, nt),
        in_specs=[pl.BlockSpec((tm, d), row),
                  pl.BlockSpec((d, PROJ_W), fixed, pipeline_mode=pl.Buffered(1)),
                  pl.BlockSpec((1, SUBLANES, w3), lambda bi, i: (bi, 0, 0)),
                  pl.BlockSpec((SUBLANES, w3), fixed),
                  pl.BlockSpec((REC_W, REC_W), fixed)],
        out_specs=[pl.BlockSpec((tm, PROJ_W), row),
                   pl.BlockSpec((gpt, SB_W, GROUP), row3),
                   pl.BlockSpec((SB_HEADS, gpt, GROUP, HEAD_DIM), lambda bi, i: (0, bi * nt + i, 0, 0)),
                   pl.BlockSpec((cpt, w3, CHUNK), row3),
                   pl.BlockSpec((cpt, LANES, CHUNK), row3),
                   pl.BlockSpec((tm, w3), row),
                   pl.BlockSpec((cpt, REC_W, CHUNK), row3)],
        out_shape=[jax.ShapeDtypeStruct((n, PROJ_W), F32),
                   jax.ShapeDtypeStruct((ng, SB_W, GROUP), BF16),
                   jax.ShapeDtypeStruct((SB_HEADS, ng, GROUP, HEAD_DIM), BF16),
                   jax.ShapeDtypeStruct((n // CHUNK, w3, CHUNK), F32),
                   jax.ShapeDtypeStruct((n // CHUNK, LANES, CHUNK), F32),
                   jax.ShapeDtypeStruct((n, w3), F32),
                   jax.ShapeDtypeStruct((n // CHUNK, REC_W, CHUNK), F32)],
        scratch_shapes=[pltpu.VMEM((tm + SUBLANES, w3), F32)],
        compiler_params=_cparams("parallel", "arbitrary"),
        name="proj_full",
    )(x, w, ctx8, conv_w8, e_heads)


def _gdn_pre_kernel(x_ref, ctx_ref, w_ref, e_ref, o_ref, xbuf, *, tt):
    _gdn_activate(x_ref[...], pl.program_id(1) == 0, ctx_ref, w_ref, e_ref, o_ref, None, xbuf, tt)


def _gdn_pre(proj, ctx8, conv_w8, e_heads, b, t, tt):
    w3 = 3 * REC_W
    nt = t // tt
    return pl.pallas_call(
        functools.partial(_gdn_pre_kernel, tt=tt),
        grid=(b, nt),
        in_specs=[pl.BlockSpec((tt, w3), lambda bi, i: (bi * nt + i, COL_GQKV // w3)),
                  pl.BlockSpec((1, SUBLANES, w3), lambda bi, i: (bi, 0, 0)),
                  pl.BlockSpec((SUBLANES, w3), lambda bi, i: (0, 0)),
                  pl.BlockSpec((REC_W, REC_W), lambda bi, i: (0, 0))],
        out_specs=pl.BlockSpec((tt, w3), lambda bi, i: (bi * nt + i, 0)),
        out_shape=jax.ShapeDtypeStruct((b * t, w3), F32),
        scratch_shapes=[pltpu.VMEM((tt + SUBLANES, w3), F32)],
        compiler_params=_cparams("parallel", "arbitrary"),
        name="gdn_pre",
    )(proj, ctx8, conv_w8, e_heads)


def _sb_kernel(q_ref, kt_ref, v_ref, o_ref, acc_ref, c_ref, zmax_ref, kmax_ref,
               *, hb, tq, tk, nkb, q_off, diag_static):
    qi = pl.program_id(1)
    q_lo = q_off + qi * tq
    nblk = jnp.minimum(nkb, (q_lo + tq - 1 + tk - 1) // tk)
    nfull = jnp.minimum(nblk, q_lo // tk)
    scale = HEAD_DIM ** -0.5 * LOG2E
    qs = [(q_ref[:, h * HEAD_DIM:(h + 1) * HEAD_DIM] * scale).astype(BF16) for h in range(hb)]
    lower_incl = _ones_where(lax.broadcasted_iota(jnp.int32, (tk, tk), 0)
                             >= lax.broadcasted_iota(jnp.int32, (tk, tk), 1), BF16)
    acc_ref[...] = jnp.zeros_like(acc_ref)
    c_ref[...] = jnp.zeros_like(c_ref)

    @pl.when(qi == 0)
    def _():
        def knorm(kb, best):
            k32 = kt_ref[0, kb].astype(F32)
            sq = k32 * k32
            return tuple(jnp.maximum(best[h], jnp.sum(sq[h * HEAD_DIM:(h + 1) * HEAD_DIM, :], axis=0, keepdims=True))
                         for h in range(hb))
        best = lax.fori_loop(0, nkb, knorm, tuple(jnp.zeros((1, tk), F32) for _ in range(hb)))
        for h in range(hb):
            kmax_ref[h] = jnp.broadcast_to(jnp.sqrt(jnp.max(best[h], axis=1, keepdims=True)), (1, LANES))

    for h in range(hb):
        q32 = qs[h].astype(F32)
        qn = jnp.sqrt(jnp.sum(q32 * q32, axis=1, keepdims=True))
        zmax_ref[h] = qn * kmax_ref[h] * BOUND_SLACK

    def all_weights_vanish():
        margin = c_ref[0] - zmax_ref[0]
        for h in range(1, hb):
            margin = jnp.minimum(margin, c_ref[h] - zmax_ref[h])
        return jnp.min(margin) > F32_EXP2_ZERO

    def blocks(kbs, r0=0, valids=None):
        for h in range(hb):
            q = qs[h][r0:, :]
            c = c_ref[h, r0:, :]
            av = None
            for i, kb in enumerate(kbs):
                kt = kt_ref[0, kb, h * HEAD_DIM:(h + 1) * HEAD_DIM, :]
                z = jnp.dot(q, kt, preferred_element_type=F32)
                lneg = jnp.maximum(z, jnp.log2(1.0 + jnp.exp2(jnp.minimum(z, SOFTPLUS2_CLAMP))))
                if valids is not None:
                    lneg = jnp.where(valids[i], lneg, 0.0)
                incl = jnp.dot(lneg.astype(BF16), lower_incl, preferred_element_type=F32)
                a = jnp.exp2(z - incl - jnp.tile(c, (1, tk // LANES)))
                if valids is not None:
                    a = jnp.where(valids[i], a, 0.0)
                t = jnp.dot(a.astype(BF16), v_ref[h, kb], preferred_element_type=F32)
                av = t if av is None else av + t
                c = c + incl[:, 0:1]
            acc_ref[h, r0:, :] += av
            c_ref[h, r0:, :] = c

    if diag_static:
        for r in reversed(range(tq // tk)):
            rows = tq - r * tk
            valid = (lax.broadcasted_iota(jnp.int32, (rows, tk), 1)
                     < lax.broadcasted_iota(jnp.int32, (rows, tk), 0))
            blocks([nfull + r], r0=r * tk, valids=[valid])
    else:
        @pl.loop(0, nblk - nfull)
        def _(i):
            kb = nblk - 1 - i
            valid = (kb * tk + lax.broadcasted_iota(jnp.int32, (tq, tk), 1)
                     < q_lo + lax.broadcasted_iota(jnp.int32, (tq, tk), 0))
            blocks([kb], valids=[valid])

    def more_blocks(carry):
        i, done = carry
        return jnp.logical_and(i < nfull, done == 0)

    def next_block(carry):
        i, _ = carry
        blocks([nfull - 1 - i])
        return i + 1, all_weights_vanish().astype(jnp.int32)

    lax.while_loop(more_blocks, next_block, (jnp.int32(0), all_weights_vanish().astype(jnp.int32)))

    for h in range(hb):
        o_ref[:, h * HEAD_DIM:(h + 1) * HEAD_DIM] = acc_ref[h]


def _sb_attn(proj, kt, vh, b, t, hb, tq, q_off):
    _, nkb, w, tk = kt.shape
    d = w // hb
    nq = t // tq
    diag_static = q_off % tk == 0 and tq % tk == 0 and q_off + t <= nkb * tk
    kv_mode = dict(pipeline_mode=pl.Buffered(1)) if b == 1 else {}
    return pl.pallas_call(
        functools.partial(_sb_kernel, hb=hb, tq=tq, tk=tk, nkb=nkb, q_off=q_off, diag_static=diag_static),
        grid=(b, nq),
        in_specs=[pl.BlockSpec((tq, w), lambda bi, i: (bi * nq + i, COL_SB // w)),
                  pl.BlockSpec((1, nkb, w, tk), lambda bi, i: (bi, 0, 0, 0), **kv_mode),
                  pl.BlockSpec((hb, nkb, tk, d), lambda bi, i: (bi, 0, 0, 0), **kv_mode)],
        out_specs=pl.BlockSpec((tq, w), lambda bi, i: (bi * nq + i, 0)),
        out_shape=jax.ShapeDtypeStruct((b * t, w), F32),
        scratch_shapes=[pltpu.VMEM((hb, tq, d), F32), pltpu.VMEM((hb, tq, LANES), F32),
                        pltpu.VMEM((hb, tq, LANES), F32), pltpu.VMEM((hb, 1, LANES), F32)],
        compiler_params=_cparams("parallel", "arbitrary"),
        name="sb_attn",
    )(proj, kt, vh)


def _gate_values(pre, neg_a, lane_id):
    sp = jnp.log1p(jnp.exp(-jnp.abs(pre)))
    softplus = jnp.maximum(pre, 0.0) + sp
    log_sig = jnp.minimum(pre, 0.0) - sp
    return jnp.where(lane_id < GATE_G, _sigmoid(pre),
                     jnp.where(lane_id < GATE_I, neg_a * softplus,
                               jnp.where(lane_id < GATE_F, pre, log_sig)))


def _stack(a, b):
    return jnp.concatenate([a, b], axis=0)


def _gates(gt, gtt, prow_ref, pcol_ref, l):
    lane = lax.broadcasted_iota(jnp.int32, (l, LANES), 1)
    val = _gate_values(gt + prow_ref[0:1, :], -jnp.exp(prow_ref[1:2, :]), lane)
    sub = lax.broadcasted_iota(jnp.int32, (LANES, l), 0)
    val_t = _gate_values(gtt + pcol_ref[:, 0:1], -jnp.exp(pcol_ref[:, 1:2]), sub)
    r = lax.broadcasted_iota(jnp.int32, (l, l), 0)
    c = lax.broadcasted_iota(jnp.int32, (l, l), 1)
    csum = _dot_exact_lhs(_ones_where(r >= c, BF16), val, 3)
    csum_t = _dot_exact_rhs(val_t, _ones_where(r <= c, BF16), 3)
    return val, csum, val_t, csum_t


def _gdn_scan_kernel(qkv_ref, kt_ref, gt_ref, gtt_ref, prow_ref, pcol_ref, s0_ref, o_ref, s_out_ref, s_scr,
                     *, l, cb, ng):
    gi = pl.program_id(1)

    @pl.when(gi == 0)
    def _():
        s_scr[...] = s0_ref[0]

    r = lax.broadcasted_iota(jnp.int32, (l, l), 0)
    c = lax.broadcasted_iota(jnp.int32, (l, l), 1)
    tri = r >= c
    stri = r > c
    eye = _ones_where(r == c, F32)
    n_double = max(int(np.ceil(np.log2(l))) - 1, 0)
    hs = range(GDN_HEADS)
    ps = [(j, h) for j in range(cb) for h in hs]
    gates = [_gates(gt_ref[j * l:(j + 1) * l, :], gtt_ref[j], prow_ref, pcol_ref, l) for j in range(cb)]

    def head_cols(j, group, h):
        return qkv_ref[j * l:(j + 1) * l, group * REC_W + h * HEAD_DIM:group * REC_W + (h + 1) * HEAD_DIM]

    q = {p: head_cols(p[0], 0, p[1]) for p in ps}
    k = {p: head_cols(p[0], 1, p[1]) for p in ps}
    v = {p: head_cols(p[0], 2, p[1]) for p in ps}
    kt = {p: kt_ref[p[0], p[1] * HEAD_DIM:(p[1] + 1) * HEAD_DIM, :] for p in ps}
    beta = {p: gates[p[0]][0][:, GATE_BETA + p[1]:GATE_BETA + p[1] + 1] for p in ps}
    g_col = {p: gates[p[0]][1][:, GATE_G + p[1]:GATE_G + p[1] + 1] for p in ps}
    g_row = {p: gates[p[0]][3][GATE_G + p[1]:GATE_G + p[1] + 1, :] for p in ps}
    g_last = {p: g_col[p][l - 1:l, :] for p in ps}
    decay = {p: jnp.where(tri, jnp.exp(jnp.where(tri, g_col[p] - g_row[p], 0.0)), 0.0) for p in ps}
    kb = {p: k[p] * beta[p] for p in ps}
    e_g = {p: jnp.exp(g_col[p]) for p in ps}
    kq = {p: _dot(_stack(kb[p], q[p]), kt[p]) for p in ps}
    x = {p: -jnp.where(stri, kq[p][:l] * decay[p], 0.0) for p in ps}
    tinv = {p: eye + x[p] for p in ps}
    if n_double > 0:
        x = {p: _dot(x[p], x[p]) for p in ps}
        for _ in range(n_double - 1):
            tx = {p: _dot(_stack(tinv[p], x[p]), x[p]) for p in ps}
            tinv = {p: tinv[p] + tx[p][:l] for p in ps}
            x = {p: tx[p][l:] for p in ps}
        tinv = {p: tinv[p] + _dot(tinv[p], x[p]) for p in ps}
    sol_v = {p: _dot(tinv[p], v[p] * beta[p]) for p in ps}
    sol_k = {p: _dot(tinv[p], kb[p] * e_g[p]) for p in ps}
    lhs_s = {p: _stack(q[p] * e_g[p], sol_k[p]) for p in ps}
    lhs_u = {p: _stack(kt[p] * jnp.exp(g_last[p] - g_row[p]), kq[p][l:] * decay[p]) for p in ps}
    e_last = {p: jnp.exp(g_last[p]) for p in ps}
    s = [s_scr[h] for h in hs]
    for j in range(cb):
        t = [_dot(lhs_s[(j, h)], s[h]) for h in hs]
        u = [sol_v[(j, h)] - t[h][l:] for h in hs]
        w = [_dot(lhs_u[(j, h)], u[h]) for h in hs]
        s = [e_last[(j, h)] * s[h] + w[h][:HEAD_DIM] for h in hs]
        for h in hs:
            o_ref[j * l:(j + 1) * l, h * HEAD_DIM:(h + 1) * HEAD_DIM] = t[h][:l] + w[h][HEAD_DIM:]
    for h in hs:
        s_scr[h] = s[h]

    @pl.when(gi == ng - 1)
    def _():
        s_out_ref[0] = s_scr[...]


def _gdn_scan(qkv, k_t, proj, gate_t, prow, pcol, s0, b, t, l, cb):
    ng = t // (l * cb)
    w3 = 3 * REC_W
    return pl.pallas_call(
        functools.partial(_gdn_scan_kernel, l=l, cb=cb, ng=ng),
        grid=(b, ng),
        in_specs=[pl.BlockSpec((cb * l, w3), lambda bi, i: (bi * ng + i, 0)),
                  pl.BlockSpec((cb, REC_W, l), lambda bi, i: (bi * ng + i, 0, 0)),
                  pl.BlockSpec((cb * l, LANES), lambda bi, i: (bi * ng + i, COL_GATE // LANES)),
                  pl.BlockSpec((cb, LANES, l), lambda bi, i: (bi * ng + i, 0, 0)),
                  pl.BlockSpec((SUBLANES, LANES), lambda bi, i: (0, 0)),
                  pl.BlockSpec((LANES, SUBLANES), lambda bi, i: (0, 0)),
                  pl.BlockSpec((1, GDN_HEADS, HEAD_DIM, HEAD_DIM), lambda bi, i: (bi, 0, 0, 0))],
        out_specs=[pl.BlockSpec((cb * l, REC_W), lambda bi, i: (bi * ng + i, 0)),
                   pl.BlockSpec((1, GDN_HEADS, HEAD_DIM, HEAD_DIM), lambda bi, i: (bi, 0, 0, 0))],
        out_shape=[jax.ShapeDtypeStruct((b * t, REC_W), F32),
                   jax.ShapeDtypeStruct((b, GDN_HEADS, HEAD_DIM, HEAD_DIM), F32)],
        scratch_shapes=[pltpu.VMEM((GDN_HEADS, HEAD_DIM, HEAD_DIM), F32)],
        compiler_params=_cparams("parallel", "arbitrary"),
        name="gdn_scan",
    )(qkv, k_t, proj, gate_t, prow, pcol, s0)


def _gdn_scan_pair_kernel(qkv_ref, kt_ref, gt_ref, gtt_ref, prow_ref, pcol_ref, s0_ref, o_ref, s_out_ref, s_scr,
                          *, l, cb, ng):
    gi = pl.program_id(1)
    npair = GDN_HEADS // 2
    row128 = lax.broadcasted_iota(jnp.int32, (LANES, LANES), 0)
    col128 = lax.broadcasted_iota(jnp.int32, (LANES, LANES), 1)
    blockmask = (row128 < HEAD_DIM) == (col128 < HEAD_DIM)

    def bd(a):
        return jnp.where(blockmask, _stack(a, a), 0.0)

    @pl.when(gi == 0)
    def _():
        for p in range(npair):
            s_scr[p] = bd(jnp.concatenate([s0_ref[0, 2 * p], s0_ref[0, 2 * p + 1]], axis=1))

    trow = lax.broadcasted_iota(jnp.int32, (l, LANES), 0)
    tcol = lax.broadcasted_iota(jnp.int32, (l, LANES), 1) % HEAD_DIM
    left = lax.broadcasted_iota(jnp.int32, (l, LANES), 1) < HEAD_DIM
    tri = trow >= tcol
    stri = trow > tcol
    eye = _ones_where(trow == tcol, F32)
    krow_left = lax.broadcasted_iota(jnp.int32, (LANES, l), 0) < HEAD_DIM
    n_double = max(int(np.ceil(np.log2(l))) - 1, 0)
    ps = [(j, p) for j in range(cb) for p in range(npair)]
    gates = [_gates(gt_ref[j * l:(j + 1) * l, :], gtt_ref[j], prow_ref, pcol_ref, l) for j in range(cb)]

    def pair_cols(j, group, p):
        c0 = group * REC_W + p * LANES
        return qkv_ref[j * l:(j + 1) * l, c0:c0 + LANES]

    def col_pair(a, lane0, p):
        return jnp.where(left, a[:, lane0 + 2 * p:lane0 + 2 * p + 1], a[:, lane0 + 2 * p + 1:lane0 + 2 * p + 2])

    q = {x: pair_cols(x[0], 0, x[1]) for x in ps}
    k = {x: pair_cols(x[0], 1, x[1]) for x in ps}
    v = {x: pair_cols(x[0], 2, x[1]) for x in ps}
    ktp = {x: kt_ref[x[0], x[1] * LANES:(x[1] + 1) * LANES, :] for x in ps}
    beta = {x: col_pair(gates[x[0]][0], GATE_BETA, x[1]) for x in ps}
    g_col = {x: col_pair(gates[x[0]][1], GATE_G, x[1]) for x in ps}
    g_row_a = {x: gates[x[0]][3][GATE_G + 2 * x[1]:GATE_G + 2 * x[1] + 1, :] for x in ps}
    g_row_b = {x: gates[x[0]][3][GATE_G + 2 * x[1] + 1:GATE_G + 2 * x[1] + 2, :] for x in ps}
    g_row = {x: jnp.concatenate([g_row_a[x], g_row_b[x]], axis=1) for x in ps}
    g_last = {x: g_col[x][l - 1:l, :] for x in ps}
    decay = {x: jnp.where(tri, jnp.exp(jnp.where(tri, g_col[x] - g_row[x], 0.0)), 0.0) for x in ps}
    kb = {x: k[x] * beta[x] for x in ps}
    e_g = {x: jnp.exp(g_col[x]) for x in ps}
    bd_kt = {x: jnp.concatenate([jnp.where(krow_left, ktp[x], 0.0), jnp.where(krow_left, 0.0, ktp[x])], axis=1)
             for x in ps}
    kq = {x: _dot(_stack(kb[x], q[x]), bd_kt[x]) for x in ps}
    y = {x: -jnp.where(stri, kq[x][:l] * decay[x], 0.0) for x in ps}
    tinv = {x: eye + y[x] for x in ps}
    if n_double > 0:
        y = {x: _dot(y[x], bd(y[x])) for x in ps}
        for _ in range(n_double - 1):
            ty = {x: _dot(_stack(tinv[x], y[x]), bd(y[x])) for x in ps}
            tinv = {x: tinv[x] + ty[x][:l] for x in ps}
            y = {x: ty[x][l:] for x in ps}
        tinv = {x: tinv[x] + _dot(tinv[x], bd(y[x])) for x in ps}
    sol = {x: _dot(tinv[x], jnp.concatenate([bd(v[x] * beta[x]), bd(kb[x] * e_g[x])], axis=1)) for x in ps}
    lhs_s = {x: _stack(q[x] * e_g[x], sol[x][:, LANES:]) for x in ps}
    g_last_a = {x: g_last[x][:, 0:1] for x in ps}
    g_last_b = {x: g_last[x][:, HEAD_DIM:HEAD_DIM + 1] for x in ps}
    ktd = {x: ktp[x] * jnp.where(krow_left, jnp.exp(g_last_a[x] - g_row_a[x]), jnp.exp(g_last_b[x] - g_row_b[x]))
           for x in ps}
    attn = {x: kq[x][l:] * decay[x] for x in ps}
    e_last = {x: jnp.exp(g_last[x]) for x in ps}
    s = [s_scr[p] for p in range(npair)]
    for j in range(cb):
        t = [_dot(lhs_s[(j, p)], s[p]) for p in range(npair)]
        u = [sol[(j, p)][:, :LANES] - t[p][l:] for p in range(npair)]
        du = [_dot(ktd[(j, p)], u[p]) for p in range(npair)]
        ou = [_dot(attn[(j, p)], bd(u[p])) for p in range(npair)]
        s = [e_last[(j, p)] * s[p] + jnp.where(blockmask, du[p], 0.0) for p in range(npair)]
        for p in range(npair):
            o_ref[j * l:(j + 1) * l, p * LANES:(p + 1) * LANES] = t[p][:l] + ou[p]
    for p in range(npair):
        s_scr[p] = s[p]

    @pl.when(gi == ng - 1)
    def _():
        for p in range(npair):
            s_out_ref[0, 2 * p] = s_scr[p, 0:HEAD_DIM, 0:HEAD_DIM]
            s_out_ref[0, 2 * p + 1] = s_scr[p, HEAD_DIM:LANES, HEAD_DIM:LANES]


def _gdn_scan_pair(qkv, k_t, proj, gate_t, prow, pcol, s0, b, t, l, cb):
    ng = t // (l * cb)
    w3 = 3 * REC_W
    return pl.pallas_call(
        functools.partial(_gdn_scan_pair_kernel, l=l, cb=cb, ng=ng),
        grid=(b, ng),
        in_specs=[pl.BlockSpec((cb * l, w3), lambda bi, i: (bi * ng + i, 0)),
                  pl.BlockSpec((cb, REC_W, l), lambda bi, i: (bi * ng + i, 0, 0)),
                  pl.BlockSpec((cb * l, LANES), lambda bi, i: (bi * ng + i, COL_GATE // LANES)),
                  pl.BlockSpec((cb, LANES, l), lambda bi, i: (bi * ng + i, 0, 0)),
                  pl.BlockSpec((SUBLANES, LANES), lambda bi, i: (0, 0)),
                  pl.BlockSpec((LANES, SUBLANES), lambda bi, i: (0, 0)),
                  pl.BlockSpec((1, GDN_HEADS, HEAD_DIM, HEAD_DIM), lambda bi, i: (bi, 0, 0, 0))],
        out_specs=[pl.BlockSpec((cb * l, REC_W), lambda bi, i: (bi * ng + i, 0)),
                   pl.BlockSpec((1, GDN_HEADS, HEAD_DIM, HEAD_DIM), lambda bi, i: (bi, 0, 0, 0))],
        out_shape=[jax.ShapeDtypeStruct((b * t, REC_W), F32),
                   jax.ShapeDtypeStruct((b, GDN_HEADS, HEAD_DIM, HEAD_DIM), F32)],
        scratch_shapes=[pltpu.VMEM((GDN_HEADS // 2, LANES, LANES), F32)],
        compiler_params=_cparams("parallel", "arbitrary"),
        name="gdn_scan_pair",
    )(qkv, k_t, proj, gate_t, prow, pcol, s0)


def _ml_scan_kernel(qkv_ref, kt_ref, gt_ref, gtt_ref, prow_ref, pcol_ref, c0_ref, n0_ref, m0_ref,
                    o_ref, c_out_ref, n_out_ref, m_out_ref, c_scr, n_scr, m_scr, *, l, cb, ng):
    gi = pl.program_id(1)

    @pl.when(gi == 0)
    def _():
        c_scr[...] = c0_ref[0]
        n_scr[...] = n0_ref[0]
        m_scr[...] = m0_ref[0]

    r = lax.broadcasted_iota(jnp.int32, (l, l), 0)
    c = lax.broadcasted_iota(jnp.int32, (l, l), 1)
    tri = r >= c
    lane = lax.broadcasted_iota(jnp.int32, (1, LANES), 1)
    m_all = m_scr[...]
    hs = range(ML_HEADS)
    ps = [(j, h) for j in range(cb) for h in hs]
    kscale = HEAD_DIM ** -0.5
    gates = [_gates(gt_ref[j * l:(j + 1) * l, :], gtt_ref[j], prow_ref, pcol_ref, l) for j in range(cb)]

    def head_cols(j, group, h):
        return qkv_ref[j * l:(j + 1) * l, group * REC_W + h * HEAD_DIM:group * REC_W + (h + 1) * HEAD_DIM]

    q = {p: head_cols(p[0], 0, p[1]) for p in ps}
    k = {p: head_cols(p[0], 1, p[1]) * kscale for p in ps}
    v = {p: head_cols(p[0], 2, p[1]) for p in ps}
    kt = {p: kt_ref[p[0], p[1] * HEAD_DIM:(p[1] + 1) * HEAD_DIM, :] * kscale for p in ps}
    ig_col = {p: gates[p[0]][0][:, GATE_I + p[1]:GATE_I + p[1] + 1] for p in ps}
    ig_row = {p: gates[p[0]][2][GATE_I + p[1]:GATE_I + p[1] + 1, :] for p in ps}
    f_col = {p: gates[p[0]][1][:, GATE_F + p[1]:GATE_F + p[1] + 1] for p in ps}
    f_row = {p: gates[p[0]][3][GATE_F + p[1]:GATE_F + p[1] + 1, :] for p in ps}
    f_last = {p: f_col[p][l - 1:l, :] for p in ps}
    qk = {p: _dot(q[p], kt[p]) for p in ps}
    d = {p: jnp.where(tri, f_col[p] - f_row[p] + ig_row[p], -jnp.inf) for p in ps}
    d_max = {p: jnp.max(d[p], axis=1, keepdims=True) for p in ps}
    m_prev, m_t = {}, {}
    m_run = [m_all[:, h:h + 1] for h in hs]
    for j in range(cb):
        for h in hs:
            m_prev[(j, h)] = m_run[h]
            m_t[(j, h)] = jnp.maximum(f_col[(j, h)] + m_run[h], d_max[(j, h)])
            m_run[h] = m_t[(j, h)][l - 1:l, :]
    m_new = {p: m_t[p][l - 1:l, :] for p in ps}
    w = {p: jnp.exp(d[p] - m_t[p]) * qk[p] for p in ps}
    c_inter = {p: jnp.exp(f_col[p] + m_prev[p] - m_t[p]) for p in ps}
    wv = {p: _dot(w[p], v[p]) for p in ps}
    w_sum = {p: jnp.sum(w[p], axis=1, keepdims=True) for p in ps}
    e_m = {p: jnp.exp(-m_t[p]) for p in ps}
    w_end_row = {p: jnp.exp(f_last[p] - f_row[p] + ig_row[p] - m_new[p]) for p in ps}
    w_end_col = {p: jnp.exp(f_last[p] - f_col[p] + ig_col[p] - m_new[p]) for p in ps}
    c_prev = {p: jnp.exp(f_last[p] + m_prev[p] - m_new[p]) for p in ps}
    dc = {p: _dot(kt[p] * w_end_row[p], v[p]) for p in ps}
    dn = {p: jnp.sum(w_end_col[p] * k[p], axis=0, keepdims=True) for p in ps}
    c_h = [c_scr[h] for h in hs]
    n_h = [n_scr[h:h + 1, :] for h in hs]
    for j in range(cb):
        for h in hs:
            p = (j, h)
            num = c_inter[p] * _dot(q[p], c_h[h]) + wv[p]
            den = c_inter[p] * jnp.sum(q[p] * n_h[h], axis=1, keepdims=True) + w_sum[p]
            o_ref[j * l:(j + 1) * l, h * HEAD_DIM:(h + 1) * HEAD_DIM] = num / jnp.maximum(jnp.abs(den), e_m[p])
            c_h[h] = c_prev[p] * c_h[h] + dc[p]
            n_h[h] = c_prev[p] * n_h[h] + dn[p]
    m_next = m_all
    for h in hs:
        c_scr[h] = c_h[h]
        n_scr[h:h + 1, :] = n_h[h]
        m_next = jnp.where(lane == h, m_run[h], m_next)
    m_scr[...] = m_next

    @pl.when(gi == ng - 1)
    def _():
        c_out_ref[0] = c_scr[...]
        n_out_ref[0] = n_scr[...]
        m_out_ref[0] = m_scr[...]


def _ml_scan(proj, k_t, gate_t, prow, pcol, c0, n0, m0, b, t, l, cb):
    ng = t // (l * cb)
    w3 = 3 * REC_W
    st = lambda bi, i: (bi, 0, 0, 0)
    st3 = lambda bi, i: (bi, 0, 0)
    return pl.pallas_call(
        functools.partial(_ml_scan_kernel, l=l, cb=cb, ng=ng),
        grid=(b, ng),
        in_specs=[pl.BlockSpec((cb * l, w3), lambda bi, i: (bi * ng + i, COL_MQKV // w3)),
                  pl.BlockSpec((cb, REC_W, l), lambda bi, i: (bi * ng + i, 0, 0)),
                  pl.BlockSpec((cb * l, LANES), lambda bi, i: (bi * ng + i, COL_GATE // LANES)),
                  pl.BlockSpec((cb, LANES, l), lambda bi, i: (bi * ng + i, 0, 0)),
                  pl.BlockSpec((SUBLANES, LANES), lambda bi, i: (0, 0)),
                  pl.BlockSpec((LANES, SUBLANES), lambda bi, i: (0, 0)),
                  pl.BlockSpec((1, ML_HEADS, HEAD_DIM, HEAD_DIM), st),
                  pl.BlockSpec((1, SUBLANES, HEAD_DIM), st3),
                  pl.BlockSpec((1, 1, LANES), st3)],
        out_specs=[pl.BlockSpec((cb * l, REC_W), lambda bi, i: (bi * ng + i, 0)),
                   pl.BlockSpec((1, ML_HEADS, HEAD_DIM, HEAD_DIM), st),
                   pl.BlockSpec((1, SUBLANES, HEAD_DIM), st3),
                   pl.BlockSpec((1, 1, LANES), st3)],
        out_shape=[jax.ShapeDtypeStruct((b * t, REC_W), F32),
                   jax.ShapeDtypeStruct((b, ML_HEADS, HEAD_DIM, HEAD_DIM), F32),
                   jax.ShapeDtypeStruct((b, SUBLANES, HEAD_DIM), F32),
                   jax.ShapeDtypeStruct((b, 1, LANES), F32)],
        scratch_shapes=[pltpu.VMEM((ML_HEADS, HEAD_DIM, HEAD_DIM), F32),
                        pltpu.VMEM((SUBLANES, HEAD_DIM), F32),
                        pltpu.VMEM((1, LANES), F32)],
        compiler_params=_cparams("parallel", "arbitrary"),
        name="ml_scan",
    )(proj, k_t, proj, gate_t, prow, pcol, c0, n0, m0)


def _ml_scan_t_kernel(qkv_ref, qkvt_ref, gt_ref, gtt_ref, prow_ref, pcol_ref, c0_ref, n0_ref, m0_ref,
                      o_ref, c_out_ref, n_out_ref, m_out_ref, c_scr, n_scr, m_scr, *, l, cb, ng):
    gi = pl.program_id(1)

    @pl.when(gi == 0)
    def _():
        c_scr[...] = c0_ref[0]
        n_scr[...] = n0_ref[0]
        m_scr[...] = m0_ref[0]

    r = lax.broadcasted_iota(jnp.int32, (l, l), 0)
    c = lax.broadcasted_iota(jnp.int32, (l, l), 1)
    tri_t = r <= c
    lane = lax.broadcasted_iota(jnp.int32, (1, LANES), 1)
    m_all = m_scr[...]
    hs = range(ML_HEADS)
    ps = [(j, h) for j in range(cb) for h in hs]
    kscale = HEAD_DIM ** -0.5
    zeros_t = jnp.zeros((HEAD_DIM, l), F32)
    gates = [_gates(gt_ref[j * l:(j + 1) * l, :], gtt_ref[j], prow_ref, pcol_ref, l) for j in range(cb)]

    def t_rows(j, group, h):
        return qkvt_ref[j, group * REC_W + h * HEAD_DIM:group * REC_W + (h + 1) * HEAD_DIM, :]

    qt = {p: t_rows(p[0], 0, p[1]) for p in ps}
    qt_pad = {p: (_stack(qt[p], zeros_t) if p[1] % 2 == 0 else _stack(zeros_t, qt[p])) for p in ps}
    vt = {p: t_rows(p[0], 2, p[1]) for p in ps}
    k2 = {p: qkv_ref[p[0] * l:(p[0] + 1) * l, REC_W + (p[1] // 2) * LANES:REC_W + (p[1] // 2 + 1) * LANES] * kscale
          for p in ps}
    ig_col = {p: gates[p[0]][0][:, GATE_I + p[1]:GATE_I + p[1] + 1] for p in ps}
    ig_row = {p: gates[p[0]][2][GATE_I + p[1]:GATE_I + p[1] + 1, :] for p in ps}
    f_col = {p: gates[p[0]][1][:, GATE_F + p[1]:GATE_F + p[1] + 1] for p in ps}
    f_row = {p: gates[p[0]][3][GATE_F + p[1]:GATE_F + p[1] + 1, :] for p in ps}
    f_last = {p: f_row[p][:, l - 1:l] for p in ps}
    kq = {p: _dot(k2[p], qt_pad[p]) for p in ps}
    d = {p: jnp.where(tri_t, f_row[p] + (ig_col[p] - f_col[p]), -jnp.inf) for p in ps}
    d_max = {p: jnp.max(d[p], axis=0, keepdims=True) for p in ps}
    m_prev, m_t = {}, {}
    m_run = [m_all[:, h:h + 1] for h in hs]
    for j in range(cb):
        for h in hs:
            m_prev[(j, h)] = m_run[h]
            m_t[(j, h)] = jnp.maximum(f_row[(j, h)] + m_run[h], d_max[(j, h)])
            m_run[h] = m_t[(j, h)][:, l - 1:l]
    m_new = {p: m_t[p][:, l - 1:l] for p in ps}
    w = {p: jnp.exp(d[p] - m_t[p]) * kq[p] for p in ps}
    c_inter = {p: jnp.exp(f_row[p] + m_prev[p] - m_t[p]) for p in ps}
    vw = {p: _dot(vt[p], w[p]) for p in ps}
    w_sum = {p: jnp.sum(w[p], axis=0, keepdims=True) for p in ps}
    e_m = {p: jnp.exp(-m_t[p]) for p in ps}
    w_end_col = {p: jnp.exp(f_last[p] - f_col[p] + ig_col[p] - m_new[p]) for p in ps}
    c_prev = {p: jnp.exp(f_last[p] + m_prev[p] - m_new[p]) for p in ps}
    wk = {p: w_end_col[p] * k2[p] for p in ps}
    dc = {p: _dot(vt[p], wk[p]) for p in ps}
    dn = {p: jnp.sum(wk[p], axis=0, keepdims=True) for p in ps}
    c_h = [c_scr[h] for h in hs]
    n_h = [n_scr[h:h + 1, :] for h in hs]
    ht = {}
    for j in range(cb):
        for h in hs:
            p = (j, h)
            num = c_inter[p] * _dot(c_h[h], qt_pad[p]) + vw[p]
            den = c_inter[p] * _dot(n_h[h], qt_pad[p]) + w_sum[p]
            ht[p] = num / jnp.maximum(jnp.abs(den), e_m[p])
            c_h[h] = c_prev[p] * c_h[h] + dc[p]
            n_h[h] = c_prev[p] * n_h[h] + dn[p]
    for i in range(ML_HEADS // 2):
        pair_t = _stack(jnp.concatenate([ht[(j, 2 * i)] for j in range(cb)], axis=1),
                        jnp.concatenate([ht[(j, 2 * i + 1)] for j in range(cb)], axis=1))
        o_ref[:, i * LANES:(i + 1) * LANES] = pair_t.T
    m_next = m_all
    for h in hs:
        c_scr[h] = c_h[h]
        n_scr[h:h + 1, :] = n_h[h]
        m_next = jnp.where(lane == h, m_run[h], m_next)
    m_scr[...] = m_next

    @pl.when(gi == ng - 1)
    def _():
        c_out_ref[0] = c_scr[...]
        n_out_ref[0] = n_scr[...]
        m_out_ref[0] = m_scr[...]


def _ml_scan_t(proj, qkv_t, gate_t, prow, pcol, c0, n0, m0, b, t, l, cb):
    ng = t // (l * cb)
    w3 = 3 * REC_W
    st = lambda bi, i: (bi, 0, 0, 0)
    st3 = lambda bi, i: (bi, 0, 0)
    return pl.pallas_call(
        functools.partial(_ml_scan_t_kernel, l=l, cb=cb, ng=ng),
        grid=(b, ng),
        in_specs=[pl.BlockSpec((cb * l, w3), lambda bi, i: (bi * ng + i, COL_MQKV // w3)),
                  pl.BlockSpec((cb, w3, l), lambda bi, i: (bi * ng + i, 0, 0)),
                  pl.BlockSpec((cb * l, LANES), lambda bi, i: (bi * ng + i, COL_GATE // LANES)),
                  pl.BlockSpec((cb, LANES, l), lambda bi, i: (bi * ng + i, 0, 0)),
                  pl.BlockSpec((SUBLANES, LANES), lambda bi, i: (0, 0)),
                  pl.BlockSpec((LANES, SUBLANES), lambda bi, i: (0, 0)),
                  pl.BlockSpec((1, ML_HEADS, HEAD_DIM, LANES), st),
                  pl.BlockSpec((1, SUBLANES, LANES), st3),
                  pl.BlockSpec((1, 1, LANES), st3)],
        out_specs=[pl.BlockSpec((cb * l, REC_W), lambda bi, i: (bi * ng + i, 0)),
                   pl.BlockSpec((1, ML_HEADS, HEAD_DIM, LANES), st),
                   pl.BlockSpec((1, SUBLANES, LANES), st3),
                   pl.BlockSpec((1, 1, LANES), st3)],
        out_shape=[jax.ShapeDtypeStruct((b * t, REC_W), F32),
                   jax.ShapeDtypeStruct((b, ML_HEADS, HEAD_DIM, LANES), F32),
                   jax.ShapeDtypeStruct((b, SUBLANES, LANES), F32),
                   jax.ShapeDtypeStruct((b, 1, LANES), F32)],
        scratch_shapes=[pltpu.VMEM((ML_HEADS, HEAD_DIM, LANES), F32),
                        pltpu.VMEM((SUBLANES, LANES), F32),
                        pltpu.VMEM((1, LANES), F32)],
        compiler_params=_cparams("parallel", "arbitrary"),
        name="ml_scan_t",
    )(proj, qkv_t, proj, gate_t, prow, pcol, c0, n0, m0)


def _post_kernel(x_ref, osb_ref, og_ref, hm_ref, gz_ref, mo_ref, w_ref, e_ref, gnw_ref, mnw_ref,
                 lg_ref, lb_ref, o_ref, ob_ref):
    e = e_ref[...]
    og = og_ref[...]
    hm = hm_ref[...]
    og_ms = _dot_exact_rhs(og * og, e, NORM_SUM_PARTS) * (1.0 / HEAD_DIM)
    hm_ms = _dot_exact_rhs(hm * hm, e, NORM_SUM_PARTS) * (1.0 / HEAD_DIM)
    gz = gz_ref[...]
    o_g = og * lax.rsqrt(og_ms + NORM_EPS) * gnw_ref[...] * (gz * _sigmoid(gz))
    o_m = _sigmoid(mo_ref[...]) * (hm * lax.rsqrt(hm_ms + NORM_EPS) * mnw_ref[...])
    mixed = (_dot(osb_ref[...], w_ref[0:SB_W, :])
             + _dot(o_g, w_ref[SB_W:SB_W + REC_W, :])
             + _dot(o_m, w_ref[SB_W + REC_W:SB_W + 2 * REC_W, :]))
    y = _layer_norm(DEEPNORM_ALPHA * x_ref[...] + mixed, lg_ref[...], lb_ref[...])
    o_ref[...] = y
    ob_ref[...] = y.astype(BF16)


def _post(x, osb, og, hm, proj, w_out, e_heads, gnw, mnw, lg, lb, tm):
    n, d = x.shape
    row = lambda i: (i, 0)
    fixed = lambda i: (0, 0)
    return pl.pallas_call(
        _post_kernel,
        grid=(n // tm,),
        in_specs=[pl.BlockSpec((tm, d), row),
                  pl.BlockSpec((tm, SB_W), row),
                  pl.BlockSpec((tm, REC_W), row),
                  pl.BlockSpec((tm, REC_W), row),
                  pl.BlockSpec((tm, REC_W), lambda i: (i, COL_GZ // REC_W)),
                  pl.BlockSpec((tm, REC_W), lambda i: (i, COL_MO // REC_W)),
                  pl.BlockSpec(w_out.shape, fixed),
                  pl.BlockSpec((REC_W, REC_W), fixed),
                  pl.BlockSpec((1, REC_W), fixed),
                  pl.BlockSpec((1, REC_W), fixed),
                  pl.BlockSpec((1, d), fixed),
                  pl.BlockSpec((1, d), fixed)],
        out_specs=[pl.BlockSpec((tm, d), row), pl.BlockSpec((tm, d), row)],
        out_shape=[jax.ShapeDtypeStruct((n, d), F32), jax.ShapeDtypeStruct((n, d), BF16)],
        compiler_params=_cparams("parallel"),
        name="post",
    )(x, osb, og, hm, proj, proj, w_out, e_heads, gnw, mnw, lg, lb)


def _memattn_kernel(x_ref, xb_ref, wq_ref, wo_ref, mk_ref, mv_ref, lg_ref, lb_ref, o_ref, ob_ref):
    d = x_ref.shape[1]
    hd = d // MEM_HEADS
    q = jnp.dot(xb_ref[...], wq_ref[...], preferred_element_type=F32)
    out = None
    for h in range(MEM_HEADS):
        sl = slice(h * hd, (h + 1) * hd)
        s = _dot_nt(q[:, sl], mk_ref[0, :, sl]) * (hd ** -0.5)
        p = jnp.exp(s - jnp.max(s, axis=1, keepdims=True))
        o_h = _dot(p, mv_ref[0, :, sl]) / jnp.sum(p, axis=1, keepdims=True)
        t = _dot(o_h, wo_ref[sl, :])
        out = t if out is None else out + t
    y = _layer_norm(DEEPNORM_ALPHA * x_ref[...] + out, lg_ref[...], lb_ref[...])
    o_ref[...] = y
    ob_ref[...] = y.astype(BF16)


def _memattn(x, xb, w_cq, w_co, mk, mv, lg, lb, b, t, tm):
    n, d = x.shape
    nt = t // tm
    nm = mk.shape[1]
    row = lambda bi, i: (bi * nt + i, 0)
    fixed = lambda bi, i: (0, 0)
    return pl.pallas_call(
        _memattn_kernel,
        grid=(b, nt),
        in_specs=[pl.BlockSpec((tm, d), row),
                  pl.BlockSpec((tm, d), row),
                  pl.BlockSpec((d, d), fixed),
                  pl.BlockSpec((d, d), fixed),
                  pl.BlockSpec((1, nm, d), lambda bi, i: (bi, 0, 0)),
                  pl.BlockSpec((1, nm, d), lambda bi, i: (bi, 0, 0)),
                  pl.BlockSpec((1, d), fixed),
                  pl.BlockSpec((1, d), fixed)],
        out_specs=[pl.BlockSpec((tm, d), row), pl.BlockSpec((tm, d), row)],
        out_shape=[jax.ShapeDtypeStruct((n, d), F32), jax.ShapeDtypeStruct((n, d), BF16)],
        compiler_params=_cparams("parallel", "parallel"),
        name="memattn",
    )(x, xb, w_cq, w_co, mk, mv, lg, lb)


def _ffn_kernel(x_ref, xb_ref, wu_ref, wd_ref, lg_ref, lb_ref, o_ref, ob_ref, *, tf):
    xb = xb_ref[...]
    dff = wd_ref.shape[0]
    acc = None
    for j in range(dff // tf):
        gate = jnp.dot(xb, wu_ref[:, j * tf:(j + 1) * tf], preferred_element_type=F32)
        up = jnp.dot(xb, wu_ref[:, dff + j * tf:dff + (j + 1) * tf], preferred_element_type=F32)
        t = _dot(gate * _sigmoid(gate) * up, wd_ref[j * tf:(j + 1) * tf, :])
        acc = t if acc is None else acc + t
    y = _layer_norm(DEEPNORM_ALPHA * x_ref[...] + acc, lg_ref[...], lb_ref[...])
    o_ref[...] = y
    ob_ref[...] = y.astype(BF16)


def _ffn(x, xb, w_up, w_down, lg, lb, tm, tf):
    n, d = x.shape
    row = lambda i: (i, 0)
    fixed = lambda i: (0, 0)
    return pl.pallas_call(
        functools.partial(_ffn_kernel, tf=tf),
        grid=(n // tm,),
        in_specs=[pl.BlockSpec((tm, d), row),
                  pl.BlockSpec((tm, d), row),
                  pl.BlockSpec(w_up.shape, fixed, pipeline_mode=pl.Buffered(1)),
                  pl.BlockSpec(w_down.shape, fixed, pipeline_mode=pl.Buffered(1)),
                  pl.BlockSpec((1, d), fixed),
                  pl.BlockSpec((1, d), fixed)],
        out_specs=[pl.BlockSpec((tm, d), row), pl.BlockSpec((tm, d), row)],
        out_shape=[jax.ShapeDtypeStruct((n, d), F32), jax.ShapeDtypeStruct((n, d), BF16)],
        compiler_params=_cparams("parallel"),
        name="ffn",
    )(x, xb, w_up, w_down, lg, lb)


def _pack_layer(p):
    w_in = p['w_in']
    d = w_in.shape[0]
    o_sb, o_gqkv, o_gz = 0, 3 * SB_W, 3 * SB_W + 3 * REC_W
    o_gb = o_gz + REC_W
    o_mqkv = o_gb + 2 * GDN_HEADS
    o_mo = o_mqkv + 3 * REC_W
    o_mi = o_mo + REC_W
    gate_cols = jnp.concatenate([w_in[:, o_gb:o_gb + 2 * GDN_HEADS], w_in[:, o_mi:o_mi + 2 * ML_HEADS]], axis=1)
    used = COL_GATE + gate_cols.shape[1]
    w_al = jnp.concatenate([w_in[:, o_gqkv:o_gqkv + 3 * REC_W], w_in[:, o_mqkv:o_mqkv + 3 * REC_W],
                            w_in[:, o_gz:o_gz + REC_W], w_in[:, o_mo:o_mo + REC_W],
                            w_in[:, o_sb:o_sb + 3 * SB_W], gate_cols,
                            jnp.zeros((d, PROJ_W - used), w_in.dtype)], axis=1).astype(BF16)
    prow = jnp.zeros((SUBLANES, LANES), F32)
    prow = prow.at[0, GATE_G:GATE_G + GDN_HEADS].set(p['gdn_dt_bias'])
    prow = prow.at[0, GATE_I:GATE_I + ML_HEADS].set(p['mlstm_i_bias'])
    prow = prow.at[0, GATE_F:GATE_F + ML_HEADS].set(p['mlstm_f_bias'])
    prow = prow.at[1, GATE_G:GATE_G + GDN_HEADS].set(p['gdn_A_log'])
    conv_w8 = jnp.zeros((SUBLANES, 3 * REC_W), F32).at[0:CONV_W].set(p['gdn_conv_w'])
    return dict(
        w_al=w_al, prow=prow, pcol=prow.T, conv_w8=conv_w8,
        gnw=jnp.tile(p['gdn_norm_w'], GDN_HEADS)[None, :], mnw=p['mlstm_norm_w'][None, :],
        w_out=p['w_out'].astype(BF16), w_cq=p['w_cq'].astype(BF16), w_co=p['w_co'].astype(BF16),
        w_up=p['w_up'].astype(BF16), w_down=p['w_down'].astype(BF16),
        ln1=(p['ln1_g'][None, :], p['ln1_b'][None, :]), ln2=(p['ln2_g'][None, :], p['ln2_b'][None, :]),
        ln3=(p['ln3_g'][None, :], p['ln3_b'][None, :]))


def _head_consts():
    hid = np.arange(REC_W) // HEAD_DIM
    return jnp.asarray(hid[:, None] == hid[None, :], BF16)


def _sb_layout(k, v, tk):
    b, tkv, _ = k.shape
    nkb = tkv // tk
    kt = k.reshape(b, nkb, tk, SB_W).transpose(0, 1, 3, 2)
    vh = v.reshape(b, nkb, tk, SB_HEADS, HEAD_DIM).transpose(0, 3, 1, 2, 4)
    vh = vh.reshape(b * SB_HEADS, nkb, tk, HEAD_DIM)
    return kt.astype(BF16), vh.astype(BF16)


def _to_pair_lanes(a):
    z = jnp.zeros_like(a)
    even = (jnp.arange(a.shape[1]) % 2 == 0)[None, :, None, None]
    return jnp.where(even, jnp.concatenate([a, z], axis=-1), jnp.concatenate([z, a], axis=-1))


def _from_pair_lanes(a):
    even = (jnp.arange(a.shape[1]) % 2 == 0)[None, :, None, None]
    return jnp.where(even, a[..., :HEAD_DIM], a[..., HEAD_DIM:])


def _time_on_lanes(a, groups, width):
    return a.reshape(groups, width, a.shape[1]).transpose(0, 2, 1)


def _trunk_layer(x, xb, pk, e_heads, mk, mv, b, t, cfg, sb_past, conv_ctx, gdn_s, ml_c, ml_n, ml_m):
    n, d = x.shape
    l = min(t, CHUNK)
    nc = t // l
    cb = min(cfg['scan_cb'], nc)
    ng = b * nc // cb
    tk = cfg['sb_tk']
    ctx8 = jnp.pad(conv_ctx, ((0, 0), (SUBLANES - (CONV_W - 1), 0), (0, 0)))
    if sb_past is None:
        proj, kt, vh, mk_t, gate_t, gqkv_act, gk_t = _proj(x if xb is None else xb, pk['w_al'], ctx8,
                                                           pk['conv_w8'], e_heads, b, t, cfg['tm'])
        kt = kt.reshape(b, t // tk, SB_W, tk)
        q_off = 0
    else:
        proj = _matmul(xb, pk['w_al'], cfg['tm'], 1024)
    sk = proj[:, COL_SB + SB_W:COL_SB + 2 * SB_W].reshape(b, t, SB_W)
    sv = proj[:, COL_SB + 2 * SB_W:COL_SB + 3 * SB_W].reshape(b, t, SB_W)
    if sb_past is not None:
        past_k, past_v = sb_past
        q_off = past_k.shape[1]
        kv_len = q_off + t
        kv_pad = -(-kv_len // tk) * tk - kv_len
        kk = jnp.pad(jnp.concatenate([past_k.reshape(b, q_off, SB_W), sk], axis=1), ((0, 0), (0, kv_pad), (0, 0)))
        vv = jnp.pad(jnp.concatenate([past_v.reshape(b, q_off, SB_W), sv], axis=1), ((0, 0), (0, kv_pad), (0, 0)))
        kt, vh = _sb_layout(kk, vv, tk)
        gqkv_act = _gdn_pre(proj, ctx8, pk['conv_w8'], e_heads, b, t, cfg['conv_tt'])
        gate_t = _time_on_lanes(proj[:, COL_GATE:COL_GATE + LANES], b * nc, l)
        gk_t = _time_on_lanes(gqkv_act[:, REC_W:2 * REC_W], b * nc, l)
        mk_t = _time_on_lanes(proj[:, COL_MQKV + REC_W:COL_MQKV + 2 * REC_W], b * nc, l)

    o_sb = _sb_attn(proj, kt, vh, b, t, SB_HEADS, cfg['sb_tq'], q_off)
    gdn_scan = _gdn_scan_pair if l == HEAD_DIM else _gdn_scan
    o_g, gdn_s_new = gdn_scan(gqkv_act, gk_t, proj, gate_t, pk['prow'], pk['pcol'], gdn_s, b, t, l,
                              min(cfg['gdn_cb'], nc))
    m0 = jnp.pad(ml_m, ((0, 0), (0, LANES - ML_HEADS)))[:, None, :]
    if sb_past is None:
        c0 = _to_pair_lanes(jnp.swapaxes(ml_c, -1, -2))
        n0 = jnp.pad(_to_pair_lanes(ml_n[:, :, None, :])[:, :, 0, :], ((0, 0), (0, SUBLANES - ML_HEADS), (0, 0)))
        h_m, c_new, n_new, m_new = _ml_scan_t(proj, mk_t, gate_t, pk['prow'], pk['pcol'], c0, n0, m0, b, t, l, cb)
        c_new = jnp.swapaxes(_from_pair_lanes(c_new), -1, -2)
        n_new = _from_pair_lanes(n_new[:, :ML_HEADS, None, :])[:, :, 0, :]
    else:
        n0 = jnp.pad(ml_n, ((0, 0), (0, SUBLANES - ML_HEADS), (0, 0)))
        h_m, c_new, n_new, m_new = _ml_scan(proj, mk_t, gate_t, pk['prow'], pk['pcol'], ml_c, n0, m0, b, t, l, cb)
        n_new = n_new[:, :ML_HEADS, :]

    x1, x1b = _post(x, o_sb, o_g, h_m, proj, pk['w_out'], e_heads, pk['gnw'], pk['mnw'], *pk['ln1'], cfg['post_tm'])
    x2, x2b = _memattn(x1, x1b, pk['w_cq'], pk['w_co'], mk, mv, *pk['ln2'], b, t, cfg['mem_tm'])
    x3, x3b = _ffn(x2, x2b, pk['w_up'], pk['w_down'], *pk['ln3'], cfg['tm'], cfg['ffn_tf'])

    new_conv = proj.reshape(b, t, PROJ_W)[:, t - (CONV_W - 1):, COL_GQKV:COL_GQKV + 3 * REC_W]
    return (x3, x3b, sk.reshape(b, t, SB_HEADS, HEAD_DIM), sv.reshape(b, t, SB_HEADS, HEAD_DIM), new_conv,
            gdn_s_new, c_new, n_new, m_new[:, 0, :ML_HEADS])


def kernel(x_prompt, x_sample, cache_sb_k, cache_sb_v, cache_gdn_conv, state_gdn, state_mlstm_C, state_mlstm_n, state_mlstm_m, cache_mem_k, cache_mem_v, mem_prompt, w_in, gdn_conv_w, gdn_A_log, gdn_dt_bias, gdn_norm_w, mlstm_i_bias, mlstm_f_bias, mlstm_norm_w, w_out, ln1_g, ln1_b, w_cq, w_ckv, w_co, ln2_g, ln2_b, w_up, w_down, ln3_g, ln3_b):
    bp, tp, d = x_prompt.shape
    bs, ts, _ = x_sample.shape
    depth = w_in.shape[0]
    n_mem = mem_prompt.shape[1]
    dff = w_down.shape[1]
    cfg_p = dict(tm=512, post_tm=1024, sb_tq=512, sb_tk=256, conv_tt=512, mem_tm=1024, ffn_tf=dff, scan_cb=4, gdn_cb=16)
    cfg_s = dict(tm=bs * ts, post_tm=bs * ts, sb_tq=ts, sb_tk=256, conv_tt=ts, mem_tm=ts, ffn_tf=dff // 2, scan_cb=1, gdn_cb=1)
    e_heads = _head_consts()

    xp = x_prompt.reshape(bp * tp, d)
    xs = x_sample.reshape(bs * ts, d)
    xpb, xsb = None, xs.astype(BF16)
    mem_b = mem_prompt.reshape(bp * n_mem, d).astype(BF16)
    p_out = [[] for _ in range(9)]
    s_out = [[] for _ in range(7)]
    for li in range(depth):
        p = {'w_in': w_in[li], 'gdn_conv_w': gdn_conv_w[li], 'gdn_A_log': gdn_A_log[li],
             'gdn_dt_bias': gdn_dt_bias[li], 'gdn_norm_w': gdn_norm_w[li], 'mlstm_i_bias': mlstm_i_bias[li],
             'mlstm_f_bias': mlstm_f_bias[li], 'mlstm_norm_w': mlstm_norm_w[li], 'w_out': w_out[li],
             'ln1_g': ln1_g[li], 'ln1_b': ln1_b[li], 'w_cq': w_cq[li], 'w_co': w_co[li],
             'ln2_g': ln2_g[li], 'ln2_b': ln2_b[li], 'w_up': w_up[li], 'w_down': w_down[li],
             'ln3_g': ln3_g[li], 'ln3_b': ln3_b[li]}
        pk = _pack_layer(p)
        mkv = _matmul(mem_b, w_ckv[li].astype(BF16), n_mem, 1024)
        mk_p = mkv[:, :d].reshape(bp, n_mem, d)
        mv_p = mkv[:, d:].reshape(bp, n_mem, d)
        res = _trunk_layer(xp, xpb, pk, e_heads, mk_p.astype(BF16), mv_p.astype(BF16), bp, tp, cfg_p, None,
                           jnp.zeros((bp, CONV_W - 1, 3 * REC_W), F32),
                           jnp.zeros((bp, GDN_HEADS, HEAD_DIM, HEAD_DIM), F32),
                           jnp.zeros((bp, ML_HEADS, HEAD_DIM, HEAD_DIM), F32),
                           jnp.zeros((bp, ML_HEADS, HEAD_DIM), F32),
                           jnp.zeros((bp, ML_HEADS), F32))
        xp, xpb = res[0], res[1]
        hd = d // MEM_HEADS
        for j, a in enumerate(res[2:] + (mk_p.reshape(bp, n_mem, MEM_HEADS, hd), mv_p.reshape(bp, n_mem, MEM_HEADS, hd))):
            p_out[j].append(a)
        res = _trunk_layer(xs, xsb, pk, e_heads,
                           cache_mem_k[li].reshape(bs, n_mem, d).astype(BF16),
                           cache_mem_v[li].reshape(bs, n_mem, d).astype(BF16),
                           bs, ts, cfg_s, (cache_sb_k[li], cache_sb_v[li]), cache_gdn_conv[li],
                           state_gdn[li], state_mlstm_C[li], state_mlstm_n[li], state_mlstm_m[li])
        xs, xsb = res[0], res[1]
        for j, a in enumerate(res[2:]):
            s_out[j].append(a)
    p_st = [jnp.stack(a) for a in p_out]
    s_st = [jnp.stack(a) for a in s_out]
    return (xp.reshape(bp, tp, d), xs.reshape(bs, ts, d), *p_st, *s_st)
```
